```python
import math
import jax, jax.numpy as jnp
from jax import lax
import numpy as np

D_MODEL = 1024
BATCH = 8
SEQ = 4096
DEPTH = 1

MIX_WIDTH = D_MODEL
ATTN_WIDTH = MIX_WIDTH // 2
LRU_WIDTH = MIX_WIDTH - ATTN_WIDTH
DIFF_HEAD_DIM = 64
DIFF_V_DIM = 2 * DIFF_HEAD_DIM
N_DIFF_HEADS = ATTN_WIDTH // DIFF_V_DIM
ROPE_DIM = DIFF_HEAD_DIM // 4
ROPE_THETA = 500000.0
Q_BLOCK = 128
LRU_BLOCKS = 8
LRU_BLOCK_WIDTH = LRU_WIDTH // LRU_BLOCKS
CONV_WIDTH = 4
CONV_PAD = (2, 1)
LRU_C = 8.0
N_DIRS = 2
N_KEYS = 128
N_EXPERTS = N_KEYS * N_KEYS
PEER_HEADS = 8
PEER_TOPK = 16
DIM_KEY = 128
HALF_KEY = DIM_KEY // 2
PEER_BLOCK = 128
PLE_DIM = 256
POS_OFFSET_MAX = 1024
EPS = 1e-6

Q_COLS = N_DIFF_HEADS * 2 * DIFF_HEAD_DIM
K_COLS = N_DIFF_HEADS * 2 * DIFF_HEAD_DIM
V_COLS = N_DIFF_HEADS * DIFF_V_DIM
IN_COLS = Q_COLS + K_COLS + V_COLS + 2 * LRU_WIDTH

kernel_name = "hybrid_diffattn_rglru_peer_encoder"


def rms_norm(x, g):
    xf = x.astype(jnp.float32)
    y = xf * lax.rsqrt(jnp.mean(xf * xf, axis=-1, keepdims=True) + EPS)
    return (y * g.astype(jnp.float32)).astype(x.dtype)


def rope_cos_sin(positions):
    inv_freq = ROPE_THETA ** (-jnp.arange(0, ROPE_DIM, 2, dtype=jnp.float32) / ROPE_DIM)
    ang = positions.astype(jnp.float32)[..., None] * inv_freq
    return jnp.cos(ang), jnp.sin(ang)


def apply_partial_rope(t, cos, sin):
    half = ROPE_DIM // 2
    c = cos[:, :, None, None, :]
    s = sin[:, :, None, None, :]
    tf = t.astype(jnp.float32)
    x1 = tf[..., :half]
    x2 = tf[..., half:ROPE_DIM]
    rot = jnp.concatenate([x1 * c - x2 * s, x2 * c + x1 * s], axis=-1)
    return jnp.concatenate([rot, tf[..., ROPE_DIM:]], axis=-1).astype(t.dtype)


def diff_attention(q, k, v, lam, positions):
    B, S = q.shape[0], q.shape[1]
    cos, sin = rope_cos_sin(positions)
    q = apply_partial_rope(q, cos, sin) * (DIFF_HEAD_DIM ** -0.5)
    k = apply_partial_rope(k, cos, sin)
    kh = k.transpose(0, 2, 3, 1, 4)
    vh = v.transpose(0, 2, 1, 3)
    nb = S // Q_BLOCK
    q_blocks = q.transpose(0, 2, 3, 1, 4).reshape(
        B, N_DIFF_HEADS, 2, nb, Q_BLOCK, DIFF_HEAD_DIM).transpose(3, 0, 1, 2, 4, 5)

    def block(qb):
        s = jnp.einsum('bhmqd,bhmkd->bhmqk', qb, kh).astype(jnp.float32)
        pr = jax.nn.softmax(s, axis=-1)
        w = pr[:, :, 0] - lam * pr[:, :, 1]
        return jnp.einsum('bhqk,bhkd->bhqd', w.astype(vh.dtype), vh)

    o = lax.map(block, q_blocks)
    return o.transpose(1, 0, 3, 2, 4).reshape(B, S, N_DIFF_HEADS, DIFF_V_DIM)


def _lin_combine(left, right):
    a_l, b_l = left
    a_r, b_r = right
    return a_l * a_r, a_r * b_l + b_r


def block_diag_linear(u, w, b):
    B, S = u.shape[0], u.shape[1]
    ub = u.reshape(B, S, LRU_BLOCKS, LRU_BLOCK_WIDTH)
    y = jnp.einsum('bsnc,nce->bsne', ub, w).reshape(B, S, LRU_WIDTH)
    return y + b


def rg_lru_direction(u, wa, ba, wx, bx, lam, reverse):
    uf = u.astype(jnp.float32)
    r = jax.nn.sigmoid(block_diag_linear(u, wa, ba).astype(jnp.float32))
    i = jax.nn.sigmoid(block_diag_linear(u, wx, bx).astype(jnp.float32))
    log_a = -LRU_C * r * jax.nn.softplus(-lam.astype(jnp.float32))
    a = jnp.exp(log_a)
    mult = jnp.sqrt(-jnp.expm1(2.0 * log_a))
    bterm = mult * (i * uf)
    _, h = lax.associative_scan(_lin_combine, (a, bterm), reverse=reverse, axis=1)
    return h


def bi_rglru(u, gate, conv_w, conv_b, lru_wa, lru_ba, lru_wx, lru_bx, lru_lambda):
    C = u.shape[-1]
    uc = lax.conv_general_dilated(
        u, conv_w[:, None, :].astype(u.dtype), window_strides=(1,), padding=[CONV_PAD],
        dimension_numbers=('NWC', 'WIO', 'NWC'), feature_group_count=C) + conv_b
    h_f = rg_lru_direction(uc, lru_wa[0], lru_ba[0], lru_wx[0], lru_bx[0], lru_lambda[0], False)
    h_b = rg_lru_direction(uc, lru_wa[1], lru_ba[1], lru_wx[1], lru_bx[1], lru_lambda[1], True)
    y = (h_f + h_b) * jax.nn.gelu(gate.astype(jnp.float32))
    return y.astype(u.dtype)


def peer_layer(xn, wq, keys1, keys2, exp_u, exp_v):
    B, S, D = xn.shape
    q = (xn @ wq).reshape(B, S, PEER_HEADS, 2, HALF_KEY)
    s1 = jnp.einsum('bshd,kd->bshk', q[..., 0, :], keys1).astype(jnp.float32)
    s2 = jnp.einsum('bshd,kd->bshk', q[..., 1, :], keys2).astype(jnp.float32)
    v1, i1 = lax.top_k(s1, PEER_TOPK)
    v2, i2 = lax.top_k(s2, PEER_TOPK)
    cand = (v1[..., :, None] + v2[..., None, :]).reshape(B, S, PEER_HEADS, PEER_TOPK * PEER_TOPK)
    cand_idx = (i1[..., :, None] * N_KEYS + i2[..., None, :]).reshape(
        B, S, PEER_HEADS, PEER_TOPK * PEER_TOPK)
    sc, pos = lax.top_k(cand, PEER_TOPK)
    idx = jnp.take_along_axis(cand_idx, pos, axis=-1)
    g = jax.nn.softmax(sc, axis=-1)

    T = B * S
    nb = T // PEER_BLOCK
    NK = PEER_HEADS * PEER_TOPK
    xt = xn.reshape(nb, PEER_BLOCK, D)
    idx_t = idx.reshape(nb, PEER_BLOCK, NK)
    g_t = g.reshape(nb, PEER_BLOCK, NK)

    def block(args):
        xb, ib, gb = args
        ue = exp_u[ib]
        logits = jnp.einsum('td,tkd->tk', xb, ue).astype(jnp.float32)
        act = (jax.nn.gelu(logits) * gb).astype(xb.dtype)
        ve = exp_v[ib]
        return jnp.einsum('tk,tkd->td', act, ve)

    out = lax.map(block, (xt, idx_t, g_t))
    return out.reshape(B, S, D)


def setup_inputs(seed: int = 0) -> dict:
    key = jax.random.key(seed)
    ks = iter(jax.random.split(key, 40))

    def nrm(shape, scale):
        return jax.random.normal(next(ks), shape, jnp.float32) * scale

    def gain(shape):
        return 1.0 + nrm(shape, 0.01)

    x = nrm((BATCH, SEQ, D_MODEL), 1.0)
    p = nrm((DEPTH, BATCH, SEQ, PLE_DIM), 1.0)
    positions = (jnp.arange(SEQ, dtype=jnp.int32)[None, :]
                 + jax.random.randint(next(ks), (BATCH, 1), 0, POS_OFFSET_MAX, dtype=jnp.int32))
    u_a = jax.random.uniform(next(ks), (DEPTH, N_DIRS, LRU_WIDTH), jnp.float32, 0.9, 0.999)
    a0 = u_a ** (1.0 / LRU_C)
    lru_lambda = jnp.log(a0) - jnp.log1p(-a0)
    return {
        "x": x,
        "p": p,
        "positions": positions,
        "norm_mix_g": gain((DEPTH, D_MODEL)),
        "w_in": nrm((DEPTH, D_MODEL, IN_COLS), D_MODEL ** -0.5),
        "lambda_q1": nrm((DEPTH, DIFF_HEAD_DIM), 0.1),
        "lambda_k1": nrm((DEPTH, DIFF_HEAD_DIM), 0.1),
        "lambda_q2": nrm((DEPTH, DIFF_HEAD_DIM), 0.1),
        "lambda_k2": nrm((DEPTH, DIFF_HEAD_DIM), 0.1),
        "diff_norm_g": gain((DEPTH, DIFF_V_DIM)),
        "conv_w": nrm((DEPTH, CONV_WIDTH, LRU_WIDTH), CONV_WIDTH ** -0.5),
        "conv_b": nrm((DEPTH, LRU_WIDTH), 0.01),
        "lru_wa": nrm((DEPTH, N_DIRS, LRU_BLOCKS, LRU_BLOCK_WIDTH, LRU_BLOCK_WIDTH), LRU_BLOCK_WIDTH ** -0.5),
        "lru_ba": nrm((DEPTH, N_DIRS, LRU_WIDTH), 0.01),
        "lru_wx": nrm((DEPTH, N_DIRS, LRU_BLOCKS, LRU_BLOCK_WIDTH, LRU_BLOCK_WIDTH), LRU_BLOCK_WIDTH ** -0.5),
        "lru_bx": nrm((DEPTH, N_DIRS, LRU_WIDTH), 0.01),
        "lru_lambda": lru_lambda,
        "lru_norm_g": gain((DEPTH, LRU_WIDTH)),
        "w_out": nrm((DEPTH, MIX_WIDTH, D_MODEL), MIX_WIDTH ** -0.5),
        "norm_ffn_g": gain((DEPTH, D_MODEL)),
        "peer_wq": nrm((DEPTH, D_MODEL, PEER_HEADS * DIM_KEY), D_MODEL ** -0.5),
        "peer_keys1": nrm((DEPTH, N_KEYS, HALF_KEY), HALF_KEY ** -0.5),
        "peer_keys2": nrm((DEPTH, N_KEYS, HALF_KEY), HALF_KEY ** -0.5),
        "peer_u": nrm((DEPTH, N_EXPERTS, D_MODEL), D_MODEL ** -0.5),
        "peer_v": nrm((DEPTH, N_EXPERTS, D_MODEL), PEER_HEADS ** -0.5),
        "norm_ple_g": gain((DEPTH, D_MODEL)),
        "ple_w_gate": nrm((DEPTH, D_MODEL, D_MODEL), D_MODEL ** -0.5),
        "ple_w_proj": nrm((DEPTH, PLE_DIM, D_MODEL), PLE_DIM ** -0.5),
        "final_norm_g": gain((D_MODEL,)),
    }


def reference(x, p, positions, norm_mix_g, w_in, lambda_q1, lambda_k1, lambda_q2, lambda_k2,
              diff_norm_g, conv_w, conv_b, lru_wa, lru_ba, lru_wx, lru_bx, lru_lambda,
              lru_norm_g, w_out, norm_ffn_g, peer_wq, peer_keys1, peer_keys2, peer_u, peer_v,
              norm_ple_g, ple_w_gate, ple_w_proj, final_norm_g):
    B, S, _ = x.shape
    h = x
    for i in range(DEPTH):
        lambda_init = 0.8 - 0.6 * math.exp(-0.3 * i)
        xn = rms_norm(h, norm_mix_g[i])
        proj = xn @ w_in[i]
        q = proj[..., :Q_COLS].reshape(B, S, N_DIFF_HEADS, 2, DIFF_HEAD_DIM)
        k = proj[..., Q_COLS:Q_COLS + K_COLS].reshape(B, S, N_DIFF_HEADS, 2, DIFF_HEAD_DIM)
        o0 = Q_COLS + K_COLS
        v = proj[..., o0:o0 + V_COLS].reshape(B, S, N_DIFF_HEADS, DIFF_V_DIM)
        o1 = o0 + V_COLS
        lru_in = proj[..., o1:o1 + LRU_WIDTH]
        lru_gate = proj[..., o1 + LRU_WIDTH:o1 + 2 * LRU_WIDTH]

        lam = (jnp.exp(jnp.sum(lambda_q1[i].astype(jnp.float32) * lambda_k1[i].astype(jnp.float32)))
               - jnp.exp(jnp.sum(lambda_q2[i].astype(jnp.float32) * lambda_k2[i].astype(jnp.float32)))
               + lambda_init)
        attn = diff_attention(q, k, v, lam, positions)
        attn = (rms_norm(attn, diff_norm_g[i]) * (1.0 - lambda_init)).reshape(B, S, ATTN_WIDTH)

        rec = bi_rglru(lru_in, lru_gate, conv_w[i], conv_b[i], lru_wa[i], lru_ba[i],
                       lru_wx[i], lru_bx[i], lru_lambda[i])
        rec = rms_norm(rec, lru_norm_g[i])

        mixed = jnp.concatenate([attn.astype(h.dtype), rec.astype(h.dtype)], axis=-1)
        h = h + mixed @ w_out[i]

        xn2 = rms_norm(h, norm_ffn_g[i])
        h = h + peer_layer(xn2, peer_wq[i], peer_keys1[i], peer_keys2[i], peer_u[i], peer_v[i])

        xn3 = rms_norm(h, norm_ple_g[i])
        gate = jax.nn.sigmoid((xn3 @ ple_w_gate[i]).astype(jnp.float32)).astype(h.dtype)
        h = h + gate * (p[i] @ ple_w_proj[i])
    return rms_norm(h, final_norm_g)
```

```python
import functools
import math

import jax
import jax.numpy as jnp
from jax import lax
from jax.experimental import pallas as pl
from jax.experimental.pallas import tpu as pltpu

F32 = jnp.float32
BF16 = jnp.bfloat16
I32 = jnp.int32

EPS = 1e-6
DIFF_HEAD_DIM = 64
DIFF_V_DIM = 128
N_DIFF_HEADS = 4
ROPE_DIM = 16
ROPE_THETA = 500000.0
LRU_WIDTH = 512
LRU_C = 8.0
N_KEYS = 128
PEER_HEADS = 8
PEER_TOPK = 16
HALF_KEY = 64
LANES = 128
SUBLANES = 8
VMEM_LIMIT = 56 * 1024 * 1024


def _rms(x, g):
    return x * lax.rsqrt(jnp.mean(x * x, axis=-1, keepdims=True) + EPS) * g


def _gelu_tanh(x):
    return 0.5 * x * (1.0 + jnp.tanh(math.sqrt(2.0 / math.pi) * (x + 0.044715 * (x * x * x))))


def _sigmoid(x):
    return 1.0 / (1.0 + jnp.exp(-x))


def _inproj_kernel(x_ref, pos_ref, g_ref, w_ref, q_ref, k_ref, v_ref, u_ref, gate_ref):
    x = x_ref[0]
    xn = _rms(x, g_ref[...]).astype(BF16)
    pos = pos_ref[0].astype(F32)
    lane = lax.broadcasted_iota(I32, (1, LANES), 1)
    p = lane & (DIFF_HEAD_DIM - 1)
    freq = (p & (ROPE_DIM // 2 - 1)).astype(F32)
    inv_freq = jnp.exp(freq * (-2.0 / ROPE_DIM * math.log(ROPE_THETA)))
    ang = pos * inv_freq
    cs = jnp.cos(ang)
    sn = jnp.sin(ang)
    half = ROPE_DIM // 2
    c_mul = jnp.where(p < ROPE_DIM, cs, 1.0)
    s_up = jnp.where(p < half, -sn, 0.0)
    s_dn = jnp.where((p >= half) & (p < ROPE_DIM), sn, 0.0)

    def rope(t):
        return t * c_mul + pltpu.roll(t, LANES - half, 1) * s_up + pltpu.roll(t, half, 1) * s_dn

    nq = q_ref.shape[-1]
    pq = jnp.dot(xn, w_ref[:, 0:nq], preferred_element_type=F32)
    pk = jnp.dot(xn, w_ref[:, nq:2 * nq], preferred_element_type=F32)
    scale = DIFF_HEAD_DIM ** -0.5
    for j in range(nq // LANES):
        sl = slice(j * LANES, (j + 1) * LANES)
        q_ref[0, :, sl] = (rope(pq[:, sl]) * scale).astype(BF16)
        k_ref[0, :, sl] = rope(pk[:, sl]).astype(BF16)
    v_ref[0] = jnp.dot(xn, w_ref[:, 2 * nq:3 * nq], preferred_element_type=F32).astype(BF16)
    u_ref[0] = jnp.dot(xn, w_ref[:, 3 * nq:3 * nq + LRU_WIDTH], preferred_element_type=F32)
    gate_ref[0] = jnp.dot(xn, w_ref[:, 3 * nq + LRU_WIDTH:3 * nq + 2 * LRU_WIDTH],
                          preferred_element_type=F32).astype(BF16)


def _in_proj(x, positions, g, w_in_bf, tm):
    B, S, D = x.shape
    nq = N_DIFF_HEADS * 2 * DIFF_HEAD_DIM
    ncols = w_in_bf.shape[1]
    row = lambda b, i: (b, i, 0)
    return pl.pallas_call(
        _inproj_kernel,
        grid=(B, S // tm),
        in_specs=[
            pl.BlockSpec((1, tm, D), row),
            pl.BlockSpec((1, tm, 1), row),
            pl.BlockSpec((1, D), lambda b, i: (0, 0)),
            pl.BlockSpec((D, ncols), lambda b, i: (0, 0)),
        ],
        out_specs=[
            pl.BlockSpec((1, tm, nq), row),
            pl.BlockSpec((1, tm, nq), row),
            pl.BlockSpec((1, tm, nq), row),
            pl.BlockSpec((1, tm, LRU_WIDTH), row),
            pl.BlockSpec((1, tm, LRU_WIDTH), row),
        ],
        out_shape=[
            jax.ShapeDtypeStruct((B, S, nq), BF16),
            jax.ShapeDtypeStruct((B, S, nq), BF16),
            jax.ShapeDtypeStruct((B, S, nq), BF16),
            jax.ShapeDtypeStruct((B, S, LRU_WIDTH), F32),
            jax.ShapeDtypeStruct((B, S, LRU_WIDTH), BF16),
        ],
        compiler_params=pltpu.CompilerParams(
            dimension_semantics=("parallel", "parallel"), vmem_limit_bytes=VMEM_LIMIT),
        name="in_proj",
    )(x, positions.reshape(B, S, 1), g.reshape(1, D), w_in_bf)


def _attn_kernel(lq1_ref, lk1_ref, lq2_ref, lk2_ref, g_ref, q_ref, k_ref, v_ref, o_ref, *, lambda_init):
    lam = (jnp.exp(jnp.sum(lq1_ref[...] * lk1_ref[...], axis=-1, keepdims=True))
           - jnp.exp(jnp.sum(lq2_ref[...] * lk2_ref[...], axis=-1, keepdims=True))
           + lambda_init)
    q = q_ref[0]
    k = k_ref[0]
    v = v_ref[0]
    lane = lax.broadcasted_iota(I32, q.shape, 1)
    zero = jnp.zeros_like(q)
    q0 = jnp.where(lane < DIFF_HEAD_DIM, q, zero)
    q1 = jnp.where(lane >= DIFF_HEAD_DIM, q, zero)
    nt = (((1,), (1,)), ((), ()))
    s0 = lax.dot_general(q0, k, nt, preferred_element_type=F32)
    s1 = lax.dot_general(q1, k, nt, preferred_element_type=F32)
    p0 = jnp.exp(s0 - jnp.max(s0, axis=-1, keepdims=True))
    p1 = jnp.exp(s1 - jnp.max(s1, axis=-1, keepdims=True))
    r0 = 1.0 / jnp.sum(p0, axis=-1, keepdims=True)
    r1 = lam / jnp.sum(p1, axis=-1, keepdims=True)
    w = (p0 * r0 - p1 * r1).astype(BF16)
    o = jnp.dot(w, v, preferred_element_type=F32)
    o_ref[0] = (_rms(o, g_ref[...]) * (1.0 - lambda_init)).astype(BF16)


def _diff_attn(q, k, v, lq1, lk1, lq2, lk2, g, lambda_init, tq):
    B, S, W = q.shape
    H = W // DIFF_V_DIM
    vec = lambda b, h, i: (0, 0)
    return pl.pallas_call(
        functools.partial(_attn_kernel, lambda_init=lambda_init),
        grid=(B, H, S // tq),
        in_specs=[
            pl.BlockSpec((1, DIFF_HEAD_DIM), vec),
            pl.BlockSpec((1, DIFF_HEAD_DIM), vec),
            pl.BlockSpec((1, DIFF_HEAD_DIM), vec),
            pl.BlockSpec((1, DIFF_HEAD_DIM), vec),
            pl.BlockSpec((1, DIFF_V_DIM), vec),
            pl.BlockSpec((1, tq, DIFF_V_DIM), lambda b, h, i: (b, i, h)),
            pl.BlockSpec((1, S, DIFF_V_DIM), lambda b, h, i: (b, 0, h)),
            pl.BlockSpec((1, S, DIFF_V_DIM), lambda b, h, i: (b, 0, h)),
        ],
        out_specs=pl.BlockSpec((1, tq, DIFF_V_DIM), lambda b, h, i: (b, i, h)),
        out_shape=jax.ShapeDtypeStruct((B, S, W), BF16),
        compiler_params=pltpu.CompilerParams(
            dimension_semantics=("parallel", "parallel", "parallel"), vmem_limit_bytes=VMEM_LIMIT),
        name="diff_attn",
    )(lq1.reshape(1, -1), lk1.reshape(1, -1), lq2.reshape(1, -1), lk2.reshape(1, -1),
      g.reshape(1, -1), q, k, v)


def _lru_kernel(u_ref, gate_ref, cw_ref, cb_ref, w_ref, bias_ref, lam_ref, g_ref, out_ref, hf_ref, *, tc):
    S = u_ref.shape[1]
    C = u_ref.shape[2]
    nc = S // tc
    halo = SUBLANES
    neg_lam = -lam_ref[...]
    sp = jnp.maximum(neg_lam, 0.0) + jnp.log(1.0 + jnp.exp(-jnp.abs(neg_lam)))
    row = lax.broadcasted_iota(I32, (tc, 1), 0)
    conv_taps = cw_ref.shape[0]
    conv_left = 2

    def gates(c, d):
        r0 = pl.multiple_of(c * tc, tc)
        x = u_ref[0, pl.ds(r0, tc), :]
        prev = u_ref[0, pl.ds(pl.multiple_of(jnp.maximum(r0 - halo, 0), halo), halo), :]
        nxt = u_ref[0, pl.ds(pl.multiple_of(jnp.minimum(r0 + tc, S - halo), halo), halo), :]
        prev = jnp.where(c > 0, prev, 0.0)
        nxt = jnp.where(c < nc - 1, nxt, 0.0)
        win = jnp.concatenate([prev, x, nxt], axis=0)
        uc = cb_ref[...]
        for j in range(conv_taps):
            o = halo - conv_left + j
            uc = uc + cw_ref[j:j + 1, :] * win[o:o + tc, :]
        pre = (jnp.dot(uc.astype(BF16), w_ref[:, d * 2 * C:(d + 1) * 2 * C], preferred_element_type=F32)
               + bias_ref[:, d * 2 * C:(d + 1) * 2 * C])
        r = _sigmoid(pre[:, :C])
        i = _sigmoid(pre[:, C:])
        log_a = -LRU_C * r * sp[d:d + 1, :]
        a = jnp.exp(log_a)
        th = jnp.tanh(log_a)
        mult = jnp.sqrt(-2.0 * th / (1.0 - th))
        return a, mult * (i * uc)

    def scan(a, b, reverse):
        d = 1
        while d < tc:
            if reverse:
                a_s = pltpu.roll(a, tc - d, 0)
                b_s = pltpu.roll(b, tc - d, 0)
                m = row < tc - d
            else:
                a_s = pltpu.roll(a, d, 0)
                b_s = pltpu.roll(b, d, 0)
                m = row >= d
            a_s = jnp.where(m, a_s, 1.0)
            b_s = jnp.where(m, b_s, 0.0)
            b = a * b_s + b
            a = a * a_s
            d *= 2
        return a, b

    def fwd_body(c, h0):
        a, b = gates(c, 0)
        a, b = scan(a, b, False)
        h = a * h0 + b
        hf_ref[pl.ds(pl.multiple_of(c * tc, tc), tc), :] = h
        return h[tc - 1:tc, :]

    lax.fori_loop(0, nc, fwd_body, jnp.zeros((1, C), F32))

    def bwd_body(j, h0):
        c = nc - 1 - j
        r0 = pl.multiple_of(c * tc, tc)
        a, b = gates(c, 1)
        a, b = scan(a, b, True)
        h = a * h0 + b
        y = (hf_ref[pl.ds(r0, tc), :] + h) * _gelu_tanh(gate_ref[0, pl.ds(r0, tc), :].astype(F32))
        out_ref[0, pl.ds(r0, tc), :] = _rms(y, g_ref[...]).astype(BF16)
        return h[0:1, :]

    lax.fori_loop(0, nc, bwd_body, jnp.zeros((1, C), F32))


def _bi_rglru(u, gate, conv_w, conv_b, w_all_bf, bias_all, lru_lambda, g, tc):
    B, S, C = u.shape
    full = lambda b: (0, 0)
    return pl.pallas_call(
        functools.partial(_lru_kernel, tc=tc),
        grid=(B,),
        in_specs=[
            pl.BlockSpec((1, S, C), lambda b: (b, 0, 0)),
            pl.BlockSpec((1, S, C), lambda b: (b, 0, 0)),
            pl.BlockSpec(conv_w.shape, full),
            pl.BlockSpec((1, C), full),
            pl.BlockSpec(w_all_bf.shape, full),
            pl.BlockSpec(bias_all.shape, full),
            pl.BlockSpec(lru_lambda.shape, full),
            pl.BlockSpec((1, C), full),
        ],
        out_specs=pl.BlockSpec((1, S, C), lambda b: (b, 0, 0)),
        out_shape=jax.ShapeDtypeStruct((B, S, C), BF16),
        scratch_shapes=[pltpu.VMEM((S, C), F32)],
        compiler_params=pltpu.CompilerParams(
            dimension_semantics=("parallel",), vmem_limit_bytes=VMEM_LIMIT),
        name="bi_rglru",
    )(u, gate, conv_w, conv_b.reshape(1, C), w_all_bf, bias_all, lru_lambda, g.reshape(1, C))


def _topk_rows(s, k):
    n = s.shape[0]
    rid = lax.broadcasted_iota(I32, s.shape, 0)
    vals, ids = [], []
    for _ in range(k):
        m = jnp.max(s, axis=0, keepdims=True)
        sel = jnp.min(jnp.where(s == m, rid, n), axis=0, keepdims=True)
        vals.append(m)
        ids.append(sel)
        s = jnp.where(rid == sel, -jnp.inf, s)
    return jnp.concatenate(vals, axis=0), jnp.concatenate(ids, axis=0)


def _route_kernel(attn_ref, rec_ref, x_ref, wo_ref, g_ref, wq_ref, k1_ref, k2_ref,
                  h_ref, xn_ref, idx_ref, gate_ref):
    aw = attn_ref.shape[1]
    h = (x_ref[...]
         + jnp.dot(attn_ref[...], wo_ref[0:aw, :], preferred_element_type=F32)
         + jnp.dot(rec_ref[...], wo_ref[aw:, :], preferred_element_type=F32))
    h_ref[...] = h
    xn = _rms(h, g_ref[...]).astype(BF16)
    xn_ref[...] = xn
    q = jnp.dot(xn, wq_ref[...], preferred_element_type=F32).astype(BF16)
    nt = (((1,), (1,)), ((), ()))
    s1 = lax.dot_general(k1_ref[...], q, nt, preferred_element_type=F32)
    s2 = lax.dot_general(k2_ref[...], q, nt, preferred_element_type=F32)
    k = PEER_TOPK
    idx_rows, gate_rows = [], []
    for hd in range(PEER_HEADS):
        v1, i1 = _topk_rows(s1[hd * N_KEYS:(hd + 1) * N_KEYS, :], k)
        v2, i2 = _topk_rows(s2[hd * N_KEYS:(hd + 1) * N_KEYS, :], k)
        cand = jnp.concatenate([v1[i:i + 1, :] + v2 for i in range(k)], axis=0)
        cidx = jnp.concatenate([i1[i:i + 1, :] * N_KEYS + i2 for i in range(k)], axis=0)
        pos = lax.broadcasted_iota(I32, cand.shape, 0)
        sc, ids = [], []
        for _ in range(k):
            m = jnp.max(cand, axis=0, keepdims=True)
            sel = jnp.min(jnp.where(cand == m, pos, k * k), axis=0, keepdims=True)
            hit = pos == sel
            sc.append(m)
            ids.append(jnp.max(jnp.where(hit, cidx, -1), axis=0, keepdims=True))
            cand = jnp.where(hit, -jnp.inf, cand)
        sc = jnp.concatenate(sc, axis=0)
        e = jnp.exp(sc - sc[0:1, :])
        gate_rows.append(e / jnp.sum(e, axis=0, keepdims=True))
        idx_rows.append(jnp.concatenate(ids, axis=0))
    idx_ref[...] = jnp.concatenate(idx_rows, axis=0)
    gate_ref[...] = jnp.concatenate(gate_rows, axis=0)


def _mix_route(attn, rec, x2, wo_bf, g, wq_bf, k1t_bf, k2t_bf, tm):
    T, D = x2.shape
    NK = PEER_HEADS * PEER_TOPK
    row = lambda i: (i, 0)
    full = lambda i: (0, 0)
    return pl.pallas_call(
        _route_kernel,
        grid=(T // tm,),
        in_specs=[
            pl.BlockSpec((tm, attn.shape[1]), row),
            pl.BlockSpec((tm, rec.shape[1]), row),
            pl.BlockSpec((tm, D), row),
            pl.BlockSpec(wo_bf.shape, full),
            pl.BlockSpec((1, D), full),
            pl.BlockSpec(wq_bf.shape, full),
            pl.BlockSpec(k1t_bf.shape, full),
            pl.BlockSpec(k2t_bf.shape, full),
        ],
        out_specs=[
            pl.BlockSpec((tm, D), row),
            pl.BlockSpec((tm, D), row),
            pl.BlockSpec((NK, tm), lambda i: (0, i)),
            pl.BlockSpec((NK, tm), lambda i: (0, i)),
        ],
        out_shape=[
            jax.ShapeDtypeStruct((T, D), F32),
            jax.ShapeDtypeStruct((T, D), BF16),
            jax.ShapeDtypeStruct((NK, T), I32),
            jax.ShapeDtypeStruct((NK, T), F32),
        ],
        compiler_params=pltpu.CompilerParams(
            dimension_semantics=("parallel",), vmem_limit_bytes=VMEM_LIMIT),
        name="mix_route",
    )(attn, rec, x2, wo_bf, g.reshape(1, D), wq_bf, k1t_bf, k2t_bf)


def _peer_kernel(idx_ref, idx_next_ref, xn_ref, gate_ref, u_hbm, v_hbm, o_ref, ubuf, vbuf, sems, *, tt, nk):
    i = pl.program_id(0)
    n = pl.num_programs(0)
    rows = tt * nk

    def issue(ids_ref, slot):
        def body(r, carry):
            e = ids_ref[0, 0, r]
            pltpu.make_async_copy(u_hbm.at[pl.ds(e, 1), :], ubuf.at[slot, pl.ds(r, 1), :], sems.at[0, slot]).start()
            pltpu.make_async_copy(v_hbm.at[pl.ds(e, 1), :], vbuf.at[slot, pl.ds(r, 1), :], sems.at[1, slot]).start()
            return carry
        lax.fori_loop(0, rows, body, 0)

    slot = i % 2

    @pl.when(i == 0)
    def _():
        issue(idx_ref, 0)

    @pl.when(i + 1 < n)
    def _():
        issue(idx_next_ref, 1 - slot)

    pltpu.make_async_copy(u_hbm.at[pl.ds(0, rows), :], ubuf.at[slot], sems.at[0, slot]).wait()
    pltpu.make_async_copy(v_hbm.at[pl.ds(0, rows), :], vbuf.at[slot], sems.at[1, slot]).wait()

    nt = (((1,), (1,)), ((), ()))
    outs = []
    for t in range(tt):
        xt = xn_ref[t:t + 1, :]
        ut = ubuf[slot, t * nk:(t + 1) * nk, :].astype(BF16)
        logits = lax.dot_general(xt, ut, nt, preferred_element_type=F32)
        act = (_gelu_tanh(logits) * gate_ref[t:t + 1, :]).astype(BF16)
        vt = vbuf[slot, t * nk:(t + 1) * nk, :].astype(BF16)
        outs.append(jnp.dot(act, vt, preferred_element_type=F32))
    o_ref[...] = jnp.concatenate(outs, axis=0)


def _peer(idx, gate, xn, peer_u, peer_v, tt):
    T, D = xn.shape
    nk = idx.shape[1]
    n = T // tt
    idx3 = idx.reshape(n, 1, tt * nk)
    return pl.pallas_call(
        functools.partial(_peer_kernel, tt=tt, nk=nk),
        grid=(n,),
        in_specs=[
            pl.BlockSpec((1, 1, tt * nk), lambda i: (i, 0, 0), memory_space=pltpu.SMEM),
            pl.BlockSpec((1, 1, tt * nk), lambda i: (jnp.minimum(i + 1, n - 1), 0, 0), memory_space=pltpu.SMEM),
            pl.BlockSpec((tt, D), lambda i: (i, 0)),
            pl.BlockSpec((tt, nk), lambda i: (i, 0)),
            pl.BlockSpec(memory_space=pl.ANY),
            pl.BlockSpec(memory_space=pl.ANY),
        ],
        out_specs=pl.BlockSpec((tt, D), lambda i: (i, 0)),
        out_shape=jax.ShapeDtypeStruct((T, D), F32),
        scratch_shapes=[
            pltpu.VMEM((2, tt * nk, D), F32),
            pltpu.VMEM((2, tt * nk, D), F32),
            pltpu.SemaphoreType.DMA((2, 2)),
        ],
        compiler_params=pltpu.CompilerParams(
            dimension_semantics=("arbitrary",), vmem_limit_bytes=VMEM_LIMIT),
        name="peer",
    )(idx3, idx3, xn, gate, peer_u, peer_v)


def _ple_kernel(h_ref, po_ref, p_ref, g3_ref, wg_ref, wp_ref, gf_ref, o_ref, *, final):
    h = h_ref[...] + po_ref[...]
    xn = _rms(h, g3_ref[...]).astype(BF16)
    gate = _sigmoid(jnp.dot(xn, wg_ref[...], preferred_element_type=F32))
    proj = jnp.dot(p_ref[...].astype(BF16), wp_ref[...], preferred_element_type=F32)
    h = h + gate * proj
    o_ref[...] = _rms(h, gf_ref[...]) if final else h


def _ple_out(h1, peer_out, p2, g3, wg_bf, wp_bf, gf, final, tm):
    T, D = h1.shape
    row = lambda i: (i, 0)
    full = lambda i: (0, 0)
    return pl.pallas_call(
        functools.partial(_ple_kernel, final=final),
        grid=(T // tm,),
        in_specs=[
            pl.BlockSpec((tm, D), row),
            pl.BlockSpec((tm, D), row),
            pl.BlockSpec((tm, p2.shape[1]), row),
            pl.BlockSpec((1, D), full),
            pl.BlockSpec(wg_bf.shape, full),
            pl.BlockSpec(wp_bf.shape, full),
            pl.BlockSpec((1, D), full),
        ],
        out_specs=pl.BlockSpec((tm, D), row),
        out_shape=jax.ShapeDtypeStruct((T, D), F32),
        compiler_params=pltpu.CompilerParams(
            dimension_semantics=("parallel",), vmem_limit_bytes=VMEM_LIMIT),
        name="ple_out",
    )(h1, peer_out, p2, g3.reshape(1, D), wg_bf, wp_bf, gf.reshape(1, D))


def _block_diag(w):
    nb, bw, _ = w.shape
    eye = jnp.eye(nb, dtype=w.dtype)
    return (eye[:, None, :, None] * w[:, :, None, :]).reshape(nb * bw, nb * bw)


def _key_matrix(keys, half):
    z = jnp.zeros_like(keys)
    blk = jnp.concatenate([keys, z] if half == 0 else [z, keys], axis=1)
    return jnp.kron(jnp.eye(PEER_HEADS, dtype=keys.dtype), blk)


def kernel(x, p, positions, norm_mix_g, w_in, lambda_q1, lambda_k1, lambda_q2, lambda_k2, diff_norm_g, conv_w, conv_b, lru_wa, lru_ba, lru_wx, lru_bx, lru_lambda, lru_norm_g, w_out, norm_ffn_g, peer_wq, peer_keys1, peer_keys2, peer_u, peer_v, norm_ple_g, ple_w_gate, ple_w_proj, final_norm_g):
    B, S, D = x.shape
    T = B * S
    depth = w_in.shape[0]
    h = x
    for i in range(depth):
        lambda_init = 0.8 - 0.6 * math.exp(-0.3 * i)
        q, k, v, u, gate = _in_proj(h, positions, norm_mix_g[i], w_in[i].astype(BF16), tm=512)
        attn = _diff_attn(q, k, v, lambda_q1[i], lambda_k1[i], lambda_q2[i], lambda_k2[i],
                          diff_norm_g[i], lambda_init, tq=256)
        w_all = jnp.concatenate([_block_diag(lru_wa[i, 0]), _block_diag(lru_wx[i, 0]),
                                 _block_diag(lru_wa[i, 1]), _block_diag(lru_wx[i, 1])], axis=1).astype(BF16)
        bias_all = jnp.concatenate([lru_ba[i, 0], lru_bx[i, 0], lru_ba[i, 1], lru_bx[i, 1]]).reshape(1, -1)
        rec = _bi_rglru(u, gate, conv_w[i], conv_b[i], w_all, bias_all, lru_lambda[i], lru_norm_g[i], tc=256)
        h1, xn2, idx_t, gate_t = _mix_route(
            attn.reshape(T, -1), rec.reshape(T, -1), h.reshape(T, D), w_out[i].astype(BF16), norm_ffn_g[i],
            peer_wq[i].astype(BF16), _key_matrix(peer_keys1[i], 0).astype(BF16),
            _key_matrix(peer_keys2[i], 1).astype(BF16), tm=256)
        peer_out = _peer(idx_t.T, gate_t.T, xn2, peer_u[i], peer_v[i], tt=8)
        h = _ple_out(h1, peer_out, p[i].reshape(T, -1), norm_ple_g[i], ple_w_gate[i].astype(BF16),
                     ple_w_proj[i].astype(BF16), final_norm_g, final=(i == depth - 1), tm=512)
        h = h.reshape(B, S, D)
    return h
```

```python
import functools
import math

import jax
import jax.numpy as jnp
from jax import lax
from jax.experimental import pallas as pl
from jax.experimental.pallas import tpu as pltpu

F32 = jnp.float32
BF16 = jnp.bfloat16
I32 = jnp.int32

EPS = 1e-6
DIFF_HEAD_DIM = 64
DIFF_V_DIM = 128
N_DIFF_HEADS = 4
ROPE_DIM = 16
ROPE_THETA = 500000.0
LRU_WIDTH = 512
LRU_C = 8.0
N_KEYS = 128
PEER_HEADS = 8
PEER_TOPK = 16
HALF_KEY = 64
LANES = 128
SUBLANES = 8
VMEM_LIMIT = 56 * 1024 * 1024


def _rms(x, g):
    return x * lax.rsqrt(jnp.mean(x * x, axis=-1, keepdims=True) + EPS) * g


def _gelu_tanh(x):
    return 0.5 * x * (1.0 + jnp.tanh(math.sqrt(2.0 / math.pi) * (x + 0.044715 * (x * x * x))))


def _sigmoid(x):
    return 1.0 / (1.0 + jnp.exp(-x))


def _inproj_kernel(x_ref, pos_ref, g_ref, w_ref, q_ref, k_ref, v_ref, u_ref, gate_ref):
    x = x_ref[0]
    xn = _rms(x, g_ref[...]).astype(BF16)
    pos = pos_ref[0].astype(F32)
    lane = lax.broadcasted_iota(I32, (1, LANES), 1)
    p = lane & (DIFF_HEAD_DIM - 1)
    freq = (p & (ROPE_DIM // 2 - 1)).astype(F32)
    inv_freq = jnp.exp(freq * (-2.0 / ROPE_DIM * math.log(ROPE_THETA)))
    ang = pos * inv_freq
    cs = jnp.cos(ang)
    sn = jnp.sin(ang)
    half = ROPE_DIM // 2
    c_mul = jnp.where(p < ROPE_DIM, cs, 1.0)
    s_up = jnp.where(p < half, -sn, 0.0)
    s_dn = jnp.where((p >= half) & (p < ROPE_DIM), sn, 0.0)

    def rope(t):
        return t * c_mul + pltpu.roll(t, LANES - half, 1) * s_up + pltpu.roll(t, half, 1) * s_dn

    nq = q_ref.shape[-1]
    pq = jnp.dot(xn, w_ref[:, 0:nq], preferred_element_type=F32)
    pk = jnp.dot(xn, w_ref[:, nq:2 * nq], preferred_element_type=F32)
    scale = DIFF_HEAD_DIM ** -0.5
    for j in range(nq // LANES):
        sl = slice(j * LANES, (j + 1) * LANES)
        q_ref[0, :, sl] = (rope(pq[:, sl]) * scale).astype(BF16)
        k_ref[0, :, sl] = rope(pk[:, sl]).astype(BF16)
    v_ref[0] = jnp.dot(xn, w_ref[:, 2 * nq:3 * nq], preferred_element_type=F32).astype(BF16)
    u_ref[0] = jnp.dot(xn, w_ref[:, 3 * nq:3 * nq + LRU_WIDTH], preferred_element_type=F32)
    gate_ref[0] = jnp.dot(xn, w_ref[:, 3 * nq + LRU_WIDTH:3 * nq + 2 * LRU_WIDTH],
                          preferred_element_type=F32).astype(BF16)


def _in_proj(x, positions, g, w_in_bf, tm):
    B, S, D = x.shape
    nq = N_DIFF_HEADS * 2 * DIFF_HEAD_DIM
    ncols = w_in_bf.shape[1]
    row = lambda b, i: (b, i, 0)
    return pl.pallas_call(
        _inproj_kernel,
        grid=(B, S // tm),
        in_specs=[
            pl.BlockSpec((1, tm, D), row),
            pl.BlockSpec((1, tm, 1), row),
            pl.BlockSpec((1, D), lambda b, i: (0, 0)),
            pl.BlockSpec((D, ncols), lambda b, i: (0, 0)),
        ],
        out_specs=[
            pl.BlockSpec((1, tm, nq), row),
            pl.BlockSpec((1, tm, nq), row),
            pl.BlockSpec((1, tm, nq), row),
            pl.BlockSpec((1, tm, LRU_WIDTH), row),
            pl.BlockSpec((1, tm, LRU_WIDTH), row),
        ],
        out_shape=[
            jax.ShapeDtypeStruct((B, S, nq), BF16),
            jax.ShapeDtypeStruct((B, S, nq), BF16),
            jax.ShapeDtypeStruct((B, S, nq), BF16),
            jax.ShapeDtypeStruct((B, S, LRU_WIDTH), F32),
            jax.ShapeDtypeStruct((B, S, LRU_WIDTH), BF16),
        ],
        compiler_params=pltpu.CompilerParams(
            dimension_semantics=("parallel", "parallel"), vmem_limit_bytes=VMEM_LIMIT),
        name="in_proj",
    )(x, positions.reshape(B, S, 1), g.reshape(1, D), w_in_bf)


def _attn_kernel(lq1_ref, lk1_ref, lq2_ref, lk2_ref, g_ref, q_ref, k_ref, v_ref, o_ref, *, lambda_init):
    lam = (jnp.exp(jnp.sum(lq1_ref[...] * lk1_ref[...], axis=-1, keepdims=True))
           - jnp.exp(jnp.sum(lq2_ref[...] * lk2_ref[...], axis=-1, keepdims=True))
           + lambda_init)
    q = q_ref[0]
    k = k_ref[0]
    v = v_ref[0]
    lane = lax.broadcasted_iota(I32, q.shape, 1)
    zero = jnp.zeros_like(q)
    q0 = jnp.where(lane < DIFF_HEAD_DIM, q, zero)
    q1 = jnp.where(lane >= DIFF_HEAD_DIM, q, zero)
    nt = (((1,), (1,)), ((), ()))
    s0 = lax.dot_general(q0, k, nt, preferred_element_type=F32)
    s1 = lax.dot_general(q1, k, nt, preferred_element_type=F32)
    p0 = jnp.exp(s0 - jnp.max(s0, axis=-1, keepdims=True))
    p1 = jnp.exp(s1 - jnp.max(s1, axis=-1, keepdims=True))
    r0 = 1.0 / jnp.sum(p0, axis=-1, keepdims=True)
    r1 = lam / jnp.sum(p1, axis=-1, keepdims=True)
    w = (p0 * r0 - p1 * r1).astype(BF16)
    o = jnp.dot(w, v, preferred_element_type=F32)
    o_ref[0] = (_rms(o, g_ref[...]) * (1.0 - lambda_init)).astype(BF16)


def _diff_attn(q, k, v, lq1, lk1, lq2, lk2, g, lambda_init, tq):
    B, S, W = q.shape
    H = W // DIFF_V_DIM
    vec = lambda b, h, i: (0, 0)
    return pl.pallas_call(
        functools.partial(_attn_kernel, lambda_init=lambda_init),
        grid=(B, H, S // tq),
        in_specs=[
            pl.BlockSpec((1, DIFF_HEAD_DIM), vec),
            pl.BlockSpec((1, DIFF_HEAD_DIM), vec),
            pl.BlockSpec((1, DIFF_HEAD_DIM), vec),
            pl.BlockSpec((1, DIFF_HEAD_DIM), vec),
            pl.BlockSpec((1, DIFF_V_DIM), vec),
            pl.BlockSpec((1, tq, DIFF_V_DIM), lambda b, h, i: (b, i, h)),
            pl.BlockSpec((1, S, DIFF_V_DIM), lambda b, h, i: (b, 0, h)),
            pl.BlockSpec((1, S, DIFF_V_DIM), lambda b, h, i: (b, 0, h)),
        ],
        out_specs=pl.BlockSpec((1, tq, DIFF_V_DIM), lambda b, h, i: (b, i, h)),
        out_shape=jax.ShapeDtypeStruct((B, S, W), BF16),
        compiler_params=pltpu.CompilerParams(
            dimension_semantics=("parallel", "parallel", "parallel"), vmem_limit_bytes=VMEM_LIMIT),
        name="diff_attn",
    )(lq1.reshape(1, -1), lk1.reshape(1, -1), lq2.reshape(1, -1), lk2.reshape(1, -1),
      g.reshape(1, -1), q, k, v)


def _lru_kernel(u_ref, gate_ref, cw_ref, cb_ref, w_ref, bias_ref, lam_ref, g_ref, out_ref, hf_ref, *, tc):
    S = u_ref.shape[1]
    C = u_ref.shape[2]
    nc = S // tc
    halo = SUBLANES
    neg_lam = -lam_ref[...]
    sp = jnp.maximum(neg_lam, 0.0) + jnp.log(1.0 + jnp.exp(-jnp.abs(neg_lam)))
    row = lax.broadcasted_iota(I32, (tc, 1), 0)
    conv_taps = cw_ref.shape[0]
    conv_left = 2

    def gates(c, d):
        r0 = pl.multiple_of(c * tc, tc)
        x = u_ref[0, pl.ds(r0, tc), :]
        prev = u_ref[0, pl.ds(pl.multiple_of(jnp.maximum(r0 - halo, 0), halo), halo), :]
        nxt = u_ref[0, pl.ds(pl.multiple_of(jnp.minimum(r0 + tc, S - halo), halo), halo), :]
        prev = jnp.where(c > 0, prev, 0.0)
        nxt = jnp.where(c < nc - 1, nxt, 0.0)
        win = jnp.concatenate([prev, x, nxt], axis=0)
        uc = cb_ref[...]
        for j in range(conv_taps):
            o = halo - conv_left + j
            uc = uc + cw_ref[j:j + 1, :] * win[o:o + tc, :]
        pre = (jnp.dot(uc.astype(BF16), w_ref[:, d * 2 * C:(d + 1) * 2 * C], preferred_element_type=F32)
               + bias_ref[:, d * 2 * C:(d + 1) * 2 * C])
        r = _sigmoid(pre[:, :C])
        i = _sigmoid(pre[:, C:])
        log_a = -LRU_C * r * sp[d:d + 1, :]
        a = jnp.exp(log_a)
        th = jnp.tanh(log_a)
        mult = jnp.sqrt(-2.0 * th / (1.0 - th))
        return a, mult * (i * uc)

    def scan(a, b, reverse):
        d = 1
        while d < tc:
            if reverse:
                a_s = pltpu.roll(a, tc - d, 0)
                b_s = pltpu.roll(b, tc - d, 0)
                m = row < tc - d
            else:
                a_s = pltpu.roll(a, d, 0)
                b_s = pltpu.roll(b, d, 0)
                m = row >= d
            a_s = jnp.where(m, a_s, 1.0)
            b_s = jnp.where(m, b_s, 0.0)
            b = a * b_s + b
            a = a * a_s
            d *= 2
        return a, b

    def fwd_body(c, h0):
        a, b = gates(c, 0)
        a, b = scan(a, b, False)
        h = a * h0 + b
        hf_ref[pl.ds(pl.multiple_of(c * tc, tc), tc), :] = h
        return h[tc - 1:tc, :]

    lax.fori_loop(0, nc, fwd_body, jnp.zeros((1, C), F32))

    def bwd_body(j, h0):
        c = nc - 1 - j
        r0 = pl.multiple_of(c * tc, tc)
        a, b = gates(c, 1)
        a, b = scan(a, b, True)
        h = a * h0 + b
        y = (hf_ref[pl.ds(r0, tc), :] + h) * _gelu_tanh(gate_ref[0, pl.ds(r0, tc), :].astype(F32))
        out_ref[0, pl.ds(r0, tc), :] = _rms(y, g_ref[...]).astype(BF16)
        return h[0:1, :]

    lax.fori_loop(0, nc, bwd_body, jnp.zeros((1, C), F32))


def _bi_rglru(u, gate, conv_w, conv_b, w_all_bf, bias_all, lru_lambda, g, tc):
    B, S, C = u.shape
    full = lambda b: (0, 0)
    return pl.pallas_call(
        functools.partial(_lru_kernel, tc=tc),
        grid=(B,),
        in_specs=[
            pl.BlockSpec((1, S, C), lambda b: (b, 0, 0)),
            pl.BlockSpec((1, S, C), lambda b: (b, 0, 0)),
            pl.BlockSpec(conv_w.shape, full),
            pl.BlockSpec((1, C), full),
            pl.BlockSpec(w_all_bf.shape, full),
            pl.BlockSpec(bias_all.shape, full),
            pl.BlockSpec(lru_lambda.shape, full),
            pl.BlockSpec((1, C), full),
        ],
        out_specs=pl.BlockSpec((1, S, C), lambda b: (b, 0, 0)),
        out_shape=jax.ShapeDtypeStruct((B, S, C), BF16),
        scratch_shapes=[pltpu.VMEM((S, C), F32)],
        compiler_params=pltpu.CompilerParams(
            dimension_semantics=("parallel",), vmem_limit_bytes=VMEM_LIMIT),
        name="bi_rglru",
    )(u, gate, conv_w, conv_b.reshape(1, C), w_all_bf, bias_all, lru_lambda, g.reshape(1, C))


def _topk_rows(s, k):
    n = s.shape[0]
    rid = lax.broadcasted_iota(I32, s.shape, 0)
    vals, ids = [], []
    for _ in range(k):
        m = jnp.max(s, axis=0, keepdims=True)
        sel = jnp.min(jnp.where(s == m, rid, n), axis=0, keepdims=True)
        vals.append(m)
        ids.append(sel)
        s = jnp.where(rid == sel, -jnp.inf, s)
    return jnp.concatenate(vals, axis=0), jnp.concatenate(ids, axis=0)


def _route_kernel(attn_ref, rec_ref, x_ref, wo_ref, g_ref, wq_ref, k1_ref, k2_ref,
                  h_ref, xn_ref, idx_ref, gate_ref):
    aw = attn_ref.shape[1]
    h = (x_ref[...]
         + jnp.dot(attn_ref[...], wo_ref[0:aw, :], preferred_element_type=F32)
         + jnp.dot(rec_ref[...], wo_ref[aw:, :], preferred_element_type=F32))
    h_ref[...] = h
    xn = _rms(h, g_ref[...])
    xn_ref[...] = xn
    q = jnp.dot(xn.astype(BF16), wq_ref[...], preferred_element_type=F32).astype(BF16)
    nt = (((1,), (1,)), ((), ()))
    s1 = lax.dot_general(k1_ref[...], q, nt, preferred_element_type=F32)
    s2 = lax.dot_general(k2_ref[...], q, nt, preferred_element_type=F32)
    k = PEER_TOPK
    idx_rows, gate_rows = [], []
    for hd in range(PEER_HEADS):
        v1, i1 = _topk_rows(s1[hd * N_KEYS:(hd + 1) * N_KEYS, :], k)
        v2, i2 = _topk_rows(s2[hd * N_KEYS:(hd + 1) * N_KEYS, :], k)
        cand = jnp.concatenate([v1[i:i + 1, :] + v2 for i in range(k)], axis=0)
        cidx = jnp.concatenate([i1[i:i + 1, :] * N_KEYS + i2 for i in range(k)], axis=0)
        pos = lax.broadcasted_iota(I32, cand.shape, 0)
        sc, ids = [], []
        for _ in range(k):
            m = jnp.max(cand, axis=0, keepdims=True)
            sel = jnp.min(jnp.where(cand == m, pos, k * k), axis=0, keepdims=True)
            hit = pos == sel
            sc.append(m)
            ids.append(jnp.max(jnp.where(hit, cidx, -1), axis=0, keepdims=True))
            cand = jnp.where(hit, -jnp.inf, cand)
        sc = jnp.concatenate(sc, axis=0)
        e = jnp.exp(sc - sc[0:1, :])
        gate_rows.append(e / jnp.sum(e, axis=0, keepdims=True))
        idx_rows.append(jnp.concatenate(ids, axis=0))
    idx_ref[...] = jnp.concatenate(idx_rows, axis=0)
    gate_ref[...] = jnp.concatenate(gate_rows, axis=0)


def _mix_route(attn, rec, x2, wo_bf, g, wq_bf, k1t_bf, k2t_bf, tm):
    T, D = x2.shape
    NK = PEER_HEADS * PEER_TOPK
    row = lambda i: (i, 0)
    full = lambda i: (0, 0)
    return pl.pallas_call(
        _route_kernel,
        grid=(T // tm,),
        in_specs=[
            pl.BlockSpec((tm, attn.shape[1]), row),
            pl.BlockSpec((tm, rec.shape[1]), row),
            pl.BlockSpec((tm, D), row),
            pl.BlockSpec(wo_bf.shape, full),
            pl.BlockSpec((1, D), full),
            pl.BlockSpec(wq_bf.shape, full),
            pl.BlockSpec(k1t_bf.shape, full),
            pl.BlockSpec(k2t_bf.shape, full),
        ],
        out_specs=[
            pl.BlockSpec((tm, D), row),
            pl.BlockSpec((tm, D), row),
            pl.BlockSpec((NK, tm), lambda i: (0, i)),
            pl.BlockSpec((NK, tm), lambda i: (0, i)),
        ],
        out_shape=[
            jax.ShapeDtypeStruct((T, D), F32),
            jax.ShapeDtypeStruct((T, D), F32),
            jax.ShapeDtypeStruct((NK, T), I32),
            jax.ShapeDtypeStruct((NK, T), F32),
        ],
        compiler_params=pltpu.CompilerParams(
            dimension_semantics=("parallel",), vmem_limit_bytes=VMEM_LIMIT),
        name="mix_route",
    )(attn, rec, x2, wo_bf, g.reshape(1, D), wq_bf, k1t_bf, k2t_bf)


ROW_CHUNKS = 8
TOKEN_UNROLL = 2


def _peer_kernel(idx_ref, idx_next_ref, xn_ref, gate_ref, uv_hbm, o_ref, buf_a, buf_b, lg_ref, act_ref, sems,
                 *, tt, nk):
    g = pl.program_id(0)
    n = pl.num_programs(0)
    nt = (((1,), (1,)), ((), ()))
    ch = ROW_CHUNKS

    def row_copy(ids_ref, base, buf, sem, t, k):
        e = ids_ref[0, 0, base + t * nk + k]
        return pltpu.make_async_copy(uv_hbm.at[e], buf.at[t, :, pl.ds(k, 1), :], sem)

    def wait_all(buf, other, sem):
        pltpu.make_async_copy(other, buf, sem).wait()

    @pl.when(g == 0)
    def _():
        def body(t, carry):
            for k in range(nk):
                row_copy(idx_ref, 0, buf_a, sems.at[0], t, k).start()
            return carry
        lax.fori_loop(0, tt, body, 0)

    def half(buf, row0, issue):
        def phase_u(tb, carry):
            for j in range(TOKEN_UNROLL):
                t = tb * TOKEN_UNROLL + j
                issue(t, 0, nk // 2)
                xt = xn_ref[row0 + t]
                acc = buf[t, 0] * xt[0:1, :]
                for s in range(1, ch):
                    acc = acc + buf[t, s] * xt[s:s + 1, :]
                hi = acc.astype(BF16)
                lo = (acc - hi.astype(F32)).astype(BF16)
                ones = jnp.ones((SUBLANES, LANES), BF16)
                red = (lax.dot_general(ones, hi, nt, preferred_element_type=F32)
                       + lax.dot_general(ones, lo, nt, preferred_element_type=F32))
                lg_ref[t] = red[0:1, :]
            return carry
        lax.fori_loop(0, tt // TOKEN_UNROLL, phase_u, 0)
        act_ref[...] = _gelu_tanh(lg_ref[...]) * gate_ref[row0:row0 + tt]

        def phase_v(tb, carry):
            for j in range(TOKEN_UNROLL):
                t = tb * TOKEN_UNROLL + j
                issue(t, nk // 2, nk)
                act = act_ref[t].astype(BF16)
                cols = [jnp.dot(act, buf[t, ch + s].astype(BF16), preferred_element_type=F32)
                        for s in range(ch)]
                o_ref[row0 + t] = jnp.concatenate(cols, axis=0)
            return carry
        lax.fori_loop(0, tt // TOKEN_UNROLL, phase_v, 0)

    def issue_b(t, k0, k1):
        for k in range(k0, k1):
            row_copy(idx_ref, tt * nk, buf_b, sems.at[1], t, k).start()

    def issue_a_next(t, k0, k1):
        for k in range(k0, k1):
            row_copy(idx_next_ref, 0, buf_a, sems.at[0], t, k).start()

    wait_all(buf_a, buf_b, sems.at[0])
    half(buf_a, 0, issue_b)
    wait_all(buf_b, buf_a, sems.at[1])
    half(buf_b, tt, issue_a_next)

    @pl.when(g == n - 1)
    def _():
        wait_all(buf_a, buf_b, sems.at[0])


def _peer(idx, gate, xn3, uv4, tt):
    T, ch, _ = xn3.shape
    nk = idx.shape[1]
    n = T // (2 * tt)
    idx3 = idx.reshape(n, 1, 2 * tt * nk)
    tok = lambda i: (i, 0, 0)
    return pl.pallas_call(
        functools.partial(_peer_kernel, tt=tt, nk=nk),
        grid=(n,),
        in_specs=[
            pl.BlockSpec((1, 1, 2 * tt * nk), tok, memory_space=pltpu.SMEM),
            pl.BlockSpec((1, 1, 2 * tt * nk), lambda i: (jnp.minimum(i + 1, n - 1), 0, 0), memory_space=pltpu.SMEM),
            pl.BlockSpec((2 * tt, ch, LANES), tok),
            pl.BlockSpec((2 * tt, 1, nk), tok),
            pl.BlockSpec(memory_space=pl.ANY),
        ],
        out_specs=pl.BlockSpec((2 * tt, ch, LANES), tok),
        out_shape=jax.ShapeDtypeStruct((T, ch, LANES), F32),
        scratch_shapes=[
            pltpu.VMEM((tt, 2 * ch, nk, LANES), F32),
            pltpu.VMEM((tt, 2 * ch, nk, LANES), F32),
            pltpu.VMEM((tt, 1, nk), F32),
            pltpu.VMEM((tt, 1, nk), F32),
            pltpu.SemaphoreType.DMA((2,)),
        ],
        compiler_params=pltpu.CompilerParams(
            dimension_semantics=("arbitrary",), vmem_limit_bytes=VMEM_LIMIT),
        name="peer",
    )(idx3, idx3, xn3, gate.reshape(T, 1, nk), uv4)


def _ple_kernel(h_ref, po_ref, p_ref, g3_ref, wg_ref, wp_ref, gf_ref, o_ref, *, final):
    h = h_ref[...] + po_ref[...]
    xn = _rms(h, g3_ref[...]).astype(BF16)
    gate = _sigmoid(jnp.dot(xn, wg_ref[...], preferred_element_type=F32))
    proj = jnp.dot(p_ref[...].astype(BF16), wp_ref[...], preferred_element_type=F32)
    h = h + gate * proj
    o_ref[...] = _rms(h, gf_ref[...]) if final else h


def _ple_out(h1, peer_out, p2, g3, wg_bf, wp_bf, gf, final, tm):
    T, D = h1.shape
    row = lambda i: (i, 0)
    full = lambda i: (0, 0)
    return pl.pallas_call(
        functools.partial(_ple_kernel, final=final),
        grid=(T // tm,),
        in_specs=[
            pl.BlockSpec((tm, D), row),
            pl.BlockSpec((tm, D), row),
            pl.BlockSpec((tm, p2.shape[1]), row),
            pl.BlockSpec((1, D), full),
            pl.BlockSpec(wg_bf.shape, full),
            pl.BlockSpec(wp_bf.shape, full),
            pl.BlockSpec((1, D), full),
        ],
        out_specs=pl.BlockSpec((tm, D), row),
        out_shape=jax.ShapeDtypeStruct((T, D), F32),
        compiler_params=pltpu.CompilerParams(
            dimension_semantics=("parallel",), vmem_limit_bytes=VMEM_LIMIT),
        name="ple_out",
    )(h1, peer_out, p2, g3.reshape(1, D), wg_bf, wp_bf, gf.reshape(1, D))


def _block_diag(w):
    nb, bw, _ = w.shape
    eye = jnp.eye(nb, dtype=w.dtype)
    return (eye[:, None, :, None] * w[:, :, None, :]).reshape(nb * bw, nb * bw)


def _key_matrix(keys, half):
    z = jnp.zeros_like(keys)
    blk = jnp.concatenate([keys, z] if half == 0 else [z, keys], axis=1)
    return jnp.kron(jnp.eye(PEER_HEADS, dtype=keys.dtype), blk)


def kernel(x, p, positions, norm_mix_g, w_in, lambda_q1, lambda_k1, lambda_q2, lambda_k2, diff_norm_g, conv_w, conv_b, lru_wa, lru_ba, lru_wx, lru_bx, lru_lambda, lru_norm_g, w_out, norm_ffn_g, peer_wq, peer_keys1, peer_keys2, peer_u, peer_v, norm_ple_g, ple_w_gate, ple_w_proj, final_norm_g):
    B, S, D = x.shape
    T = B * S
    depth = w_in.shape[0]
    h = x
    for i in range(depth):
        lambda_init = 0.8 - 0.6 * math.exp(-0.3 * i)
        q, k, v, u, gate = _in_proj(h, positions, norm_mix_g[i], w_in[i].astype(BF16), tm=512)
        attn = _diff_attn(q, k, v, lambda_q1[i], lambda_k1[i], lambda_q2[i], lambda_k2[i],
                          diff_norm_g[i], lambda_init, tq=256)
        w_all = jnp.concatenate([_block_diag(lru_wa[i, 0]), _block_diag(lru_wx[i, 0]),
                                 _block_diag(lru_wa[i, 1]), _block_diag(lru_wx[i, 1])], axis=1).astype(BF16)
        bias_all = jnp.concatenate([lru_ba[i, 0], lru_bx[i, 0], lru_ba[i, 1], lru_bx[i, 1]]).reshape(1, -1)
        rec = _bi_rglru(u, gate, conv_w[i], conv_b[i], w_all, bias_all, lru_lambda[i], lru_norm_g[i], tc=256)
        h1, xn2, idx_t, gate_t = _mix_route(
            attn.reshape(T, -1), rec.reshape(T, -1), h.reshape(T, D), w_out[i].astype(BF16), norm_ffn_g[i],
            peer_wq[i].astype(BF16), _key_matrix(peer_keys1[i], 0).astype(BF16),
            _key_matrix(peer_keys2[i], 1).astype(BF16), tm=256)
        n_exp = peer_u.shape[1]
        uv4 = jnp.concatenate([peer_u[i].reshape(n_exp, ROW_CHUNKS, 1, LANES),
                               peer_v[i].reshape(n_exp, ROW_CHUNKS, 1, LANES)], axis=1)
        peer_out = _peer(idx_t.T, gate_t.T, xn2.reshape(T, ROW_CHUNKS, LANES), uv4, tt=8).reshape(T, D)
        h = _ple_out(h1, peer_out, p[i].reshape(T, -1), norm_ple_g[i], ple_w_gate[i].astype(BF16),
                     ple_w_proj[i].astype(BF16), final_norm_g, final=(i == depth - 1), tm=512)
        h = h.reshape(B, S, D)
    return h
```

```python
import functools
import math

import jax
import jax.numpy as jnp
from jax import lax
from jax.experimental import pallas as pl
from jax.experimental.pallas import tpu as pltpu

F32 = jnp.float32
BF16 = jnp.bfloat16
I32 = jnp.int32

EPS = 1e-6
DIFF_HEAD_DIM = 64
DIFF_V_DIM = 128
N_DIFF_HEADS = 4
ROPE_DIM = 16
ROPE_THETA = 500000.0
LRU_WIDTH = 512
LRU_C = 8.0
N_KEYS = 128
PEER_HEADS = 8
PEER_TOPK = 16
HALF_KEY = 64
LANES = 128
SUBLANES = 8
VMEM_LIMIT = 56 * 1024 * 1024


def _rms(x, g):
    return x * lax.rsqrt(jnp.mean(x * x, axis=-1, keepdims=True) + EPS) * g


def _gelu_tanh(x):
    return 0.5 * x * (1.0 + jnp.tanh(math.sqrt(2.0 / math.pi) * (x + 0.044715 * (x * x * x))))


def _sigmoid(x):
    return 1.0 / (1.0 + jnp.exp(-x))


def _inproj_kernel(x_ref, pos_ref, g_ref, w_ref, q_ref, k_ref, v_ref, u_ref, gate_ref):
    x = x_ref[0]
    xn = _rms(x, g_ref[...]).astype(BF16)
    pos = pos_ref[0].astype(F32)
    lane = lax.broadcasted_iota(I32, (1, LANES), 1)
    p = lane & (DIFF_HEAD_DIM - 1)
    freq = (p & (ROPE_DIM // 2 - 1)).astype(F32)
    inv_freq = jnp.exp(freq * (-2.0 / ROPE_DIM * math.log(ROPE_THETA)))
    ang = pos * inv_freq
    cs = jnp.cos(ang)
    sn = jnp.sin(ang)
    half = ROPE_DIM // 2
    c_mul = jnp.where(p < ROPE_DIM, cs, 1.0)
    s_up = jnp.where(p < half, -sn, 0.0)
    s_dn = jnp.where((p >= half) & (p < ROPE_DIM), sn, 0.0)

    def rope(t):
        return t * c_mul + pltpu.roll(t, LANES - half, 1) * s_up + pltpu.roll(t, half, 1) * s_dn

    nq = q_ref.shape[-1]
    pq = jnp.dot(xn, w_ref[:, 0:nq], preferred_element_type=F32)
    pk = jnp.dot(xn, w_ref[:, nq:2 * nq], preferred_element_type=F32)
    scale = DIFF_HEAD_DIM ** -0.5
    for j in range(nq // LANES):
        sl = slice(j * LANES, (j + 1) * LANES)
        q_ref[0, :, sl] = (rope(pq[:, sl]) * scale).astype(BF16)
        k_ref[0, :, sl] = rope(pk[:, sl]).astype(BF16)
    v_ref[0] = jnp.dot(xn, w_ref[:, 2 * nq:3 * nq], preferred_element_type=F32).astype(BF16)
    u_ref[0] = jnp.dot(xn, w_ref[:, 3 * nq:3 * nq + LRU_WIDTH], preferred_element_type=F32)
    gate_ref[0] = jnp.dot(xn, w_ref[:, 3 * nq + LRU_WIDTH:3 * nq + 2 * LRU_WIDTH],
                          preferred_element_type=F32).astype(BF16)


def _in_proj(x, positions, g, w_in_bf, tm):
    B, S, D = x.shape
    nq = N_DIFF_HEADS * 2 * DIFF_HEAD_DIM
    ncols = w_in_bf.shape[1]
    row = lambda b, i: (b, i, 0)
    return pl.pallas_call(
        _inproj_kernel,
        grid=(B, S // tm),
        in_specs=[
            pl.BlockSpec((1, tm, D), row),
            pl.BlockSpec((1, tm, 1), row),
            pl.BlockSpec((1, D), lambda b, i: (0, 0)),
            pl.BlockSpec((D, ncols), lambda b, i: (0, 0)),
        ],
        out_specs=[
            pl.BlockSpec((1, tm, nq), row),
            pl.BlockSpec((1, tm, nq), row),
            pl.BlockSpec((1, tm, nq), row),
            pl.BlockSpec((1, tm, LRU_WIDTH), row),
            pl.BlockSpec((1, tm, LRU_WIDTH), row),
        ],
        out_shape=[
            jax.ShapeDtypeStruct((B, S, nq), BF16),
            jax.ShapeDtypeStruct((B, S, nq), BF16),
            jax.ShapeDtypeStruct((B, S, nq), BF16),
            jax.ShapeDtypeStruct((B, S, LRU_WIDTH), F32),
            jax.ShapeDtypeStruct((B, S, LRU_WIDTH), BF16),
        ],
        compiler_params=pltpu.CompilerParams(
            dimension_semantics=("parallel", "parallel"), vmem_limit_bytes=VMEM_LIMIT),
        name="in_proj",
    )(x, positions.reshape(B, S, 1), g.reshape(1, D), w_in_bf)


def _attn_kernel(lq1_ref, lk1_ref, lq2_ref, lk2_ref, g_ref, q_ref, k_ref, v_ref, o_ref, *, lambda_init):
    lam = (jnp.exp(jnp.sum(lq1_ref[...] * lk1_ref[...], axis=-1, keepdims=True))
           - jnp.exp(jnp.sum(lq2_ref[...] * lk2_ref[...], axis=-1, keepdims=True))
           + lambda_init)
    q = q_ref[0]
    k = k_ref[0]
    v = v_ref[0]
    lane = lax.broadcasted_iota(I32, q.shape, 1)
    zero = jnp.zeros_like(q)
    q0 = jnp.where(lane < DIFF_HEAD_DIM, q, zero)
    q1 = jnp.where(lane >= DIFF_HEAD_DIM, q, zero)
    nt = (((1,), (1,)), ((), ()))
    s0 = lax.dot_general(q0, k, nt, preferred_element_type=F32)
    s1 = lax.dot_general(q1, k, nt, preferred_element_type=F32)
    p0 = jnp.exp(s0 - jnp.max(s0, axis=-1, keepdims=True))
    p1 = jnp.exp(s1 - jnp.max(s1, axis=-1, keepdims=True))
    r0 = 1.0 / jnp.sum(p0, axis=-1, keepdims=True)
    r1 = lam / jnp.sum(p1, axis=-1, keepdims=True)
    w = (p0 * r0 - p1 * r1).astype(BF16)
    o = jnp.dot(w, v, preferred_element_type=F32)
    o_ref[0] = (_rms(o, g_ref[...]) * (1.0 - lambda_init)).astype(BF16)


def _diff_attn(q, k, v, lq1, lk1, lq2, lk2, g, lambda_init, tq):
    B, S, W = q.shape
    H = W // DIFF_V_DIM
    vec = lambda b, h, i: (0, 0)
    return pl.pallas_call(
        functools.partial(_attn_kernel, lambda_init=lambda_init),
        grid=(B, H, S // tq),
        in_specs=[
            pl.BlockSpec((1, DIFF_HEAD_DIM), vec),
            pl.BlockSpec((1, DIFF_HEAD_DIM), vec),
            pl.BlockSpec((1, DIFF_HEAD_DIM), vec),
            pl.BlockSpec((1, DIFF_HEAD_DIM), vec),
            pl.BlockSpec((1, DIFF_V_DIM), vec),
            pl.BlockSpec((1, tq, DIFF_V_DIM), lambda b, h, i: (b, i, h)),
            pl.BlockSpec((1, S, DIFF_V_DIM), lambda b, h, i: (b, 0, h)),
            pl.BlockSpec((1, S, DIFF_V_DIM), lambda b, h, i: (b, 0, h)),
        ],
        out_specs=pl.BlockSpec((1, tq, DIFF_V_DIM), lambda b, h, i: (b, i, h)),
        out_shape=jax.ShapeDtypeStruct((B, S, W), BF16),
        compiler_params=pltpu.CompilerParams(
            dimension_semantics=("parallel", "parallel", "parallel"), vmem_limit_bytes=VMEM_LIMIT),
        name="diff_attn",
    )(lq1.reshape(1, -1), lk1.reshape(1, -1), lq2.reshape(1, -1), lk2.reshape(1, -1),
      g.reshape(1, -1), q, k, v)


def _lru_kernel(u_ref, gate_ref, cw_ref, cb_ref, w_ref, bias_ref, lam_ref, g_ref, out_ref, hf_ref, *, tc):
    S = u_ref.shape[1]
    C = u_ref.shape[2]
    nc = S // tc
    halo = SUBLANES
    neg_lam = -lam_ref[...]
    sp = jnp.maximum(neg_lam, 0.0) + jnp.log(1.0 + jnp.exp(-jnp.abs(neg_lam)))
    row = lax.broadcasted_iota(I32, (tc, 1), 0)
    conv_taps = cw_ref.shape[0]
    conv_left = 2

    def gates(c, d):
        r0 = pl.multiple_of(c * tc, tc)
        x = u_ref[0, pl.ds(r0, tc), :]
        prev = u_ref[0, pl.ds(pl.multiple_of(jnp.maximum(r0 - halo, 0), halo), halo), :]
        nxt = u_ref[0, pl.ds(pl.multiple_of(jnp.minimum(r0 + tc, S - halo), halo), halo), :]
        prev = jnp.where(c > 0, prev, 0.0)
        nxt = jnp.where(c < nc - 1, nxt, 0.0)
        win = jnp.concatenate([prev, x, nxt], axis=0)
        uc = cb_ref[...]
        for j in range(conv_taps):
            o = halo - conv_left + j
            uc = uc + cw_ref[j:j + 1, :] * win[o:o + tc, :]
        pre = (jnp.dot(uc.astype(BF16), w_ref[:, d * 2 * C:(d + 1) * 2 * C], preferred_element_type=F32)
               + bias_ref[:, d * 2 * C:(d + 1) * 2 * C])
        r = _sigmoid(pre[:, :C])
        i = _sigmoid(pre[:, C:])
        log_a = -LRU_C * r * sp[d:d + 1, :]
        a = jnp.exp(log_a)
        th = jnp.tanh(log_a)
        mult = jnp.sqrt(-2.0 * th / (1.0 - th))
        return a, mult * (i * uc)

    def scan(a, b, reverse):
        d = 1
        while d < tc:
            if reverse:
                a_s = pltpu.roll(a, tc - d, 0)
                b_s = pltpu.roll(b, tc - d, 0)
                m = row < tc - d
            else:
                a_s = pltpu.roll(a, d, 0)
                b_s = pltpu.roll(b, d, 0)
                m = row >= d
            a_s = jnp.where(m, a_s, 1.0)
            b_s = jnp.where(m, b_s, 0.0)
            b = a * b_s + b
            a = a * a_s
            d *= 2
        return a, b

    def fwd_body(c, h0):
        a, b = gates(c, 0)
        a, b = scan(a, b, False)
        h = a * h0 + b
        hf_ref[pl.ds(pl.multiple_of(c * tc, tc), tc), :] = h
        return h[tc - 1:tc, :]

    lax.fori_loop(0, nc, fwd_body, jnp.zeros((1, C), F32))

    def bwd_body(j, h0):
        c = nc - 1 - j
        r0 = pl.multiple_of(c * tc, tc)
        a, b = gates(c, 1)
        a, b = scan(a, b, True)
        h = a * h0 + b
        y = (hf_ref[pl.ds(r0, tc), :] + h) * _gelu_tanh(gate_ref[0, pl.ds(r0, tc), :].astype(F32))
        out_ref[0, pl.ds(r0, tc), :] = _rms(y, g_ref[...]).astype(BF16)
        return h[0:1, :]

    lax.fori_loop(0, nc, bwd_body, jnp.zeros((1, C), F32))


def _bi_rglru(u, gate, conv_w, conv_b, w_all_bf, bias_all, lru_lambda, g, tc):
    B, S, C = u.shape
    full = lambda b: (0, 0)
    return pl.pallas_call(
        functools.partial(_lru_kernel, tc=tc),
        grid=(B,),
        in_specs=[
            pl.BlockSpec((1, S, C), lambda b: (b, 0, 0)),
            pl.BlockSpec((1, S, C), lambda b: (b, 0, 0)),
            pl.BlockSpec(conv_w.shape, full),
            pl.BlockSpec((1, C), full),
            pl.BlockSpec(w_all_bf.shape, full),
            pl.BlockSpec(bias_all.shape, full),
            pl.BlockSpec(lru_lambda.shape, full),
            pl.BlockSpec((1, C), full),
        ],
        out_specs=pl.BlockSpec((1, S, C), lambda b: (b, 0, 0)),
        out_shape=jax.ShapeDtypeStruct((B, S, C), BF16),
        scratch_shapes=[pltpu.VMEM((S, C), F32)],
        compiler_params=pltpu.CompilerParams(
            dimension_semantics=("parallel",), vmem_limit_bytes=VMEM_LIMIT),
        name="bi_rglru",
    )(u, gate, conv_w, conv_b.reshape(1, C), w_all_bf, bias_all, lru_lambda, g.reshape(1, C))


def _topk_rows(s, k):
    n = s.shape[0]
    rid = lax.broadcasted_iota(I32, s.shape, 0)
    vals, ids = [], []
    for _ in range(k):
        m = jnp.max(s, axis=0, keepdims=True)
        sel = jnp.min(jnp.where(s == m, rid, n), axis=0, keepdims=True)
        vals.append(m)
        ids.append(sel)
        s = jnp.where(rid == sel, -jnp.inf, s)
    return jnp.concatenate(vals, axis=0), jnp.concatenate(ids, axis=0)


def _route_kernel(attn_ref, rec_ref, x_ref, wo_ref, g_ref, wq_ref, k1_ref, k2_ref,
                  h_ref, xn_ref, idx_ref, gate_ref):
    aw = attn_ref.shape[1]
    h = (x_ref[...]
         + jnp.dot(attn_ref[...], wo_ref[0:aw, :], preferred_element_type=F32)
         + jnp.dot(rec_ref[...], wo_ref[aw:, :], preferred_element_type=F32))
    h_ref[...] = h
    xn = _rms(h, g_ref[...])
    xn_ref[...] = xn
    q = jnp.dot(xn.astype(BF16), wq_ref[...], preferred_element_type=F32).astype(BF16)
    nt = (((1,), (1,)), ((), ()))
    s1 = lax.dot_general(k1_ref[...], q, nt, preferred_element_type=F32)
    s2 = lax.dot_general(k2_ref[...], q, nt, preferred_element_type=F32)
    k = PEER_TOPK
    idx_rows, gate_rows = [], []
    for hd in range(PEER_HEADS):
        v1, i1 = _topk_rows(s1[hd * N_KEYS:(hd + 1) * N_KEYS, :], k)
        v2, i2 = _topk_rows(s2[hd * N_KEYS:(hd + 1) * N_KEYS, :], k)
        cand = jnp.concatenate([v1[i:i + 1, :] + v2 for i in range(k)], axis=0)
        cidx = jnp.concatenate([i1[i:i + 1, :] * N_KEYS + i2 for i in range(k)], axis=0)
        pos = lax.broadcasted_iota(I32, cand.shape, 0)
        sc, ids = [], []
        for _ in range(k):
            m = jnp.max(cand, axis=0, keepdims=True)
            sel = jnp.min(jnp.where(cand == m, pos, k * k), axis=0, keepdims=True)
            hit = pos == sel
            sc.append(m)
            ids.append(jnp.max(jnp.where(hit, cidx, -1), axis=0, keepdims=True))
            cand = jnp.where(hit, -jnp.inf, cand)
        sc = jnp.concatenate(sc, axis=0)
        e = jnp.exp(sc - sc[0:1, :])
        gate_rows.append(e / jnp.sum(e, axis=0, keepdims=True))
        idx_rows.append(jnp.concatenate(ids, axis=0))
    idx_ref[...] = jnp.concatenate(idx_rows, axis=0)
    gate_ref[...] = jnp.concatenate(gate_rows, axis=0)


def _mix_route(attn, rec, x2, wo_bf, g, wq_bf, k1t_bf, k2t_bf, tm):
    T, D = x2.shape
    NK = PEER_HEADS * PEER_TOPK
    row = lambda i: (i, 0)
    full = lambda i: (0, 0)
    return pl.pallas_call(
        _route_kernel,
        grid=(T // tm,),
        in_specs=[
            pl.BlockSpec((tm, attn.shape[1]), row),
            pl.BlockSpec((tm, rec.shape[1]), row),
            pl.BlockSpec((tm, D), row),
            pl.BlockSpec(wo_bf.shape, full),
            pl.BlockSpec((1, D), full),
            pl.BlockSpec(wq_bf.shape, full),
            pl.BlockSpec(k1t_bf.shape, full),
            pl.BlockSpec(k2t_bf.shape, full),
        ],
        out_specs=[
            pl.BlockSpec((tm, D), row),
            pl.BlockSpec((tm, D), row),
            pl.BlockSpec((NK, tm), lambda i: (0, i)),
            pl.BlockSpec((NK, tm), lambda i: (0, i)),
        ],
        out_shape=[
            jax.ShapeDtypeStruct((T, D), F32),
            jax.ShapeDtypeStruct((T, D), F32),
            jax.ShapeDtypeStruct((NK, T), I32),
            jax.ShapeDtypeStruct((NK, T), F32),
        ],
        compiler_params=pltpu.CompilerParams(
            dimension_semantics=("parallel",), vmem_limit_bytes=VMEM_LIMIT),
        name="mix_route",
    )(attn, rec, x2, wo_bf, g.reshape(1, D), wq_bf, k1t_bf, k2t_bf)


ROW_CHUNKS = 8
TOKEN_UNROLL = 2
DMA_PRIORITIES = 2


def _peer_kernel(idx_ref, idx_next_ref, xn_ref, gate_ref, uv_hbm, o_ref, buf_a, buf_b, lg_ref, act_ref, sems,
                 *, tt, nk):
    g = pl.program_id(0)
    n = pl.num_programs(0)
    nt = (((1,), (1,)), ((), ()))
    ch = ROW_CHUNKS

    def row_copy(ids_ref, base, buf, sem, t, k):
        e = ids_ref[0, 0, base + t * nk + k]
        return pltpu.make_async_copy(uv_hbm.at[e], buf.at[t, :, pl.ds(k, 1), :], sem)

    def wait_all(buf, other, sem):
        pltpu.make_async_copy(other, buf, sem).wait()

    @pl.when(g == 0)
    def _():
        def body(t, carry):
            for k in range(nk):
                row_copy(idx_ref, 0, buf_a, sems.at[0], t, k).start(priority=k % DMA_PRIORITIES)
            return carry
        lax.fori_loop(0, tt, body, 0)

    def half(buf, row0, issue):
        def phase_u(tb, carry):
            for j in range(TOKEN_UNROLL):
                t = tb * TOKEN_UNROLL + j
                issue(t, 0, nk // 2)
                xt = xn_ref[row0 + t]
                acc = buf[t, 0] * xt[0:1, :]
                for s in range(1, ch):
                    acc = acc + buf[t, s] * xt[s:s + 1, :]
                hi = acc.astype(BF16)
                lo = (acc - hi.astype(F32)).astype(BF16)
                ones = jnp.ones((SUBLANES, LANES), BF16)
                red = (lax.dot_general(ones, hi, nt, preferred_element_type=F32)
                       + lax.dot_general(ones, lo, nt, preferred_element_type=F32))
                lg_ref[t] = red[0:1, :]
            return carry
        lax.fori_loop(0, tt // TOKEN_UNROLL, phase_u, 0)
        act_ref[...] = _gelu_tanh(lg_ref[...]) * gate_ref[row0:row0 + tt]

        def phase_v(tb, carry):
            for j in range(TOKEN_UNROLL):
                t = tb * TOKEN_UNROLL + j
                issue(t, nk // 2, nk)
                act = act_ref[t].astype(BF16)
                cols = [jnp.dot(act, buf[t, ch + s].astype(BF16), preferred_element_type=F32)
                        for s in range(ch)]
                o_ref[row0 + t] = jnp.concatenate(cols, axis=0)
            return carry
        lax.fori_loop(0, tt // TOKEN_UNROLL, phase_v, 0)

    def issue_b(t, k0, k1):
        for k in range(k0, k1):
            row_copy(idx_ref, tt * nk, buf_b, sems.at[1], t, k).start(priority=k % DMA_PRIORITIES)

    def issue_a_next(t, k0, k1):
        for k in range(k0, k1):
            row_copy(idx_next_ref, 0, buf_a, sems.at[0], t, k).start(priority=k % DMA_PRIORITIES)

    wait_all(buf_a, buf_b, sems.at[0])
    half(buf_a, 0, issue_b)
    wait_all(buf_b, buf_a, sems.at[1])
    half(buf_b, tt, issue_a_next)

    @pl.when(g == n - 1)
    def _():
        wait_all(buf_a, buf_b, sems.at[0])


def _peer(idx, gate, xn3, uv4, tt):
    T, ch, _ = xn3.shape
    nk = idx.shape[1]
    n = T // (2 * tt)
    idx3 = idx.reshape(n, 1, 2 * tt * nk)
    tok = lambda i: (i, 0, 0)
    return pl.pallas_call(
        functools.partial(_peer_kernel, tt=tt, nk=nk),
        grid=(n,),
        in_specs=[
            pl.BlockSpec((1, 1, 2 * tt * nk), tok, memory_space=pltpu.SMEM),
            pl.BlockSpec((1, 1, 2 * tt * nk), lambda i: (jnp.minimum(i + 1, n - 1), 0, 0), memory_space=pltpu.SMEM),
            pl.BlockSpec((2 * tt, ch, LANES), tok),
            pl.BlockSpec((2 * tt, 1, nk), tok),
            pl.BlockSpec(memory_space=pl.ANY),
        ],
        out_specs=pl.BlockSpec((2 * tt, ch, LANES), tok),
        out_shape=jax.ShapeDtypeStruct((T, ch, LANES), F32),
        scratch_shapes=[
            pltpu.VMEM((tt, 2 * ch, nk, LANES), F32),
            pltpu.VMEM((tt, 2 * ch, nk, LANES), F32),
            pltpu.VMEM((tt, 1, nk), F32),
            pltpu.VMEM((tt, 1, nk), F32),
            pltpu.SemaphoreType.DMA((2,)),
        ],
        compiler_params=pltpu.CompilerParams(
            dimension_semantics=("arbitrary",), vmem_limit_bytes=VMEM_LIMIT),
        name="peer",
    )(idx3, idx3, xn3, gate.reshape(T, 1, nk), uv4)


def _ple_kernel(h_ref, po_ref, p_ref, g3_ref, wg_ref, wp_ref, gf_ref, o_ref, *, final):
    h = h_ref[...] + po_ref[...]
    xn = _rms(h, g3_ref[...]).astype(BF16)
    gate = _sigmoid(jnp.dot(xn, wg_ref[...], preferred_element_type=F32))
    proj = jnp.dot(p_ref[...].astype(BF16), wp_ref[...], preferred_element_type=F32)
    h = h + gate * proj
    o_ref[...] = _rms(h, gf_ref[...]) if final else h


def _ple_out(h1, peer_out, p2, g3, wg_bf, wp_bf, gf, final, tm):
    T, D = h1.shape
    row = lambda i: (i, 0)
    full = lambda i: (0, 0)
    return pl.pallas_call(
        functools.partial(_ple_kernel, final=final),
        grid=(T // tm,),
        in_specs=[
            pl.BlockSpec((tm, D), row),
            pl.BlockSpec((tm, D), row),
            pl.BlockSpec((tm, p2.shape[1]), row),
            pl.BlockSpec((1, D), full),
            pl.BlockSpec(wg_bf.shape, full),
            pl.BlockSpec(wp_bf.shape, full),
            pl.BlockSpec((1, D), full),
        ],
        out_specs=pl.BlockSpec((tm, D), row),
        out_shape=jax.ShapeDtypeStruct((T, D), F32),
        compiler_params=pltpu.CompilerParams(
            dimension_semantics=("parallel",), vmem_limit_bytes=VMEM_LIMIT),
        name="ple_out",
    )(h1, peer_out, p2, g3.reshape(1, D), wg_bf, wp_bf, gf.reshape(1, D))


def _block_diag(w):
    nb, bw, _ = w.shape
    eye = jnp.eye(nb, dtype=w.dtype)
    return (eye[:, None, :, None] * w[:, :, None, :]).reshape(nb * bw, nb * bw)


def _key_matrix(keys, half):
    z = jnp.zeros_like(keys)
    blk = jnp.concatenate([keys, z] if half == 0 else [z, keys], axis=1)
    return jnp.kron(jnp.eye(PEER_HEADS, dtype=keys.dtype), blk)


def kernel(x, p, positions, norm_mix_g, w_in, lambda_q1, lambda_k1, lambda_q2, lambda_k2, diff_norm_g, conv_w, conv_b, lru_wa, lru_ba, lru_wx, lru_bx, lru_lambda, lru_norm_g, w_out, norm_ffn_g, peer_wq, peer_keys1, peer_keys2, peer_u, peer_v, norm_ple_g, ple_w_gate, ple_w_proj, final_norm_g):
    B, S, D = x.shape
    T = B * S
    depth = w_in.shape[0]
    h = x
    for i in range(depth):
        lambda_init = 0.8 - 0.6 * math.exp(-0.3 * i)
        q, k, v, u, gate = _in_proj(h, positions, norm_mix_g[i], w_in[i].astype(BF16), tm=512)
        attn = _diff_attn(q, k, v, lambda_q1[i], lambda_k1[i], lambda_q2[i], lambda_k2[i],
                          diff_norm_g[i], lambda_init, tq=256)
        w_all = jnp.concatenate([_block_diag(lru_wa[i, 0]), _block_diag(lru_wx[i, 0]),
                                 _block_diag(lru_wa[i, 1]), _block_diag(lru_wx[i, 1])], axis=1).astype(BF16)
        bias_all = jnp.concatenate([lru_ba[i, 0], lru_bx[i, 0], lru_ba[i, 1], lru_bx[i, 1]]).reshape(1, -1)
        rec = _bi_rglru(u, gate, conv_w[i], conv_b[i], w_all, bias_all, lru_lambda[i], lru_norm_g[i], tc=256)
        h1, xn2, idx_t, gate_t = _mix_route(
            attn.reshape(T, -1), rec.reshape(T, -1), h.reshape(T, D), w_out[i].astype(BF16), norm_ffn_g[i],
            peer_wq[i].astype(BF16), _key_matrix(peer_keys1[i], 0).astype(BF16),
            _key_matrix(peer_keys2[i], 1).astype(BF16), tm=256)
        n_exp = peer_u.shape[1]
        uv4 = jnp.concatenate([peer_u[i].reshape(n_exp, ROW_CHUNKS, 1, LANES),
                               peer_v[i].reshape(n_exp, ROW_CHUNKS, 1, LANES)], axis=1)
        peer_out = _peer(idx_t.T, gate_t.T, xn2.reshape(T, ROW_CHUNKS, LANES), uv4, tt=8).reshape(T, D)
        h = _ple_out(h1, peer_out, p[i].reshape(T, -1), norm_ple_g[i], ple_w_gate[i].astype(BF16),
                     ple_w_proj[i].astype(BF16), final_norm_g, final=(i == depth - 1), tm=512)
        h = h.reshape(B, S, D)
    return h
```

```python
import functools
import math

import jax
import jax.numpy as jnp
from jax import lax
from jax.experimental import pallas as pl
from jax.experimental.pallas import tpu as pltpu

F32 = jnp.float32
BF16 = jnp.bfloat16
I32 = jnp.int32

EPS = 1e-6
DIFF_HEAD_DIM = 64
DIFF_V_DIM = 128
N_DIFF_HEADS = 4
ROPE_DIM = 16
ROPE_THETA = 500000.0
LRU_WIDTH = 512
LRU_C = 8.0
N_KEYS = 128
PEER_HEADS = 8
PEER_TOPK = 16
HALF_KEY = 64
LANES = 128
SUBLANES = 8
VMEM_LIMIT = 56 * 1024 * 1024


def _rms(x, g):
    return x * lax.rsqrt(jnp.mean(x * x, axis=-1, keepdims=True) + EPS) * g


def _gelu_tanh(x):
    return 0.5 * x * (1.0 + jnp.tanh(math.sqrt(2.0 / math.pi) * (x + 0.044715 * (x * x * x))))


def _sigmoid(x):
    return 1.0 / (1.0 + jnp.exp(-x))


def _inproj_kernel(x_ref, pos_ref, g_ref, w_ref, q_ref, k_ref, v_ref, u_ref, gate_ref):
    x = x_ref[0]
    xn = _rms(x, g_ref[...]).astype(BF16)
    pos = pos_ref[0].astype(F32)
    lane = lax.broadcasted_iota(I32, (1, LANES), 1)
    p = lane & (DIFF_HEAD_DIM - 1)
    freq = (p & (ROPE_DIM // 2 - 1)).astype(F32)
    inv_freq = jnp.exp(freq * (-2.0 / ROPE_DIM * math.log(ROPE_THETA)))
    ang = pos * inv_freq
    cs = jnp.cos(ang)
    sn = jnp.sin(ang)
    half = ROPE_DIM // 2
    c_mul = jnp.where(p < ROPE_DIM, cs, 1.0)
    s_up = jnp.where(p < half, -sn, 0.0)
    s_dn = jnp.where((p >= half) & (p < ROPE_DIM), sn, 0.0)

    def rope(t):
        return t * c_mul + pltpu.roll(t, LANES - half, 1) * s_up + pltpu.roll(t, half, 1) * s_dn

    nq = q_ref.shape[-1]
    pq = jnp.dot(xn, w_ref[:, 0:nq], preferred_element_type=F32)
    pk = jnp.dot(xn, w_ref[:, nq:2 * nq], preferred_element_type=F32)
    scale = DIFF_HEAD_DIM ** -0.5
    for j in range(nq // LANES):
        sl = slice(j * LANES, (j + 1) * LANES)
        q_ref[0, :, sl] = (rope(pq[:, sl]) * scale).astype(BF16)
        k_ref[0, :, sl] = rope(pk[:, sl]).astype(BF16)
    v_ref[0] = jnp.dot(xn, w_ref[:, 2 * nq:3 * nq], preferred_element_type=F32).astype(BF16)
    u_ref[0] = jnp.dot(xn, w_ref[:, 3 * nq:3 * nq + LRU_WIDTH], preferred_element_type=F32)
    gate_ref[0] = jnp.dot(xn, w_ref[:, 3 * nq + LRU_WIDTH:3 * nq + 2 * LRU_WIDTH],
                          preferred_element_type=F32).astype(BF16)


def _in_proj(x, positions, g, w_in_bf, tm):
    B, S, D = x.shape
    nq = N_DIFF_HEADS * 2 * DIFF_HEAD_DIM
    ncols = w_in_bf.shape[1]
    row = lambda b, i: (b, i, 0)
    return pl.pallas_call(
        _inproj_kernel,
        grid=(B, S // tm),
        in_specs=[
            pl.BlockSpec((1, tm, D), row),
            pl.BlockSpec((1, tm, 1), row),
            pl.BlockSpec((1, D), lambda b, i: (0, 0)),
            pl.BlockSpec((D, ncols), lambda b, i: (0, 0)),
        ],
        out_specs=[
            pl.BlockSpec((1, tm, nq), row),
            pl.BlockSpec((1, tm, nq), row),
            pl.BlockSpec((1, tm, nq), row),
            pl.BlockSpec((1, tm, LRU_WIDTH), row),
            pl.BlockSpec((1, tm, LRU_WIDTH), row),
        ],
        out_shape=[
            jax.ShapeDtypeStruct((B, S, nq), BF16),
            jax.ShapeDtypeStruct((B, S, nq), BF16),
            jax.ShapeDtypeStruct((B, S, nq), BF16),
            jax.ShapeDtypeStruct((B, S, LRU_WIDTH), F32),
            jax.ShapeDtypeStruct((B, S, LRU_WIDTH), BF16),
        ],
        compiler_params=pltpu.CompilerParams(
            dimension_semantics=("parallel", "parallel"), vmem_limit_bytes=VMEM_LIMIT),
        name="in_proj",
    )(x, positions.reshape(B, S, 1), g.reshape(1, D), w_in_bf)


def _attn_kernel(lq1_ref, lk1_ref, lq2_ref, lk2_ref, g_ref, q_ref, k_ref, v_ref, o_ref, *, lambda_init):
    lam = (jnp.exp(jnp.sum(lq1_ref[...] * lk1_ref[...], axis=-1, keepdims=True))
           - jnp.exp(jnp.sum(lq2_ref[...] * lk2_ref[...], axis=-1, keepdims=True))
           + lambda_init)
    q = q_ref[0]
    k = k_ref[0]
    v = v_ref[0]
    lane = lax.broadcasted_iota(I32, q.shape, 1)
    zero = jnp.zeros_like(q)
    q0 = jnp.where(lane < DIFF_HEAD_DIM, q, zero)
    q1 = jnp.where(lane >= DIFF_HEAD_DIM, q, zero)
    nt = (((1,), (1,)), ((), ()))
    s0 = lax.dot_general(q0, k, nt, preferred_element_type=F32)
    s1 = lax.dot_general(q1, k, nt, preferred_element_type=F32)
    p0 = jnp.exp(s0 - jnp.max(s0, axis=-1, keepdims=True))
    p1 = jnp.exp(s1 - jnp.max(s1, axis=-1, keepdims=True))
    r0 = 1.0 / jnp.sum(p0, axis=-1, keepdims=True)
    r1 = lam / jnp.sum(p1, axis=-1, keepdims=True)
    w = (p0 * r0 - p1 * r1).astype(BF16)
    o = jnp.dot(w, v, preferred_element_type=F32)
    o_ref[0] = (_rms(o, g_ref[...]) * (1.0 - lambda_init)).astype(BF16)


def _diff_attn(q, k, v, lq1, lk1, lq2, lk2, g, lambda_init, tq):
    B, S, W = q.shape
    H = W // DIFF_V_DIM
    vec = lambda b, h, i: (0, 0)
    return pl.pallas_call(
        functools.partial(_attn_kernel, lambda_init=lambda_init),
        grid=(B, H, S // tq),
        in_specs=[
            pl.BlockSpec((1, DIFF_HEAD_DIM), vec),
            pl.BlockSpec((1, DIFF_HEAD_DIM), vec),
            pl.BlockSpec((1, DIFF_HEAD_DIM), vec),
            pl.BlockSpec((1, DIFF_HEAD_DIM), vec),
            pl.BlockSpec((1, DIFF_V_DIM), vec),
            pl.BlockSpec((1, tq, DIFF_V_DIM), lambda b, h, i: (b, i, h)),
            pl.BlockSpec((1, S, DIFF_V_DIM), lambda b, h, i: (b, 0, h)),
            pl.BlockSpec((1, S, DIFF_V_DIM), lambda b, h, i: (b, 0, h)),
        ],
        out_specs=pl.BlockSpec((1, tq, DIFF_V_DIM), lambda b, h, i: (b, i, h)),
        out_shape=jax.ShapeDtypeStruct((B, S, W), BF16),
        compiler_params=pltpu.CompilerParams(
            dimension_semantics=("parallel", "parallel", "parallel"), vmem_limit_bytes=VMEM_LIMIT),
        name="diff_attn",
    )(lq1.reshape(1, -1), lk1.reshape(1, -1), lq2.reshape(1, -1), lk2.reshape(1, -1),
      g.reshape(1, -1), q, k, v)


def _lru_kernel(u_ref, gate_ref, cw_ref, cb_ref, w_ref, bias_ref, lam_ref, g_ref, out_ref, hf_ref, *, tc):
    S = u_ref.shape[1]
    C = u_ref.shape[2]
    nc = S // tc
    halo = SUBLANES
    neg_lam = -lam_ref[...]
    sp = jnp.maximum(neg_lam, 0.0) + jnp.log(1.0 + jnp.exp(-jnp.abs(neg_lam)))
    row = lax.broadcasted_iota(I32, (tc, 1), 0)
    conv_taps = cw_ref.shape[0]
    conv_left = 2

    def gates(c, d):
        r0 = pl.multiple_of(c * tc, tc)
        x = u_ref[0, pl.ds(r0, tc), :]
        prev = u_ref[0, pl.ds(pl.multiple_of(jnp.maximum(r0 - halo, 0), halo), halo), :]
        nxt = u_ref[0, pl.ds(pl.multiple_of(jnp.minimum(r0 + tc, S - halo), halo), halo), :]
        prev = jnp.where(c > 0, prev, 0.0)
        nxt = jnp.where(c < nc - 1, nxt, 0.0)
        win = jnp.concatenate([prev, x, nxt], axis=0)
        uc = cb_ref[...]
        for j in range(conv_taps):
            o = halo - conv_left + j
            uc = uc + cw_ref[j:j + 1, :] * win[o:o + tc, :]
        pre = (jnp.dot(uc.astype(BF16), w_ref[:, d * 2 * C:(d + 1) * 2 * C], preferred_element_type=F32)
               + bias_ref[:, d * 2 * C:(d + 1) * 2 * C])
        r = _sigmoid(pre[:, :C])
        i = _sigmoid(pre[:, C:])
        log_a = -LRU_C * r * sp[d:d + 1, :]
        a = jnp.exp(log_a)
        th = jnp.tanh(log_a)
        mult = jnp.sqrt(-2.0 * th / (1.0 - th))
        return a, mult * (i * uc)

    def scan(a, b, reverse):
        d = 1
        while d < tc:
            if reverse:
                a_s = pltpu.roll(a, tc - d, 0)
                b_s = pltpu.roll(b, tc - d, 0)
                m = row < tc - d
            else:
                a_s = pltpu.roll(a, d, 0)
                b_s = pltpu.roll(b, d, 0)
                m = row >= d
            a_s = jnp.where(m, a_s, 1.0)
            b_s = jnp.where(m, b_s, 0.0)
            b = a * b_s + b
            a = a * a_s
            d *= 2
        return a, b

    def fwd_body(c, h0):
        a, b = gates(c, 0)
        a, b = scan(a, b, False)
        h = a * h0 + b
        hf_ref[pl.ds(pl.multiple_of(c * tc, tc), tc), :] = h
        return h[tc - 1:tc, :]

    lax.fori_loop(0, nc, fwd_body, jnp.zeros((1, C), F32))

    def bwd_body(j, h0):
        c = nc - 1 - j
        r0 = pl.multiple_of(c * tc, tc)
        a, b = gates(c, 1)
        a, b = scan(a, b, True)
        h = a * h0 + b
        y = (hf_ref[pl.ds(r0, tc), :] + h) * _gelu_tanh(gate_ref[0, pl.ds(r0, tc), :].astype(F32))
        out_ref[0, pl.ds(r0, tc), :] = _rms(y, g_ref[...]).astype(BF16)
        return h[0:1, :]

    lax.fori_loop(0, nc, bwd_body, jnp.zeros((1, C), F32))


def _bi_rglru(u, gate, conv_w, conv_b, w_all_bf, bias_all, lru_lambda, g, tc):
    B, S, C = u.shape
    full = lambda b: (0, 0)
    return pl.pallas_call(
        functools.partial(_lru_kernel, tc=tc),
        grid=(B,),
        in_specs=[
            pl.BlockSpec((1, S, C), lambda b: (b, 0, 0)),
            pl.BlockSpec((1, S, C), lambda b: (b, 0, 0)),
            pl.BlockSpec(conv_w.shape, full),
            pl.BlockSpec((1, C), full),
            pl.BlockSpec(w_all_bf.shape, full),
            pl.BlockSpec(bias_all.shape, full),
            pl.BlockSpec(lru_lambda.shape, full),
            pl.BlockSpec((1, C), full),
        ],
        out_specs=pl.BlockSpec((1, S, C), lambda b: (b, 0, 0)),
        out_shape=jax.ShapeDtypeStruct((B, S, C), BF16),
        scratch_shapes=[pltpu.VMEM((S, C), F32)],
        compiler_params=pltpu.CompilerParams(
            dimension_semantics=("parallel",), vmem_limit_bytes=VMEM_LIMIT),
        name="bi_rglru",
    )(u, gate, conv_w, conv_b.reshape(1, C), w_all_bf, bias_all, lru_lambda, g.reshape(1, C))


def _topk_rows(s, k):
    n = s.shape[0]
    rid = lax.broadcasted_iota(I32, s.shape, 0)
    vals, ids = [], []
    for _ in range(k):
        m = jnp.max(s, axis=0, keepdims=True)
        sel = jnp.min(jnp.where(s == m, rid, n), axis=0, keepdims=True)
        vals.append(m)
        ids.append(sel)
        s = jnp.where(rid == sel, -jnp.inf, s)
    return jnp.concatenate(vals, axis=0), jnp.concatenate(ids, axis=0)


def _route_kernel(attn_ref, rec_ref, x_ref, wo_ref, g_ref, wq_ref, k1_ref, k2_ref,
                  h_ref, xn_ref, idx_ref, gate_ref):
    aw = attn_ref.shape[1]
    h = (x_ref[...]
         + jnp.dot(attn_ref[...], wo_ref[0:aw, :], preferred_element_type=F32)
         + jnp.dot(rec_ref[...], wo_ref[aw:, :], preferred_element_type=F32))
    h_ref[...] = h
    xn = _rms(h, g_ref[...])
    xn_ref[...] = xn
    q = jnp.dot(xn.astype(BF16), wq_ref[...], preferred_element_type=F32).astype(BF16)
    nt = (((1,), (1,)), ((), ()))
    s1 = lax.dot_general(k1_ref[...], q, nt, preferred_element_type=F32)
    s2 = lax.dot_general(k2_ref[...], q, nt, preferred_element_type=F32)
    k = PEER_TOPK
    idx_rows, gate_rows = [], []
    for hd in range(PEER_HEADS):
        v1, i1 = _topk_rows(s1[hd * N_KEYS:(hd + 1) * N_KEYS, :], k)
        v2, i2 = _topk_rows(s2[hd * N_KEYS:(hd + 1) * N_KEYS, :], k)
        cand = jnp.concatenate([v1[i:i + 1, :] + v2 for i in range(k)], axis=0)
        cidx = jnp.concatenate([i1[i:i + 1, :] * N_KEYS + i2 for i in range(k)], axis=0)
        pos = lax.broadcasted_iota(I32, cand.shape, 0)
        sc, ids = [], []
        for _ in range(k):
            m = jnp.max(cand, axis=0, keepdims=True)
            sel = jnp.min(jnp.where(cand == m, pos, k * k), axis=0, keepdims=True)
            hit = pos == sel
            sc.append(m)
            ids.append(jnp.max(jnp.where(hit, cidx, -1), axis=0, keepdims=True))
            cand = jnp.where(hit, -jnp.inf, cand)
        sc = jnp.concatenate(sc, axis=0)
        e = jnp.exp(sc - sc[0:1, :])
        gate_rows.append(e / jnp.sum(e, axis=0, keepdims=True))
        idx_rows.append(jnp.concatenate(ids, axis=0))
    idx_ref[...] = jnp.concatenate(idx_rows, axis=0)
    gate_ref[...] = jnp.concatenate(gate_rows, axis=0)


def _mix_route(attn, rec, x2, wo_bf, g, wq_bf, k1t_bf, k2t_bf, tm):
    T, D = x2.shape
    NK = PEER_HEADS * PEER_TOPK
    row = lambda i: (i, 0)
    full = lambda i: (0, 0)
    return pl.pallas_call(
        _route_kernel,
        grid=(T // tm,),
        in_specs=[
            pl.BlockSpec((tm, attn.shape[1]), row),
            pl.BlockSpec((tm, rec.shape[1]), row),
            pl.BlockSpec((tm, D), row),
            pl.BlockSpec(wo_bf.shape, full),
            pl.BlockSpec((1, D), full),
            pl.BlockSpec(wq_bf.shape, full),
            pl.BlockSpec(k1t_bf.shape, full),
            pl.BlockSpec(k2t_bf.shape, full),
        ],
        out_specs=[
            pl.BlockSpec((tm, D), row),
            pl.BlockSpec((tm, D), row),
            pl.BlockSpec((NK, tm), lambda i: (0, i)),
            pl.BlockSpec((NK, tm), lambda i: (0, i)),
        ],
        out_shape=[
            jax.ShapeDtypeStruct((T, D), F32),
            jax.ShapeDtypeStruct((T, D), F32),
            jax.ShapeDtypeStruct((NK, T), I32),
            jax.ShapeDtypeStruct((NK, T), F32),
        ],
        compiler_params=pltpu.CompilerParams(
            dimension_semantics=("parallel",), vmem_limit_bytes=VMEM_LIMIT),
        name="mix_route",
    )(attn, rec, x2, wo_bf, g.reshape(1, D), wq_bf, k1t_bf, k2t_bf)


ROW_CHUNKS = 8
TOKEN_UNROLL = 2
DMA_PRIORITIES = 2
PEER_RING = 4
PEER_PREFETCH = 2


def _peer_kernel(idx_ref, idx_next_ref, xn_ref, gate_ref, uv_hbm, o_ref, *scratch, tt, nk):
    bufs = scratch[:PEER_RING]
    lg_ref, act_ref, sems = scratch[PEER_RING:]
    g = pl.program_id(0)
    n = pl.num_programs(0)
    nt = (((1,), (1,)), ((), ()))
    ch = ROW_CHUNKS

    def row_copy(ids_ref, q, t, k):
        e = ids_ref[0, 0, (q * tt + t) * nk + k]
        return pltpu.make_async_copy(uv_hbm.at[e], bufs[q].at[t, :, pl.ds(k, 1), :], sems.at[q])

    def wait_group(q):
        pltpu.make_async_copy(bufs[(q + 1) % PEER_RING], bufs[q], sems.at[q]).wait()

    @pl.when(g == 0)
    def _():
        for q in range(PEER_PREFETCH):
            def body(t, carry, q=q):
                for k in range(nk):
                    row_copy(idx_ref, q, t, k).start(priority=k % DMA_PRIORITIES)
                return carry
            lax.fori_loop(0, tt, body, 0)

    def group(q):
        buf = bufs[q]
        row0 = q * tt
        ahead = q + PEER_PREFETCH
        ids_ahead = idx_ref if ahead < PEER_RING else idx_next_ref

        def issue(t, k0, k1):
            for k in range(k0, k1):
                row_copy(ids_ahead, ahead % PEER_RING, t, k).start(priority=k % DMA_PRIORITIES)

        def phase_u(tb, carry):
            for j in range(TOKEN_UNROLL):
                t = tb * TOKEN_UNROLL + j
                issue(t, 0, nk // 2)
                xt = xn_ref[row0 + t]
                acc = buf[t, 0] * xt[0:1, :]
                for s in range(1, ch):
                    acc = acc + buf[t, s] * xt[s:s + 1, :]
                hi = acc.astype(BF16)
                lo = (acc - hi.astype(F32)).astype(BF16)
                ones = jnp.ones((SUBLANES, LANES), BF16)
                red = (lax.dot_general(ones, hi, nt, preferred_element_type=F32)
                       + lax.dot_general(ones, lo, nt, preferred_element_type=F32))
                lg_ref[t] = red[0:1, :]
            return carry
        lax.fori_loop(0, tt // TOKEN_UNROLL, phase_u, 0)
        act_ref[...] = _gelu_tanh(lg_ref[...]) * gate_ref[row0:row0 + tt]

        def phase_v(tb, carry):
            for j in range(TOKEN_UNROLL):
                t = tb * TOKEN_UNROLL + j
                issue(t, nk // 2, nk)
                act = act_ref[t].astype(BF16)
                cols = [jnp.dot(act, buf[t, ch + s].astype(BF16), preferred_element_type=F32)
                        for s in range(ch)]
                o_ref[row0 + t] = jnp.concatenate(cols, axis=0)
            return carry
        lax.fori_loop(0, tt // TOKEN_UNROLL, phase_v, 0)

    for q in range(PEER_RING):
        wait_group(q)
        group(q)

    @pl.when(g == n - 1)
    def _():
        for q in range(PEER_PREFETCH):
            wait_group(q)


def _peer(idx, gate, xn3, uv4, tt):
    T, ch, _ = xn3.shape
    nk = idx.shape[1]
    step = PEER_RING * tt
    n = T // step
    idx3 = idx.reshape(n, 1, step * nk)
    tok = lambda i: (i, 0, 0)
    return pl.pallas_call(
        functools.partial(_peer_kernel, tt=tt, nk=nk),
        grid=(n,),
        in_specs=[
            pl.BlockSpec((1, 1, step * nk), tok, memory_space=pltpu.SMEM),
            pl.BlockSpec((1, 1, step * nk), lambda i: (jnp.minimum(i + 1, n - 1), 0, 0), memory_space=pltpu.SMEM),
            pl.BlockSpec((step, ch, LANES), tok),
            pl.BlockSpec((step, 1, nk), tok),
            pl.BlockSpec(memory_space=pl.ANY),
        ],
        out_specs=pl.BlockSpec((step, ch, LANES), tok),
        out_shape=jax.ShapeDtypeStruct((T, ch, LANES), F32),
        scratch_shapes=(
            [pltpu.VMEM((tt, 2 * ch, nk, LANES), F32) for _ in range(PEER_RING)]
            + [pltpu.VMEM((tt, 1, nk), F32), pltpu.VMEM((tt, 1, nk), F32), pltpu.SemaphoreType.DMA((PEER_RING,))]),
        compiler_params=pltpu.CompilerParams(
            dimension_semantics=("arbitrary",), vmem_limit_bytes=VMEM_LIMIT),
        name="peer",
    )(idx3, idx3, xn3, gate.reshape(T, 1, nk), uv4)


def _ple_kernel(h_ref, po_ref, p_ref, g3_ref, wg_ref, wp_ref, gf_ref, o_ref, *, final):
    h = h_ref[...] + po_ref[...]
    xn = _rms(h, g3_ref[...]).astype(BF16)
    gate = _sigmoid(jnp.dot(xn, wg_ref[...], preferred_element_type=F32))
    proj = jnp.dot(p_ref[...].astype(BF16), wp_ref[...], preferred_element_type=F32)
    h = h + gate * proj
    o_ref[...] = _rms(h, gf_ref[...]) if final else h


def _ple_out(h1, peer_out, p2, g3, wg_bf, wp_bf, gf, final, tm):
    T, D = h1.shape
    row = lambda i: (i, 0)
    full = lambda i: (0, 0)
    return pl.pallas_call(
        functools.partial(_ple_kernel, final=final),
        grid=(T // tm,),
        in_specs=[
            pl.BlockSpec((tm, D), row),
            pl.BlockSpec((tm, D), row),
            pl.BlockSpec((tm, p2.shape[1]), row),
            pl.BlockSpec((1, D), full),
            pl.BlockSpec(wg_bf.shape, full),
            pl.BlockSpec(wp_bf.shape, full),
            pl.BlockSpec((1, D), full),
        ],
        out_specs=pl.BlockSpec((tm, D), row),
        out_shape=jax.ShapeDtypeStruct((T, D), F32),
        compiler_params=pltpu.CompilerParams(
            dimension_semantics=("parallel",), vmem_limit_bytes=VMEM_LIMIT),
        name="ple_out",
    )(h1, peer_out, p2, g3.reshape(1, D), wg_bf, wp_bf, gf.reshape(1, D))


def _block_diag(w):
    nb, bw, _ = w.shape
    eye = jnp.eye(nb, dtype=w.dtype)
    return (eye[:, None, :, None] * w[:, :, None, :]).reshape(nb * bw, nb * bw)


def _key_matrix(keys, half):
    z = jnp.zeros_like(keys)
    blk = jnp.concatenate([keys, z] if half == 0 else [z, keys], axis=1)
    return jnp.kron(jnp.eye(PEER_HEADS, dtype=keys.dtype), blk)


def kernel(x, p, positions, norm_mix_g, w_in, lambda_q1, lambda_k1, lambda_q2, lambda_k2, diff_norm_g, conv_w, conv_b, lru_wa, lru_ba, lru_wx, lru_bx, lru_lambda, lru_norm_g, w_out, norm_ffn_g, peer_wq, peer_keys1, peer_keys2, peer_u, peer_v, norm_ple_g, ple_w_gate, ple_w_proj, final_norm_g):
    B, S, D = x.shape
    T = B * S
    depth = w_in.shape[0]
    h = x
    for i in range(depth):
        lambda_init = 0.8 - 0.6 * math.exp(-0.3 * i)
        q, k, v, u, gate = _in_proj(h, positions, norm_mix_g[i], w_in[i].astype(BF16), tm=512)
        attn = _diff_attn(q, k, v, lambda_q1[i], lambda_k1[i], lambda_q2[i], lambda_k2[i],
                          diff_norm_g[i], lambda_init, tq=256)
        w_all = jnp.concatenate([_block_diag(lru_wa[i, 0]), _block_diag(lru_wx[i, 0]),
                                 _block_diag(lru_wa[i, 1]), _block_diag(lru_wx[i, 1])], axis=1).astype(BF16)
        bias_all = jnp.concatenate([lru_ba[i, 0], lru_bx[i, 0], lru_ba[i, 1], lru_bx[i, 1]]).reshape(1, -1)
        rec = _bi_rglru(u, gate, conv_w[i], conv_b[i], w_all, bias_all, lru_lambda[i], lru_norm_g[i], tc=256)
        h1, xn2, idx_t, gate_t = _mix_route(
            attn.reshape(T, -1), rec.reshape(T, -1), h.reshape(T, D), w_out[i].astype(BF16), norm_ffn_g[i],
            peer_wq[i].astype(BF16), _key_matrix(peer_keys1[i], 0).astype(BF16),
            _key_matrix(peer_keys2[i], 1).astype(BF16), tm=256)
        n_exp = peer_u.shape[1]
        uv4 = jnp.concatenate([peer_u[i].reshape(n_exp, ROW_CHUNKS, 1, LANES),
                               peer_v[i].reshape(n_exp, ROW_CHUNKS, 1, LANES)], axis=1)
        peer_out = _peer(idx_t.T, gate_t.T, xn2.reshape(T, ROW_CHUNKS, LANES), uv4, tt=8).reshape(T, D)
        h = _ple_out(h1, peer_out, p[i].reshape(T, -1), norm_ple_g[i], ple_w_gate[i].astype(BF16),
                     ple_w_proj[i].astype(BF16), final_norm_g, final=(i == depth - 1), tm=512)
        h = h.reshape(B, S, D)
    return h
```

```python
import functools
import math

import jax
import jax.numpy as jnp
from jax import lax
from jax.experimental import pallas as pl
from jax.experimental.pallas import tpu as pltpu

F32 = jnp.float32
BF16 = jnp.bfloat16
I32 = jnp.int32

EPS = 1e-6
DIFF_HEAD_DIM = 64
DIFF_V_DIM = 128
N_DIFF_HEADS = 4
ROPE_DIM = 16
ROPE_THETA = 500000.0
LRU_WIDTH = 512
LRU_C = 8.0
N_KEYS = 128
PEER_HEADS = 8
PEER_TOPK = 16
HALF_KEY = 64
LOG2_E = math.log2(math.e)
LANES = 128
SUBLANES = 8
VMEM_LIMIT = 56 * 1024 * 1024


def _rms(x, g):
    return x * lax.rsqrt(jnp.mean(x * x, axis=-1, keepdims=True) + EPS) * g


def _gelu_tanh(x):
    return 0.5 * x * (1.0 + jnp.tanh(math.sqrt(2.0 / math.pi) * (x + 0.044715 * (x * x * x))))


def _sigmoid(x):
    return 1.0 / (1.0 + jnp.exp(-x))


def _inproj_kernel(x_ref, pos_ref, g_ref, w_ref, q_ref, k_ref, v_ref, u_ref, gate_ref):
    x = x_ref[0]
    xn = _rms(x, g_ref[...]).astype(BF16)
    pos = pos_ref[0].astype(F32)
    lane = lax.broadcasted_iota(I32, (1, LANES), 1)
    p = lane & (DIFF_HEAD_DIM - 1)
    freq = (p & (ROPE_DIM // 2 - 1)).astype(F32)
    inv_freq = jnp.exp(freq * (-2.0 / ROPE_DIM * math.log(ROPE_THETA)))
    ang = pos * inv_freq
    cs = jnp.cos(ang)
    sn = jnp.sin(ang)
    half = ROPE_DIM // 2
    c_mul = jnp.where(p < ROPE_DIM, cs, 1.0)
    s_up = jnp.where(p < half, -sn, 0.0)
    s_dn = jnp.where((p >= half) & (p < ROPE_DIM), sn, 0.0)

    def rope(t):
        return t * c_mul + pltpu.roll(t, LANES - half, 1) * s_up + pltpu.roll(t, half, 1) * s_dn

    nq = q_ref.shape[-1]
    pq = jnp.dot(xn, w_ref[:, 0:nq], preferred_element_type=F32)
    pk = jnp.dot(xn, w_ref[:, nq:2 * nq], preferred_element_type=F32)
    scale = DIFF_HEAD_DIM ** -0.5 * LOG2_E
    for j in range(nq // LANES):
        sl = slice(j * LANES, (j + 1) * LANES)
        q_ref[0, :, sl] = (rope(pq[:, sl]) * scale).astype(BF16)
        k_ref[0, :, sl] = rope(pk[:, sl]).astype(BF16)
    v_ref[0] = jnp.dot(xn, w_ref[:, 2 * nq:3 * nq], preferred_element_type=F32).astype(BF16)
    u_ref[0] = jnp.dot(xn, w_ref[:, 3 * nq:3 * nq + LRU_WIDTH], preferred_element_type=F32)
    gate_ref[0] = jnp.dot(xn, w_ref[:, 3 * nq + LRU_WIDTH:3 * nq + 2 * LRU_WIDTH],
                          preferred_element_type=F32).astype(BF16)


def _in_proj(x, positions, g, w_in_bf, tm):
    B, S, D = x.shape
    nq = N_DIFF_HEADS * 2 * DIFF_HEAD_DIM
    ncols = w_in_bf.shape[1]
    row = lambda b, i: (b, i, 0)
    return pl.pallas_call(
        _inproj_kernel,
        grid=(B, S // tm),
        in_specs=[
            pl.BlockSpec((1, tm, D), row),
            pl.BlockSpec((1, tm, 1), row),
            pl.BlockSpec((1, D), lambda b, i: (0, 0)),
            pl.BlockSpec((D, ncols), lambda b, i: (0, 0)),
        ],
        out_specs=[
            pl.BlockSpec((1, tm, nq), row),
            pl.BlockSpec((1, tm, nq), row),
            pl.BlockSpec((1, tm, nq), row),
            pl.BlockSpec((1, tm, LRU_WIDTH), row),
            pl.BlockSpec((1, tm, LRU_WIDTH), row),
        ],
        out_shape=[
            jax.ShapeDtypeStruct((B, S, nq), BF16),
            jax.ShapeDtypeStruct((B, S, nq), BF16),
            jax.ShapeDtypeStruct((B, S, nq), BF16),
            jax.ShapeDtypeStruct((B, S, LRU_WIDTH), F32),
            jax.ShapeDtypeStruct((B, S, LRU_WIDTH), BF16),
        ],
        compiler_params=pltpu.CompilerParams(
            dimension_semantics=("parallel", "parallel"), vmem_limit_bytes=VMEM_LIMIT),
        name="in_proj",
    )(x, positions.reshape(B, S, 1), g.reshape(1, D), w_in_bf)


def _attn_kernel(lq1_ref, lk1_ref, lq2_ref, lk2_ref, g_ref, q_ref, k_ref, v_ref, o_ref, *, lambda_init):
    lam = (jnp.exp(jnp.sum(lq1_ref[...] * lk1_ref[...], axis=-1, keepdims=True))
           - jnp.exp(jnp.sum(lq2_ref[...] * lk2_ref[...], axis=-1, keepdims=True))
           + lambda_init)
    q = q_ref[0]
    k = k_ref[0]
    v = v_ref[0]
    lane = lax.broadcasted_iota(I32, q.shape, 1)
    zero = jnp.zeros_like(q)
    q0 = jnp.where(lane < DIFF_HEAD_DIM, q, zero)
    q1 = jnp.where(lane >= DIFF_HEAD_DIM, q, zero)
    nt = (((1,), (1,)), ((), ()))
    s0 = lax.dot_general(q0, k, nt, preferred_element_type=F32)
    s1 = lax.dot_general(q1, k, nt, preferred_element_type=F32)
    p0 = jnp.exp2(s0 - jnp.max(s0, axis=-1, keepdims=True))
    p1 = jnp.exp2(s1 - jnp.max(s1, axis=-1, keepdims=True))
    l0 = jnp.sum(p0, axis=-1, keepdims=True)
    l1 = jnp.sum(p1, axis=-1, keepdims=True)
    w = (p0 - p1 * (lam * l0 / l1)).astype(BF16)
    o = jnp.dot(w, v, preferred_element_type=F32) / l0
    o_ref[0] = (_rms(o, g_ref[...]) * (1.0 - lambda_init)).astype(BF16)


def _diff_attn(q, k, v, lq1, lk1, lq2, lk2, g, lambda_init, tq):
    B, S, W = q.shape
    H = W // DIFF_V_DIM
    vec = lambda b, h, i: (0, 0)
    return pl.pallas_call(
        functools.partial(_attn_kernel, lambda_init=lambda_init),
        grid=(B, H, S // tq),
        in_specs=[
            pl.BlockSpec((1, DIFF_HEAD_DIM), vec),
            pl.BlockSpec((1, DIFF_HEAD_DIM), vec),
            pl.BlockSpec((1, DIFF_HEAD_DIM), vec),
            pl.BlockSpec((1, DIFF_HEAD_DIM), vec),
            pl.BlockSpec((1, DIFF_V_DIM), vec),
            pl.BlockSpec((1, tq, DIFF_V_DIM), lambda b, h, i: (b, i, h)),
            pl.BlockSpec((1, S, DIFF_V_DIM), lambda b, h, i: (b, 0, h)),
            pl.BlockSpec((1, S, DIFF_V_DIM), lambda b, h, i: (b, 0, h)),
        ],
        out_specs=pl.BlockSpec((1, tq, DIFF_V_DIM), lambda b, h, i: (b, i, h)),
        out_shape=jax.ShapeDtypeStruct((B, S, W), BF16),
        compiler_params=pltpu.CompilerParams(
            dimension_semantics=("parallel", "parallel", "parallel"), vmem_limit_bytes=VMEM_LIMIT),
        name="diff_attn",
    )(lq1.reshape(1, -1), lk1.reshape(1, -1), lq2.reshape(1, -1), lk2.reshape(1, -1),
      g.reshape(1, -1), q, k, v)


def _lru_kernel(u_ref, gate_ref, cw_ref, cb_ref, w_ref, bias_ref, lam_ref, g_ref, out_ref, hf_ref, *, tc):
    S = u_ref.shape[1]
    C = u_ref.shape[2]
    nc = S // tc
    halo = SUBLANES
    neg_lam = -lam_ref[...]
    sp = jnp.maximum(neg_lam, 0.0) + jnp.log(1.0 + jnp.exp(-jnp.abs(neg_lam)))
    row = lax.broadcasted_iota(I32, (tc, 1), 0)
    conv_taps = cw_ref.shape[0]
    conv_left = 2

    def gates(c, d):
        r0 = pl.multiple_of(c * tc, tc)
        x = u_ref[0, pl.ds(r0, tc), :]
        prev = u_ref[0, pl.ds(pl.multiple_of(jnp.maximum(r0 - halo, 0), halo), halo), :]
        nxt = u_ref[0, pl.ds(pl.multiple_of(jnp.minimum(r0 + tc, S - halo), halo), halo), :]
        prev = jnp.where(c > 0, prev, 0.0)
        nxt = jnp.where(c < nc - 1, nxt, 0.0)
        win = jnp.concatenate([prev, x, nxt], axis=0)
        uc = cb_ref[...]
        for j in range(conv_taps):
            o = halo - conv_left + j
            uc = uc + cw_ref[j:j + 1, :] * win[o:o + tc, :]
        pre = (jnp.dot(uc.astype(BF16), w_ref[:, d * 2 * C:(d + 1) * 2 * C], preferred_element_type=F32)
               + bias_ref[:, d * 2 * C:(d + 1) * 2 * C])
        r = _sigmoid(pre[:, :C])
        i = _sigmoid(pre[:, C:])
        log_a = -LRU_C * r * sp[d:d + 1, :]
        a = jnp.exp(log_a)
        th = jnp.tanh(log_a)
        mult = jnp.sqrt(-2.0 * th / (1.0 - th))
        return a, mult * (i * uc)

    def scan(a, b, reverse):
        d = 1
        while d < tc:
            if reverse:
                a_s = pltpu.roll(a, tc - d, 0)
                b_s = pltpu.roll(b, tc - d, 0)
                m = row < tc - d
            else:
                a_s = pltpu.roll(a, d, 0)
                b_s = pltpu.roll(b, d, 0)
                m = row >= d
            a_s = jnp.where(m, a_s, 1.0)
            b_s = jnp.where(m, b_s, 0.0)
            b = a * b_s + b
            a = a * a_s
            d *= 2
        return a, b

    def fwd_body(c, h0):
        a, b = gates(c, 0)
        a, b = scan(a, b, False)
        h = a * h0 + b
        hf_ref[pl.ds(pl.multiple_of(c * tc, tc), tc), :] = h
        return h[tc - 1:tc, :]

    lax.fori_loop(0, nc, fwd_body, jnp.zeros((1, C), F32))

    def bwd_body(j, h0):
        c = nc - 1 - j
        r0 = pl.multiple_of(c * tc, tc)
        a, b = gates(c, 1)
        a, b = scan(a, b, True)
        h = a * h0 + b
        y = (hf_ref[pl.ds(r0, tc), :] + h) * _gelu_tanh(gate_ref[0, pl.ds(r0, tc), :].astype(F32))
        out_ref[0, pl.ds(r0, tc), :] = _rms(y, g_ref[...]).astype(BF16)
        return h[0:1, :]

    lax.fori_loop(0, nc, bwd_body, jnp.zeros((1, C), F32))


def _bi_rglru(u, gate, conv_w, conv_b, w_all_bf, bias_all, lru_lambda, g, tc):
    B, S, C = u.shape
    full = lambda b: (0, 0)
    return pl.pallas_call(
        functools.partial(_lru_kernel, tc=tc),
        grid=(B,),
        in_specs=[
            pl.BlockSpec((1, S, C), lambda b: (b, 0, 0)),
            pl.BlockSpec((1, S, C), lambda b: (b, 0, 0)),
            pl.BlockSpec(conv_w.shape, full),
            pl.BlockSpec((1, C), full),
            pl.BlockSpec(w_all_bf.shape, full),
            pl.BlockSpec(bias_all.shape, full),
            pl.BlockSpec(lru_lambda.shape, full),
            pl.BlockSpec((1, C), full),
        ],
        out_specs=pl.BlockSpec((1, S, C), lambda b: (b, 0, 0)),
        out_shape=jax.ShapeDtypeStruct((B, S, C), BF16),
        scratch_shapes=[pltpu.VMEM((S, C), F32)],
        compiler_params=pltpu.CompilerParams(
            dimension_semantics=("parallel",), vmem_limit_bytes=VMEM_LIMIT),
        name="bi_rglru",
    )(u, gate, conv_w, conv_b.reshape(1, C), w_all_bf, bias_all, lru_lambda, g.reshape(1, C))


def _topk_rows(s, k):
    n = s.shape[0]
    rid = lax.broadcasted_iota(I32, s.shape, 0)
    vals, ids = [], []
    for _ in range(k):
        m = jnp.max(s, axis=0, keepdims=True)
        sel = jnp.min(jnp.where(s == m, rid, n), axis=0, keepdims=True)
        vals.append(m)
        ids.append(sel)
        s = jnp.where(rid == sel, -jnp.inf, s)
    return jnp.concatenate(vals, axis=0), jnp.concatenate(ids, axis=0)


CAND_ROW_PIECES = 4


def _candidate_pieces(k):
    up = lambda n: -(-n // SUBLANES) * SUBLANES
    pieces = [("row", i, 0, up(k // (i + 1))) for i in range(CAND_ROW_PIECES)]
    for j in range(k // (CAND_ROW_PIECES + 1)):
        for i0 in range(0, k // (j + 1), SUBLANES):
            pieces.append(("col", j, i0, SUBLANES))
    return pieces


def _route_kernel(attn_ref, rec_ref, x_ref, wo_ref, g_ref, wq_ref, k1_ref, k2_ref,
                  h_ref, xn_ref, idx_ref, gate_ref):
    aw = attn_ref.shape[1]
    h = (x_ref[...]
         + jnp.dot(attn_ref[...], wo_ref[0:aw, :], preferred_element_type=F32)
         + jnp.dot(rec_ref[...], wo_ref[aw:, :], preferred_element_type=F32))
    h_ref[...] = h
    xn = _rms(h, g_ref[...])
    xn_ref[...] = xn
    q = jnp.dot(xn.astype(BF16), wq_ref[...], preferred_element_type=F32).astype(BF16)
    nt = (((1,), (1,)), ((), ()))
    s1 = lax.dot_general(k1_ref[...], q, nt, preferred_element_type=F32)
    s2 = lax.dot_general(k2_ref[...], q, nt, preferred_element_type=F32)
    k = PEER_TOPK
    tm = q.shape[0]
    pieces = _candidate_pieces(k)
    pos_parts, ok_parts = [], []
    for kind, fixed, start, length in pieces:
        r = lax.broadcasted_iota(I32, (length, tm), 0) + start
        i, j = (fixed, r) if kind == "row" else (r, fixed)
        pos_parts.append(i * k + j)
        ok = (i + 1) * (j + 1) <= k
        ok_parts.append(ok if kind == "row" else ok & (r >= CAND_ROW_PIECES))
    pos = jnp.concatenate(pos_parts, axis=0)
    ok = jnp.concatenate(ok_parts, axis=0)
    idx_rows, gate_rows = [], []
    for hd in range(PEER_HEADS):
        v1, i1 = _topk_rows(s1[hd * N_KEYS:(hd + 1) * N_KEYS, :], k)
        v2, i2 = _topk_rows(s2[hd * N_KEYS:(hd + 1) * N_KEYS, :], k)
        cand_parts, cidx_parts = [], []
        for kind, fixed, start, length in pieces:
            if kind == "row":
                cand_parts.append(v1[fixed:fixed + 1, :] + v2[start:start + length, :])
                cidx_parts.append(i1[fixed:fixed + 1, :] * N_KEYS + i2[start:start + length, :])
            else:
                cand_parts.append(v1[start:start + length, :] + v2[fixed:fixed + 1, :])
                cidx_parts.append(i1[start:start + length, :] * N_KEYS + i2[fixed:fixed + 1, :])
        cand = jnp.where(ok, jnp.concatenate(cand_parts, axis=0), -jnp.inf)
        cidx = jnp.concatenate(cidx_parts, axis=0)
        sc, ids = [], []
        for _ in range(k):
            m = jnp.max(cand, axis=0, keepdims=True)
            sel = jnp.min(jnp.where(cand == m, pos, k * k), axis=0, keepdims=True)
            hit = pos == sel
            sc.append(m)
            ids.append(jnp.max(jnp.where(hit, cidx, -1), axis=0, keepdims=True))
            cand = jnp.where(hit, -jnp.inf, cand)
        sc = jnp.concatenate(sc, axis=0)
        e = jnp.exp(sc - sc[0:1, :])
        gate_rows.append(e / jnp.sum(e, axis=0, keepdims=True))
        idx_rows.append(jnp.concatenate(ids, axis=0))
    idx_ref[...] = jnp.concatenate(idx_rows, axis=0)
    gate_ref[...] = jnp.concatenate(gate_rows, axis=0)


def _mix_route(attn, rec, x2, wo_bf, g, wq_bf, k1t_bf, k2t_bf, tm):
    T, D = x2.shape
    NK = PEER_HEADS * PEER_TOPK
    row = lambda i: (i, 0)
    full = lambda i: (0, 0)
    return pl.pallas_call(
        _route_kernel,
        grid=(T // tm,),
        in_specs=[
            pl.BlockSpec((tm, attn.shape[1]), row),
            pl.BlockSpec((tm, rec.shape[1]), row),
            pl.BlockSpec((tm, D), row),
            pl.BlockSpec(wo_bf.shape, full),
            pl.BlockSpec((1, D), full),
            pl.BlockSpec(wq_bf.shape, full),
            pl.BlockSpec(k1t_bf.shape, full),
            pl.BlockSpec(k2t_bf.shape, full),
        ],
        out_specs=[
            pl.BlockSpec((tm, D), row),
            pl.BlockSpec((tm, D), row),
            pl.BlockSpec((NK, tm), lambda i: (0, i)),
            pl.BlockSpec((NK, tm), lambda i: (0, i)),
        ],
        out_shape=[
            jax.ShapeDtypeStruct((T, D), F32),
            jax.ShapeDtypeStruct((T, D), F32),
            jax.ShapeDtypeStruct((NK, T), I32),
            jax.ShapeDtypeStruct((NK, T), F32),
        ],
        compiler_params=pltpu.CompilerParams(
            dimension_semantics=("parallel",), vmem_limit_bytes=VMEM_LIMIT),
        name="mix_route",
    )(attn, rec, x2, wo_bf, g.reshape(1, D), wq_bf, k1t_bf, k2t_bf)


ROW_CHUNKS = 8
TOKEN_UNROLL = 8
DMA_PRIORITIES = 2
PEER_RING = 4
PEER_PREFETCH = 2


def _peer_kernel(idx_ref, idx_next_ref, xn_ref, gate_ref, uv_hbm, o_ref, *scratch, tt, nk):
    bufs = scratch[:PEER_RING]
    lg_ref, act_ref, sems = scratch[PEER_RING:]
    g = pl.program_id(0)
    n = pl.num_programs(0)
    nt = (((1,), (1,)), ((), ()))
    ch = ROW_CHUNKS

    def row_copy(ids_ref, q, t, k):
        e = ids_ref[0, 0, (q * tt + t) * nk + k]
        return pltpu.make_async_copy(uv_hbm.at[e], bufs[q].at[t, :, pl.ds(k, 1), :], sems.at[q])

    def wait_group(q):
        pltpu.make_async_copy(bufs[(q + 1) % PEER_RING], bufs[q], sems.at[q]).wait()

    @pl.when(g == 0)
    def _():
        for q in range(PEER_PREFETCH):
            def body(t, carry, q=q):
                for k in range(nk):
                    row_copy(idx_ref, q, t, k).start(priority=k % DMA_PRIORITIES)
                return carry
            lax.fori_loop(0, tt, body, 0)

    def group(q):
        buf = bufs[q]
        row0 = q * tt
        ahead = q + PEER_PREFETCH
        ids_ahead = idx_ref if ahead < PEER_RING else idx_next_ref

        def issue(t, k0, k1):
            for k in range(k0, k1):
                row_copy(ids_ahead, ahead % PEER_RING, t, k).start(priority=k % DMA_PRIORITIES)

        def phase_u(tb, carry):
            for j in range(TOKEN_UNROLL):
                t = tb * TOKEN_UNROLL + j
                issue(t, 0, nk // 2)
                xt = xn_ref[row0 + t]
                acc = buf[t, 0] * xt[0:1, :]
                for s in range(1, ch):
                    acc = acc + buf[t, s] * xt[s:s + 1, :]
                hi = acc.astype(BF16)
                lo = (acc - hi.astype(F32)).astype(BF16)
                ones = jnp.ones((SUBLANES, LANES), BF16)
                red = (lax.dot_general(ones, hi, nt, preferred_element_type=F32)
                       + lax.dot_general(ones, lo, nt, preferred_element_type=F32))
                lg_ref[t] = red[0:1, :]
            return carry
        lax.fori_loop(0, tt // TOKEN_UNROLL, phase_u, 0)
        act_ref[...] = _gelu_tanh(lg_ref[...]) * gate_ref[row0:row0 + tt]

        def phase_v(tb, carry):
            for j in range(TOKEN_UNROLL):
                t = tb * TOKEN_UNROLL + j
                issue(t, nk // 2, nk)
                act = act_ref[t].astype(BF16)
                cols = [jnp.dot(act, buf[t, ch + s].astype(BF16), preferred_element_type=F32)
                        for s in range(ch)]
                o_ref[row0 + t] = jnp.concatenate(cols, axis=0)
            return carry
        lax.fori_loop(0, tt // TOKEN_UNROLL, phase_v, 0)

    for q in range(PEER_RING):
        wait_group(q)
        group(q)

    @pl.when(g == n - 1)
    def _():
        for q in range(PEER_PREFETCH):
            wait_group(q)


def _peer(idx, gate, xn3, uv4, tt):
    T, ch, _ = xn3.shape
    nk = idx.shape[1]
    step = PEER_RING * tt
    n = T // step
    idx3 = idx.reshape(n, 1, step * nk)
    tok = lambda i: (i, 0, 0)
    return pl.pallas_call(
        functools.partial(_peer_kernel, tt=tt, nk=nk),
        grid=(n,),
        in_specs=[
            pl.BlockSpec((1, 1, step * nk), tok, memory_space=pltpu.SMEM),
            pl.BlockSpec((1, 1, step * nk), lambda i: (jnp.minimum(i + 1, n - 1), 0, 0), memory_space=pltpu.SMEM),
            pl.BlockSpec((step, ch, LANES), tok),
            pl.BlockSpec((step, 1, nk), tok),
            pl.BlockSpec(memory_space=pl.ANY),
        ],
        out_specs=pl.BlockSpec((step, ch, LANES), tok),
        out_shape=jax.ShapeDtypeStruct((T, ch, LANES), F32),
        scratch_shapes=(
            [pltpu.VMEM((tt, 2 * ch, nk, LANES), F32) for _ in range(PEER_RING)]
            + [pltpu.VMEM((tt, 1, nk), F32), pltpu.VMEM((tt, 1, nk), F32), pltpu.SemaphoreType.DMA((PEER_RING,))]),
        compiler_params=pltpu.CompilerParams(
            dimension_semantics=("arbitrary",), vmem_limit_bytes=VMEM_LIMIT),
        name="peer",
    )(idx3, idx3, xn3, gate.reshape(T, 1, nk), uv4)


def _ple_kernel(h_ref, po_ref, p_ref, g3_ref, wg_ref, wp_ref, gf_ref, o_ref, *, final):
    h = h_ref[...] + po_ref[...]
    xn = _rms(h, g3_ref[...]).astype(BF16)
    gate = _sigmoid(jnp.dot(xn, wg_ref[...], preferred_element_type=F32))
    proj = jnp.dot(p_ref[...].astype(BF16), wp_ref[...], preferred_element_type=F32)
    h = h + gate * proj
    o_ref[...] = _rms(h, gf_ref[...]) if final else h


def _ple_out(h1, peer_out, p2, g3, wg_bf, wp_bf, gf, final, tm):
    T, D = h1.shape
    row = lambda i: (i, 0)
    full = lambda i: (0, 0)
    return pl.pallas_call(
        functools.partial(_ple_kernel, final=final),
        grid=(T // tm,),
        in_specs=[
            pl.BlockSpec((tm, D), row),
            pl.BlockSpec((tm, D), row),
            pl.BlockSpec((tm, p2.shape[1]), row),
            pl.BlockSpec((1, D), full),
            pl.BlockSpec(wg_bf.shape, full),
            pl.BlockSpec(wp_bf.shape, full),
            pl.BlockSpec((1, D), full),
        ],
        out_specs=pl.BlockSpec((tm, D), row),
        out_shape=jax.ShapeDtypeStruct((T, D), F32),
        compiler_params=pltpu.CompilerParams(
            dimension_semantics=("parallel",), vmem_limit_bytes=VMEM_LIMIT),
        name="ple_out",
    )(h1, peer_out, p2, g3.reshape(1, D), wg_bf, wp_bf, gf.reshape(1, D))


def _block_diag(w):
    nb, bw, _ = w.shape
    eye = jnp.eye(nb, dtype=w.dtype)
    return (eye[:, None, :, None] * w[:, :, None, :]).reshape(nb * bw, nb * bw)


def _key_matrix(keys, half):
    z = jnp.zeros_like(keys)
    blk = jnp.concatenate([keys, z] if half == 0 else [z, keys], axis=1)
    return jnp.kron(jnp.eye(PEER_HEADS, dtype=keys.dtype), blk)


def kernel(x, p, positions, norm_mix_g, w_in, lambda_q1, lambda_k1, lambda_q2, lambda_k2, diff_norm_g, conv_w, conv_b, lru_wa, lru_ba, lru_wx, lru_bx, lru_lambda, lru_norm_g, w_out, norm_ffn_g, peer_wq, peer_keys1, peer_keys2, peer_u, peer_v, norm_ple_g, ple_w_gate, ple_w_proj, final_norm_g):
    B, S, D = x.shape
    T = B * S
    depth = w_in.shape[0]
    h = x
    for i in range(depth):
        lambda_init = 0.8 - 0.6 * math.exp(-0.3 * i)
        q, k, v, u, gate = _in_proj(h, positions, norm_mix_g[i], w_in[i].astype(BF16), tm=512)
        attn = _diff_attn(q, k, v, lambda_q1[i], lambda_k1[i], lambda_q2[i], lambda_k2[i],
                          diff_norm_g[i], lambda_init, tq=256)
        w_all = jnp.concatenate([_block_diag(lru_wa[i, 0]), _block_diag(lru_wx[i, 0]),
                                 _block_diag(lru_wa[i, 1]), _block_diag(lru_wx[i, 1])], axis=1).astype(BF16)
        bias_all = jnp.concatenate([lru_ba[i, 0], lru_bx[i, 0], lru_ba[i, 1], lru_bx[i, 1]]).reshape(1, -1)
        rec = _bi_rglru(u, gate, conv_w[i], conv_b[i], w_all, bias_all, lru_lambda[i], lru_norm_g[i], tc=256)
        h1, xn2, idx_t, gate_t = _mix_route(
            attn.reshape(T, -1), rec.reshape(T, -1), h.reshape(T, D), w_out[i].astype(BF16), norm_ffn_g[i],
            peer_wq[i].astype(BF16), _key_matrix(peer_keys1[i], 0).astype(BF16),
            _key_matrix(peer_keys2[i], 1).astype(BF16), tm=256)
        n_exp = peer_u.shape[1]
        uv4 = jnp.concatenate([peer_u[i].reshape(n_exp, ROW_CHUNKS, 1, LANES),
                               peer_v[i].reshape(n_exp, ROW_CHUNKS, 1, LANES)], axis=1)
        peer_out = _peer(idx_t.T, gate_t.T, xn2.reshape(T, ROW_CHUNKS, LANES), uv4, tt=8).reshape(T, D)
        h = _ple_out(h1, peer_out, p[i].reshape(T, -1), norm_ple_g[i], ple_w_gate[i].astype(BF16),
                     ple_w_proj[i].astype(BF16), final_norm_g, final=(i == depth - 1), tm=512)
        h = h.reshape(B, S, D)
    return h
```

```python
import functools
import math

import jax
import jax.numpy as jnp
from jax import lax
from jax.experimental import pallas as pl
from jax.experimental.pallas import tpu as pltpu
from jax.experimental.pallas import tpu_sc as plsc

F32 = jnp.float32
BF16 = jnp.bfloat16
I32 = jnp.int32

EPS = 1e-6
DIFF_HEAD_DIM = 64
DIFF_V_DIM = 128
N_DIFF_HEADS = 4
ROPE_DIM = 16
ROPE_THETA = 500000.0
LRU_WIDTH = 512
LRU_C = 8.0
N_KEYS = 128
PEER_HEADS = 8
PEER_TOPK = 16
HALF_KEY = 64
LOG2_E = math.log2(math.e)
LANES = 128
SUBLANES = 8
VMEM_LIMIT = 56 * 1024 * 1024


def _rms(x, g):
    return x * lax.rsqrt(jnp.mean(x * x, axis=-1, keepdims=True) + EPS) * g


def _gelu_tanh(x):
    return 0.5 * x * (1.0 + jnp.tanh(math.sqrt(2.0 / math.pi) * (x + 0.044715 * (x * x * x))))


def _sigmoid(x):
    return 1.0 / (1.0 + jnp.exp(-x))


def _inproj_kernel(x_ref, pos_ref, g_ref, w_ref, q_ref, k_ref, v_ref, u_ref, gate_ref):
    x = x_ref[0]
    xn = _rms(x, g_ref[...]).astype(BF16)
    pos = pos_ref[0].astype(F32)
    lane = lax.broadcasted_iota(I32, (1, LANES), 1)
    p = lane & (DIFF_HEAD_DIM - 1)
    freq = (p & (ROPE_DIM // 2 - 1)).astype(F32)
    inv_freq = jnp.exp(freq * (-2.0 / ROPE_DIM * math.log(ROPE_THETA)))
    ang = pos * inv_freq
    cs = jnp.cos(ang)
    sn = jnp.sin(ang)
    half = ROPE_DIM // 2
    c_mul = jnp.where(p < ROPE_DIM, cs, 1.0)
    s_up = jnp.where(p < half, -sn, 0.0)
    s_dn = jnp.where((p >= half) & (p < ROPE_DIM), sn, 0.0)

    def rope(t):
        return t * c_mul + pltpu.roll(t, LANES - half, 1) * s_up + pltpu.roll(t, half, 1) * s_dn

    nq = q_ref.shape[-1]
    pq = jnp.dot(xn, w_ref[:, 0:nq], preferred_element_type=F32)
    pk = jnp.dot(xn, w_ref[:, nq:2 * nq], preferred_element_type=F32)
    scale = DIFF_HEAD_DIM ** -0.5 * LOG2_E
    for j in range(nq // LANES):
        sl = slice(j * LANES, (j + 1) * LANES)
        q_ref[0, :, sl] = (rope(pq[:, sl]) * scale).astype(BF16)
        k_ref[0, :, sl] = rope(pk[:, sl]).astype(BF16)
    v_ref[0] = jnp.dot(xn, w_ref[:, 2 * nq:3 * nq], preferred_element_type=F32).astype(BF16)
    u_ref[0] = jnp.dot(xn, w_ref[:, 3 * nq:3 * nq + LRU_WIDTH], preferred_element_type=F32)
    gate_ref[0] = jnp.dot(xn, w_ref[:, 3 * nq + LRU_WIDTH:3 * nq + 2 * LRU_WIDTH],
                          preferred_element_type=F32).astype(BF16)


def _in_proj(x, positions, g, w_in_bf, tm):
    B, S, D = x.shape
    nq = N_DIFF_HEADS * 2 * DIFF_HEAD_DIM
    ncols = w_in_bf.shape[1]
    row = lambda b, i: (b, i, 0)
    return pl.pallas_call(
        _inproj_kernel,
        grid=(B, S // tm),
        in_specs=[
            pl.BlockSpec((1, tm, D), row),
            pl.BlockSpec((1, tm, 1), row),
            pl.BlockSpec((1, D), lambda b, i: (0, 0)),
            pl.BlockSpec((D, ncols), lambda b, i: (0, 0)),
        ],
        out_specs=[
            pl.BlockSpec((1, tm, nq), row),
            pl.BlockSpec((1, tm, nq), row),
            pl.BlockSpec((1, tm, nq), row),
            pl.BlockSpec((1, tm, LRU_WIDTH), row),
            pl.BlockSpec((1, tm, LRU_WIDTH), row),
        ],
        out_shape=[
            jax.ShapeDtypeStruct((B, S, nq), BF16),
            jax.ShapeDtypeStruct((B, S, nq), BF16),
            jax.ShapeDtypeStruct((B, S, nq), BF16),
            jax.ShapeDtypeStruct((B, S, LRU_WIDTH), F32),
            jax.ShapeDtypeStruct((B, S, LRU_WIDTH), BF16),
        ],
        compiler_params=pltpu.CompilerParams(
            dimension_semantics=("parallel", "parallel"), vmem_limit_bytes=VMEM_LIMIT),
        name="in_proj",
    )(x, positions.reshape(B, S, 1), g.reshape(1, D), w_in_bf)


def _attn_kernel(lq1_ref, lk1_ref, lq2_ref, lk2_ref, g_ref, q_ref, k_ref, v_ref, o_ref, *, lambda_init):
    lam = (jnp.exp(jnp.sum(lq1_ref[...] * lk1_ref[...], axis=-1, keepdims=True))
           - jnp.exp(jnp.sum(lq2_ref[...] * lk2_ref[...], axis=-1, keepdims=True))
           + lambda_init)
    q = q_ref[0]
    k = k_ref[0]
    v = v_ref[0]
    lane = lax.broadcasted_iota(I32, q.shape, 1)
    zero = jnp.zeros_like(q)
    q0 = jnp.where(lane < DIFF_HEAD_DIM, q, zero)
    q1 = jnp.where(lane >= DIFF_HEAD_DIM, q, zero)
    nt = (((1,), (1,)), ((), ()))
    s0 = lax.dot_general(q0, k, nt, preferred_element_type=F32)
    s1 = lax.dot_general(q1, k, nt, preferred_element_type=F32)
    p0 = jnp.exp2(s0 - jnp.max(s0, axis=-1, keepdims=True))
    p1 = jnp.exp2(s1 - jnp.max(s1, axis=-1, keepdims=True))
    l0 = jnp.sum(p0, axis=-1, keepdims=True)
    l1 = jnp.sum(p1, axis=-1, keepdims=True)
    w = (p0 - p1 * (lam * l0 / l1)).astype(BF16)
    o = jnp.dot(w, v, preferred_element_type=F32) / l0
    o_ref[0] = (_rms(o, g_ref[...]) * (1.0 - lambda_init)).astype(BF16)


def _diff_attn(q, k, v, lq1, lk1, lq2, lk2, g, lambda_init, tq):
    B, S, W = q.shape
    H = W // DIFF_V_DIM
    vec = lambda b, h, i: (0, 0)
    return pl.pallas_call(
        functools.partial(_attn_kernel, lambda_init=lambda_init),
        grid=(B, H, S // tq),
        in_specs=[
            pl.BlockSpec((1, DIFF_HEAD_DIM), vec),
            pl.BlockSpec((1, DIFF_HEAD_DIM), vec),
            pl.BlockSpec((1, DIFF_HEAD_DIM), vec),
            pl.BlockSpec((1, DIFF_HEAD_DIM), vec),
            pl.BlockSpec((1, DIFF_V_DIM), vec),
            pl.BlockSpec((1, tq, DIFF_V_DIM), lambda b, h, i: (b, i, h)),
            pl.BlockSpec((1, S, DIFF_V_DIM), lambda b, h, i: (b, 0, h)),
            pl.BlockSpec((1, S, DIFF_V_DIM), lambda b, h, i: (b, 0, h)),
        ],
        out_specs=pl.BlockSpec((1, tq, DIFF_V_DIM), lambda b, h, i: (b, i, h)),
        out_shape=jax.ShapeDtypeStruct((B, S, W), BF16),
        compiler_params=pltpu.CompilerParams(
            dimension_semantics=("parallel", "parallel", "parallel"), vmem_limit_bytes=VMEM_LIMIT),
        name="diff_attn",
    )(lq1.reshape(1, -1), lk1.reshape(1, -1), lq2.reshape(1, -1), lk2.reshape(1, -1),
      g.reshape(1, -1), q, k, v)


def _lru_kernel(u_ref, gate_ref, cw_ref, cb_ref, w_ref, bias_ref, lam_ref, g_ref, out_ref, hf_ref, *, tc):
    S = u_ref.shape[1]
    C = u_ref.shape[2]
    nc = S // tc
    halo = SUBLANES
    neg_lam = -lam_ref[...]
    sp = jnp.maximum(neg_lam, 0.0) + jnp.log(1.0 + jnp.exp(-jnp.abs(neg_lam)))
    row = lax.broadcasted_iota(I32, (tc, 1), 0)
    conv_taps = cw_ref.shape[0]
    conv_left = 2

    def gates(c, d):
        r0 = pl.multiple_of(c * tc, tc)
        x = u_ref[0, pl.ds(r0, tc), :]
        prev = u_ref[0, pl.ds(pl.multiple_of(jnp.maximum(r0 - halo, 0), halo), halo), :]
        nxt = u_ref[0, pl.ds(pl.multiple_of(jnp.minimum(r0 + tc, S - halo), halo), halo), :]
        prev = jnp.where(c > 0, prev, 0.0)
        nxt = jnp.where(c < nc - 1, nxt, 0.0)
        win = jnp.concatenate([prev, x, nxt], axis=0)
        uc = cb_ref[...]
        for j in range(conv_taps):
            o = halo - conv_left + j
            uc = uc + cw_ref[j:j + 1, :] * win[o:o + tc, :]
        pre = (jnp.dot(uc.astype(BF16), w_ref[:, d * 2 * C:(d + 1) * 2 * C], preferred_element_type=F32)
               + bias_ref[:, d * 2 * C:(d + 1) * 2 * C])
        r = _sigmoid(pre[:, :C])
        i = _sigmoid(pre[:, C:])
        log_a = -LRU_C * r * sp[d:d + 1, :]
        a = jnp.exp(log_a)
        th = jnp.tanh(log_a)
        mult = jnp.sqrt(-2.0 * th / (1.0 - th))
        return a, mult * (i * uc)

    def scan(a, b, reverse):
        d = 1
        while d < tc:
            if reverse:
                a_s = pltpu.roll(a, tc - d, 0)
                b_s = pltpu.roll(b, tc - d, 0)
                m = row < tc - d
            else:
                a_s = pltpu.roll(a, d, 0)
                b_s = pltpu.roll(b, d, 0)
                m = row >= d
            a_s = jnp.where(m, a_s, 1.0)
            b_s = jnp.where(m, b_s, 0.0)
            b = a * b_s + b
            a = a * a_s
            d *= 2
        return a, b

    def fwd_body(c, h0):
        a, b = gates(c, 0)
        a, b = scan(a, b, False)
        h = a * h0 + b
        hf_ref[pl.ds(pl.multiple_of(c * tc, tc), tc), :] = h
        return h[tc - 1:tc, :]

    lax.fori_loop(0, nc, fwd_body, jnp.zeros((1, C), F32))

    def bwd_body(j, h0):
        c = nc - 1 - j
        r0 = pl.multiple_of(c * tc, tc)
        a, b = gates(c, 1)
        a, b = scan(a, b, True)
        h = a * h0 + b
        y = (hf_ref[pl.ds(r0, tc), :] + h) * _gelu_tanh(gate_ref[0, pl.ds(r0, tc), :].astype(F32))
        out_ref[0, pl.ds(r0, tc), :] = _rms(y, g_ref[...]).astype(BF16)
        return h[0:1, :]

    lax.fori_loop(0, nc, bwd_body, jnp.zeros((1, C), F32))


def _bi_rglru(u, gate, conv_w, conv_b, w_all_bf, bias_all, lru_lambda, g, tc):
    B, S, C = u.shape
    full = lambda b: (0, 0)
    return pl.pallas_call(
        functools.partial(_lru_kernel, tc=tc),
        grid=(B,),
        in_specs=[
            pl.BlockSpec((1, S, C), lambda b: (b, 0, 0)),
            pl.BlockSpec((1, S, C), lambda b: (b, 0, 0)),
            pl.BlockSpec(conv_w.shape, full),
            pl.BlockSpec((1, C), full),
            pl.BlockSpec(w_all_bf.shape, full),
            pl.BlockSpec(bias_all.shape, full),
            pl.BlockSpec(lru_lambda.shape, full),
            pl.BlockSpec((1, C), full),
        ],
        out_specs=pl.BlockSpec((1, S, C), lambda b: (b, 0, 0)),
        out_shape=jax.ShapeDtypeStruct((B, S, C), BF16),
        scratch_shapes=[pltpu.VMEM((S, C), F32)],
        compiler_params=pltpu.CompilerParams(
            dimension_semantics=("parallel",), vmem_limit_bytes=VMEM_LIMIT),
        name="bi_rglru",
    )(u, gate, conv_w, conv_b.reshape(1, C), w_all_bf, bias_all, lru_lambda, g.reshape(1, C))


def _topk_rows(s, k):
    n = s.shape[0]
    rid = lax.broadcasted_iota(I32, s.shape, 0)
    vals, ids = [], []
    for _ in range(k):
        m = jnp.max(s, axis=0, keepdims=True)
        sel = jnp.min(jnp.where(s == m, rid, n), axis=0, keepdims=True)
        vals.append(m)
        ids.append(sel)
        s = jnp.where(rid == sel, -jnp.inf, s)
    return jnp.concatenate(vals, axis=0), jnp.concatenate(ids, axis=0)


CAND_ROW_PIECES = 4


def _candidate_pieces(k):
    up = lambda n: -(-n // SUBLANES) * SUBLANES
    pieces = [("row", i, 0, up(k // (i + 1))) for i in range(CAND_ROW_PIECES)]
    for j in range(k // (CAND_ROW_PIECES + 1)):
        for i0 in range(0, k // (j + 1), SUBLANES):
            pieces.append(("col", j, i0, SUBLANES))
    return pieces


def _route_kernel(attn_ref, rec_ref, x_ref, wo_ref, g_ref, wq_ref, k1_ref, k2_ref,
                  h_ref, xn_ref, idx_ref, gate_ref):
    aw = attn_ref.shape[1]
    h = (x_ref[...]
         + jnp.dot(attn_ref[...], wo_ref[0:aw, :], preferred_element_type=F32)
         + jnp.dot(rec_ref[...], wo_ref[aw:, :], preferred_element_type=F32))
    h_ref[...] = h
    xn = _rms(h, g_ref[...])
    xn_ref[...] = xn
    q = jnp.dot(xn.astype(BF16), wq_ref[...], preferred_element_type=F32).astype(BF16)
    nt = (((1,), (1,)), ((), ()))
    s1 = lax.dot_general(k1_ref[...], q, nt, preferred_element_type=F32)
    s2 = lax.dot_general(k2_ref[...], q, nt, preferred_element_type=F32)
    k = PEER_TOPK
    tm = q.shape[0]
    pieces = _candidate_pieces(k)
    pos_parts, ok_parts = [], []
    for kind, fixed, start, length in pieces:
        r = lax.broadcasted_iota(I32, (length, tm), 0) + start
        i, j = (fixed, r) if kind == "row" else (r, fixed)
        pos_parts.append(i * k + j)
        ok = (i + 1) * (j + 1) <= k
        ok_parts.append(ok if kind == "row" else ok & (r >= CAND_ROW_PIECES))
    pos = jnp.concatenate(pos_parts, axis=0)
    ok = jnp.concatenate(ok_parts, axis=0)
    idx_rows, gate_rows = [], []
    for hd in range(PEER_HEADS):
        v1, i1 = _topk_rows(s1[hd * N_KEYS:(hd + 1) * N_KEYS, :], k)
        v2, i2 = _topk_rows(s2[hd * N_KEYS:(hd + 1) * N_KEYS, :], k)
        cand_parts, cidx_parts = [], []
        for kind, fixed, start, length in pieces:
            if kind == "row":
                cand_parts.append(v1[fixed:fixed + 1, :] + v2[start:start + length, :])
                cidx_parts.append(i1[fixed:fixed + 1, :] * N_KEYS + i2[start:start + length, :])
            else:
                cand_parts.append(v1[start:start + length, :] + v2[fixed:fixed + 1, :])
                cidx_parts.append(i1[start:start + length, :] * N_KEYS + i2[fixed:fixed + 1, :])
        cand = jnp.where(ok, jnp.concatenate(cand_parts, axis=0), -jnp.inf)
        cidx = jnp.concatenate(cidx_parts, axis=0)
        sc, ids = [], []
        for _ in range(k):
            m = jnp.max(cand, axis=0, keepdims=True)
            sel = jnp.min(jnp.where(cand == m, pos, k * k), axis=0, keepdims=True)
            hit = pos == sel
            sc.append(m)
            ids.append(jnp.max(jnp.where(hit, cidx, -1), axis=0, keepdims=True))
            cand = jnp.where(hit, -jnp.inf, cand)
        sc = jnp.concatenate(sc, axis=0)
        e = jnp.exp(sc - sc[0:1, :])
        gate_rows.append(e / jnp.sum(e, axis=0, keepdims=True))
        idx_rows.append(jnp.concatenate(ids, axis=0))
    idx_ref[...] = jnp.concatenate(idx_rows, axis=0)
    gate_ref[...] = jnp.concatenate(gate_rows, axis=0)


def _mix_route(attn, rec, x2, wo_bf, g, wq_bf, k1t_bf, k2t_bf, tm):
    T, D = x2.shape
    NK = PEER_HEADS * PEER_TOPK
    row = lambda i: (i, 0)
    full = lambda i: (0, 0)
    return pl.pallas_call(
        _route_kernel,
        grid=(T // tm,),
        in_specs=[
            pl.BlockSpec((tm, attn.shape[1]), row),
            pl.BlockSpec((tm, rec.shape[1]), row),
            pl.BlockSpec((tm, D), row),
            pl.BlockSpec(wo_bf.shape, full),
            pl.BlockSpec((1, D), full),
            pl.BlockSpec(wq_bf.shape, full),
            pl.BlockSpec(k1t_bf.shape, full),
            pl.BlockSpec(k2t_bf.shape, full),
        ],
        out_specs=[
            pl.BlockSpec((tm, D), row),
            pl.BlockSpec((tm, D), row),
            pl.BlockSpec((NK, tm), lambda i: (0, i)),
            pl.BlockSpec((NK, tm), lambda i: (0, i)),
        ],
        out_shape=[
            jax.ShapeDtypeStruct((T, D), F32),
            jax.ShapeDtypeStruct((T, D), F32),
            jax.ShapeDtypeStruct((NK, T), I32),
            jax.ShapeDtypeStruct((NK, T), F32),
        ],
        compiler_params=pltpu.CompilerParams(
            dimension_semantics=("parallel",), vmem_limit_bytes=VMEM_LIMIT),
        name="mix_route",
    )(attn, rec, x2, wo_bf, g.reshape(1, D), wq_bf, k1t_bf, k2t_bf)


ROW_CHUNKS = 8
TOKEN_UNROLL = 2
DMA_PRIORITIES = 2
PEER_RING = 4
PEER_PREFETCH = 2


def _peer_kernel(idx_ref, idx_next_ref, xn_ref, gate_ref, uv_hbm, o_ref, *scratch, tt, nk):
    bufs = scratch[:PEER_RING]
    lg_ref, act_ref, sems = scratch[PEER_RING:]
    g = pl.program_id(0)
    n = pl.num_programs(0)
    nt = (((1,), (1,)), ((), ()))
    ch = ROW_CHUNKS

    def row_copy(ids_ref, q, t, k):
        e = ids_ref[0, 0, (q * tt + t) * nk + k]
        return pltpu.make_async_copy(uv_hbm.at[e], bufs[q].at[t, :, pl.ds(k, 1), :], sems.at[q])

    def wait_group(q):
        pltpu.make_async_copy(bufs[(q + 1) % PEER_RING], bufs[q], sems.at[q]).wait()

    @pl.when(g == 0)
    def _():
        for q in range(PEER_PREFETCH):
            def body(t, carry, q=q):
                for k in range(nk):
                    row_copy(idx_ref, q, t, k).start(priority=k % DMA_PRIORITIES)
                return carry
            lax.fori_loop(0, tt, body, 0)

    def group(q):
        buf = bufs[q]
        row0 = q * tt
        ahead = q + PEER_PREFETCH
        ids_ahead = idx_ref if ahead < PEER_RING else idx_next_ref

        def issue(t, k0, k1):
            for k in range(k0, k1):
                row_copy(ids_ahead, ahead % PEER_RING, t, k).start(priority=k % DMA_PRIORITIES)

        def phase_u(tb, carry):
            for j in range(TOKEN_UNROLL):
                t = tb * TOKEN_UNROLL + j
                issue(t, 0, nk // 2)
                xt = xn_ref[row0 + t]
                acc = buf[t, 0] * xt[0:1, :]
                for s in range(1, ch):
                    acc = acc + buf[t, s] * xt[s:s + 1, :]
                hi = acc.astype(BF16)
                lo = (acc - hi.astype(F32)).astype(BF16)
                ones = jnp.ones((SUBLANES, LANES), BF16)
                red = (lax.dot_general(ones, hi, nt, preferred_element_type=F32)
                       + lax.dot_general(ones, lo, nt, preferred_element_type=F32))
                lg_ref[t] = red[0:1, :]
            return carry
        lax.fori_loop(0, tt // TOKEN_UNROLL, phase_u, 0)
        act_ref[...] = _gelu_tanh(lg_ref[...]) * gate_ref[row0:row0 + tt]

        def phase_v(tb, carry):
            for j in range(TOKEN_UNROLL):
                t = tb * TOKEN_UNROLL + j
                issue(t, nk // 2, nk)
                act = act_ref[t].astype(BF16)
                cols = [jnp.dot(act, buf[t, ch + s].astype(BF16), preferred_element_type=F32)
                        for s in range(ch)]
                o_ref[row0 + t] = jnp.concatenate(cols, axis=0)
            return carry
        lax.fori_loop(0, tt // TOKEN_UNROLL, phase_v, 0)

    for q in range(PEER_RING):
        wait_group(q)
        group(q)

    @pl.when(g == n - 1)
    def _():
        for q in range(PEER_PREFETCH):
            wait_group(q)


def _peer(idx, gate, xn3, uv4, n_tokens, tt):
    T, ch, _ = xn3.shape
    nk = idx.shape[1]
    step = PEER_RING * tt
    n = n_tokens // step
    idx3 = idx.reshape(T // step, 1, step * nk)
    tok = lambda i: (i, 0, 0)
    return pl.pallas_call(
        functools.partial(_peer_kernel, tt=tt, nk=nk),
        grid=(n,),
        in_specs=[
            pl.BlockSpec((1, 1, step * nk), tok, memory_space=pltpu.SMEM),
            pl.BlockSpec((1, 1, step * nk), lambda i: (jnp.minimum(i + 1, n - 1), 0, 0), memory_space=pltpu.SMEM),
            pl.BlockSpec((step, ch, LANES), tok),
            pl.BlockSpec((step, 1, nk), tok),
            pl.BlockSpec(memory_space=pl.ANY),
        ],
        out_specs=pl.BlockSpec((step, ch, LANES), tok),
        out_shape=jax.ShapeDtypeStruct((n_tokens, ch, LANES), F32),
        scratch_shapes=(
            [pltpu.VMEM((tt, 2 * ch, nk, LANES), F32) for _ in range(PEER_RING)]
            + [pltpu.VMEM((tt, 1, nk), F32), pltpu.VMEM((tt, 1, nk), F32), pltpu.SemaphoreType.DMA((PEER_RING,))]),
        compiler_params=pltpu.CompilerParams(
            dimension_semantics=("arbitrary",), vmem_limit_bytes=VMEM_LIMIT),
        name="peer",
    )(idx3, idx3, xn3, gate.reshape(T, 1, nk), uv4)


SC_CORES = 2
SC_SUBCORES = 16
SC_LANES = 16
SC_WORKERS = SC_CORES * SC_SUBCORES
SC_EXPERT_BLOCK = 16
SC_CHUNK_GROUP = 4
SC_TOKEN_SHARE_NUM, SC_TOKEN_SHARE_DEN = 3, 8


def _peer_sc(idx, gate, xn, uv, tok0, toks_per_worker):
    T, nk = idx.shape
    D = xn.shape[1]
    L = SC_LANES
    kb_rows = SC_EXPERT_BLOCK
    n_blocks = nk // kb_rows
    n_chunks = D // L
    uc = SC_CHUNK_GROUP
    mesh = plsc.VectorSubcoreMesh(core_axis_name="c", subcore_axis_name="s")

    def body(idx_hbm, gate_hbm, xn_hbm, uv_hbm, out_hbm, idx_v, gate_v, x_v, o_v, rows_v, acc_v, sems):
        wid = lax.axis_index("s") * SC_CORES + lax.axis_index("c")
        base = wid * toks_per_worker
        lane = lax.iota(I32, L)

        def gather(slot, kb, buf):
            return pltpu.make_async_copy(uv_hbm.at[idx_v.at[slot].at[pl.ds(kb * kb_rows, kb_rows)]],
                                         rows_v.at[buf], sems.at[buf])

        def evaluate(kb, rv):
            for j in range(kb_rows):
                acc_v[j] = jnp.zeros((L,), F32)

            @plsc.parallel_loop(0, n_chunks // uc, unroll=2)
            def _(cg):
                xs = [x_v[pl.ds((cg * uc + cc) * L, L)] for cc in range(uc)]
                for j in range(kb_rows):
                    pr = rv[j, pl.ds(cg * uc * L, L)] * xs[0]
                    for cc in range(1, uc):
                        pr = pr + rv[j, pl.ds((cg * uc + cc) * L, L)] * xs[cc]
                    plsc.addupdate(acc_v.at[j], pr)
            logits = jnp.zeros((L,), F32)
            for j in range(kb_rows):
                logits = jnp.where(lane == j, jnp.sum(acc_v[j]), logits)
            z = math.sqrt(2.0 / math.pi) * (logits + 0.044715 * (logits * logits * logits))
            th = 1.0 - 2.0 / (1.0 + jnp.exp(2.0 * z))
            act = 0.5 * logits * (1.0 + th) * gate_v[pl.ds(kb * kb_rows, kb_rows)]
            acts = [jnp.sum(jnp.where(lane == j, act, 0.0)) for j in range(kb_rows)]

            @plsc.parallel_loop(0, n_chunks, unroll=2)
            def _(c):
                acc = acts[0] * rv[0, pl.ds(D + c * L, L)]
                for j in range(1, kb_rows):
                    acc = acc + acts[j] * rv[j, pl.ds(D + c * L, L)]
                plsc.addupdate(o_v.at[pl.ds(c * L, L)], acc)

        def token(ti, carry):
            slot = ti % 2
            tok = tok0 + base + ti
            nxt = tok0 + base + jnp.minimum(ti + 1, toks_per_worker - 1)
            pltpu.sync_copy(idx_hbm.at[nxt], idx_v.at[1 - slot])
            pltpu.sync_copy(gate_hbm.at[tok], gate_v)
            pltpu.sync_copy(xn_hbm.at[tok], x_v)

            @plsc.parallel_loop(0, n_chunks)
            def _(c):
                o_v[pl.ds(c * L, L)] = jnp.zeros((L,), F32)

            def block(kb, carry):
                buf = kb % 2
                last = kb + 1 >= n_blocks
                gather(slot, kb, buf).wait()
                gather(jnp.where(last, 1 - slot, slot), jnp.where(last, 0, kb + 1), 1 - buf).start()
                evaluate(kb, rows_v.at[buf])
                return carry
            lax.fori_loop(0, n_blocks, block, 0)
            pltpu.sync_copy(o_v, out_hbm.at[base + ti])
            return carry

        pltpu.sync_copy(idx_hbm.at[tok0 + base], idx_v.at[0])
        gather(0, 0, 0).start()
        lax.fori_loop(0, toks_per_worker, token, 0)
        gather(0, 0, 0).wait()

    return pl.kernel(
        body, mesh=mesh,
        compiler_params=pltpu.CompilerParams(needs_layout_passes=False),
        out_type=jax.ShapeDtypeStruct((SC_WORKERS * toks_per_worker, D), F32),
        scratch_types=[
            pltpu.VMEM((2, nk), I32),
            pltpu.VMEM((nk,), F32),
            pltpu.VMEM((D,), F32),
            pltpu.VMEM((D,), F32),
            pltpu.VMEM((2, kb_rows, 2 * D), F32),
            pltpu.VMEM((kb_rows, L), F32),
            pltpu.SemaphoreType.DMA((2,)),
        ],
    )(idx, gate, xn, uv)


def _ple_kernel(h_ref, po_ref, p_ref, g3_ref, wg_ref, wp_ref, gf_ref, o_ref, *, final):
    h = h_ref[...] + po_ref[...]
    xn = _rms(h, g3_ref[...]).astype(BF16)
    gate = _sigmoid(jnp.dot(xn, wg_ref[...], preferred_element_type=F32))
    proj = jnp.dot(p_ref[...].astype(BF16), wp_ref[...], preferred_element_type=F32)
    h = h + gate * proj
    o_ref[...] = _rms(h, gf_ref[...]) if final else h


def _ple_out(h1, peer_out, p2, g3, wg_bf, wp_bf, gf, final, tm):
    T, D = h1.shape
    row = lambda i: (i, 0)
    full = lambda i: (0, 0)
    return pl.pallas_call(
        functools.partial(_ple_kernel, final=final),
        grid=(T // tm,),
        in_specs=[
            pl.BlockSpec((tm, D), row),
            pl.BlockSpec((tm, D), row),
            pl.BlockSpec((tm, p2.shape[1]), row),
            pl.BlockSpec((1, D), full),
            pl.BlockSpec(wg_bf.shape, full),
            pl.BlockSpec(wp_bf.shape, full),
            pl.BlockSpec((1, D), full),
        ],
        out_specs=pl.BlockSpec((tm, D), row),
        out_shape=jax.ShapeDtypeStruct((T, D), F32),
        compiler_params=pltpu.CompilerParams(
            dimension_semantics=("parallel",), vmem_limit_bytes=VMEM_LIMIT),
        name="ple_out",
    )(h1, peer_out, p2, g3.reshape(1, D), wg_bf, wp_bf, gf.reshape(1, D))


def _block_diag(w):
    nb, bw, _ = w.shape
    eye = jnp.eye(nb, dtype=w.dtype)
    return (eye[:, None, :, None] * w[:, :, None, :]).reshape(nb * bw, nb * bw)


def _key_matrix(keys, half):
    z = jnp.zeros_like(keys)
    blk = jnp.concatenate([keys, z] if half == 0 else [z, keys], axis=1)
    return jnp.kron(jnp.eye(PEER_HEADS, dtype=keys.dtype), blk)


def kernel(x, p, positions, norm_mix_g, w_in, lambda_q1, lambda_k1, lambda_q2, lambda_k2, diff_norm_g, conv_w, conv_b, lru_wa, lru_ba, lru_wx, lru_bx, lru_lambda, lru_norm_g, w_out, norm_ffn_g, peer_wq, peer_keys1, peer_keys2, peer_u, peer_v, norm_ple_g, ple_w_gate, ple_w_proj, final_norm_g):
    B, S, D = x.shape
    T = B * S
    depth = w_in.shape[0]
    h = x
    for i in range(depth):
        lambda_init = 0.8 - 0.6 * math.exp(-0.3 * i)
        q, k, v, u, gate = _in_proj(h, positions, norm_mix_g[i], w_in[i].astype(BF16), tm=512)
        attn = _diff_attn(q, k, v, lambda_q1[i], lambda_k1[i], lambda_q2[i], lambda_k2[i],
                          diff_norm_g[i], lambda_init, tq=256)
        w_all = jnp.concatenate([_block_diag(lru_wa[i, 0]), _block_diag(lru_wx[i, 0]),
                                 _block_diag(lru_wa[i, 1]), _block_diag(lru_wx[i, 1])], axis=1).astype(BF16)
        bias_all = jnp.concatenate([lru_ba[i, 0], lru_bx[i, 0], lru_ba[i, 1], lru_bx[i, 1]]).reshape(1, -1)
        rec = _bi_rglru(u, gate, conv_w[i], conv_b[i], w_all, bias_all, lru_lambda[i], lru_norm_g[i], tc=256)
        h1, xn2, idx_t, gate_t = _mix_route(
            attn.reshape(T, -1), rec.reshape(T, -1), h.reshape(T, D), w_out[i].astype(BF16), norm_ffn_g[i],
            peer_wq[i].astype(BF16), _key_matrix(peer_keys1[i], 0).astype(BF16),
            _key_matrix(peer_keys2[i], 1).astype(BF16), tm=256)
        n_exp = peer_u.shape[1]
        uv4 = jnp.concatenate([peer_u[i].reshape(n_exp, ROW_CHUNKS, 1, LANES),
                               peer_v[i].reshape(n_exp, ROW_CHUNKS, 1, LANES)], axis=1)
        uv2 = jnp.concatenate([peer_u[i], peer_v[i]], axis=1)
        idx, gate_tk = idx_t.T, gate_t.T
        t_sc = T * SC_TOKEN_SHARE_NUM // SC_TOKEN_SHARE_DEN // (2 * SC_WORKERS) * (2 * SC_WORKERS)
        t_tc = T - t_sc
        peer_sc = _peer_sc(idx, gate_tk, xn2, uv2, t_tc, t_sc // SC_WORKERS)
        peer_tc = _peer(idx, gate_tk, xn2.reshape(T, ROW_CHUNKS, LANES), uv4, t_tc, tt=8)
        peer_out = jnp.concatenate([peer_tc.reshape(t_tc, D), peer_sc], axis=0)
        h = _ple_out(h1, peer_out, p[i].reshape(T, -1), norm_ple_g[i], ple_w_gate[i].astype(BF16),
                     ple_w_proj[i].astype(BF16), final_norm_g, final=(i == depth - 1), tm=512)
        h = h.reshape(B, S, D)
    return h
```

```python
import functools
import math

import jax
import jax.numpy as jnp
from jax import lax
from jax.experimental import pallas as pl
from jax.experimental.pallas import tpu as pltpu
from jax.experimental.pallas import tpu_sc as plsc

F32 = jnp.float32
BF16 = jnp.bfloat16
I32 = jnp.int32

EPS = 1e-6
DIFF_HEAD_DIM = 64
DIFF_V_DIM = 128
N_DIFF_HEADS = 4
ROPE_DIM = 16
ROPE_THETA = 500000.0
LRU_WIDTH = 512
LRU_C = 8.0
N_KEYS = 128
PEER_HEADS = 8
PEER_TOPK = 16
HALF_KEY = 64
LOG2_E = math.log2(math.e)
LANES = 128
SUBLANES = 8
VMEM_LIMIT = 56 * 1024 * 1024


def _rms(x, g):
    return x * lax.rsqrt(jnp.mean(x * x, axis=-1, keepdims=True) + EPS) * g


def _gelu_tanh(x):
    return 0.5 * x * (1.0 + jnp.tanh(math.sqrt(2.0 / math.pi) * (x + 0.044715 * (x * x * x))))


def _sigmoid(x):
    return 1.0 / (1.0 + jnp.exp(-x))


def _inproj_kernel(x_ref, pos_ref, g_ref, w_ref, q_ref, k_ref, v_ref, u_ref, gate_ref):
    x = x_ref[0]
    xn = _rms(x, g_ref[...]).astype(BF16)
    pos = pos_ref[0].astype(F32)
    lane = lax.broadcasted_iota(I32, (1, LANES), 1)
    p = lane & (DIFF_HEAD_DIM - 1)
    freq = (p & (ROPE_DIM // 2 - 1)).astype(F32)
    inv_freq = jnp.exp(freq * (-2.0 / ROPE_DIM * math.log(ROPE_THETA)))
    ang = pos * inv_freq
    cs = jnp.cos(ang)
    sn = jnp.sin(ang)
    half = ROPE_DIM // 2
    c_mul = jnp.where(p < ROPE_DIM, cs, 1.0)
    s_up = jnp.where(p < half, -sn, 0.0)
    s_dn = jnp.where((p >= half) & (p < ROPE_DIM), sn, 0.0)

    def rope(t):
        return t * c_mul + pltpu.roll(t, LANES - half, 1) * s_up + pltpu.roll(t, half, 1) * s_dn

    nq = q_ref.shape[-1]
    pq = jnp.dot(xn, w_ref[:, 0:nq], preferred_element_type=F32)
    pk = jnp.dot(xn, w_ref[:, nq:2 * nq], preferred_element_type=F32)
    scale = DIFF_HEAD_DIM ** -0.5 * LOG2_E
    for j in range(nq // LANES):
        sl = slice(j * LANES, (j + 1) * LANES)
        q_ref[0, :, sl] = (rope(pq[:, sl]) * scale).astype(BF16)
        k_ref[0, :, sl] = rope(pk[:, sl]).astype(BF16)
    v_ref[0] = jnp.dot(xn, w_ref[:, 2 * nq:3 * nq], preferred_element_type=F32).astype(BF16)
    u_ref[0] = jnp.dot(xn, w_ref[:, 3 * nq:3 * nq + LRU_WIDTH], preferred_element_type=F32)
    gate_ref[0] = jnp.dot(xn, w_ref[:, 3 * nq + LRU_WIDTH:3 * nq + 2 * LRU_WIDTH],
                          preferred_element_type=F32).astype(BF16)


def _in_proj(x, positions, g, w_in_bf, tm):
    B, S, D = x.shape
    nq = N_DIFF_HEADS * 2 * DIFF_HEAD_DIM
    ncols = w_in_bf.shape[1]
    row = lambda b, i: (b, i, 0)
    return pl.pallas_call(
        _inproj_kernel,
        grid=(B, S // tm),
        in_specs=[
            pl.BlockSpec((1, tm, D), row),
            pl.BlockSpec((1, tm, 1), row),
            pl.BlockSpec((1, D), lambda b, i: (0, 0)),
            pl.BlockSpec((D, ncols), lambda b, i: (0, 0)),
        ],
        out_specs=[
            pl.BlockSpec((1, tm, nq), row),
            pl.BlockSpec((1, tm, nq), row),
            pl.BlockSpec((1, tm, nq), row),
            pl.BlockSpec((1, tm, LRU_WIDTH), row),
            pl.BlockSpec((1, tm, LRU_WIDTH), row),
        ],
        out_shape=[
            jax.ShapeDtypeStruct((B, S, nq), BF16),
            jax.ShapeDtypeStruct((B, S, nq), BF16),
            jax.ShapeDtypeStruct((B, S, nq), BF16),
            jax.ShapeDtypeStruct((B, S, LRU_WIDTH), F32),
            jax.ShapeDtypeStruct((B, S, LRU_WIDTH), BF16),
        ],
        compiler_params=pltpu.CompilerParams(
            dimension_semantics=("parallel", "parallel"), vmem_limit_bytes=VMEM_LIMIT),
        name="in_proj",
    )(x, positions.reshape(B, S, 1), g.reshape(1, D), w_in_bf)


def _attn_kernel(lq1_ref, lk1_ref, lq2_ref, lk2_ref, g_ref, q_ref, k_ref, v_ref, o_ref, *, lambda_init):
    lam = (jnp.exp(jnp.sum(lq1_ref[...] * lk1_ref[...], axis=-1, keepdims=True))
           - jnp.exp(jnp.sum(lq2_ref[...] * lk2_ref[...], axis=-1, keepdims=True))
           + lambda_init)
    q = q_ref[0]
    k = k_ref[0]
    v = v_ref[0]
    lane = lax.broadcasted_iota(I32, q.shape, 1)
    zero = jnp.zeros_like(q)
    q0 = jnp.where(lane < DIFF_HEAD_DIM, q, zero)
    q1 = jnp.where(lane >= DIFF_HEAD_DIM, q, zero)
    nt = (((1,), (1,)), ((), ()))
    s0 = lax.dot_general(q0, k, nt, preferred_element_type=F32)
    s1 = lax.dot_general(q1, k, nt, preferred_element_type=F32)
    p0 = jnp.exp2(s0 - jnp.max(s0, axis=-1, keepdims=True))
    p1 = jnp.exp2(s1 - jnp.max(s1, axis=-1, keepdims=True))
    l0 = jnp.sum(p0, axis=-1, keepdims=True)
    l1 = jnp.sum(p1, axis=-1, keepdims=True)
    w = (p0 - p1 * (lam * l0 / l1)).astype(BF16)
    o = jnp.dot(w, v, preferred_element_type=F32) / l0
    o_ref[0] = (_rms(o, g_ref[...]) * (1.0 - lambda_init)).astype(BF16)


def _diff_attn(q, k, v, lq1, lk1, lq2, lk2, g, lambda_init, tq):
    B, S, W = q.shape
    H = W // DIFF_V_DIM
    vec = lambda b, h, i: (0, 0)
    return pl.pallas_call(
        functools.partial(_attn_kernel, lambda_init=lambda_init),
        grid=(B, H, S // tq),
        in_specs=[
            pl.BlockSpec((1, DIFF_HEAD_DIM), vec),
            pl.BlockSpec((1, DIFF_HEAD_DIM), vec),
            pl.BlockSpec((1, DIFF_HEAD_DIM), vec),
            pl.BlockSpec((1, DIFF_HEAD_DIM), vec),
            pl.BlockSpec((1, DIFF_V_DIM), vec),
            pl.BlockSpec((1, tq, DIFF_V_DIM), lambda b, h, i: (b, i, h)),
            pl.BlockSpec((1, S, DIFF_V_DIM), lambda b, h, i: (b, 0, h)),
            pl.BlockSpec((1, S, DIFF_V_DIM), lambda b, h, i: (b, 0, h)),
        ],
        out_specs=pl.BlockSpec((1, tq, DIFF_V_DIM), lambda b, h, i: (b, i, h)),
        out_shape=jax.ShapeDtypeStruct((B, S, W), BF16),
        compiler_params=pltpu.CompilerParams(
            dimension_semantics=("parallel", "parallel", "parallel"), vmem_limit_bytes=VMEM_LIMIT),
        name="diff_attn",
    )(lq1.reshape(1, -1), lk1.reshape(1, -1), lq2.reshape(1, -1), lk2.reshape(1, -1),
      g.reshape(1, -1), q, k, v)


def _lru_kernel(u_ref, gate_ref, cw_ref, cb_ref, w_ref, bias_ref, lam_ref, g_ref, out_ref, hf_ref, *, tc):
    S = u_ref.shape[1]
    C = u_ref.shape[2]
    nc = S // tc
    halo = SUBLANES
    neg_lam = -lam_ref[...]
    sp = jnp.maximum(neg_lam, 0.0) + jnp.log(1.0 + jnp.exp(-jnp.abs(neg_lam)))
    row = lax.broadcasted_iota(I32, (tc, 1), 0)
    conv_taps = cw_ref.shape[0]
    conv_left = 2

    def gates(c, d):
        r0 = pl.multiple_of(c * tc, tc)
        x = u_ref[0, pl.ds(r0, tc), :]
        prev = u_ref[0, pl.ds(pl.multiple_of(jnp.maximum(r0 - halo, 0), halo), halo), :]
        nxt = u_ref[0, pl.ds(pl.multiple_of(jnp.minimum(r0 + tc, S - halo), halo), halo), :]
        prev = jnp.where(c > 0, prev, 0.0)
        nxt = jnp.where(c < nc - 1, nxt, 0.0)
        win = jnp.concatenate([prev, x, nxt], axis=0)
        uc = cb_ref[...]
        for j in range(conv_taps):
            o = halo - conv_left + j
            uc = uc + cw_ref[j:j + 1, :] * win[o:o + tc, :]
        pre = (jnp.dot(uc.astype(BF16), w_ref[:, d * 2 * C:(d + 1) * 2 * C], preferred_element_type=F32)
               + bias_ref[:, d * 2 * C:(d + 1) * 2 * C])
        r = _sigmoid(pre[:, :C])
        i = _sigmoid(pre[:, C:])
        log_a = -LRU_C * r * sp[d:d + 1, :]
        a = jnp.exp(log_a)
        th = jnp.tanh(log_a)
        mult = jnp.sqrt(-2.0 * th / (1.0 - th))
        return a, mult * (i * uc)

    def scan(a, b, reverse):
        d = 1
        while d < tc:
            if reverse:
                a_s = pltpu.roll(a, tc - d, 0)
                b_s = pltpu.roll(b, tc - d, 0)
                m = row < tc - d
            else:
                a_s = pltpu.roll(a, d, 0)
                b_s = pltpu.roll(b, d, 0)
                m = row >= d
            a_s = jnp.where(m, a_s, 1.0)
            b_s = jnp.where(m, b_s, 0.0)
            b = a * b_s + b
            a = a * a_s
            d *= 2
        return a, b

    def fwd_body(c, h0):
        a, b = gates(c, 0)
        a, b = scan(a, b, False)
        h = a * h0 + b
        hf_ref[pl.ds(pl.multiple_of(c * tc, tc), tc), :] = h
        return h[tc - 1:tc, :]

    lax.fori_loop(0, nc, fwd_body, jnp.zeros((1, C), F32))

    def bwd_body(j, h0):
        c = nc - 1 - j
        r0 = pl.multiple_of(c * tc, tc)
        a, b = gates(c, 1)
        a, b = scan(a, b, True)
        h = a * h0 + b
        y = (hf_ref[pl.ds(r0, tc), :] + h) * _gelu_tanh(gate_ref[0, pl.ds(r0, tc), :].astype(F32))
        out_ref[0, pl.ds(r0, tc), :] = _rms(y, g_ref[...]).astype(BF16)
        return h[0:1, :]

    lax.fori_loop(0, nc, bwd_body, jnp.zeros((1, C), F32))


def _bi_rglru(u, gate, conv_w, conv_b, w_all_bf, bias_all, lru_lambda, g, tc):
    B, S, C = u.shape
    full = lambda b: (0, 0)
    return pl.pallas_call(
        functools.partial(_lru_kernel, tc=tc),
        grid=(B,),
        in_specs=[
            pl.BlockSpec((1, S, C), lambda b: (b, 0, 0)),
            pl.BlockSpec((1, S, C), lambda b: (b, 0, 0)),
            pl.BlockSpec(conv_w.shape, full),
            pl.BlockSpec((1, C), full),
            pl.BlockSpec(w_all_bf.shape, full),
            pl.BlockSpec(bias_all.shape, full),
            pl.BlockSpec(lru_lambda.shape, full),
            pl.BlockSpec((1, C), full),
        ],
        out_specs=pl.BlockSpec((1, S, C), lambda b: (b, 0, 0)),
        out_shape=jax.ShapeDtypeStruct((B, S, C), BF16),
        scratch_shapes=[pltpu.VMEM((S, C), F32)],
        compiler_params=pltpu.CompilerParams(
            dimension_semantics=("parallel",), vmem_limit_bytes=VMEM_LIMIT),
        name="bi_rglru",
    )(u, gate, conv_w, conv_b.reshape(1, C), w_all_bf, bias_all, lru_lambda, g.reshape(1, C))


def _topk_rows(s, k):
    n = s.shape[0]
    rid = lax.broadcasted_iota(I32, s.shape, 0)
    vals, ids = [], []
    for _ in range(k):
        m = jnp.max(s, axis=0, keepdims=True)
        sel = jnp.min(jnp.where(s == m, rid, n), axis=0, keepdims=True)
        vals.append(m)
        ids.append(sel)
        s = jnp.where(rid == sel, -jnp.inf, s)
    return jnp.concatenate(vals, axis=0), jnp.concatenate(ids, axis=0)


CAND_ROW_PIECES = 4


def _candidate_pieces(k):
    up = lambda n: -(-n // SUBLANES) * SUBLANES
    pieces = [("row", i, 0, up(k // (i + 1))) for i in range(CAND_ROW_PIECES)]
    for j in range(k // (CAND_ROW_PIECES + 1)):
        for i0 in range(0, k // (j + 1), SUBLANES):
            pieces.append(("col", j, i0, SUBLANES))
    return pieces


def _route_kernel(attn_ref, rec_ref, x_ref, wo_ref, g_ref, wq_ref, k1_ref, k2_ref,
                  h_ref, xn_ref, idx_ref, gate_ref):
    aw = attn_ref.shape[1]
    h = (x_ref[...]
         + jnp.dot(attn_ref[...], wo_ref[0:aw, :], preferred_element_type=F32)
         + jnp.dot(rec_ref[...], wo_ref[aw:, :], preferred_element_type=F32))
    h_ref[...] = h
    xn = _rms(h, g_ref[...])
    xn_ref[...] = xn
    q = jnp.dot(xn.astype(BF16), wq_ref[...], preferred_element_type=F32).astype(BF16)
    nt = (((1,), (1,)), ((), ()))
    s1 = lax.dot_general(k1_ref[...], q, nt, preferred_element_type=F32)
    s2 = lax.dot_general(k2_ref[...], q, nt, preferred_element_type=F32)
    k = PEER_TOPK
    tm = q.shape[0]
    pieces = _candidate_pieces(k)
    pos_parts, ok_parts = [], []
    for kind, fixed, start, length in pieces:
        r = lax.broadcasted_iota(I32, (length, tm), 0) + start
        i, j = (fixed, r) if kind == "row" else (r, fixed)
        pos_parts.append(i * k + j)
        ok = (i + 1) * (j + 1) <= k
        ok_parts.append(ok if kind == "row" else ok & (r >= CAND_ROW_PIECES))
    pos = jnp.concatenate(pos_parts, axis=0)
    ok = jnp.concatenate(ok_parts, axis=0)
    idx_rows, gate_rows = [], []
    for hd in range(PEER_HEADS):
        v1, i1 = _topk_rows(s1[hd * N_KEYS:(hd + 1) * N_KEYS, :], k)
        v2, i2 = _topk_rows(s2[hd * N_KEYS:(hd + 1) * N_KEYS, :], k)
        cand_parts, cidx_parts = [], []
        for kind, fixed, start, length in pieces:
            if kind == "row":
                cand_parts.append(v1[fixed:fixed + 1, :] + v2[start:start + length, :])
                cidx_parts.append(i1[fixed:fixed + 1, :] * N_KEYS + i2[start:start + length, :])
            else:
                cand_parts.append(v1[start:start + length, :] + v2[fixed:fixed + 1, :])
                cidx_parts.append(i1[start:start + length, :] * N_KEYS + i2[fixed:fixed + 1, :])
        cand = jnp.where(ok, jnp.concatenate(cand_parts, axis=0), -jnp.inf)
        cidx = jnp.concatenate(cidx_parts, axis=0)
        sc, ids = [], []
        for _ in range(k):
            m = jnp.max(cand, axis=0, keepdims=True)
            sel = jnp.min(jnp.where(cand == m, pos, k * k), axis=0, keepdims=True)
            hit = pos == sel
            sc.append(m)
            ids.append(jnp.max(jnp.where(hit, cidx, -1), axis=0, keepdims=True))
            cand = jnp.where(hit, -jnp.inf, cand)
        sc = jnp.concatenate(sc, axis=0)
        e = jnp.exp(sc - sc[0:1, :])
        gate_rows.append(e / jnp.sum(e, axis=0, keepdims=True))
        idx_rows.append(jnp.concatenate(ids, axis=0))
    idx_ref[...] = jnp.concatenate(idx_rows, axis=0)
    gate_ref[...] = jnp.concatenate(gate_rows, axis=0)


def _mix_route(attn, rec, x2, wo_bf, g, wq_bf, k1t_bf, k2t_bf, tm):
    T, D = x2.shape
    NK = PEER_HEADS * PEER_TOPK
    row = lambda i: (i, 0)
    full = lambda i: (0, 0)
    return pl.pallas_call(
        _route_kernel,
        grid=(T // tm,),
        in_specs=[
            pl.BlockSpec((tm, attn.shape[1]), row),
            pl.BlockSpec((tm, rec.shape[1]), row),
            pl.BlockSpec((tm, D), row),
            pl.BlockSpec(wo_bf.shape, full),
            pl.BlockSpec((1, D), full),
            pl.BlockSpec(wq_bf.shape, full),
            pl.BlockSpec(k1t_bf.shape, full),
            pl.BlockSpec(k2t_bf.shape, full),
        ],
        out_specs=[
            pl.BlockSpec((tm, D), row),
            pl.BlockSpec((tm, D), row),
            pl.BlockSpec((NK, tm), lambda i: (0, i)),
            pl.BlockSpec((NK, tm), lambda i: (0, i)),
        ],
        out_shape=[
            jax.ShapeDtypeStruct((T, D), F32),
            jax.ShapeDtypeStruct((T, D), F32),
            jax.ShapeDtypeStruct((NK, T), I32),
            jax.ShapeDtypeStruct((NK, T), F32),
        ],
        compiler_params=pltpu.CompilerParams(
            dimension_semantics=("parallel",), vmem_limit_bytes=VMEM_LIMIT),
        name="mix_route",
    )(attn, rec, x2, wo_bf, g.reshape(1, D), wq_bf, k1t_bf, k2t_bf)


ROW_CHUNKS = 8
TOKEN_UNROLL = 2
DMA_PRIORITIES = 2
PEER_RING = 4
PEER_PREFETCH = 2


def _peer_kernel(idx_ref, idx_next_ref, xn_ref, gate_ref, uv_hbm, o_ref, *scratch, tt, nk):
    bufs = scratch[:PEER_RING]
    lg_ref, act_ref, sems = scratch[PEER_RING:]
    g = pl.program_id(0)
    n = pl.num_programs(0)
    nt = (((1,), (1,)), ((), ()))
    ch = ROW_CHUNKS

    def row_copy(ids_ref, q, t, k):
        e = ids_ref[0, 0, (q * tt + t) * nk + k]
        return pltpu.make_async_copy(uv_hbm.at[e], bufs[q].at[t, :, pl.ds(k, 1), :], sems.at[q])

    def wait_group(q):
        pltpu.make_async_copy(bufs[(q + 1) % PEER_RING], bufs[q], sems.at[q]).wait()

    @pl.when(g == 0)
    def _():
        for q in range(PEER_PREFETCH):
            def body(t, carry, q=q):
                for k in range(nk):
                    row_copy(idx_ref, q, t, k).start(priority=k % DMA_PRIORITIES)
                return carry
            lax.fori_loop(0, tt, body, 0)

    def group(q):
        buf = bufs[q]
        row0 = q * tt
        ahead = q + PEER_PREFETCH
        ids_ahead = idx_ref if ahead < PEER_RING else idx_next_ref

        def issue(t, k0, k1):
            for k in range(k0, k1):
                row_copy(ids_ahead, ahead % PEER_RING, t, k).start(priority=k % DMA_PRIORITIES)

        def phase_u(tb, carry):
            for j in range(TOKEN_UNROLL):
                t = tb * TOKEN_UNROLL + j
                issue(t, 0, nk // 2)
                xt = xn_ref[row0 + t]
                acc = buf[t, 0] * xt[0:1, :]
                for s in range(1, ch):
                    acc = acc + buf[t, s] * xt[s:s + 1, :]
                hi = acc.astype(BF16)
                lo = (acc - hi.astype(F32)).astype(BF16)
                ones = jnp.ones((SUBLANES, LANES), BF16)
                red = (lax.dot_general(ones, hi, nt, preferred_element_type=F32)
                       + lax.dot_general(ones, lo, nt, preferred_element_type=F32))
                lg_ref[t] = red[0:1, :]
            return carry
        lax.fori_loop(0, tt // TOKEN_UNROLL, phase_u, 0)
        act_ref[...] = _gelu_tanh(lg_ref[...]) * gate_ref[row0:row0 + tt]

        def phase_v(tb, carry):
            for j in range(TOKEN_UNROLL):
                t = tb * TOKEN_UNROLL + j
                issue(t, nk // 2, nk)
                act = act_ref[t].astype(BF16)
                cols = [jnp.dot(act, buf[t, ch + s].astype(BF16), preferred_element_type=F32)
                        for s in range(ch)]
                o_ref[row0 + t] = jnp.concatenate(cols, axis=0)
            return carry
        lax.fori_loop(0, tt // TOKEN_UNROLL, phase_v, 0)

    for q in range(PEER_RING):
        wait_group(q)
        group(q)

    @pl.when(g == n - 1)
    def _():
        for q in range(PEER_PREFETCH):
            wait_group(q)


def _peer(idx, gate, xn3, uv4, n_tokens, tt):
    T, ch, _ = xn3.shape
    nk = idx.shape[1]
    step = PEER_RING * tt
    n = n_tokens // step
    idx3 = idx.reshape(T // step, 1, step * nk)
    tok = lambda i: (i, 0, 0)
    return pl.pallas_call(
        functools.partial(_peer_kernel, tt=tt, nk=nk),
        grid=(n,),
        in_specs=[
            pl.BlockSpec((1, 1, step * nk), tok, memory_space=pltpu.SMEM),
            pl.BlockSpec((1, 1, step * nk), lambda i: (jnp.minimum(i + 1, n - 1), 0, 0), memory_space=pltpu.SMEM),
            pl.BlockSpec((step, ch, LANES), tok),
            pl.BlockSpec((step, 1, nk), tok),
            pl.BlockSpec(memory_space=pl.ANY),
        ],
        out_specs=pl.BlockSpec((step, ch, LANES), tok),
        out_shape=jax.ShapeDtypeStruct((n_tokens, ch, LANES), F32),
        scratch_shapes=(
            [pltpu.VMEM((tt, 2 * ch, nk, LANES), F32) for _ in range(PEER_RING)]
            + [pltpu.VMEM((tt, 1, nk), F32), pltpu.VMEM((tt, 1, nk), F32), pltpu.SemaphoreType.DMA((PEER_RING,))]),
        compiler_params=pltpu.CompilerParams(
            dimension_semantics=("arbitrary",), vmem_limit_bytes=VMEM_LIMIT),
        name="peer",
    )(idx3, idx3, xn3, gate.reshape(T, 1, nk), uv4)


SC_CORES = 2
SC_SUBCORES = 16
SC_LANES = 16
SC_WORKERS = SC_CORES * SC_SUBCORES
SC_EXPERT_BLOCK = 16
SC_CHUNK_GROUP = 4
SC_TOKEN_SHARE_NUM, SC_TOKEN_SHARE_DEN = 7, 16


def _peer_sc(idx, gate, xn, uv, tok0, toks_per_worker):
    T, nk = idx.shape
    D = xn.shape[1]
    L = SC_LANES
    kb_rows = SC_EXPERT_BLOCK
    n_blocks = nk // kb_rows
    n_chunks = D // L
    uc = SC_CHUNK_GROUP
    mesh = plsc.VectorSubcoreMesh(core_axis_name="c", subcore_axis_name="s")

    def body(idx_hbm, gate_hbm, xn_hbm, uv_hbm, out_hbm, idx_v, gate_v, x_v, o_v, rows_v, acc_v, sems):
        wid = lax.axis_index("s") * SC_CORES + lax.axis_index("c")
        base = wid * toks_per_worker
        lane = lax.iota(I32, L)

        def gather(slot, kb, buf):
            return pltpu.make_async_copy(uv_hbm.at[idx_v.at[slot].at[pl.ds(kb * kb_rows, kb_rows)]],
                                         rows_v.at[buf], sems.at[buf])

        def evaluate(kb, rv):
            for j in range(kb_rows):
                acc_v[j] = jnp.zeros((L,), F32)

            @plsc.parallel_loop(0, n_chunks // uc, unroll=2)
            def _(cg):
                xs = [x_v[pl.ds((cg * uc + cc) * L, L)] for cc in range(uc)]
                for j in range(kb_rows):
                    pr = rv[j, pl.ds(cg * uc * L, L)] * xs[0]
                    for cc in range(1, uc):
                        pr = pr + rv[j, pl.ds((cg * uc + cc) * L, L)] * xs[cc]
                    plsc.addupdate(acc_v.at[j], pr)
            logits = jnp.zeros((L,), F32)
            for j in range(kb_rows):
                logits = jnp.where(lane == j, jnp.sum(acc_v[j]), logits)
            z = math.sqrt(2.0 / math.pi) * (logits + 0.044715 * (logits * logits * logits))
            th = 1.0 - 2.0 / (1.0 + jnp.exp(2.0 * z))
            act = 0.5 * logits * (1.0 + th) * gate_v[pl.ds(kb * kb_rows, kb_rows)]
            acts = [jnp.sum(jnp.where(lane == j, act, 0.0)) for j in range(kb_rows)]

            @plsc.parallel_loop(0, n_chunks, unroll=2)
            def _(c):
                acc = acts[0] * rv[0, pl.ds(D + c * L, L)]
                for j in range(1, kb_rows):
                    acc = acc + acts[j] * rv[j, pl.ds(D + c * L, L)]
                plsc.addupdate(o_v.at[pl.ds(c * L, L)], acc)

        def token(ti, carry):
            slot = ti % 2
            tok = tok0 + base + ti
            nxt = tok0 + base + jnp.minimum(ti + 1, toks_per_worker - 1)
            pltpu.sync_copy(idx_hbm.at[nxt], idx_v.at[1 - slot])
            pltpu.sync_copy(gate_hbm.at[tok], gate_v)
            pltpu.sync_copy(xn_hbm.at[tok], x_v)

            @plsc.parallel_loop(0, n_chunks)
            def _(c):
                o_v[pl.ds(c * L, L)] = jnp.zeros((L,), F32)

            def block(kb, carry):
                buf = kb % 2
                last = kb + 1 >= n_blocks
                gather(slot, kb, buf).wait()
                gather(jnp.where(last, 1 - slot, slot), jnp.where(last, 0, kb + 1), 1 - buf).start()
                evaluate(kb, rows_v.at[buf])
                return carry
            lax.fori_loop(0, n_blocks, block, 0)
            pltpu.sync_copy(o_v, out_hbm.at[base + ti])
            return carry

        pltpu.sync_copy(idx_hbm.at[tok0 + base], idx_v.at[0])
        gather(0, 0, 0).start()
        lax.fori_loop(0, toks_per_worker, token, 0)
        gather(0, 0, 0).wait()

    return pl.kernel(
        body, mesh=mesh,
        compiler_params=pltpu.CompilerParams(needs_layout_passes=False),
        out_type=jax.ShapeDtypeStruct((SC_WORKERS * toks_per_worker, D), F32),
        scratch_types=[
            pltpu.VMEM((2, nk), I32),
            pltpu.VMEM((nk,), F32),
            pltpu.VMEM((D,), F32),
            pltpu.VMEM((D,), F32),
            pltpu.VMEM((2, kb_rows, 2 * D), F32),
            pltpu.VMEM((kb_rows, L), F32),
            pltpu.SemaphoreType.DMA((2,)),
        ],
    )(idx, gate, xn, uv)


def _ple_kernel(h_ref, po_ref, p_ref, g3_ref, wg_ref, wp_ref, gf_ref, o_ref, *, final):
    h = h_ref[...] + po_ref[...]
    xn = _rms(h, g3_ref[...]).astype(BF16)
    gate = _sigmoid(jnp.dot(xn, wg_ref[...], preferred_element_type=F32))
    proj = jnp.dot(p_ref[...].astype(BF16), wp_ref[...], preferred_element_type=F32)
    h = h + gate * proj
    o_ref[...] = _rms(h, gf_ref[...]) if final else h


def _ple_out(h1, peer_out, p2, g3, wg_bf, wp_bf, gf, final, tm):
    T, D = h1.shape
    row = lambda i: (i, 0)
    full = lambda i: (0, 0)
    return pl.pallas_call(
        functools.partial(_ple_kernel, final=final),
        grid=(T // tm,),
        in_specs=[
            pl.BlockSpec((tm, D), row),
            pl.BlockSpec((tm, D), row),
            pl.BlockSpec((tm, p2.shape[1]), row),
            pl.BlockSpec((1, D), full),
            pl.BlockSpec(wg_bf.shape, full),
            pl.BlockSpec(wp_bf.shape, full),
            pl.BlockSpec((1, D), full),
        ],
        out_specs=pl.BlockSpec((tm, D), row),
        out_shape=jax.ShapeDtypeStruct((T, D), F32),
        compiler_params=pltpu.CompilerParams(
            dimension_semantics=("parallel",), vmem_limit_bytes=VMEM_LIMIT),
        name="ple_out",
    )(h1, peer_out, p2, g3.reshape(1, D), wg_bf, wp_bf, gf.reshape(1, D))


def _block_diag(w):
    nb, bw, _ = w.shape
    eye = jnp.eye(nb, dtype=w.dtype)
    return (eye[:, None, :, None] * w[:, :, None, :]).reshape(nb * bw, nb * bw)


def _key_matrix(keys, half):
    z = jnp.zeros_like(keys)
    blk = jnp.concatenate([keys, z] if half == 0 else [z, keys], axis=1)
    return jnp.kron(jnp.eye(PEER_HEADS, dtype=keys.dtype), blk)


def kernel(x, p, positions, norm_mix_g, w_in, lambda_q1, lambda_k1, lambda_q2, lambda_k2, diff_norm_g, conv_w, conv_b, lru_wa, lru_ba, lru_wx, lru_bx, lru_lambda, lru_norm_g, w_out, norm_ffn_g, peer_wq, peer_keys1, peer_keys2, peer_u, peer_v, norm_ple_g, ple_w_gate, ple_w_proj, final_norm_g):
    B, S, D = x.shape
    T = B * S
    depth = w_in.shape[0]
    h = x
    for i in range(depth):
        lambda_init = 0.8 - 0.6 * math.exp(-0.3 * i)
        q, k, v, u, gate = _in_proj(h, positions, norm_mix_g[i], w_in[i].astype(BF16), tm=512)
        attn = _diff_attn(q, k, v, lambda_q1[i], lambda_k1[i], lambda_q2[i], lambda_k2[i],
                          diff_norm_g[i], lambda_init, tq=256)
        w_all = jnp.concatenate([_block_diag(lru_wa[i, 0]), _block_diag(lru_wx[i, 0]),
                                 _block_diag(lru_wa[i, 1]), _block_diag(lru_wx[i, 1])], axis=1).astype(BF16)
        bias_all = jnp.concatenate([lru_ba[i, 0], lru_bx[i, 0], lru_ba[i, 1], lru_bx[i, 1]]).reshape(1, -1)
        rec = _bi_rglru(u, gate, conv_w[i], conv_b[i], w_all, bias_all, lru_lambda[i], lru_norm_g[i], tc=256)
        h1, xn2, idx_t, gate_t = _mix_route(
            attn.reshape(T, -1), rec.reshape(T, -1), h.reshape(T, D), w_out[i].astype(BF16), norm_ffn_g[i],
            peer_wq[i].astype(BF16), _key_matrix(peer_keys1[i], 0).astype(BF16),
            _key_matrix(peer_keys2[i], 1).astype(BF16), tm=256)
        n_exp = peer_u.shape[1]
        uv4 = jnp.concatenate([peer_u[i].reshape(n_exp, ROW_CHUNKS, 1, LANES),
                               peer_v[i].reshape(n_exp, ROW_CHUNKS, 1, LANES)], axis=1)
        uv2 = jnp.concatenate([peer_u[i], peer_v[i]], axis=1)
        idx, gate_tk = idx_t.T, gate_t.T
        t_sc = T * SC_TOKEN_SHARE_NUM // SC_TOKEN_SHARE_DEN // (2 * SC_WORKERS) * (2 * SC_WORKERS)
        t_tc = T - t_sc
        peer_sc = _peer_sc(idx, gate_tk, xn2, uv2, t_tc, t_sc // SC_WORKERS)
        peer_tc = _peer(idx, gate_tk, xn2.reshape(T, ROW_CHUNKS, LANES), uv4, t_tc, tt=8)
        peer_out = jnp.concatenate([peer_tc.reshape(t_tc, D), peer_sc], axis=0)
        h = _ple_out(h1, peer_out, p[i].reshape(T, -1), norm_ple_g[i], ple_w_gate[i].astype(BF16),
                     ple_w_proj[i].astype(BF16), final_norm_g, final=(i == depth - 1), tm=512)
        h = h.reshape(B, S, D)
    return h
```

```python
import functools
import math

import jax
import jax.numpy as jnp
from jax import lax
from jax.experimental import pallas as pl
from jax.experimental.pallas import tpu as pltpu
from jax.experimental.pallas import tpu_sc as plsc

F32 = jnp.float32
BF16 = jnp.bfloat16
I32 = jnp.int32

EPS = 1e-6
DIFF_HEAD_DIM = 64
DIFF_V_DIM = 128
N_DIFF_HEADS = 4
ROPE_DIM = 16
ROPE_THETA = 500000.0
LRU_WIDTH = 512
LRU_C = 8.0
N_KEYS = 128
PEER_HEADS = 8
PEER_TOPK = 16
HALF_KEY = 64
LOG2_E = math.log2(math.e)
LANES = 128
SUBLANES = 8
VMEM_LIMIT = 56 * 1024 * 1024


def _rms(x, g):
    return x * lax.rsqrt(jnp.mean(x * x, axis=-1, keepdims=True) + EPS) * g


def _gelu_tanh(x):
    return 0.5 * x * (1.0 + jnp.tanh(math.sqrt(2.0 / math.pi) * (x + 0.044715 * (x * x * x))))


def _sigmoid(x):
    return 1.0 / (1.0 + jnp.exp(-x))


def _inproj_kernel(x_ref, pos_ref, g_ref, w_ref, q_ref, k_ref, v_ref, u_ref, gate_ref):
    x = x_ref[0]
    xn = _rms(x, g_ref[...]).astype(BF16)
    pos = pos_ref[0].astype(F32)
    lane = lax.broadcasted_iota(I32, (1, LANES), 1)
    p = lane & (DIFF_HEAD_DIM - 1)
    freq = (p & (ROPE_DIM // 2 - 1)).astype(F32)
    inv_freq = jnp.exp(freq * (-2.0 / ROPE_DIM * math.log(ROPE_THETA)))
    ang = pos * inv_freq
    cs = jnp.cos(ang)
    sn = jnp.sin(ang)
    half = ROPE_DIM // 2
    c_mul = jnp.where(p < ROPE_DIM, cs, 1.0)
    s_up = jnp.where(p < half, -sn, 0.0)
    s_dn = jnp.where((p >= half) & (p < ROPE_DIM), sn, 0.0)

    def rope(t):
        return t * c_mul + pltpu.roll(t, LANES - half, 1) * s_up + pltpu.roll(t, half, 1) * s_dn

    nq = q_ref.shape[-1]
    pq = jnp.dot(xn, w_ref[:, 0:nq], preferred_element_type=F32)
    pk = jnp.dot(xn, w_ref[:, nq:2 * nq], preferred_element_type=F32)
    scale = DIFF_HEAD_DIM ** -0.5 * LOG2_E
    for j in range(nq // LANES):
        sl = slice(j * LANES, (j + 1) * LANES)
        q_ref[0, :, sl] = (rope(pq[:, sl]) * scale).astype(BF16)
        k_ref[0, :, sl] = rope(pk[:, sl]).astype(BF16)
    v_ref[0] = jnp.dot(xn, w_ref[:, 2 * nq:3 * nq], preferred_element_type=F32).astype(BF16)
    u_ref[0] = jnp.dot(xn, w_ref[:, 3 * nq:3 * nq + LRU_WIDTH], preferred_element_type=F32)
    gate_ref[0] = jnp.dot(xn, w_ref[:, 3 * nq + LRU_WIDTH:3 * nq + 2 * LRU_WIDTH],
                          preferred_element_type=F32).astype(BF16)


def _in_proj(x, positions, g, w_in_bf, tm):
    B, S, D = x.shape
    nq = N_DIFF_HEADS * 2 * DIFF_HEAD_DIM
    ncols = w_in_bf.shape[1]
    row = lambda b, i: (b, i, 0)
    return pl.pallas_call(
        _inproj_kernel,
        grid=(B, S // tm),
        in_specs=[
            pl.BlockSpec((1, tm, D), row),
            pl.BlockSpec((1, tm, 1), row),
            pl.BlockSpec((1, D), lambda b, i: (0, 0)),
            pl.BlockSpec((D, ncols), lambda b, i: (0, 0)),
        ],
        out_specs=[
            pl.BlockSpec((1, tm, nq), row),
            pl.BlockSpec((1, tm, nq), row),
            pl.BlockSpec((1, tm, nq), row),
            pl.BlockSpec((1, tm, LRU_WIDTH), row),
            pl.BlockSpec((1, tm, LRU_WIDTH), row),
        ],
        out_shape=[
            jax.ShapeDtypeStruct((B, S, nq), BF16),
            jax.ShapeDtypeStruct((B, S, nq), BF16),
            jax.ShapeDtypeStruct((B, S, nq), BF16),
            jax.ShapeDtypeStruct((B, S, LRU_WIDTH), F32),
            jax.ShapeDtypeStruct((B, S, LRU_WIDTH), BF16),
        ],
        compiler_params=pltpu.CompilerParams(
            dimension_semantics=("parallel", "parallel"), vmem_limit_bytes=VMEM_LIMIT),
        name="in_proj",
    )(x, positions.reshape(B, S, 1), g.reshape(1, D), w_in_bf)


def _attn_kernel(lq1_ref, lk1_ref, lq2_ref, lk2_ref, g_ref, q_ref, k_ref, v_ref, o_ref, *, lambda_init):
    lam = (jnp.exp(jnp.sum(lq1_ref[...] * lk1_ref[...], axis=-1, keepdims=True))
           - jnp.exp(jnp.sum(lq2_ref[...] * lk2_ref[...], axis=-1, keepdims=True))
           + lambda_init)
    q = q_ref[0]
    k = k_ref[0]
    v = v_ref[0]
    lane = lax.broadcasted_iota(I32, q.shape, 1)
    zero = jnp.zeros_like(q)
    q0 = jnp.where(lane < DIFF_HEAD_DIM, q, zero)
    q1 = jnp.where(lane >= DIFF_HEAD_DIM, q, zero)
    nt = (((1,), (1,)), ((), ()))
    s0 = lax.dot_general(q0, k, nt, preferred_element_type=F32)
    s1 = lax.dot_general(q1, k, nt, preferred_element_type=F32)
    p0 = jnp.exp2(s0 - jnp.max(s0, axis=-1, keepdims=True))
    p1 = jnp.exp2(s1 - jnp.max(s1, axis=-1, keepdims=True))
    l0 = jnp.sum(p0, axis=-1, keepdims=True)
    l1 = jnp.sum(p1, axis=-1, keepdims=True)
    w = (p0 - p1 * (lam * l0 / l1)).astype(BF16)
    o = jnp.dot(w, v, preferred_element_type=F32) / l0
    o_ref[0] = (_rms(o, g_ref[...]) * (1.0 - lambda_init)).astype(BF16)


def _diff_attn(q, k, v, lq1, lk1, lq2, lk2, g, lambda_init, tq):
    B, S, W = q.shape
    H = W // DIFF_V_DIM
    vec = lambda b, h, i: (0, 0)
    return pl.pallas_call(
        functools.partial(_attn_kernel, lambda_init=lambda_init),
        grid=(B, H, S // tq),
        in_specs=[
            pl.BlockSpec((1, DIFF_HEAD_DIM), vec),
            pl.BlockSpec((1, DIFF_HEAD_DIM), vec),
            pl.BlockSpec((1, DIFF_HEAD_DIM), vec),
            pl.BlockSpec((1, DIFF_HEAD_DIM), vec),
            pl.BlockSpec((1, DIFF_V_DIM), vec),
            pl.BlockSpec((1, tq, DIFF_V_DIM), lambda b, h, i: (b, i, h)),
            pl.BlockSpec((1, S, DIFF_V_DIM), lambda b, h, i: (b, 0, h)),
            pl.BlockSpec((1, S, DIFF_V_DIM), lambda b, h, i: (b, 0, h)),
        ],
        out_specs=pl.BlockSpec((1, tq, DIFF_V_DIM), lambda b, h, i: (b, i, h)),
        out_shape=jax.ShapeDtypeStruct((B, S, W), BF16),
        compiler_params=pltpu.CompilerParams(
            dimension_semantics=("parallel", "parallel", "parallel"), vmem_limit_bytes=VMEM_LIMIT),
        name="diff_attn",
    )(lq1.reshape(1, -1), lk1.reshape(1, -1), lq2.reshape(1, -1), lk2.reshape(1, -1),
      g.reshape(1, -1), q, k, v)


def _lru_kernel(u_ref, gate_ref, cw_ref, cb_ref, w_ref, bias_ref, lam_ref, g_ref, out_ref, hf_ref, *, tc):
    S = u_ref.shape[1]
    C = u_ref.shape[2]
    nc = S // tc
    halo = SUBLANES
    neg_lam = -lam_ref[...]
    sp = jnp.maximum(neg_lam, 0.0) + jnp.log(1.0 + jnp.exp(-jnp.abs(neg_lam)))
    row = lax.broadcasted_iota(I32, (tc, 1), 0)
    conv_taps = cw_ref.shape[0]
    conv_left = 2

    def gates(c, d):
        r0 = pl.multiple_of(c * tc, tc)
        x = u_ref[0, pl.ds(r0, tc), :]
        prev = u_ref[0, pl.ds(pl.multiple_of(jnp.maximum(r0 - halo, 0), halo), halo), :]
        nxt = u_ref[0, pl.ds(pl.multiple_of(jnp.minimum(r0 + tc, S - halo), halo), halo), :]
        prev = jnp.where(c > 0, prev, 0.0)
        nxt = jnp.where(c < nc - 1, nxt, 0.0)
        win = jnp.concatenate([prev, x, nxt], axis=0)
        uc = cb_ref[...]
        for j in range(conv_taps):
            o = halo - conv_left + j
            uc = uc + cw_ref[j:j + 1, :] * win[o:o + tc, :]
        pre = (jnp.dot(uc.astype(BF16), w_ref[:, d * 2 * C:(d + 1) * 2 * C], preferred_element_type=F32)
               + bias_ref[:, d * 2 * C:(d + 1) * 2 * C])
        r = _sigmoid(pre[:, :C])
        i = _sigmoid(pre[:, C:])
        log_a = -LRU_C * r * sp[d:d + 1, :]
        a = jnp.exp(log_a)
        th = jnp.tanh(log_a)
        mult = jnp.sqrt(-2.0 * th / (1.0 - th))
        return a, mult * (i * uc)

    def scan(a, b, reverse):
        d = 1
        while d < tc:
            if reverse:
                a_s = pltpu.roll(a, tc - d, 0)
                b_s = pltpu.roll(b, tc - d, 0)
                m = row < tc - d
            else:
                a_s = pltpu.roll(a, d, 0)
                b_s = pltpu.roll(b, d, 0)
                m = row >= d
            a_s = jnp.where(m, a_s, 1.0)
            b_s = jnp.where(m, b_s, 0.0)
            b = a * b_s + b
            a = a * a_s
            d *= 2
        return a, b

    def fwd_body(c, h0):
        a, b = gates(c, 0)
        a, b = scan(a, b, False)
        h = a * h0 + b
        hf_ref[pl.ds(pl.multiple_of(c * tc, tc), tc), :] = h
        return h[tc - 1:tc, :]

    lax.fori_loop(0, nc, fwd_body, jnp.zeros((1, C), F32))

    def bwd_body(j, h0):
        c = nc - 1 - j
        r0 = pl.multiple_of(c * tc, tc)
        a, b = gates(c, 1)
        a, b = scan(a, b, True)
        h = a * h0 + b
        y = (hf_ref[pl.ds(r0, tc), :] + h) * _gelu_tanh(gate_ref[0, pl.ds(r0, tc), :].astype(F32))
        out_ref[0, pl.ds(r0, tc), :] = _rms(y, g_ref[...]).astype(BF16)
        return h[0:1, :]

    lax.fori_loop(0, nc, bwd_body, jnp.zeros((1, C), F32))


def _bi_rglru(u, gate, conv_w, conv_b, w_all_bf, bias_all, lru_lambda, g, tc):
    B, S, C = u.shape
    full = lambda b: (0, 0)
    return pl.pallas_call(
        functools.partial(_lru_kernel, tc=tc),
        grid=(B,),
        in_specs=[
            pl.BlockSpec((1, S, C), lambda b: (b, 0, 0)),
            pl.BlockSpec((1, S, C), lambda b: (b, 0, 0)),
            pl.BlockSpec(conv_w.shape, full),
            pl.BlockSpec((1, C), full),
            pl.BlockSpec(w_all_bf.shape, full),
            pl.BlockSpec(bias_all.shape, full),
            pl.BlockSpec(lru_lambda.shape, full),
            pl.BlockSpec((1, C), full),
        ],
        out_specs=pl.BlockSpec((1, S, C), lambda b: (b, 0, 0)),
        out_shape=jax.ShapeDtypeStruct((B, S, C), BF16),
        scratch_shapes=[pltpu.VMEM((S, C), F32)],
        compiler_params=pltpu.CompilerParams(
            dimension_semantics=("parallel",), vmem_limit_bytes=VMEM_LIMIT),
        name="bi_rglru",
    )(u, gate, conv_w, conv_b.reshape(1, C), w_all_bf, bias_all, lru_lambda, g.reshape(1, C))


def _topk_rows(s, k):
    n = s.shape[0]
    rid = lax.broadcasted_iota(I32, s.shape, 0)
    vals, ids = [], []
    for _ in range(k):
        m = jnp.max(s, axis=0, keepdims=True)
        sel = jnp.min(jnp.where(s == m, rid, n), axis=0, keepdims=True)
        vals.append(m)
        ids.append(sel)
        s = jnp.where(rid == sel, -jnp.inf, s)
    return jnp.concatenate(vals, axis=0), jnp.concatenate(ids, axis=0)


CAND_ROW_PIECES = 4


def _candidate_pieces(k):
    up = lambda n: -(-n // SUBLANES) * SUBLANES
    pieces = [("row", i, 0, up(k // (i + 1))) for i in range(CAND_ROW_PIECES)]
    for j in range(k // (CAND_ROW_PIECES + 1)):
        for i0 in range(0, k // (j + 1), SUBLANES):
            pieces.append(("col", j, i0, SUBLANES))
    return pieces


def _route_kernel(attn_ref, rec_ref, x_ref, wo_ref, g_ref, wq_ref, k1_ref, k2_ref,
                  h_ref, xn_ref, idx_ref, gate_ref):
    aw = attn_ref.shape[1]
    h = (x_ref[...]
         + jnp.dot(attn_ref[...], wo_ref[0:aw, :], preferred_element_type=F32)
         + jnp.dot(rec_ref[...], wo_ref[aw:, :], preferred_element_type=F32))
    h_ref[...] = h
    xn = _rms(h, g_ref[...])
    xn_ref[...] = xn
    q = jnp.dot(xn.astype(BF16), wq_ref[...], preferred_element_type=F32).astype(BF16)
    nt = (((1,), (1,)), ((), ()))
    s1 = lax.dot_general(k1_ref[...], q, nt, preferred_element_type=F32)
    s2 = lax.dot_general(k2_ref[...], q, nt, preferred_element_type=F32)
    k = PEER_TOPK
    tm = q.shape[0]
    pieces = _candidate_pieces(k)
    pos_parts, ok_parts = [], []
    for kind, fixed, start, length in pieces:
        r = lax.broadcasted_iota(I32, (length, tm), 0) + start
        i, j = (fixed, r) if kind == "row" else (r, fixed)
        pos_parts.append(i * k + j)
        ok = (i + 1) * (j + 1) <= k
        ok_parts.append(ok if kind == "row" else ok & (r >= CAND_ROW_PIECES))
    pos = jnp.concatenate(pos_parts, axis=0)
    ok = jnp.concatenate(ok_parts, axis=0)
    idx_rows, gate_rows = [], []
    for hd in range(PEER_HEADS):
        v1, i1 = _topk_rows(s1[hd * N_KEYS:(hd + 1) * N_KEYS, :], k)
        v2, i2 = _topk_rows(s2[hd * N_KEYS:(hd + 1) * N_KEYS, :], k)
        cand_parts, cidx_parts = [], []
        for kind, fixed, start, length in pieces:
            if kind == "row":
                cand_parts.append(v1[fixed:fixed + 1, :] + v2[start:start + length, :])
                cidx_parts.append(i1[fixed:fixed + 1, :] * N_KEYS + i2[start:start + length, :])
            else:
                cand_parts.append(v1[start:start + length, :] + v2[fixed:fixed + 1, :])
                cidx_parts.append(i1[start:start + length, :] * N_KEYS + i2[fixed:fixed + 1, :])
        cand = jnp.where(ok, jnp.concatenate(cand_parts, axis=0), -jnp.inf)
        cidx = jnp.concatenate(cidx_parts, axis=0)
        sc, ids = [], []
        for _ in range(k):
            m = jnp.max(cand, axis=0, keepdims=True)
            sel = jnp.min(jnp.where(cand == m, pos, k * k), axis=0, keepdims=True)
            hit = pos == sel
            sc.append(m)
            ids.append(jnp.max(jnp.where(hit, cidx, -1), axis=0, keepdims=True))
            cand = jnp.where(hit, -jnp.inf, cand)
        sc = jnp.concatenate(sc, axis=0)
        e = jnp.exp(sc - sc[0:1, :])
        gate_rows.append(e / jnp.sum(e, axis=0, keepdims=True))
        idx_rows.append(jnp.concatenate(ids, axis=0))
    idx_ref[...] = jnp.concatenate(idx_rows, axis=0)
    gate_ref[...] = jnp.concatenate(gate_rows, axis=0)


def _mix_route(attn, rec, x2, wo_bf, g, wq_bf, k1t_bf, k2t_bf, tm):
    T, D = x2.shape
    NK = PEER_HEADS * PEER_TOPK
    row = lambda i: (i, 0)
    full = lambda i: (0, 0)
    return pl.pallas_call(
        _route_kernel,
        grid=(T // tm,),
        in_specs=[
            pl.BlockSpec((tm, attn.shape[1]), row),
            pl.BlockSpec((tm, rec.shape[1]), row),
            pl.BlockSpec((tm, D), row),
            pl.BlockSpec(wo_bf.shape, full),
            pl.BlockSpec((1, D), full),
            pl.BlockSpec(wq_bf.shape, full),
            pl.BlockSpec(k1t_bf.shape, full),
            pl.BlockSpec(k2t_bf.shape, full),
        ],
        out_specs=[
            pl.BlockSpec((tm, D), row),
            pl.BlockSpec((tm, D), row),
            pl.BlockSpec((NK, tm), lambda i: (0, i)),
            pl.BlockSpec((NK, tm), lambda i: (0, i)),
        ],
        out_shape=[
            jax.ShapeDtypeStruct((T, D), F32),
            jax.ShapeDtypeStruct((T, D), F32),
            jax.ShapeDtypeStruct((NK, T), I32),
            jax.ShapeDtypeStruct((NK, T), F32),
        ],
        compiler_params=pltpu.CompilerParams(
            dimension_semantics=("parallel",), vmem_limit_bytes=VMEM_LIMIT),
        name="mix_route",
    )(attn, rec, x2, wo_bf, g.reshape(1, D), wq_bf, k1t_bf, k2t_bf)


ROW_CHUNKS = 8
HI_HALF_MASK = -65536


def _pack_rows(u, v):
    def words(a):
        half = a.shape[1] // 2
        bits = lax.bitcast_convert_type(a.astype(BF16), jnp.uint16).astype(jnp.uint32)
        return bits[:, :half] | (bits[:, half:] << 16)
    return lax.bitcast_convert_type(jnp.concatenate([words(u), words(v)], axis=1), I32)


def _word_lo(w):
    return pltpu.bitcast(w << 16, F32)


def _word_hi(w):
    return pltpu.bitcast(w & HI_HALF_MASK, F32)
TOKEN_UNROLL = 2
DMA_PRIORITIES = 2
PEER_RING = 4
PEER_PREFETCH = 2


def _peer_kernel(idx_ref, idx_next_ref, xn_ref, gate_ref, uv_hbm, o_ref, *scratch, tt, nk):
    bufs = scratch[:PEER_RING]
    lg_ref, act_ref, sems = scratch[PEER_RING:]
    g = pl.program_id(0)
    n = pl.num_programs(0)
    nt = (((1,), (1,)), ((), ()))
    ch = ROW_CHUNKS
    hc = ch // 2

    def row_copy(ids_ref, q, t, k):
        e = ids_ref[0, 0, (q * tt + t) * nk + k]
        return pltpu.make_async_copy(uv_hbm.at[e], bufs[q].at[t, :, pl.ds(k, 1), :], sems.at[q])

    def wait_group(q):
        pltpu.make_async_copy(bufs[(q + 1) % PEER_RING], bufs[q], sems.at[q]).wait()

    @pl.when(g == 0)
    def _():
        for q in range(PEER_PREFETCH):
            def body(t, carry, q=q):
                for k in range(nk):
                    row_copy(idx_ref, q, t, k).start(priority=k % DMA_PRIORITIES)
                return carry
            lax.fori_loop(0, tt, body, 0)

    def group(q):
        buf = bufs[q]
        row0 = q * tt
        ahead = q + PEER_PREFETCH
        ids_ahead = idx_ref if ahead < PEER_RING else idx_next_ref

        def issue(t, k0, k1):
            for k in range(k0, k1):
                row_copy(ids_ahead, ahead % PEER_RING, t, k).start(priority=k % DMA_PRIORITIES)

        def phase_u(tb, carry):
            for j in range(TOKEN_UNROLL):
                t = tb * TOKEN_UNROLL + j
                issue(t, 0, nk // 2)
                xt = xn_ref[row0 + t]
                acc = None
                for s in range(hc):
                    w = buf[t, s]
                    part = _word_lo(w) * xt[s:s + 1, :] + _word_hi(w) * xt[hc + s:hc + s + 1, :]
                    acc = part if acc is None else acc + part
                hi = acc.astype(BF16)
                lo = (acc - hi.astype(F32)).astype(BF16)
                ones = jnp.ones((SUBLANES, LANES), BF16)
                red = (lax.dot_general(ones, hi, nt, preferred_element_type=F32)
                       + lax.dot_general(ones, lo, nt, preferred_element_type=F32))
                lg_ref[t] = red[0:1, :]
            return carry
        lax.fori_loop(0, tt // TOKEN_UNROLL, phase_u, 0)
        act_ref[...] = _gelu_tanh(lg_ref[...]) * gate_ref[row0:row0 + tt]

        def phase_v(tb, carry):
            for j in range(TOKEN_UNROLL):
                t = tb * TOKEN_UNROLL + j
                issue(t, nk // 2, nk)
                act = act_ref[t].astype(BF16)
                words = [buf[t, hc + s] for s in range(hc)]
                cols = ([jnp.dot(act, _word_lo(w).astype(BF16), preferred_element_type=F32) for w in words]
                        + [jnp.dot(act, _word_hi(w).astype(BF16), preferred_element_type=F32) for w in words])
                o_ref[row0 + t] = jnp.concatenate(cols, axis=0)
            return carry
        lax.fori_loop(0, tt // TOKEN_UNROLL, phase_v, 0)

    for q in range(PEER_RING):
        wait_group(q)
        group(q)

    @pl.when(g == n - 1)
    def _():
        for q in range(PEER_PREFETCH):
            wait_group(q)


def _peer(idx, gate, xn3, uv4, n_tokens, tt):
    T, ch, _ = xn3.shape
    nk = idx.shape[1]
    step = PEER_RING * tt
    n = n_tokens // step
    idx3 = idx.reshape(T // step, 1, step * nk)
    tok = lambda i: (i, 0, 0)
    return pl.pallas_call(
        functools.partial(_peer_kernel, tt=tt, nk=nk),
        grid=(n,),
        in_specs=[
            pl.BlockSpec((1, 1, step * nk), tok, memory_space=pltpu.SMEM),
            pl.BlockSpec((1, 1, step * nk), lambda i: (jnp.minimum(i + 1, n - 1), 0, 0), memory_space=pltpu.SMEM),
            pl.BlockSpec((step, ch, LANES), tok),
            pl.BlockSpec((step, 1, nk), tok),
            pl.BlockSpec(memory_space=pl.ANY),
        ],
        out_specs=pl.BlockSpec((step, ch, LANES), tok),
        out_shape=jax.ShapeDtypeStruct((n_tokens, ch, LANES), F32),
        scratch_shapes=(
            [pltpu.VMEM((tt, ch, nk, LANES), I32) for _ in range(PEER_RING)]
            + [pltpu.VMEM((tt, 1, nk), F32), pltpu.VMEM((tt, 1, nk), F32), pltpu.SemaphoreType.DMA((PEER_RING,))]),
        compiler_params=pltpu.CompilerParams(
            dimension_semantics=("arbitrary",), vmem_limit_bytes=VMEM_LIMIT),
        name="peer",
    )(idx3, idx3, xn3, gate.reshape(T, 1, nk), uv4)


SC_CORES = 2
SC_SUBCORES = 16
SC_LANES = 16
SC_WORKERS = SC_CORES * SC_SUBCORES
SC_EXPERT_BLOCK = 16
SC_CHUNK_GROUP = 4
SC_TOKEN_SHARE_NUM, SC_TOKEN_SHARE_DEN = 1, 2


def _peer_sc(idx, gate, xn, uv, tok0, toks_per_worker):
    T, nk = idx.shape
    D = xn.shape[1]
    L = SC_LANES
    kb_rows = SC_EXPERT_BLOCK
    n_blocks = nk // kb_rows
    n_chunks = D // L
    half = D // 2
    uc = SC_CHUNK_GROUP
    word_lo = lambda w: plsc.bitcast(w << 16, F32)
    word_hi = lambda w: plsc.bitcast(w & HI_HALF_MASK, F32)
    mesh = plsc.VectorSubcoreMesh(core_axis_name="c", subcore_axis_name="s")

    def body(idx_hbm, gate_hbm, xn_hbm, uv_hbm, out_hbm, idx_v, gate_v, x_v, o_v, rows_v, acc_v, sems):
        wid = lax.axis_index("s") * SC_CORES + lax.axis_index("c")
        base = wid * toks_per_worker
        lane = lax.iota(I32, L)

        def gather(slot, kb, buf):
            return pltpu.make_async_copy(uv_hbm.at[idx_v.at[slot].at[pl.ds(kb * kb_rows, kb_rows)]],
                                         rows_v.at[buf], sems.at[buf])

        def evaluate(kb, rv):
            for j in range(kb_rows):
                acc_v[j] = jnp.zeros((L,), F32)

            @plsc.parallel_loop(0, half // L // uc, unroll=2)
            def _(cg):
                x_lo = [x_v[pl.ds((cg * uc + cc) * L, L)] for cc in range(uc)]
                x_hi = [x_v[pl.ds(half + (cg * uc + cc) * L, L)] for cc in range(uc)]
                for j in range(kb_rows):
                    pr = None
                    for cc in range(uc):
                        w = rv[j, pl.ds((cg * uc + cc) * L, L)]
                        part = word_lo(w) * x_lo[cc] + word_hi(w) * x_hi[cc]
                        pr = part if pr is None else pr + part
                    plsc.addupdate(acc_v.at[j], pr)
            logits = jnp.zeros((L,), F32)
            for j in range(kb_rows):
                logits = jnp.where(lane == j, jnp.sum(acc_v[j]), logits)
            z = math.sqrt(2.0 / math.pi) * (logits + 0.044715 * (logits * logits * logits))
            th = 1.0 - 2.0 / (1.0 + jnp.exp(2.0 * z))
            act = 0.5 * logits * (1.0 + th) * gate_v[pl.ds(kb * kb_rows, kb_rows)]
            acts = [jnp.sum(jnp.where(lane == j, act, 0.0)) for j in range(kb_rows)]

            @plsc.parallel_loop(0, half // L, unroll=2)
            def _(c):
                acc_lo = None
                acc_hi = None
                for j in range(kb_rows):
                    w = rv[j, pl.ds(half + c * L, L)]
                    lo = acts[j] * word_lo(w)
                    hi = acts[j] * word_hi(w)
                    acc_lo = lo if acc_lo is None else acc_lo + lo
                    acc_hi = hi if acc_hi is None else acc_hi + hi
                plsc.addupdate(o_v.at[pl.ds(c * L, L)], acc_lo)
                plsc.addupdate(o_v.at[pl.ds(half + c * L, L)], acc_hi)

        def token(ti, carry):
            slot = ti % 2
            tok = tok0 + base + ti
            nxt = tok0 + base + jnp.minimum(ti + 1, toks_per_worker - 1)
            pltpu.sync_copy(idx_hbm.at[nxt], idx_v.at[1 - slot])
            pltpu.sync_copy(gate_hbm.at[tok], gate_v)
            pltpu.sync_copy(xn_hbm.at[tok], x_v)

            @plsc.parallel_loop(0, n_chunks)
            def _(c):
                o_v[pl.ds(c * L, L)] = jnp.zeros((L,), F32)

            def block(kb, carry):
                buf = kb % 2
                last = kb + 1 >= n_blocks
                gather(slot, kb, buf).wait()
                gather(jnp.where(last, 1 - slot, slot), jnp.where(last, 0, kb + 1), 1 - buf).start()
                evaluate(kb, rows_v.at[buf])
                return carry
            lax.fori_loop(0, n_blocks, block, 0)
            pltpu.sync_copy(o_v, out_hbm.at[base + ti])
            return carry

        pltpu.sync_copy(idx_hbm.at[tok0 + base], idx_v.at[0])
        gather(0, 0, 0).start()
        lax.fori_loop(0, toks_per_worker, token, 0)
        gather(0, 0, 0).wait()

    return pl.kernel(
        body, mesh=mesh,
        compiler_params=pltpu.CompilerParams(needs_layout_passes=False),
        out_type=jax.ShapeDtypeStruct((SC_WORKERS * toks_per_worker, D), F32),
        scratch_types=[
            pltpu.VMEM((2, nk), I32),
            pltpu.VMEM((nk,), F32),
            pltpu.VMEM((D,), F32),
            pltpu.VMEM((D,), F32),
            pltpu.VMEM((2, kb_rows, D), I32),
            pltpu.VMEM((kb_rows, L), F32),
            pltpu.SemaphoreType.DMA((2,)),
        ],
    )(idx, gate, xn, uv)


def _ple_kernel(h_ref, po_ref, p_ref, g3_ref, wg_ref, wp_ref, gf_ref, o_ref, *, final):
    h = h_ref[...] + po_ref[...]
    xn = _rms(h, g3_ref[...]).astype(BF16)
    gate = _sigmoid(jnp.dot(xn, wg_ref[...], preferred_element_type=F32))
    proj = jnp.dot(p_ref[...].astype(BF16), wp_ref[...], preferred_element_type=F32)
    h = h + gate * proj
    o_ref[...] = _rms(h, gf_ref[...]) if final else h


def _ple_out(h1, peer_out, p2, g3, wg_bf, wp_bf, gf, final, tm):
    T, D = h1.shape
    row = lambda i: (i, 0)
    full = lambda i: (0, 0)
    return pl.pallas_call(
        functools.partial(_ple_kernel, final=final),
        grid=(T // tm,),
        in_specs=[
            pl.BlockSpec((tm, D), row),
            pl.BlockSpec((tm, D), row),
            pl.BlockSpec((tm, p2.shape[1]), row),
            pl.BlockSpec((1, D), full),
            pl.BlockSpec(wg_bf.shape, full),
            pl.BlockSpec(wp_bf.shape, full),
            pl.BlockSpec((1, D), full),
        ],
        out_specs=pl.BlockSpec((tm, D), row),
        out_shape=jax.ShapeDtypeStruct((T, D), F32),
        compiler_params=pltpu.CompilerParams(
            dimension_semantics=("parallel",), vmem_limit_bytes=VMEM_LIMIT),
        name="ple_out",
    )(h1, peer_out, p2, g3.reshape(1, D), wg_bf, wp_bf, gf.reshape(1, D))


def _block_diag(w):
    nb, bw, _ = w.shape
    eye = jnp.eye(nb, dtype=w.dtype)
    return (eye[:, None, :, None] * w[:, :, None, :]).reshape(nb * bw, nb * bw)


def _key_matrix(keys, half):
    z = jnp.zeros_like(keys)
    blk = jnp.concatenate([keys, z] if half == 0 else [z, keys], axis=1)
    return jnp.kron(jnp.eye(PEER_HEADS, dtype=keys.dtype), blk)


def kernel(x, p, positions, norm_mix_g, w_in, lambda_q1, lambda_k1, lambda_q2, lambda_k2, diff_norm_g, conv_w, conv_b, lru_wa, lru_ba, lru_wx, lru_bx, lru_lambda, lru_norm_g, w_out, norm_ffn_g, peer_wq, peer_keys1, peer_keys2, peer_u, peer_v, norm_ple_g, ple_w_gate, ple_w_proj, final_norm_g):
    B, S, D = x.shape
    T = B * S
    depth = w_in.shape[0]
    h = x
    for i in range(depth):
        lambda_init = 0.8 - 0.6 * math.exp(-0.3 * i)
        q, k, v, u, gate = _in_proj(h, positions, norm_mix_g[i], w_in[i].astype(BF16), tm=512)
        attn = _diff_attn(q, k, v, lambda_q1[i], lambda_k1[i], lambda_q2[i], lambda_k2[i],
                          diff_norm_g[i], lambda_init, tq=256)
        w_all = jnp.concatenate([_block_diag(lru_wa[i, 0]), _block_diag(lru_wx[i, 0]),
                                 _block_diag(lru_wa[i, 1]), _block_diag(lru_wx[i, 1])], axis=1).astype(BF16)
        bias_all = jnp.concatenate([lru_ba[i, 0], lru_bx[i, 0], lru_ba[i, 1], lru_bx[i, 1]]).reshape(1, -1)
        rec = _bi_rglru(u, gate, conv_w[i], conv_b[i], w_all, bias_all, lru_lambda[i], lru_norm_g[i], tc=256)
        h1, xn2, idx_t, gate_t = _mix_route(
            attn.reshape(T, -1), rec.reshape(T, -1), h.reshape(T, D), w_out[i].astype(BF16), norm_ffn_g[i],
            peer_wq[i].astype(BF16), _key_matrix(peer_keys1[i], 0).astype(BF16),
            _key_matrix(peer_keys2[i], 1).astype(BF16), tm=256)
        n_exp = peer_u.shape[1]
        uv2 = _pack_rows(peer_u[i], peer_v[i])
        uv4 = uv2.reshape(n_exp, ROW_CHUNKS, 1, LANES)
        idx, gate_tk = idx_t.T, gate_t.T
        t_sc = T * SC_TOKEN_SHARE_NUM // SC_TOKEN_SHARE_DEN // (2 * SC_WORKERS) * (2 * SC_WORKERS)
        t_tc = T - t_sc
        peer_sc = _peer_sc(idx, gate_tk, xn2, uv2, t_tc, t_sc // SC_WORKERS)
        peer_tc = _peer(idx, gate_tk, xn2.reshape(T, ROW_CHUNKS, LANES), uv4, t_tc, tt=8)
        peer_out = jnp.concatenate([peer_tc.reshape(t_tc, D), peer_sc], axis=0)
        h = _ple_out(h1, peer_out, p[i].reshape(T, -1), norm_ple_g[i], ple_w_gate[i].astype(BF16),
                     ple_w_proj[i].astype(BF16), final_norm_g, final=(i == depth - 1), tm=512)
        h = h.reshape(B, S, D)
    return h
```

```python
import functools
import math

import jax
import jax.numpy as jnp
from jax import lax
from jax.experimental import pallas as pl
from jax.experimental.pallas import tpu as pltpu
from jax.experimental.pallas import tpu_sc as plsc

F32 = jnp.float32
BF16 = jnp.bfloat16
I32 = jnp.int32

EPS = 1e-6
DIFF_HEAD_DIM = 64
DIFF_V_DIM = 128
N_DIFF_HEADS = 4
ROPE_DIM = 16
ROPE_THETA = 500000.0
LRU_WIDTH = 512
LRU_C = 8.0
N_KEYS = 128
PEER_HEADS = 8
PEER_TOPK = 16
HALF_KEY = 64
LOG2_E = math.log2(math.e)
LANES = 128
SUBLANES = 8
VMEM_LIMIT = 56 * 1024 * 1024


def _rms(x, g):
    return x * lax.rsqrt(jnp.mean(x * x, axis=-1, keepdims=True) + EPS) * g


def _gelu_tanh(x):
    return 0.5 * x * (1.0 + jnp.tanh(math.sqrt(2.0 / math.pi) * (x + 0.044715 * (x * x * x))))


def _sigmoid(x):
    return 1.0 / (1.0 + jnp.exp(-x))


def _inproj_kernel(x_ref, pos_ref, g_ref, w_ref, q_ref, k_ref, v_ref, u_ref, gate_ref):
    x = x_ref[0]
    xn = _rms(x, g_ref[...]).astype(BF16)
    pos = pos_ref[0].astype(F32)
    lane = lax.broadcasted_iota(I32, (1, LANES), 1)
    p = lane & (DIFF_HEAD_DIM - 1)
    freq = (p & (ROPE_DIM // 2 - 1)).astype(F32)
    inv_freq = jnp.exp(freq * (-2.0 / ROPE_DIM * math.log(ROPE_THETA)))
    ang = pos * inv_freq
    cs = jnp.cos(ang)
    sn = jnp.sin(ang)
    half = ROPE_DIM // 2
    c_mul = jnp.where(p < ROPE_DIM, cs, 1.0)
    s_up = jnp.where(p < half, -sn, 0.0)
    s_dn = jnp.where((p >= half) & (p < ROPE_DIM), sn, 0.0)

    def rope(t):
        return t * c_mul + pltpu.roll(t, LANES - half, 1) * s_up + pltpu.roll(t, half, 1) * s_dn

    nq = q_ref.shape[-1]
    pq = jnp.dot(xn, w_ref[:, 0:nq], preferred_element_type=F32)
    pk = jnp.dot(xn, w_ref[:, nq:2 * nq], preferred_element_type=F32)
    scale = DIFF_HEAD_DIM ** -0.5 * LOG2_E
    for j in range(nq // LANES):
        sl = slice(j * LANES, (j + 1) * LANES)
        q_ref[0, :, sl] = (rope(pq[:, sl]) * scale).astype(BF16)
        k_ref[0, :, sl] = rope(pk[:, sl]).astype(BF16)
    v_ref[0] = jnp.dot(xn, w_ref[:, 2 * nq:3 * nq], preferred_element_type=F32).astype(BF16)
    u_ref[0] = jnp.dot(xn, w_ref[:, 3 * nq:3 * nq + LRU_WIDTH], preferred_element_type=F32)
    gate_ref[0] = jnp.dot(xn, w_ref[:, 3 * nq + LRU_WIDTH:3 * nq + 2 * LRU_WIDTH],
                          preferred_element_type=F32).astype(BF16)


def _in_proj(x, positions, g, w_in_bf, b0, nb, tm):
    S, D = x.shape[1], x.shape[2]
    B = nb
    nq = N_DIFF_HEADS * 2 * DIFF_HEAD_DIM
    ncols = w_in_bf.shape[1]
    row = lambda b, i: (b, i, 0)
    row_in = lambda b, i: (b + b0, i, 0)
    return pl.pallas_call(
        _inproj_kernel,
        grid=(B, S // tm),
        in_specs=[
            pl.BlockSpec((1, tm, D), row_in),
            pl.BlockSpec((1, tm, 1), row_in),
            pl.BlockSpec((1, D), lambda b, i: (0, 0)),
            pl.BlockSpec((D, ncols), lambda b, i: (0, 0)),
        ],
        out_specs=[
            pl.BlockSpec((1, tm, nq), row),
            pl.BlockSpec((1, tm, nq), row),
            pl.BlockSpec((1, tm, nq), row),
            pl.BlockSpec((1, tm, LRU_WIDTH), row),
            pl.BlockSpec((1, tm, LRU_WIDTH), row),
        ],
        out_shape=[
            jax.ShapeDtypeStruct((B, S, nq), BF16),
            jax.ShapeDtypeStruct((B, S, nq), BF16),
            jax.ShapeDtypeStruct((B, S, nq), BF16),
            jax.ShapeDtypeStruct((B, S, LRU_WIDTH), F32),
            jax.ShapeDtypeStruct((B, S, LRU_WIDTH), BF16),
        ],
        compiler_params=pltpu.CompilerParams(
            dimension_semantics=("parallel", "parallel"), vmem_limit_bytes=VMEM_LIMIT),
        name="in_proj",
    )(x, positions.reshape(x.shape[0], S, 1), g.reshape(1, D), w_in_bf)


def _attn_kernel(lq1_ref, lk1_ref, lq2_ref, lk2_ref, g_ref, q_ref, k_ref, v_ref, o_ref, *, lambda_init):
    lam = (jnp.exp(jnp.sum(lq1_ref[...] * lk1_ref[...], axis=-1, keepdims=True))
           - jnp.exp(jnp.sum(lq2_ref[...] * lk2_ref[...], axis=-1, keepdims=True))
           + lambda_init)
    q = q_ref[0]
    k = k_ref[0]
    v = v_ref[0]
    lane = lax.broadcasted_iota(I32, q.shape, 1)
    zero = jnp.zeros_like(q)
    q0 = jnp.where(lane < DIFF_HEAD_DIM, q, zero)
    q1 = jnp.where(lane >= DIFF_HEAD_DIM, q, zero)
    nt = (((1,), (1,)), ((), ()))
    s0 = lax.dot_general(q0, k, nt, preferred_element_type=F32)
    s1 = lax.dot_general(q1, k, nt, preferred_element_type=F32)
    p0 = jnp.exp2(s0 - jnp.max(s0, axis=-1, keepdims=True))
    p1 = jnp.exp2(s1 - jnp.max(s1, axis=-1, keepdims=True))
    l0 = jnp.sum(p0, axis=-1, keepdims=True)
    l1 = jnp.sum(p1, axis=-1, keepdims=True)
    w = (p0 - p1 * (lam * l0 / l1)).astype(BF16)
    o = jnp.dot(w, v, preferred_element_type=F32) / l0
    o_ref[0] = (_rms(o, g_ref[...]) * (1.0 - lambda_init)).astype(BF16)


def _diff_attn(q, k, v, lq1, lk1, lq2, lk2, g, lambda_init, tq):
    B, S, W = q.shape
    H = W // DIFF_V_DIM
    vec = lambda b, h, i: (0, 0)
    return pl.pallas_call(
        functools.partial(_attn_kernel, lambda_init=lambda_init),
        grid=(B, H, S // tq),
        in_specs=[
            pl.BlockSpec((1, DIFF_HEAD_DIM), vec),
            pl.BlockSpec((1, DIFF_HEAD_DIM), vec),
            pl.BlockSpec((1, DIFF_HEAD_DIM), vec),
            pl.BlockSpec((1, DIFF_HEAD_DIM), vec),
            pl.BlockSpec((1, DIFF_V_DIM), vec),
            pl.BlockSpec((1, tq, DIFF_V_DIM), lambda b, h, i: (b, i, h)),
            pl.BlockSpec((1, S, DIFF_V_DIM), lambda b, h, i: (b, 0, h)),
            pl.BlockSpec((1, S, DIFF_V_DIM), lambda b, h, i: (b, 0, h)),
        ],
        out_specs=pl.BlockSpec((1, tq, DIFF_V_DIM), lambda b, h, i: (b, i, h)),
        out_shape=jax.ShapeDtypeStruct((B, S, W), BF16),
        compiler_params=pltpu.CompilerParams(
            dimension_semantics=("parallel", "parallel", "parallel"), vmem_limit_bytes=VMEM_LIMIT),
        name="diff_attn",
    )(lq1.reshape(1, -1), lk1.reshape(1, -1), lq2.reshape(1, -1), lk2.reshape(1, -1),
      g.reshape(1, -1), q, k, v)


def _lru_kernel(u_ref, gate_ref, cw_ref, cb_ref, w_ref, bias_ref, lam_ref, g_ref, out_ref, hf_ref, *, tc):
    S = u_ref.shape[1]
    C = u_ref.shape[2]
    nc = S // tc
    halo = SUBLANES
    neg_lam = -lam_ref[...]
    sp = jnp.maximum(neg_lam, 0.0) + jnp.log(1.0 + jnp.exp(-jnp.abs(neg_lam)))
    row = lax.broadcasted_iota(I32, (tc, 1), 0)
    conv_taps = cw_ref.shape[0]
    conv_left = 2

    def gates(c, d):
        r0 = pl.multiple_of(c * tc, tc)
        x = u_ref[0, pl.ds(r0, tc), :]
        prev = u_ref[0, pl.ds(pl.multiple_of(jnp.maximum(r0 - halo, 0), halo), halo), :]
        nxt = u_ref[0, pl.ds(pl.multiple_of(jnp.minimum(r0 + tc, S - halo), halo), halo), :]
        prev = jnp.where(c > 0, prev, 0.0)
        nxt = jnp.where(c < nc - 1, nxt, 0.0)
        win = jnp.concatenate([prev, x, nxt], axis=0)
        uc = cb_ref[...]
        for j in range(conv_taps):
            o = halo - conv_left + j
            uc = uc + cw_ref[j:j + 1, :] * win[o:o + tc, :]
        pre = (jnp.dot(uc.astype(BF16), w_ref[:, d * 2 * C:(d + 1) * 2 * C], preferred_element_type=F32)
               + bias_ref[:, d * 2 * C:(d + 1) * 2 * C])
        r = _sigmoid(pre[:, :C])
        i = _sigmoid(pre[:, C:])
        log_a = -LRU_C * r * sp[d:d + 1, :]
        a = jnp.exp(log_a)
        th = jnp.tanh(log_a)
        mult = jnp.sqrt(-2.0 * th / (1.0 - th))
        return a, mult * (i * uc)

    def scan(a, b, reverse):
        d = 1
        while d < tc:
            if reverse:
                a_s = pltpu.roll(a, tc - d, 0)
                b_s = pltpu.roll(b, tc - d, 0)
                m = row < tc - d
            else:
                a_s = pltpu.roll(a, d, 0)
                b_s = pltpu.roll(b, d, 0)
                m = row >= d
            a_s = jnp.where(m, a_s, 1.0)
            b_s = jnp.where(m, b_s, 0.0)
            b = a * b_s + b
            a = a * a_s
            d *= 2
        return a, b

    def fwd_body(c, h0):
        a, b = gates(c, 0)
        a, b = scan(a, b, False)
        h = a * h0 + b
        hf_ref[pl.ds(pl.multiple_of(c * tc, tc), tc), :] = h
        return h[tc - 1:tc, :]

    lax.fori_loop(0, nc, fwd_body, jnp.zeros((1, C), F32))

    def bwd_body(j, h0):
        c = nc - 1 - j
        r0 = pl.multiple_of(c * tc, tc)
        a, b = gates(c, 1)
        a, b = scan(a, b, True)
        h = a * h0 + b
        y = (hf_ref[pl.ds(r0, tc), :] + h) * _gelu_tanh(gate_ref[0, pl.ds(r0, tc), :].astype(F32))
        out_ref[0, pl.ds(r0, tc), :] = _rms(y, g_ref[...]).astype(BF16)
        return h[0:1, :]

    lax.fori_loop(0, nc, bwd_body, jnp.zeros((1, C), F32))


def _bi_rglru(u, gate, conv_w, conv_b, w_all_bf, bias_all, lru_lambda, g, tc):
    B, S, C = u.shape
    full = lambda b: (0, 0)
    return pl.pallas_call(
        functools.partial(_lru_kernel, tc=tc),
        grid=(B,),
        in_specs=[
            pl.BlockSpec((1, S, C), lambda b: (b, 0, 0)),
            pl.BlockSpec((1, S, C), lambda b: (b, 0, 0)),
            pl.BlockSpec(conv_w.shape, full),
            pl.BlockSpec((1, C), full),
            pl.BlockSpec(w_all_bf.shape, full),
            pl.BlockSpec(bias_all.shape, full),
            pl.BlockSpec(lru_lambda.shape, full),
            pl.BlockSpec((1, C), full),
        ],
        out_specs=pl.BlockSpec((1, S, C), lambda b: (b, 0, 0)),
        out_shape=jax.ShapeDtypeStruct((B, S, C), BF16),
        scratch_shapes=[pltpu.VMEM((S, C), F32)],
        compiler_params=pltpu.CompilerParams(
            dimension_semantics=("parallel",), vmem_limit_bytes=VMEM_LIMIT),
        name="bi_rglru",
    )(u, gate, conv_w, conv_b.reshape(1, C), w_all_bf, bias_all, lru_lambda, g.reshape(1, C))


def _topk_rows(s, k):
    n = s.shape[0]
    rid = lax.broadcasted_iota(I32, s.shape, 0)
    vals, ids = [], []
    for _ in range(k):
        m = jnp.max(s, axis=0, keepdims=True)
        sel = jnp.min(jnp.where(s == m, rid, n), axis=0, keepdims=True)
        vals.append(m)
        ids.append(sel)
        s = jnp.where(rid == sel, -jnp.inf, s)
    return jnp.concatenate(vals, axis=0), jnp.concatenate(ids, axis=0)


CAND_ROW_PIECES = 4


def _candidate_pieces(k):
    up = lambda n: -(-n // SUBLANES) * SUBLANES
    pieces = [("row", i, 0, up(k // (i + 1))) for i in range(CAND_ROW_PIECES)]
    for j in range(k // (CAND_ROW_PIECES + 1)):
        for i0 in range(0, k // (j + 1), SUBLANES):
            pieces.append(("col", j, i0, SUBLANES))
    return pieces


def _route_kernel(attn_ref, rec_ref, x_ref, wo_ref, g_ref, wq_ref, k1_ref, k2_ref,
                  h_ref, xn_ref, idx_ref, gate_ref):
    aw = attn_ref.shape[1]
    h = (x_ref[...]
         + jnp.dot(attn_ref[...], wo_ref[0:aw, :], preferred_element_type=F32)
         + jnp.dot(rec_ref[...], wo_ref[aw:, :], preferred_element_type=F32))
    h_ref[...] = h
    xn = _rms(h, g_ref[...])
    xn_ref[...] = xn
    q = jnp.dot(xn.astype(BF16), wq_ref[...], preferred_element_type=F32).astype(BF16)
    nt = (((1,), (1,)), ((), ()))
    s1 = lax.dot_general(k1_ref[...], q, nt, preferred_element_type=F32)
    s2 = lax.dot_general(k2_ref[...], q, nt, preferred_element_type=F32)
    k = PEER_TOPK
    tm = q.shape[0]
    pieces = _candidate_pieces(k)
    pos_parts, ok_parts = [], []
    for kind, fixed, start, length in pieces:
        r = lax.broadcasted_iota(I32, (length, tm), 0) + start
        i, j = (fixed, r) if kind == "row" else (r, fixed)
        pos_parts.append(i * k + j)
        ok = (i + 1) * (j + 1) <= k
        ok_parts.append(ok if kind == "row" else ok & (r >= CAND_ROW_PIECES))
    pos = jnp.concatenate(pos_parts, axis=0)
    ok = jnp.concatenate(ok_parts, axis=0)
    idx_rows, gate_rows = [], []
    for hd in range(PEER_HEADS):
        v1, i1 = _topk_rows(s1[hd * N_KEYS:(hd + 1) * N_KEYS, :], k)
        v2, i2 = _topk_rows(s2[hd * N_KEYS:(hd + 1) * N_KEYS, :], k)
        cand_parts, cidx_parts = [], []
        for kind, fixed, start, length in pieces:
            if kind == "row":
                cand_parts.append(v1[fixed:fixed + 1, :] + v2[start:start + length, :])
                cidx_parts.append(i1[fixed:fixed + 1, :] * N_KEYS + i2[start:start + length, :])
            else:
                cand_parts.append(v1[start:start + length, :] + v2[fixed:fixed + 1, :])
                cidx_parts.append(i1[start:start + length, :] * N_KEYS + i2[fixed:fixed + 1, :])
        cand = jnp.where(ok, jnp.concatenate(cand_parts, axis=0), -jnp.inf)
        cidx = jnp.concatenate(cidx_parts, axis=0)
        sc, ids = [], []
        for _ in range(k):
            m = jnp.max(cand, axis=0, keepdims=True)
            sel = jnp.min(jnp.where(cand == m, pos, k * k), axis=0, keepdims=True)
            hit = pos == sel
            sc.append(m)
            ids.append(jnp.max(jnp.where(hit, cidx, -1), axis=0, keepdims=True))
            cand = jnp.where(hit, -jnp.inf, cand)
        sc = jnp.concatenate(sc, axis=0)
        e = jnp.exp(sc - sc[0:1, :])
        gate_rows.append(e / jnp.sum(e, axis=0, keepdims=True))
        idx_rows.append(jnp.concatenate(ids, axis=0))
    idx_ref[...] = jnp.concatenate(idx_rows, axis=0)
    gate_ref[...] = jnp.concatenate(gate_rows, axis=0)


def _mix_route(attn, rec, x2, wo_bf, g, wq_bf, k1t_bf, k2t_bf, tok0, tm):
    T = attn.shape[0]
    D = x2.shape[1]
    NK = PEER_HEADS * PEER_TOPK
    row = lambda i: (i, 0)
    full = lambda i: (0, 0)
    return pl.pallas_call(
        _route_kernel,
        grid=(T // tm,),
        in_specs=[
            pl.BlockSpec((tm, attn.shape[1]), row),
            pl.BlockSpec((tm, rec.shape[1]), row),
            pl.BlockSpec((tm, D), lambda i: (i + tok0 // tm, 0)),
            pl.BlockSpec(wo_bf.shape, full),
            pl.BlockSpec((1, D), full),
            pl.BlockSpec(wq_bf.shape, full),
            pl.BlockSpec(k1t_bf.shape, full),
            pl.BlockSpec(k2t_bf.shape, full),
        ],
        out_specs=[
            pl.BlockSpec((tm, D), row),
            pl.BlockSpec((tm, D), row),
            pl.BlockSpec((NK, tm), lambda i: (0, i)),
            pl.BlockSpec((NK, tm), lambda i: (0, i)),
        ],
        out_shape=[
            jax.ShapeDtypeStruct((T, D), F32),
            jax.ShapeDtypeStruct((T, D), F32),
            jax.ShapeDtypeStruct((NK, T), I32),
            jax.ShapeDtypeStruct((NK, T), F32),
        ],
        compiler_params=pltpu.CompilerParams(
            dimension_semantics=("parallel",), vmem_limit_bytes=VMEM_LIMIT),
        name="mix_route",
    )(attn, rec, x2, wo_bf, g.reshape(1, D), wq_bf, k1t_bf, k2t_bf)


ROW_CHUNKS = 8
HI_HALF_MASK = -65536


def _pack_rows(u, v):
    def words(a):
        half = a.shape[1] // 2
        bits = lax.bitcast_convert_type(a.astype(BF16), jnp.uint16).astype(jnp.uint32)
        return bits[:, :half] | (bits[:, half:] << 16)
    return lax.bitcast_convert_type(jnp.concatenate([words(u), words(v)], axis=1), I32)


def _word_lo(w):
    return pltpu.bitcast(w << 16, F32)


def _word_hi(w):
    return pltpu.bitcast(w & HI_HALF_MASK, F32)
TOKEN_UNROLL = 2
DMA_PRIORITIES = 2
PEER_RING = 4
PEER_PREFETCH = 2


def _peer_kernel(idx_ref, idx_next_ref, xn_ref, gate_ref, uv_hbm, o_ref, *scratch, tt, nk):
    bufs = scratch[:PEER_RING]
    lg_ref, act_ref, sems = scratch[PEER_RING:]
    g = pl.program_id(0)
    n = pl.num_programs(0)
    nt = (((1,), (1,)), ((), ()))
    ch = ROW_CHUNKS
    hc = ch // 2

    def row_copy(ids_ref, q, t, k):
        e = ids_ref[0, 0, (q * tt + t) * nk + k]
        return pltpu.make_async_copy(uv_hbm.at[e], bufs[q].at[t, :, pl.ds(k, 1), :], sems.at[q])

    def wait_group(q):
        pltpu.make_async_copy(bufs[(q + 1) % PEER_RING], bufs[q], sems.at[q]).wait()

    @pl.when(g == 0)
    def _():
        for q in range(PEER_PREFETCH):
            def body(t, carry, q=q):
                for k in range(nk):
                    row_copy(idx_ref, q, t, k).start(priority=k % DMA_PRIORITIES)
                return carry
            lax.fori_loop(0, tt, body, 0)

    def group(q):
        buf = bufs[q]
        row0 = q * tt
        ahead = q + PEER_PREFETCH
        ids_ahead = idx_ref if ahead < PEER_RING else idx_next_ref

        def issue(t, k0, k1):
            for k in range(k0, k1):
                row_copy(ids_ahead, ahead % PEER_RING, t, k).start(priority=k % DMA_PRIORITIES)

        def phase_u(tb, carry):
            for j in range(TOKEN_UNROLL):
                t = tb * TOKEN_UNROLL + j
                issue(t, 0, nk // 2)
                xt = xn_ref[row0 + t]
                acc = None
                for s in range(hc):
                    w = buf[t, s]
                    part = _word_lo(w) * xt[s:s + 1, :] + _word_hi(w) * xt[hc + s:hc + s + 1, :]
                    acc = part if acc is None else acc + part
                hi = acc.astype(BF16)
                lo = (acc - hi.astype(F32)).astype(BF16)
                ones = jnp.ones((SUBLANES, LANES), BF16)
                red = (lax.dot_general(ones, hi, nt, preferred_element_type=F32)
                       + lax.dot_general(ones, lo, nt, preferred_element_type=F32))
                lg_ref[t] = red[0:1, :]
            return carry
        lax.fori_loop(0, tt // TOKEN_UNROLL, phase_u, 0)
        act_ref[...] = _gelu_tanh(lg_ref[...]) * gate_ref[row0:row0 + tt]

        def phase_v(tb, carry):
            for j in range(TOKEN_UNROLL):
                t = tb * TOKEN_UNROLL + j
                issue(t, nk // 2, nk)
                act = act_ref[t].astype(BF16)
                words = [buf[t, hc + s] for s in range(hc)]
                cols = ([jnp.dot(act, _word_lo(w).astype(BF16), preferred_element_type=F32) for w in words]
                        + [jnp.dot(act, _word_hi(w).astype(BF16), preferred_element_type=F32) for w in words])
                o_ref[row0 + t] = jnp.concatenate(cols, axis=0)
            return carry
        lax.fori_loop(0, tt // TOKEN_UNROLL, phase_v, 0)

    for q in range(PEER_RING):
        wait_group(q)
        group(q)

    @pl.when(g == n - 1)
    def _():
        for q in range(PEER_PREFETCH):
            wait_group(q)


def _peer(idx, gate, xn3, uv4, n_tokens, tt):
    T, ch, _ = xn3.shape
    nk = idx.shape[1]
    step = PEER_RING * tt
    n = n_tokens // step
    idx3 = idx.reshape(T // step, 1, step * nk)
    tok = lambda i: (i, 0, 0)
    return pl.pallas_call(
        functools.partial(_peer_kernel, tt=tt, nk=nk),
        grid=(n,),
        in_specs=[
            pl.BlockSpec((1, 1, step * nk), tok, memory_space=pltpu.SMEM),
            pl.BlockSpec((1, 1, step * nk), lambda i: (jnp.minimum(i + 1, n - 1), 0, 0), memory_space=pltpu.SMEM),
            pl.BlockSpec((step, ch, LANES), tok),
            pl.BlockSpec((step, 1, nk), tok),
            pl.BlockSpec(memory_space=pl.ANY),
        ],
        out_specs=pl.BlockSpec((step, ch, LANES), tok),
        out_shape=jax.ShapeDtypeStruct((n_tokens, ch, LANES), F32),
        scratch_shapes=(
            [pltpu.VMEM((tt, ch, nk, LANES), I32) for _ in range(PEER_RING)]
            + [pltpu.VMEM((tt, 1, nk), F32), pltpu.VMEM((tt, 1, nk), F32), pltpu.SemaphoreType.DMA((PEER_RING,))]),
        compiler_params=pltpu.CompilerParams(
            dimension_semantics=("arbitrary",), vmem_limit_bytes=VMEM_LIMIT),
        name="peer",
    )(idx3, idx3, xn3, gate.reshape(T, 1, nk), uv4)


SC_CORES = 2
SC_SUBCORES = 16
SC_LANES = 16
SC_WORKERS = SC_CORES * SC_SUBCORES
SC_EXPERT_BLOCK = 16
SC_CHUNK_GROUP = 4
SC_TAIL_SHARE_NUM, SC_TAIL_SHARE_DEN = 5, 32


def _peer_sc(idx, gate, xn, uv, tok0, toks_per_worker):
    T, nk = idx.shape
    D = xn.shape[1]
    L = SC_LANES
    kb_rows = SC_EXPERT_BLOCK
    n_blocks = nk // kb_rows
    n_chunks = D // L
    half = D // 2
    uc = SC_CHUNK_GROUP
    word_lo = lambda w: plsc.bitcast(w << 16, F32)
    word_hi = lambda w: plsc.bitcast(w & HI_HALF_MASK, F32)
    mesh = plsc.VectorSubcoreMesh(core_axis_name="c", subcore_axis_name="s")

    def body(idx_hbm, gate_hbm, xn_hbm, uv_hbm, out_hbm, idx_v, gate_v, x_v, o_v, rows_v, acc_v, sems):
        wid = lax.axis_index("s") * SC_CORES + lax.axis_index("c")
        base = wid * toks_per_worker
        lane = lax.iota(I32, L)

        def gather(slot, kb, buf):
            return pltpu.make_async_copy(uv_hbm.at[idx_v.at[slot].at[pl.ds(kb * kb_rows, kb_rows)]],
                                         rows_v.at[buf], sems.at[buf])

        def evaluate(kb, rv):
            for j in range(kb_rows):
                acc_v[j] = jnp.zeros((L,), F32)

            @plsc.parallel_loop(0, half // L // uc, unroll=2)
            def _(cg):
                x_lo = [x_v[pl.ds((cg * uc + cc) * L, L)] for cc in range(uc)]
                x_hi = [x_v[pl.ds(half + (cg * uc + cc) * L, L)] for cc in range(uc)]
                for j in range(kb_rows):
                    pr = None
                    for cc in range(uc):
                        w = rv[j, pl.ds((cg * uc + cc) * L, L)]
                        part = word_lo(w) * x_lo[cc] + word_hi(w) * x_hi[cc]
                        pr = part if pr is None else pr + part
                    plsc.addupdate(acc_v.at[j], pr)
            logits = jnp.zeros((L,), F32)
            for j in range(kb_rows):
                logits = jnp.where(lane == j, jnp.sum(acc_v[j]), logits)
            z = math.sqrt(2.0 / math.pi) * (logits + 0.044715 * (logits * logits * logits))
            th = 1.0 - 2.0 / (1.0 + jnp.exp(2.0 * z))
            act = 0.5 * logits * (1.0 + th) * gate_v[pl.ds(kb * kb_rows, kb_rows)]
            acts = [jnp.sum(jnp.where(lane == j, act, 0.0)) for j in range(kb_rows)]

            @plsc.parallel_loop(0, half // L, unroll=2)
            def _(c):
                acc_lo = None
                acc_hi = None
                for j in range(kb_rows):
                    w = rv[j, pl.ds(half + c * L, L)]
                    lo = acts[j] * word_lo(w)
                    hi = acts[j] * word_hi(w)
                    acc_lo = lo if acc_lo is None else acc_lo + lo
                    acc_hi = hi if acc_hi is None else acc_hi + hi
                plsc.addupdate(o_v.at[pl.ds(c * L, L)], acc_lo)
                plsc.addupdate(o_v.at[pl.ds(half + c * L, L)], acc_hi)

        def token(ti, carry):
            slot = ti % 2
            tok = tok0 + base + ti
            nxt = tok0 + base + jnp.minimum(ti + 1, toks_per_worker - 1)
            pltpu.sync_copy(idx_hbm.at[nxt], idx_v.at[1 - slot])
            pltpu.sync_copy(gate_hbm.at[tok], gate_v)
            pltpu.sync_copy(xn_hbm.at[tok], x_v)

            @plsc.parallel_loop(0, n_chunks)
            def _(c):
                o_v[pl.ds(c * L, L)] = jnp.zeros((L,), F32)

            def block(kb, carry):
                buf = kb % 2
                last = kb + 1 >= n_blocks
                gather(slot, kb, buf).wait()
                gather(jnp.where(last, 1 - slot, slot), jnp.where(last, 0, kb + 1), 1 - buf).start()
                evaluate(kb, rows_v.at[buf])
                return carry
            lax.fori_loop(0, n_blocks, block, 0)
            pltpu.sync_copy(o_v, out_hbm.at[base + ti])
            return carry

        pltpu.sync_copy(idx_hbm.at[tok0 + base], idx_v.at[0])
        gather(0, 0, 0).start()
        lax.fori_loop(0, toks_per_worker, token, 0)
        gather(0, 0, 0).wait()

    return pl.kernel(
        body, mesh=mesh,
        compiler_params=pltpu.CompilerParams(needs_layout_passes=False),
        out_type=jax.ShapeDtypeStruct((SC_WORKERS * toks_per_worker, D), F32),
        scratch_types=[
            pltpu.VMEM((2, nk), I32),
            pltpu.VMEM((nk,), F32),
            pltpu.VMEM((D,), F32),
            pltpu.VMEM((D,), F32),
            pltpu.VMEM((2, kb_rows, D), I32),
            pltpu.VMEM((kb_rows, L), F32),
            pltpu.SemaphoreType.DMA((2,)),
        ],
    )(idx, gate, xn, uv)


def _ple_kernel(h_ref, po_ref, p_ref, g3_ref, wg_ref, wp_ref, gf_ref, o_ref, *, final):
    h = h_ref[...] + po_ref[...]
    xn = _rms(h, g3_ref[...]).astype(BF16)
    gate = _sigmoid(jnp.dot(xn, wg_ref[...], preferred_element_type=F32))
    proj = jnp.dot(p_ref[...].astype(BF16), wp_ref[...], preferred_element_type=F32)
    h = h + gate * proj
    o_ref[...] = _rms(h, gf_ref[...]) if final else h


def _ple_out(h1, peer_out, p2, g3, wg_bf, wp_bf, gf, final, tm):
    T, D = h1.shape
    row = lambda i: (i, 0)
    full = lambda i: (0, 0)
    return pl.pallas_call(
        functools.partial(_ple_kernel, final=final),
        grid=(T // tm,),
        in_specs=[
            pl.BlockSpec((tm, D), row),
            pl.BlockSpec((tm, D), row),
            pl.BlockSpec((tm, p2.shape[1]), row),
            pl.BlockSpec((1, D), full),
            pl.BlockSpec(wg_bf.shape, full),
            pl.BlockSpec(wp_bf.shape, full),
            pl.BlockSpec((1, D), full),
        ],
        out_specs=pl.BlockSpec((tm, D), row),
        out_shape=jax.ShapeDtypeStruct((T, D), F32),
        compiler_params=pltpu.CompilerParams(
            dimension_semantics=("parallel",), vmem_limit_bytes=VMEM_LIMIT),
        name="ple_out",
    )(h1, peer_out, p2, g3.reshape(1, D), wg_bf, wp_bf, gf.reshape(1, D))


def _block_diag(w):
    nb, bw, _ = w.shape
    eye = jnp.eye(nb, dtype=w.dtype)
    return (eye[:, None, :, None] * w[:, :, None, :]).reshape(nb * bw, nb * bw)


def _key_matrix(keys, half):
    z = jnp.zeros_like(keys)
    blk = jnp.concatenate([keys, z] if half == 0 else [z, keys], axis=1)
    return jnp.kron(jnp.eye(PEER_HEADS, dtype=keys.dtype), blk)


def kernel(x, p, positions, norm_mix_g, w_in, lambda_q1, lambda_k1, lambda_q2, lambda_k2, diff_norm_g, conv_w, conv_b, lru_wa, lru_ba, lru_wx, lru_bx, lru_lambda, lru_norm_g, w_out, norm_ffn_g, peer_wq, peer_keys1, peer_keys2, peer_u, peer_v, norm_ple_g, ple_w_gate, ple_w_proj, final_norm_g):
    B, S, D = x.shape
    T = B * S
    depth = w_in.shape[0]
    assert B % 2 == 0
    nb = B // 2
    th = nb * S
    h = x
    for i in range(depth):
        lambda_init = 0.8 - 0.6 * math.exp(-0.3 * i)
        w_in_bf = w_in[i].astype(BF16)
        w_all = jnp.concatenate([_block_diag(lru_wa[i, 0]), _block_diag(lru_wx[i, 0]),
                                 _block_diag(lru_wa[i, 1]), _block_diag(lru_wx[i, 1])], axis=1).astype(BF16)
        bias_all = jnp.concatenate([lru_ba[i, 0], lru_bx[i, 0], lru_ba[i, 1], lru_bx[i, 1]]).reshape(1, -1)
        wo_bf, wq_bf = w_out[i].astype(BF16), peer_wq[i].astype(BF16)
        k1t_bf = _key_matrix(peer_keys1[i], 0).astype(BF16)
        k2t_bf = _key_matrix(peer_keys2[i], 1).astype(BF16)
        n_exp = peer_u.shape[1]
        uv2 = _pack_rows(peer_u[i], peer_v[i])
        uv4 = uv2.reshape(n_exp, ROW_CHUNKS, 1, LANES)

        def mix_and_route(b0, h=h, i=i):
            q, k, v, u, gate = _in_proj(h, positions, norm_mix_g[i], w_in_bf, b0, nb, tm=512)
            attn = _diff_attn(q, k, v, lambda_q1[i], lambda_k1[i], lambda_q2[i], lambda_k2[i],
                              diff_norm_g[i], lambda_init, tq=256)
            rec = _bi_rglru(u, gate, conv_w[i], conv_b[i], w_all, bias_all, lru_lambda[i], lru_norm_g[i], tc=256)
            h1, xn2, idx_t, gate_t = _mix_route(attn.reshape(th, -1), rec.reshape(th, -1), h.reshape(T, D), wo_bf,
                                                norm_ffn_g[i], wq_bf, k1t_bf, k2t_bf, b0 * S, tm=256)
            return h1, xn2, idx_t.T, gate_t.T

        h1_a, xn_a, idx_a, gate_a = mix_and_route(0)
        peer_a = _peer_sc(idx_a, gate_a, xn_a, uv2, 0, th // SC_WORKERS)
        h1_b, xn_b, idx_b, gate_b = mix_and_route(nb)
        t_sc = th * SC_TAIL_SHARE_NUM // SC_TAIL_SHARE_DEN // (2 * SC_WORKERS) * (2 * SC_WORKERS)
        t_tc = th - t_sc
        peer_b_sc = _peer_sc(idx_b, gate_b, xn_b, uv2, t_tc, t_sc // SC_WORKERS)
        peer_b_tc = _peer(idx_b, gate_b, xn_b.reshape(th, ROW_CHUNKS, LANES), uv4, t_tc, tt=8)
        peer_out = jnp.concatenate([peer_a, peer_b_tc.reshape(t_tc, D), peer_b_sc], axis=0)
        h1 = jnp.concatenate([h1_a, h1_b], axis=0)
        h = _ple_out(h1, peer_out, p[i].reshape(T, -1), norm_ple_g[i], ple_w_gate[i].astype(BF16),
                     ple_w_proj[i].astype(BF16), final_norm_g, final=(i == depth - 1), tm=512)
        h = h.reshape(B, S, D)
    return h
```

```python
import functools
import math

import jax
import jax.numpy as jnp
from jax import lax
from jax.experimental import pallas as pl
from jax.experimental.pallas import tpu as pltpu
from jax.experimental.pallas import tpu_sc as plsc

F32 = jnp.float32
BF16 = jnp.bfloat16
I32 = jnp.int32

EPS = 1e-6
DIFF_HEAD_DIM = 64
DIFF_V_DIM = 128
N_DIFF_HEADS = 4
ROPE_DIM = 16
ROPE_THETA = 500000.0
LRU_WIDTH = 512
LRU_C = 8.0
N_KEYS = 128
PEER_HEADS = 8
PEER_TOPK = 16
HALF_KEY = 64
LOG2_E = math.log2(math.e)
LANES = 128
SUBLANES = 8
VMEM_LIMIT = 56 * 1024 * 1024


def _rms(x, g):
    return x * lax.rsqrt(jnp.mean(x * x, axis=-1, keepdims=True) + EPS) * g


def _gelu_tanh(x):
    return 0.5 * x * (1.0 + jnp.tanh(math.sqrt(2.0 / math.pi) * (x + 0.044715 * (x * x * x))))


def _sigmoid(x):
    return 1.0 / (1.0 + jnp.exp(-x))


def _inproj_kernel(x_ref, pos_ref, g_ref, w_ref, q_ref, k_ref, v_ref, u_ref, gate_ref):
    x = x_ref[0]
    xn = _rms(x, g_ref[...]).astype(BF16)
    pos = pos_ref[0].astype(F32)
    lane = lax.broadcasted_iota(I32, (1, LANES), 1)
    p = lane & (DIFF_HEAD_DIM - 1)
    freq = (p & (ROPE_DIM // 2 - 1)).astype(F32)
    inv_freq = jnp.exp(freq * (-2.0 / ROPE_DIM * math.log(ROPE_THETA)))
    ang = pos * inv_freq
    cs = jnp.cos(ang)
    sn = jnp.sin(ang)
    half = ROPE_DIM // 2
    c_mul = jnp.where(p < ROPE_DIM, cs, 1.0)
    s_up = jnp.where(p < half, -sn, 0.0)
    s_dn = jnp.where((p >= half) & (p < ROPE_DIM), sn, 0.0)

    def rope(t):
        return t * c_mul + pltpu.roll(t, LANES - half, 1) * s_up + pltpu.roll(t, half, 1) * s_dn

    nq = q_ref.shape[-1]
    pq = jnp.dot(xn, w_ref[:, 0:nq], preferred_element_type=F32)
    pk = jnp.dot(xn, w_ref[:, nq:2 * nq], preferred_element_type=F32)
    scale = DIFF_HEAD_DIM ** -0.5 * LOG2_E
    for j in range(nq // LANES):
        sl = slice(j * LANES, (j + 1) * LANES)
        q_ref[0, :, sl] = (rope(pq[:, sl]) * scale).astype(BF16)
        k_ref[0, :, sl] = rope(pk[:, sl]).astype(BF16)
    v_ref[0] = jnp.dot(xn, w_ref[:, 2 * nq:3 * nq], preferred_element_type=F32).astype(BF16)
    u_ref[0] = jnp.dot(xn, w_ref[:, 3 * nq:3 * nq + LRU_WIDTH], preferred_element_type=F32)
    gate_ref[0] = jnp.dot(xn, w_ref[:, 3 * nq + LRU_WIDTH:3 * nq + 2 * LRU_WIDTH],
                          preferred_element_type=F32).astype(BF16)


def _in_proj(x, positions, g, w_in_bf, b0, nb, tm):
    S, D = x.shape[1], x.shape[2]
    B = nb
    nq = N_DIFF_HEADS * 2 * DIFF_HEAD_DIM
    ncols = w_in_bf.shape[1]
    row = lambda b, i: (b, i, 0)
    row_in = lambda b, i: (b + b0, i, 0)
    return pl.pallas_call(
        _inproj_kernel,
        grid=(B, S // tm),
        in_specs=[
            pl.BlockSpec((1, tm, D), row_in),
            pl.BlockSpec((1, tm, 1), row_in),
            pl.BlockSpec((1, D), lambda b, i: (0, 0)),
            pl.BlockSpec((D, ncols), lambda b, i: (0, 0)),
        ],
        out_specs=[
            pl.BlockSpec((1, tm, nq), row),
            pl.BlockSpec((1, tm, nq), row),
            pl.BlockSpec((1, tm, nq), row),
            pl.BlockSpec((1, tm, LRU_WIDTH), row),
            pl.BlockSpec((1, tm, LRU_WIDTH), row),
        ],
        out_shape=[
            jax.ShapeDtypeStruct((B, S, nq), BF16),
            jax.ShapeDtypeStruct((B, S, nq), BF16),
            jax.ShapeDtypeStruct((B, S, nq), BF16),
            jax.ShapeDtypeStruct((B, S, LRU_WIDTH), F32),
            jax.ShapeDtypeStruct((B, S, LRU_WIDTH), BF16),
        ],
        compiler_params=pltpu.CompilerParams(
            dimension_semantics=("parallel", "parallel"), vmem_limit_bytes=VMEM_LIMIT),
        name="in_proj",
    )(x, positions.reshape(x.shape[0], S, 1), g.reshape(1, D), w_in_bf)


def _attn_kernel(lq1_ref, lk1_ref, lq2_ref, lk2_ref, g_ref, q_ref, k_ref, v_ref, o_ref, *, lambda_init):
    lam = (jnp.exp(jnp.sum(lq1_ref[...] * lk1_ref[...], axis=-1, keepdims=True))
           - jnp.exp(jnp.sum(lq2_ref[...] * lk2_ref[...], axis=-1, keepdims=True))
           + lambda_init)
    q = q_ref[0]
    k = k_ref[0]
    v = v_ref[0]
    lane = lax.broadcasted_iota(I32, q.shape, 1)
    zero = jnp.zeros_like(q)
    q0 = jnp.where(lane < DIFF_HEAD_DIM, q, zero)
    q1 = jnp.where(lane >= DIFF_HEAD_DIM, q, zero)
    nt = (((1,), (1,)), ((), ()))
    s0 = lax.dot_general(q0, k, nt, preferred_element_type=F32)
    s1 = lax.dot_general(q1, k, nt, preferred_element_type=F32)
    p0 = jnp.exp2(s0 - jnp.max(s0, axis=-1, keepdims=True))
    p1 = jnp.exp2(s1 - jnp.max(s1, axis=-1, keepdims=True))
    l0 = jnp.sum(p0, axis=-1, keepdims=True)
    l1 = jnp.sum(p1, axis=-1, keepdims=True)
    w = (p0 - p1 * (lam * l0 / l1)).astype(BF16)
    o = jnp.dot(w, v, preferred_element_type=F32) / l0
    o_ref[0] = (_rms(o, g_ref[...]) * (1.0 - lambda_init)).astype(BF16)


def _diff_attn(q, k, v, lq1, lk1, lq2, lk2, g, lambda_init, tq):
    B, S, W = q.shape
    H = W // DIFF_V_DIM
    vec = lambda b, h, i: (0, 0)
    return pl.pallas_call(
        functools.partial(_attn_kernel, lambda_init=lambda_init),
        grid=(B, H, S // tq),
        in_specs=[
            pl.BlockSpec((1, DIFF_HEAD_DIM), vec),
            pl.BlockSpec((1, DIFF_HEAD_DIM), vec),
            pl.BlockSpec((1, DIFF_HEAD_DIM), vec),
            pl.BlockSpec((1, DIFF_HEAD_DIM), vec),
            pl.BlockSpec((1, DIFF_V_DIM), vec),
            pl.BlockSpec((1, tq, DIFF_V_DIM), lambda b, h, i: (b, i, h)),
            pl.BlockSpec((1, S, DIFF_V_DIM), lambda b, h, i: (b, 0, h)),
            pl.BlockSpec((1, S, DIFF_V_DIM), lambda b, h, i: (b, 0, h)),
        ],
        out_specs=pl.BlockSpec((1, tq, DIFF_V_DIM), lambda b, h, i: (b, i, h)),
        out_shape=jax.ShapeDtypeStruct((B, S, W), BF16),
        compiler_params=pltpu.CompilerParams(
            dimension_semantics=("parallel", "parallel", "parallel"), vmem_limit_bytes=VMEM_LIMIT),
        name="diff_attn",
    )(lq1.reshape(1, -1), lk1.reshape(1, -1), lq2.reshape(1, -1), lk2.reshape(1, -1),
      g.reshape(1, -1), q, k, v)


def _lru_kernel(u_ref, gate_ref, cw_ref, cb_ref, w_ref, bias_ref, lam_ref, g_ref, out_ref, hf_ref, *, tc):
    S = u_ref.shape[1]
    C = u_ref.shape[2]
    nc = S // tc
    halo = SUBLANES
    neg_lam = -lam_ref[...]
    sp = jnp.maximum(neg_lam, 0.0) + jnp.log(1.0 + jnp.exp(-jnp.abs(neg_lam)))
    row = lax.broadcasted_iota(I32, (tc, 1), 0)
    conv_taps = cw_ref.shape[0]
    conv_left = 2

    def gates(c, d):
        r0 = pl.multiple_of(c * tc, tc)
        x = u_ref[0, pl.ds(r0, tc), :]
        prev = u_ref[0, pl.ds(pl.multiple_of(jnp.maximum(r0 - halo, 0), halo), halo), :]
        nxt = u_ref[0, pl.ds(pl.multiple_of(jnp.minimum(r0 + tc, S - halo), halo), halo), :]
        prev = jnp.where(c > 0, prev, 0.0)
        nxt = jnp.where(c < nc - 1, nxt, 0.0)
        win = jnp.concatenate([prev, x, nxt], axis=0)
        uc = cb_ref[...]
        for j in range(conv_taps):
            o = halo - conv_left + j
            uc = uc + cw_ref[j:j + 1, :] * win[o:o + tc, :]
        pre = (jnp.dot(uc.astype(BF16), w_ref[:, d * 2 * C:(d + 1) * 2 * C], preferred_element_type=F32)
               + bias_ref[:, d * 2 * C:(d + 1) * 2 * C])
        r = _sigmoid(pre[:, :C])
        i = _sigmoid(pre[:, C:])
        log_a = -LRU_C * r * sp[d:d + 1, :]
        a = jnp.exp(log_a)
        th = jnp.tanh(log_a)
        mult = jnp.sqrt(-2.0 * th / (1.0 - th))
        return a, mult * (i * uc)

    def scan(a, b, reverse):
        d = 1
        while d < tc:
            if reverse:
                a_s = pltpu.roll(a, tc - d, 0)
                b_s = pltpu.roll(b, tc - d, 0)
                m = row < tc - d
            else:
                a_s = pltpu.roll(a, d, 0)
                b_s = pltpu.roll(b, d, 0)
                m = row >= d
            a_s = jnp.where(m, a_s, 1.0)
            b_s = jnp.where(m, b_s, 0.0)
            b = a * b_s + b
            a = a * a_s
            d *= 2
        return a, b

    def fwd_body(c, h0):
        a, b = gates(c, 0)
        a, b = scan(a, b, False)
        h = a * h0 + b
        hf_ref[pl.ds(pl.multiple_of(c * tc, tc), tc), :] = h
        return h[tc - 1:tc, :]

    lax.fori_loop(0, nc, fwd_body, jnp.zeros((1, C), F32))

    def bwd_body(j, h0):
        c = nc - 1 - j
        r0 = pl.multiple_of(c * tc, tc)
        a, b = gates(c, 1)
        a, b = scan(a, b, True)
        h = a * h0 + b
        y = (hf_ref[pl.ds(r0, tc), :] + h) * _gelu_tanh(gate_ref[0, pl.ds(r0, tc), :].astype(F32))
        out_ref[0, pl.ds(r0, tc), :] = _rms(y, g_ref[...]).astype(BF16)
        return h[0:1, :]

    lax.fori_loop(0, nc, bwd_body, jnp.zeros((1, C), F32))


def _bi_rglru(u, gate, conv_w, conv_b, w_all_bf, bias_all, lru_lambda, g, tc):
    B, S, C = u.shape
    full = lambda b: (0, 0)
    return pl.pallas_call(
        functools.partial(_lru_kernel, tc=tc),
        grid=(B,),
        in_specs=[
            pl.BlockSpec((1, S, C), lambda b: (b, 0, 0)),
            pl.BlockSpec((1, S, C), lambda b: (b, 0, 0)),
            pl.BlockSpec(conv_w.shape, full),
            pl.BlockSpec((1, C), full),
            pl.BlockSpec(w_all_bf.shape, full),
            pl.BlockSpec(bias_all.shape, full),
            pl.BlockSpec(lru_lambda.shape, full),
            pl.BlockSpec((1, C), full),
        ],
        out_specs=pl.BlockSpec((1, S, C), lambda b: (b, 0, 0)),
        out_shape=jax.ShapeDtypeStruct((B, S, C), BF16),
        scratch_shapes=[pltpu.VMEM((S, C), F32)],
        compiler_params=pltpu.CompilerParams(
            dimension_semantics=("parallel",), vmem_limit_bytes=VMEM_LIMIT),
        name="bi_rglru",
    )(u, gate, conv_w, conv_b.reshape(1, C), w_all_bf, bias_all, lru_lambda, g.reshape(1, C))


def _topk_rows(s, k):
    n = s.shape[0]
    rid = lax.broadcasted_iota(I32, s.shape, 0)
    vals, ids = [], []
    for _ in range(k):
        m = jnp.max(s, axis=0, keepdims=True)
        sel = jnp.min(jnp.where(s == m, rid, n), axis=0, keepdims=True)
        vals.append(m)
        ids.append(sel)
        s = jnp.where(rid == sel, -jnp.inf, s)
    return jnp.concatenate(vals, axis=0), jnp.concatenate(ids, axis=0)


CAND_ROW_PIECES = 4


def _candidate_pieces(k):
    up = lambda n: -(-n // SUBLANES) * SUBLANES
    pieces = [("row", i, 0, up(k // (i + 1))) for i in range(CAND_ROW_PIECES)]
    for j in range(k // (CAND_ROW_PIECES + 1)):
        for i0 in range(0, k // (j + 1), SUBLANES):
            pieces.append(("col", j, i0, SUBLANES))
    return pieces


def _route_kernel(attn_ref, rec_ref, x_ref, wo_ref, g_ref, wq_ref, k1_ref, k2_ref,
                  h_ref, xn_ref, idx_ref, gate_ref):
    aw = attn_ref.shape[1]
    h = (x_ref[...]
         + jnp.dot(attn_ref[...], wo_ref[0:aw, :], preferred_element_type=F32)
         + jnp.dot(rec_ref[...], wo_ref[aw:, :], preferred_element_type=F32))
    h_ref[...] = h
    xn = _rms(h, g_ref[...])
    xn_ref[...] = xn
    q = jnp.dot(xn.astype(BF16), wq_ref[...], preferred_element_type=F32).astype(BF16)
    nt = (((1,), (1,)), ((), ()))
    s1 = lax.dot_general(k1_ref[...], q, nt, preferred_element_type=F32)
    s2 = lax.dot_general(k2_ref[...], q, nt, preferred_element_type=F32)
    k = PEER_TOPK
    tm = q.shape[0]
    pieces = _candidate_pieces(k)
    pos_parts, ok_parts = [], []
    for kind, fixed, start, length in pieces:
        r = lax.broadcasted_iota(I32, (length, tm), 0) + start
        i, j = (fixed, r) if kind == "row" else (r, fixed)
        pos_parts.append(i * k + j)
        ok = (i + 1) * (j + 1) <= k
        ok_parts.append(ok if kind == "row" else ok & (r >= CAND_ROW_PIECES))
    pos = jnp.concatenate(pos_parts, axis=0)
    ok = jnp.concatenate(ok_parts, axis=0)
    idx_rows, gate_rows = [], []
    for hd in range(PEER_HEADS):
        v1, i1 = _topk_rows(s1[hd * N_KEYS:(hd + 1) * N_KEYS, :], k)
        v2, i2 = _topk_rows(s2[hd * N_KEYS:(hd + 1) * N_KEYS, :], k)
        cand_parts, cidx_parts = [], []
        for kind, fixed, start, length in pieces:
            if kind == "row":
                cand_parts.append(v1[fixed:fixed + 1, :] + v2[start:start + length, :])
                cidx_parts.append(i1[fixed:fixed + 1, :] * N_KEYS + i2[start:start + length, :])
            else:
                cand_parts.append(v1[start:start + length, :] + v2[fixed:fixed + 1, :])
                cidx_parts.append(i1[start:start + length, :] * N_KEYS + i2[fixed:fixed + 1, :])
        cand = jnp.where(ok, jnp.concatenate(cand_parts, axis=0), -jnp.inf)
        cidx = jnp.concatenate(cidx_parts, axis=0)
        sc, ids = [], []
        for _ in range(k):
            m = jnp.max(cand, axis=0, keepdims=True)
            sel = jnp.min(jnp.where(cand == m, pos, k * k), axis=0, keepdims=True)
            hit = pos == sel
            sc.append(m)
            ids.append(jnp.max(jnp.where(hit, cidx, -1), axis=0, keepdims=True))
            cand = jnp.where(hit, -jnp.inf, cand)
        sc = jnp.concatenate(sc, axis=0)
        e = jnp.exp(sc - sc[0:1, :])
        gate_rows.append(e / jnp.sum(e, axis=0, keepdims=True))
        idx_rows.append(jnp.concatenate(ids, axis=0))
    idx_ref[...] = jnp.concatenate(idx_rows, axis=0)
    gate_ref[...] = jnp.concatenate(gate_rows, axis=0)


def _mix_route(attn, rec, x2, wo_bf, g, wq_bf, k1t_bf, k2t_bf, tok0, tm):
    T = attn.shape[0]
    D = x2.shape[1]
    NK = PEER_HEADS * PEER_TOPK
    row = lambda i: (i, 0)
    full = lambda i: (0, 0)
    return pl.pallas_call(
        _route_kernel,
        grid=(T // tm,),
        in_specs=[
            pl.BlockSpec((tm, attn.shape[1]), row),
            pl.BlockSpec((tm, rec.shape[1]), row),
            pl.BlockSpec((tm, D), lambda i: (i + tok0 // tm, 0)),
            pl.BlockSpec(wo_bf.shape, full),
            pl.BlockSpec((1, D), full),
            pl.BlockSpec(wq_bf.shape, full),
            pl.BlockSpec(k1t_bf.shape, full),
            pl.BlockSpec(k2t_bf.shape, full),
        ],
        out_specs=[
            pl.BlockSpec((tm, D), row),
            pl.BlockSpec((tm, D), row),
            pl.BlockSpec((NK, tm), lambda i: (0, i)),
            pl.BlockSpec((NK, tm), lambda i: (0, i)),
        ],
        out_shape=[
            jax.ShapeDtypeStruct((T, D), F32),
            jax.ShapeDtypeStruct((T, D), F32),
            jax.ShapeDtypeStruct((NK, T), I32),
            jax.ShapeDtypeStruct((NK, T), F32),
        ],
        compiler_params=pltpu.CompilerParams(
            dimension_semantics=("parallel",), vmem_limit_bytes=VMEM_LIMIT),
        name="mix_route",
    )(attn, rec, x2, wo_bf, g.reshape(1, D), wq_bf, k1t_bf, k2t_bf)


ROW_CHUNKS = 8
HI_HALF_MASK = -65536


def _pack_rows(u, v):
    def words(a):
        half = a.shape[1] // 2
        bits = lax.bitcast_convert_type(a.astype(BF16), jnp.uint16).astype(jnp.uint32)
        return bits[:, :half] | (bits[:, half:] << 16)
    return lax.bitcast_convert_type(jnp.concatenate([words(u), words(v)], axis=1), I32)


def _word_lo(w):
    return pltpu.bitcast(w << 16, F32)


def _word_hi(w):
    return pltpu.bitcast(w & HI_HALF_MASK, F32)
TOKEN_UNROLL = 2
DMA_PRIORITIES = 2
PEER_RING = 4
PEER_PREFETCH = 2


def _peer_kernel(idx_ref, idx_next_ref, xn_ref, gate_ref, uv_hbm, o_ref, *scratch, tt, nk):
    bufs = scratch[:PEER_RING]
    lg_ref, act_ref, sems = scratch[PEER_RING:]
    g = pl.program_id(0)
    n = pl.num_programs(0)
    nt = (((1,), (1,)), ((), ()))
    ch = ROW_CHUNKS
    hc = ch // 2

    def row_copy(ids_ref, q, t, k):
        e = ids_ref[0, 0, (q * tt + t) * nk + k]
        return pltpu.make_async_copy(uv_hbm.at[e], bufs[q].at[t, :, pl.ds(k, 1), :], sems.at[q])

    def wait_group(q):
        pltpu.make_async_copy(bufs[(q + 1) % PEER_RING], bufs[q], sems.at[q]).wait()

    @pl.when(g == 0)
    def _():
        for q in range(PEER_PREFETCH):
            def body(t, carry, q=q):
                for k in range(nk):
                    row_copy(idx_ref, q, t, k).start(priority=k % DMA_PRIORITIES)
                return carry
            lax.fori_loop(0, tt, body, 0)

    def group(q):
        buf = bufs[q]
        row0 = q * tt
        ahead = q + PEER_PREFETCH
        ids_ahead = idx_ref if ahead < PEER_RING else idx_next_ref

        def issue(t, k0, k1):
            for k in range(k0, k1):
                row_copy(ids_ahead, ahead % PEER_RING, t, k).start(priority=k % DMA_PRIORITIES)

        def phase_u(tb, carry):
            for j in range(TOKEN_UNROLL):
                t = tb * TOKEN_UNROLL + j
                issue(t, 0, nk // 2)
                xt = xn_ref[row0 + t]
                acc = None
                for s in range(hc):
                    w = buf[t, s]
                    part = _word_lo(w) * xt[s:s + 1, :] + _word_hi(w) * xt[hc + s:hc + s + 1, :]
                    acc = part if acc is None else acc + part
                hi = acc.astype(BF16)
                lo = (acc - hi.astype(F32)).astype(BF16)
                ones = jnp.ones((SUBLANES, LANES), BF16)
                red = (lax.dot_general(ones, hi, nt, preferred_element_type=F32)
                       + lax.dot_general(ones, lo, nt, preferred_element_type=F32))
                lg_ref[t] = red[0:1, :]
            return carry
        lax.fori_loop(0, tt // TOKEN_UNROLL, phase_u, 0)
        act_ref[...] = _gelu_tanh(lg_ref[...]) * gate_ref[row0:row0 + tt]

        def phase_v(tb, carry):
            for j in range(TOKEN_UNROLL):
                t = tb * TOKEN_UNROLL + j
                issue(t, nk // 2, nk)
                act = act_ref[t].astype(BF16)
                words = [buf[t, hc + s] for s in range(hc)]
                cols = ([jnp.dot(act, _word_lo(w).astype(BF16), preferred_element_type=F32) for w in words]
                        + [jnp.dot(act, _word_hi(w).astype(BF16), preferred_element_type=F32) for w in words])
                o_ref[row0 + t] = jnp.concatenate(cols, axis=0)
            return carry
        lax.fori_loop(0, tt // TOKEN_UNROLL, phase_v, 0)

    for q in range(PEER_RING):
        wait_group(q)
        group(q)

    @pl.when(g == n - 1)
    def _():
        for q in range(PEER_PREFETCH):
            wait_group(q)


def _peer(idx, gate, xn3, uv4, n_tokens, tt):
    T, ch, _ = xn3.shape
    nk = idx.shape[1]
    step = PEER_RING * tt
    n = n_tokens // step
    idx3 = idx.reshape(T // step, 1, step * nk)
    tok = lambda i: (i, 0, 0)
    return pl.pallas_call(
        functools.partial(_peer_kernel, tt=tt, nk=nk),
        grid=(n,),
        in_specs=[
            pl.BlockSpec((1, 1, step * nk), tok, memory_space=pltpu.SMEM),
            pl.BlockSpec((1, 1, step * nk), lambda i: (jnp.minimum(i + 1, n - 1), 0, 0), memory_space=pltpu.SMEM),
            pl.BlockSpec((step, ch, LANES), tok),
            pl.BlockSpec((step, 1, nk), tok),
            pl.BlockSpec(memory_space=pl.ANY),
        ],
        out_specs=pl.BlockSpec((step, ch, LANES), tok),
        out_shape=jax.ShapeDtypeStruct((n_tokens, ch, LANES), F32),
        scratch_shapes=(
            [pltpu.VMEM((tt, ch, nk, LANES), I32) for _ in range(PEER_RING)]
            + [pltpu.VMEM((tt, 1, nk), F32), pltpu.VMEM((tt, 1, nk), F32), pltpu.SemaphoreType.DMA((PEER_RING,))]),
        compiler_params=pltpu.CompilerParams(
            dimension_semantics=("arbitrary",), vmem_limit_bytes=VMEM_LIMIT),
        name="peer",
    )(idx3, idx3, xn3, gate.reshape(T, 1, nk), uv4)


SC_CORES = 2
SC_SUBCORES = 16
SC_LANES = 16
SC_WORKERS = SC_CORES * SC_SUBCORES
SC_EXPERT_BLOCK = 16
SC_CHUNK_GROUP = 4
SC_TAIL_SHARE_NUM, SC_TAIL_SHARE_DEN = 5, 32


def _peer_sc(idx, gate, xn, uv, tok0, toks_per_worker):
    T, nk = idx.shape
    D = xn.shape[1]
    L = SC_LANES
    kb_rows = SC_EXPERT_BLOCK
    n_blocks = nk // kb_rows
    n_chunks = D // L
    half = D // 2
    uc = SC_CHUNK_GROUP
    word_lo = lambda w: plsc.bitcast(w << 16, F32)
    word_hi = lambda w: plsc.bitcast(w & HI_HALF_MASK, F32)
    mesh = plsc.VectorSubcoreMesh(core_axis_name="c", subcore_axis_name="s")

    def body(idx_hbm, gate_hbm, xn_hbm, uv_hbm, out_hbm, idx_v, gate_v, x_v, o_v, rows_v, acc_v, sems):
        wid = lax.axis_index("s") * SC_CORES + lax.axis_index("c")
        base = wid * toks_per_worker
        lane = lax.iota(I32, L)

        def gather(slot, kb, buf):
            return pltpu.make_async_copy(uv_hbm.at[idx_v.at[slot].at[pl.ds(kb * kb_rows, kb_rows)]],
                                         rows_v.at[buf], sems.at[buf])

        def evaluate(kb, rv):
            for j in range(kb_rows):
                acc_v[j] = jnp.zeros((L,), F32)

            @plsc.parallel_loop(0, half // L // uc, unroll=2)
            def _(cg):
                x_lo = [x_v[pl.ds((cg * uc + cc) * L, L)] for cc in range(uc)]
                x_hi = [x_v[pl.ds(half + (cg * uc + cc) * L, L)] for cc in range(uc)]
                for j in range(kb_rows):
                    pr = None
                    for cc in range(uc):
                        w = rv[j, pl.ds((cg * uc + cc) * L, L)]
                        part = word_lo(w) * x_lo[cc] + word_hi(w) * x_hi[cc]
                        pr = part if pr is None else pr + part
                    plsc.addupdate(acc_v.at[j], pr)
            logits = jnp.zeros((L,), F32)
            for j in range(kb_rows):
                logits = jnp.where(lane == j, jnp.sum(acc_v[j]), logits)
            z = math.sqrt(2.0 / math.pi) * (logits + 0.044715 * (logits * logits * logits))
            th = 1.0 - 2.0 / (1.0 + jnp.exp(2.0 * z))
            act = 0.5 * logits * (1.0 + th) * gate_v[pl.ds(kb * kb_rows, kb_rows)]
            acts = [jnp.sum(jnp.where(lane == j, act, 0.0)) for j in range(kb_rows)]

            @plsc.parallel_loop(0, half // L, unroll=2)
            def _(c):
                acc_lo = None
                acc_hi = None
                for j in range(kb_rows):
                    w = rv[j, pl.ds(half + c * L, L)]
                    lo = acts[j] * word_lo(w)
                    hi = acts[j] * word_hi(w)
                    acc_lo = lo if acc_lo is None else acc_lo + lo
                    acc_hi = hi if acc_hi is None else acc_hi + hi
                plsc.addupdate(o_v.at[pl.ds(c * L, L)], acc_lo)
                plsc.addupdate(o_v.at[pl.ds(half + c * L, L)], acc_hi)

        def token(ti, carry):
            slot = ti % 2
            tok = tok0 + base + ti
            nxt = tok0 + base + jnp.minimum(ti + 1, toks_per_worker - 1)
            pltpu.sync_copy(idx_hbm.at[nxt], idx_v.at[1 - slot])
            pltpu.sync_copy(gate_hbm.at[tok], gate_v)
            pltpu.sync_copy(xn_hbm.at[tok], x_v)

            @plsc.parallel_loop(0, n_chunks)
            def _(c):
                o_v[pl.ds(c * L, L)] = jnp.zeros((L,), F32)

            def block(kb, carry):
                buf = kb % 2
                last = kb + 1 >= n_blocks
                gather(slot, kb, buf).wait()
                gather(jnp.where(last, 1 - slot, slot), jnp.where(last, 0, kb + 1), 1 - buf).start()
                evaluate(kb, rows_v.at[buf])
                return carry
            lax.fori_loop(0, n_blocks, block, 0)
            pltpu.sync_copy(o_v, out_hbm.at[base + ti])
            return carry

        pltpu.sync_copy(idx_hbm.at[tok0 + base], idx_v.at[0])
        gather(0, 0, 0).start()
        lax.fori_loop(0, toks_per_worker, token, 0)
        gather(0, 0, 0).wait()

    n_tok = SC_WORKERS * toks_per_worker
    cost = pl.CostEstimate(
        flops=4 * n_tok * nk * D,
        transcendentals=n_tok * nk,
        bytes_accessed=4 * n_tok * (nk * D + 2 * nk + 2 * D))
    return pl.kernel(
        body, mesh=mesh, cost_estimate=cost,
        compiler_params=pltpu.CompilerParams(needs_layout_passes=False),
        out_type=jax.ShapeDtypeStruct((SC_WORKERS * toks_per_worker, D), F32),
        scratch_types=[
            pltpu.VMEM((2, nk), I32),
            pltpu.VMEM((nk,), F32),
            pltpu.VMEM((D,), F32),
            pltpu.VMEM((D,), F32),
            pltpu.VMEM((2, kb_rows, D), I32),
            pltpu.VMEM((kb_rows, L), F32),
            pltpu.SemaphoreType.DMA((2,)),
        ],
    )(idx, gate, xn, uv)


def _ple_kernel(h_ref, po_ref, p_ref, g3_ref, wg_ref, wp_ref, gf_ref, o_ref, *, final):
    h = h_ref[...] + po_ref[...]
    xn = _rms(h, g3_ref[...]).astype(BF16)
    gate = _sigmoid(jnp.dot(xn, wg_ref[...], preferred_element_type=F32))
    proj = jnp.dot(p_ref[...].astype(BF16), wp_ref[...], preferred_element_type=F32)
    h = h + gate * proj
    o_ref[...] = _rms(h, gf_ref[...]) if final else h


def _ple_out(h1, peer_out, p2, g3, wg_bf, wp_bf, gf, final, tm):
    T, D = h1.shape
    row = lambda i: (i, 0)
    full = lambda i: (0, 0)
    return pl.pallas_call(
        functools.partial(_ple_kernel, final=final),
        grid=(T // tm,),
        in_specs=[
            pl.BlockSpec((tm, D), row),
            pl.BlockSpec((tm, D), row),
            pl.BlockSpec((tm, p2.shape[1]), row),
            pl.BlockSpec((1, D), full),
            pl.BlockSpec(wg_bf.shape, full),
            pl.BlockSpec(wp_bf.shape, full),
            pl.BlockSpec((1, D), full),
        ],
        out_specs=pl.BlockSpec((tm, D), row),
        out_shape=jax.ShapeDtypeStruct((T, D), F32),
        compiler_params=pltpu.CompilerParams(
            dimension_semantics=("parallel",), vmem_limit_bytes=VMEM_LIMIT),
        name="ple_out",
    )(h1, peer_out, p2, g3.reshape(1, D), wg_bf, wp_bf, gf.reshape(1, D))


def _block_diag(w):
    nb, bw, _ = w.shape
    eye = jnp.eye(nb, dtype=w.dtype)
    return (eye[:, None, :, None] * w[:, :, None, :]).reshape(nb * bw, nb * bw)


def _key_matrix(keys, half):
    z = jnp.zeros_like(keys)
    blk = jnp.concatenate([keys, z] if half == 0 else [z, keys], axis=1)
    return jnp.kron(jnp.eye(PEER_HEADS, dtype=keys.dtype), blk)


def kernel(x, p, positions, norm_mix_g, w_in, lambda_q1, lambda_k1, lambda_q2, lambda_k2, diff_norm_g, conv_w, conv_b, lru_wa, lru_ba, lru_wx, lru_bx, lru_lambda, lru_norm_g, w_out, norm_ffn_g, peer_wq, peer_keys1, peer_keys2, peer_u, peer_v, norm_ple_g, ple_w_gate, ple_w_proj, final_norm_g):
    B, S, D = x.shape
    T = B * S
    depth = w_in.shape[0]
    assert B % 2 == 0
    nb = B // 2
    th = nb * S
    h = x
    for i in range(depth):
        lambda_init = 0.8 - 0.6 * math.exp(-0.3 * i)
        w_in_bf = w_in[i].astype(BF16)
        w_all = jnp.concatenate([_block_diag(lru_wa[i, 0]), _block_diag(lru_wx[i, 0]),
                                 _block_diag(lru_wa[i, 1]), _block_diag(lru_wx[i, 1])], axis=1).astype(BF16)
        bias_all = jnp.concatenate([lru_ba[i, 0], lru_bx[i, 0], lru_ba[i, 1], lru_bx[i, 1]]).reshape(1, -1)
        wo_bf, wq_bf = w_out[i].astype(BF16), peer_wq[i].astype(BF16)
        k1t_bf = _key_matrix(peer_keys1[i], 0).astype(BF16)
        k2t_bf = _key_matrix(peer_keys2[i], 1).astype(BF16)
        n_exp = peer_u.shape[1]
        uv2 = _pack_rows(peer_u[i], peer_v[i])
        uv4 = uv2.reshape(n_exp, ROW_CHUNKS, 1, LANES)

        def mix_and_route(b0, h=h, i=i):
            q, k, v, u, gate = _in_proj(h, positions, norm_mix_g[i], w_in_bf, b0, nb, tm=512)
            attn = _diff_attn(q, k, v, lambda_q1[i], lambda_k1[i], lambda_q2[i], lambda_k2[i],
                              diff_norm_g[i], lambda_init, tq=256)
            rec = _bi_rglru(u, gate, conv_w[i], conv_b[i], w_all, bias_all, lru_lambda[i], lru_norm_g[i], tc=256)
            h1, xn2, idx_t, gate_t = _mix_route(attn.reshape(th, -1), rec.reshape(th, -1), h.reshape(T, D), wo_bf,
                                                norm_ffn_g[i], wq_bf, k1t_bf, k2t_bf, b0 * S, tm=256)
            return h1, xn2, idx_t.T, gate_t.T

        h1_a, xn_a, idx_a, gate_a = mix_and_route(0)
        peer_a = _peer_sc(idx_a, gate_a, xn_a, uv2, 0, th // SC_WORKERS)
        h1_b, xn_b, idx_b, gate_b = mix_and_route(nb)
        t_sc = th * SC_TAIL_SHARE_NUM // SC_TAIL_SHARE_DEN // (2 * SC_WORKERS) * (2 * SC_WORKERS)
        t_tc = th - t_sc
        peer_b_sc = _peer_sc(idx_b, gate_b, xn_b, uv2, t_tc, t_sc // SC_WORKERS)
        peer_b_tc = _peer(idx_b, gate_b, xn_b.reshape(th, ROW_CHUNKS, LANES), uv4, t_tc, tt=8)
        peer_out = jnp.concatenate([peer_a, peer_b_tc.reshape(t_tc, D), peer_b_sc], axis=0)
        h1 = jnp.concatenate([h1_a, h1_b], axis=0)
        h = _ple_out(h1, peer_out, p[i].reshape(T, -1), norm_ple_g[i], ple_w_gate[i].astype(BF16),
                     ple_w_proj[i].astype(BF16), final_norm_g, final=(i == depth - 1), tm=512)
        h = h.reshape(B, S, D)
    return h
```

```python
import functools
import math

import jax
import jax.numpy as jnp
from jax import lax
from jax.experimental import pallas as pl
from jax.experimental.pallas import tpu as pltpu
from jax.experimental.pallas import tpu_sc as plsc

F32 = jnp.float32
BF16 = jnp.bfloat16
I32 = jnp.int32

EPS = 1e-6
DIFF_HEAD_DIM = 64
DIFF_V_DIM = 128
N_DIFF_HEADS = 4
ROPE_DIM = 16
ROPE_THETA = 500000.0
LRU_WIDTH = 512
LRU_C = 8.0
N_KEYS = 128
PEER_HEADS = 8
PEER_TOPK = 16
HALF_KEY = 64
LOG2_E = math.log2(math.e)
LANES = 128
SUBLANES = 8
VMEM_LIMIT = 56 * 1024 * 1024


def _rms(x, g):
    return x * lax.rsqrt(jnp.mean(x * x, axis=-1, keepdims=True) + EPS) * g


def _gelu_tanh(x):
    return 0.5 * x * (1.0 + jnp.tanh(math.sqrt(2.0 / math.pi) * (x + 0.044715 * (x * x * x))))


def _sigmoid(x):
    return 1.0 / (1.0 + jnp.exp(-x))


def _inproj_kernel(x_ref, pos_ref, g_ref, w_ref, q_ref, k_ref, v_ref, u_ref, gate_ref):
    x = x_ref[0]
    xn = _rms(x, g_ref[...]).astype(BF16)
    pos = pos_ref[0].astype(F32)
    lane = lax.broadcasted_iota(I32, (1, LANES), 1)
    p = lane & (DIFF_HEAD_DIM - 1)
    freq = (p & (ROPE_DIM // 2 - 1)).astype(F32)
    inv_freq = jnp.exp(freq * (-2.0 / ROPE_DIM * math.log(ROPE_THETA)))
    ang = pos * inv_freq
    cs = jnp.cos(ang)
    sn = jnp.sin(ang)
    half = ROPE_DIM // 2
    c_mul = jnp.where(p < ROPE_DIM, cs, 1.0)
    s_up = jnp.where(p < half, -sn, 0.0)
    s_dn = jnp.where((p >= half) & (p < ROPE_DIM), sn, 0.0)

    def rope(t):
        return t * c_mul + pltpu.roll(t, LANES - half, 1) * s_up + pltpu.roll(t, half, 1) * s_dn

    nq = q_ref.shape[-1]
    pq = jnp.dot(xn, w_ref[:, 0:nq], preferred_element_type=F32)
    pk = jnp.dot(xn, w_ref[:, nq:2 * nq], preferred_element_type=F32)
    scale = DIFF_HEAD_DIM ** -0.5 * LOG2_E
    for j in range(nq // LANES):
        sl = slice(j * LANES, (j + 1) * LANES)
        q_ref[0, :, sl] = (rope(pq[:, sl]) * scale).astype(BF16)
        k_ref[0, :, sl] = rope(pk[:, sl]).astype(BF16)
    v_ref[0] = jnp.dot(xn, w_ref[:, 2 * nq:3 * nq], preferred_element_type=F32).astype(BF16)
    u_ref[0] = jnp.dot(xn, w_ref[:, 3 * nq:3 * nq + LRU_WIDTH], preferred_element_type=F32)
    gate_ref[0] = jnp.dot(xn, w_ref[:, 3 * nq + LRU_WIDTH:3 * nq + 2 * LRU_WIDTH],
                          preferred_element_type=F32).astype(BF16)


def _in_proj(x, positions, g, w_in_bf, tm):
    B, S, D = x.shape
    nq = N_DIFF_HEADS * 2 * DIFF_HEAD_DIM
    ncols = w_in_bf.shape[1]
    row = lambda b, i: (b, i, 0)
    return pl.pallas_call(
        _inproj_kernel,
        grid=(B, S // tm),
        in_specs=[
            pl.BlockSpec((1, tm, D), row),
            pl.BlockSpec((1, tm, 1), row),
            pl.BlockSpec((1, D), lambda b, i: (0, 0)),
            pl.BlockSpec((D, ncols), lambda b, i: (0, 0)),
        ],
        out_specs=[
            pl.BlockSpec((1, tm, nq), row),
            pl.BlockSpec((1, tm, nq), row),
            pl.BlockSpec((1, tm, nq), row),
            pl.BlockSpec((1, tm, LRU_WIDTH), row),
            pl.BlockSpec((1, tm, LRU_WIDTH), row),
        ],
        out_shape=[
            jax.ShapeDtypeStruct((B, S, nq), BF16),
            jax.ShapeDtypeStruct((B, S, nq), BF16),
            jax.ShapeDtypeStruct((B, S, nq), BF16),
            jax.ShapeDtypeStruct((B, S, LRU_WIDTH), F32),
            jax.ShapeDtypeStruct((B, S, LRU_WIDTH), BF16),
        ],
        compiler_params=pltpu.CompilerParams(
            dimension_semantics=("parallel", "parallel"), vmem_limit_bytes=VMEM_LIMIT),
        name="in_proj",
    )(x, positions.reshape(B, S, 1), g.reshape(1, D), w_in_bf)


def _attn_kernel(lq1_ref, lk1_ref, lq2_ref, lk2_ref, g_ref, q_ref, k_ref, v_ref, o_ref, *, lambda_init):
    lam = (jnp.exp(jnp.sum(lq1_ref[...] * lk1_ref[...], axis=-1, keepdims=True))
           - jnp.exp(jnp.sum(lq2_ref[...] * lk2_ref[...], axis=-1, keepdims=True))
           + lambda_init)
    q = q_ref[0]
    k = k_ref[0]
    v = v_ref[0]
    lane = lax.broadcasted_iota(I32, q.shape, 1)
    zero = jnp.zeros_like(q)
    q0 = jnp.where(lane < DIFF_HEAD_DIM, q, zero)
    q1 = jnp.where(lane >= DIFF_HEAD_DIM, q, zero)
    nt = (((1,), (1,)), ((), ()))
    s0 = lax.dot_general(q0, k, nt, preferred_element_type=F32)
    s1 = lax.dot_general(q1, k, nt, preferred_element_type=F32)
    p0 = jnp.exp2(s0 - jnp.max(s0, axis=-1, keepdims=True))
    p1 = jnp.exp2(s1 - jnp.max(s1, axis=-1, keepdims=True))
    l0 = jnp.sum(p0, axis=-1, keepdims=True)
    l1 = jnp.sum(p1, axis=-1, keepdims=True)
    w = (p0 - p1 * (lam * l0 / l1)).astype(BF16)
    o = jnp.dot(w, v, preferred_element_type=F32) / l0
    o_ref[0] = (_rms(o, g_ref[...]) * (1.0 - lambda_init)).astype(BF16)


def _diff_attn(q, k, v, lq1, lk1, lq2, lk2, g, lambda_init, tq):
    B, S, W = q.shape
    H = W // DIFF_V_DIM
    vec = lambda b, h, i: (0, 0)
    return pl.pallas_call(
        functools.partial(_attn_kernel, lambda_init=lambda_init),
        grid=(B, H, S // tq),
        in_specs=[
            pl.BlockSpec((1, DIFF_HEAD_DIM), vec),
            pl.BlockSpec((1, DIFF_HEAD_DIM), vec),
            pl.BlockSpec((1, DIFF_HEAD_DIM), vec),
            pl.BlockSpec((1, DIFF_HEAD_DIM), vec),
            pl.BlockSpec((1, DIFF_V_DIM), vec),
            pl.BlockSpec((1, tq, DIFF_V_DIM), lambda b, h, i: (b, i, h)),
            pl.BlockSpec((1, S, DIFF_V_DIM), lambda b, h, i: (b, 0, h)),
            pl.BlockSpec((1, S, DIFF_V_DIM), lambda b, h, i: (b, 0, h)),
        ],
        out_specs=pl.BlockSpec((1, tq, DIFF_V_DIM), lambda b, h, i: (b, i, h)),
        out_shape=jax.ShapeDtypeStruct((B, S, W), BF16),
        compiler_params=pltpu.CompilerParams(
            dimension_semantics=("parallel", "parallel", "parallel"), vmem_limit_bytes=VMEM_LIMIT),
        name="diff_attn",
    )(lq1.reshape(1, -1), lk1.reshape(1, -1), lq2.reshape(1, -1), lk2.reshape(1, -1),
      g.reshape(1, -1), q, k, v)


def _lru_kernel(u_ref, gate_ref, cw_ref, cb_ref, w_ref, bias_ref, lam_ref, g_ref, out_ref, hf_ref, *, tc):
    S = u_ref.shape[1]
    C = u_ref.shape[2]
    nc = S // tc
    halo = SUBLANES
    neg_lam = -lam_ref[...]
    sp = jnp.maximum(neg_lam, 0.0) + jnp.log(1.0 + jnp.exp(-jnp.abs(neg_lam)))
    row = lax.broadcasted_iota(I32, (tc, 1), 0)
    conv_taps = cw_ref.shape[0]
    conv_left = 2

    def gates(c, d):
        r0 = pl.multiple_of(c * tc, tc)
        x = u_ref[0, pl.ds(r0, tc), :]
        prev = u_ref[0, pl.ds(pl.multiple_of(jnp.maximum(r0 - halo, 0), halo), halo), :]
        nxt = u_ref[0, pl.ds(pl.multiple_of(jnp.minimum(r0 + tc, S - halo), halo), halo), :]
        prev = jnp.where(c > 0, prev, 0.0)
        nxt = jnp.where(c < nc - 1, nxt, 0.0)
        win = jnp.concatenate([prev, x, nxt], axis=0)
        uc = cb_ref[...]
        for j in range(conv_taps):
            o = halo - conv_left + j
            uc = uc + cw_ref[j:j + 1, :] * win[o:o + tc, :]
        pre = (jnp.dot(uc.astype(BF16), w_ref[:, d * 2 * C:(d + 1) * 2 * C], preferred_element_type=F32)
               + bias_ref[:, d * 2 * C:(d + 1) * 2 * C])
        r = _sigmoid(pre[:, :C])
        i = _sigmoid(pre[:, C:])
        log_a = -LRU_C * r * sp[d:d + 1, :]
        a = jnp.exp(log_a)
        th = jnp.tanh(log_a)
        mult = jnp.sqrt(-2.0 * th / (1.0 - th))
        return a, mult * (i * uc)

    def scan(a, b, reverse):
        d = 1
        while d < tc:
            if reverse:
                a_s = pltpu.roll(a, tc - d, 0)
                b_s = pltpu.roll(b, tc - d, 0)
                m = row < tc - d
            else:
                a_s = pltpu.roll(a, d, 0)
                b_s = pltpu.roll(b, d, 0)
                m = row >= d
            a_s = jnp.where(m, a_s, 1.0)
            b_s = jnp.where(m, b_s, 0.0)
            b = a * b_s + b
            a = a * a_s
            d *= 2
        return a, b

    def fwd_body(c, h0):
        a, b = gates(c, 0)
        a, b = scan(a, b, False)
        h = a * h0 + b
        hf_ref[pl.ds(pl.multiple_of(c * tc, tc), tc), :] = h
        return h[tc - 1:tc, :]

    lax.fori_loop(0, nc, fwd_body, jnp.zeros((1, C), F32))

    def bwd_body(j, h0):
        c = nc - 1 - j
        r0 = pl.multiple_of(c * tc, tc)
        a, b = gates(c, 1)
        a, b = scan(a, b, True)
        h = a * h0 + b
        y = (hf_ref[pl.ds(r0, tc), :] + h) * _gelu_tanh(gate_ref[0, pl.ds(r0, tc), :].astype(F32))
        out_ref[0, pl.ds(r0, tc), :] = _rms(y, g_ref[...]).astype(BF16)
        return h[0:1, :]

    lax.fori_loop(0, nc, bwd_body, jnp.zeros((1, C), F32))


def _bi_rglru(u, gate, conv_w, conv_b, w_all_bf, bias_all, lru_lambda, g, tc):
    B, S, C = u.shape
    full = lambda b: (0, 0)
    return pl.pallas_call(
        functools.partial(_lru_kernel, tc=tc),
        grid=(B,),
        in_specs=[
            pl.BlockSpec((1, S, C), lambda b: (b, 0, 0)),
            pl.BlockSpec((1, S, C), lambda b: (b, 0, 0)),
            pl.BlockSpec(conv_w.shape, full),
            pl.BlockSpec((1, C), full),
            pl.BlockSpec(w_all_bf.shape, full),
            pl.BlockSpec(bias_all.shape, full),
            pl.BlockSpec(lru_lambda.shape, full),
            pl.BlockSpec((1, C), full),
        ],
        out_specs=pl.BlockSpec((1, S, C), lambda b: (b, 0, 0)),
        out_shape=jax.ShapeDtypeStruct((B, S, C), BF16),
        scratch_shapes=[pltpu.VMEM((S, C), F32)],
        compiler_params=pltpu.CompilerParams(
            dimension_semantics=("parallel",), vmem_limit_bytes=VMEM_LIMIT),
        name="bi_rglru",
    )(u, gate, conv_w, conv_b.reshape(1, C), w_all_bf, bias_all, lru_lambda, g.reshape(1, C))


def _topk_rows(s, k):
    n = s.shape[0]
    rid = lax.broadcasted_iota(I32, s.shape, 0)
    vals, ids = [], []
    for _ in range(k):
        m = jnp.max(s, axis=0, keepdims=True)
        sel = jnp.min(jnp.where(s == m, rid, n), axis=0, keepdims=True)
        vals.append(m)
        ids.append(sel)
        s = jnp.where(rid == sel, -jnp.inf, s)
    return jnp.concatenate(vals, axis=0), jnp.concatenate(ids, axis=0)


CAND_ROW_PIECES = 4


def _candidate_pieces(k):
    up = lambda n: -(-n // SUBLANES) * SUBLANES
    pieces = [("row", i, 0, up(k // (i + 1))) for i in range(CAND_ROW_PIECES)]
    for j in range(k // (CAND_ROW_PIECES + 1)):
        for i0 in range(0, k // (j + 1), SUBLANES):
            pieces.append(("col", j, i0, SUBLANES))
    return pieces


def _route_kernel(attn_ref, rec_ref, x_ref, wo_ref, g_ref, wq_ref, k1_ref, k2_ref,
                  h_ref, xn_ref, idx_ref, gate_ref):
    aw = attn_ref.shape[1]
    h = (x_ref[...]
         + jnp.dot(attn_ref[...], wo_ref[0:aw, :], preferred_element_type=F32)
         + jnp.dot(rec_ref[...], wo_ref[aw:, :], preferred_element_type=F32))
    h_ref[...] = h
    xn = _rms(h, g_ref[...])
    xn_ref[...] = xn
    q = jnp.dot(xn.astype(BF16), wq_ref[...], preferred_element_type=F32).astype(BF16)
    nt = (((1,), (1,)), ((), ()))
    s1 = lax.dot_general(k1_ref[...], q, nt, preferred_element_type=F32)
    s2 = lax.dot_general(k2_ref[...], q, nt, preferred_element_type=F32)
    k = PEER_TOPK
    tm = q.shape[0]
    pieces = _candidate_pieces(k)
    pos_parts, ok_parts = [], []
    for kind, fixed, start, length in pieces:
        r = lax.broadcasted_iota(I32, (length, tm), 0) + start
        i, j = (fixed, r) if kind == "row" else (r, fixed)
        pos_parts.append(i * k + j)
        ok = (i + 1) * (j + 1) <= k
        ok_parts.append(ok if kind == "row" else ok & (r >= CAND_ROW_PIECES))
    pos = jnp.concatenate(pos_parts, axis=0)
    ok = jnp.concatenate(ok_parts, axis=0)
    idx_rows, gate_rows = [], []
    for hd in range(PEER_HEADS):
        v1, i1 = _topk_rows(s1[hd * N_KEYS:(hd + 1) * N_KEYS, :], k)
        v2, i2 = _topk_rows(s2[hd * N_KEYS:(hd + 1) * N_KEYS, :], k)
        cand_parts, cidx_parts = [], []
        for kind, fixed, start, length in pieces:
            if kind == "row":
                cand_parts.append(v1[fixed:fixed + 1, :] + v2[start:start + length, :])
                cidx_parts.append(i1[fixed:fixed + 1, :] * N_KEYS + i2[start:start + length, :])
            else:
                cand_parts.append(v1[start:start + length, :] + v2[fixed:fixed + 1, :])
                cidx_parts.append(i1[start:start + length, :] * N_KEYS + i2[fixed:fixed + 1, :])
        cand = jnp.where(ok, jnp.concatenate(cand_parts, axis=0), -jnp.inf)
        cidx = jnp.concatenate(cidx_parts, axis=0)
        sc, ids = [], []
        for _ in range(k):
            m = jnp.max(cand, axis=0, keepdims=True)
            sel = jnp.min(jnp.where(cand == m, pos, k * k), axis=0, keepdims=True)
            hit = pos == sel
            sc.append(m)
            ids.append(jnp.max(jnp.where(hit, cidx, -1), axis=0, keepdims=True))
            cand = jnp.where(hit, -jnp.inf, cand)
        sc = jnp.concatenate(sc, axis=0)
        e = jnp.exp(sc - sc[0:1, :])
        gate_rows.append(e / jnp.sum(e, axis=0, keepdims=True))
        idx_rows.append(jnp.concatenate(ids, axis=0))
    idx_ref[...] = jnp.concatenate(idx_rows, axis=0)
    gate_ref[...] = jnp.concatenate(gate_rows, axis=0)


def _mix_route(attn, rec, x2, wo_bf, g, wq_bf, k1t_bf, k2t_bf, tm):
    T, D = x2.shape
    NK = PEER_HEADS * PEER_TOPK
    row = lambda i: (i, 0)
    full = lambda i: (0, 0)
    return pl.pallas_call(
        _route_kernel,
        grid=(T // tm,),
        in_specs=[
            pl.BlockSpec((tm, attn.shape[1]), row),
            pl.BlockSpec((tm, rec.shape[1]), row),
            pl.BlockSpec((tm, D), row),
            pl.BlockSpec(wo_bf.shape, full),
            pl.BlockSpec((1, D), full),
            pl.BlockSpec(wq_bf.shape, full),
            pl.BlockSpec(k1t_bf.shape, full),
            pl.BlockSpec(k2t_bf.shape, full),
        ],
        out_specs=[
            pl.BlockSpec((tm, D), row),
            pl.BlockSpec((tm, D), row),
            pl.BlockSpec((NK, tm), lambda i: (0, i)),
            pl.BlockSpec((NK, tm), lambda i: (0, i)),
        ],
        out_shape=[
            jax.ShapeDtypeStruct((T, D), F32),
            jax.ShapeDtypeStruct((T, D), F32),
            jax.ShapeDtypeStruct((NK, T), I32),
            jax.ShapeDtypeStruct((NK, T), F32),
        ],
        compiler_params=pltpu.CompilerParams(
            dimension_semantics=("parallel",), vmem_limit_bytes=VMEM_LIMIT),
        name="mix_route",
    )(attn, rec, x2, wo_bf, g.reshape(1, D), wq_bf, k1t_bf, k2t_bf)


ROW_CHUNKS = 8
HI_HALF_MASK = -65536


def _pack_rows(u, v):
    def words(a):
        half = a.shape[1] // 2
        bits = lax.bitcast_convert_type(a.astype(BF16), jnp.uint16).astype(jnp.uint32)
        return bits[:, :half] | (bits[:, half:] << 16)
    return lax.bitcast_convert_type(jnp.concatenate([words(u), words(v)], axis=1), I32)


def _word_lo(w):
    return pltpu.bitcast(w << 16, F32)


def _word_hi(w):
    return pltpu.bitcast(w & HI_HALF_MASK, F32)
TOKEN_UNROLL = 2
DMA_PRIORITIES = 2
PEER_RING = 4
PEER_PREFETCH = 2


def _peer_kernel(idx_ref, idx_next_ref, xn_ref, gate_ref, uv_hbm, o_ref, *scratch, tt, nk):
    bufs = scratch[:PEER_RING]
    lg_ref, act_ref, sems = scratch[PEER_RING:]
    g = pl.program_id(0)
    n = pl.num_programs(0)
    nt = (((1,), (1,)), ((), ()))
    ch = ROW_CHUNKS
    hc = ch // 2

    def row_copy(ids_ref, q, t, k):
        e = ids_ref[0, 0, (q * tt + t) * nk + k]
        return pltpu.make_async_copy(uv_hbm.at[e], bufs[q].at[t, :, pl.ds(k, 1), :], sems.at[q])

    def wait_group(q):
        pltpu.make_async_copy(bufs[(q + 1) % PEER_RING], bufs[q], sems.at[q]).wait()

    @pl.when(g == 0)
    def _():
        for q in range(PEER_PREFETCH):
            def body(t, carry, q=q):
                for k in range(nk):
                    row_copy(idx_ref, q, t, k).start(priority=k % DMA_PRIORITIES)
                return carry
            lax.fori_loop(0, tt, body, 0)

    def group(q):
        buf = bufs[q]
        row0 = q * tt
        ahead = q + PEER_PREFETCH
        ids_ahead = idx_ref if ahead < PEER_RING else idx_next_ref

        def issue(t, k0, k1):
            for k in range(k0, k1):
                row_copy(ids_ahead, ahead % PEER_RING, t, k).start(priority=k % DMA_PRIORITIES)

        def phase_u(tb, carry):
            for j in range(TOKEN_UNROLL):
                t = tb * TOKEN_UNROLL + j
                issue(t, 0, nk // 2)
                xt = xn_ref[row0 + t]
                acc = None
                for s in range(hc):
                    w = buf[t, s]
                    part = _word_lo(w) * xt[s:s + 1, :] + _word_hi(w) * xt[hc + s:hc + s + 1, :]
                    acc = part if acc is None else acc + part
                hi = acc.astype(BF16)
                lo = (acc - hi.astype(F32)).astype(BF16)
                ones = jnp.ones((SUBLANES, LANES), BF16)
                red = (lax.dot_general(ones, hi, nt, preferred_element_type=F32)
                       + lax.dot_general(ones, lo, nt, preferred_element_type=F32))
                lg_ref[t] = red[0:1, :]
            return carry
        lax.fori_loop(0, tt // TOKEN_UNROLL, phase_u, 0)
        act_ref[...] = _gelu_tanh(lg_ref[...]) * gate_ref[row0:row0 + tt]

        def phase_v(tb, carry):
            for j in range(TOKEN_UNROLL):
                t = tb * TOKEN_UNROLL + j
                issue(t, nk // 2, nk)
                act = act_ref[t].astype(BF16)
                words = [buf[t, hc + s] for s in range(hc)]
                cols = ([jnp.dot(act, _word_lo(w).astype(BF16), preferred_element_type=F32) for w in words]
                        + [jnp.dot(act, _word_hi(w).astype(BF16), preferred_element_type=F32) for w in words])
                o_ref[row0 + t] = jnp.concatenate(cols, axis=0)
            return carry
        lax.fori_loop(0, tt // TOKEN_UNROLL, phase_v, 0)

    for q in range(PEER_RING):
        wait_group(q)
        group(q)

    @pl.when(g == n - 1)
    def _():
        for q in range(PEER_PREFETCH):
            wait_group(q)


def _peer(idx, gate, xn3, uv4, n_tokens, tt):
    T, ch, _ = xn3.shape
    nk = idx.shape[1]
    step = PEER_RING * tt
    n = n_tokens // step
    idx3 = idx.reshape(T // step, 1, step * nk)
    tok = lambda i: (i, 0, 0)
    return pl.pallas_call(
        functools.partial(_peer_kernel, tt=tt, nk=nk),
        grid=(n,),
        in_specs=[
            pl.BlockSpec((1, 1, step * nk), tok, memory_space=pltpu.SMEM),
            pl.BlockSpec((1, 1, step * nk), lambda i: (jnp.minimum(i + 1, n - 1), 0, 0), memory_space=pltpu.SMEM),
            pl.BlockSpec((step, ch, LANES), tok),
            pl.BlockSpec((step, 1, nk), tok),
            pl.BlockSpec(memory_space=pl.ANY),
        ],
        out_specs=pl.BlockSpec((step, ch, LANES), tok),
        out_shape=jax.ShapeDtypeStruct((n_tokens, ch, LANES), F32),
        scratch_shapes=(
            [pltpu.VMEM((tt, ch, nk, LANES), I32) for _ in range(PEER_RING)]
            + [pltpu.VMEM((tt, 1, nk), F32), pltpu.VMEM((tt, 1, nk), F32), pltpu.SemaphoreType.DMA((PEER_RING,))]),
        compiler_params=pltpu.CompilerParams(
            dimension_semantics=("arbitrary",), vmem_limit_bytes=VMEM_LIMIT),
        name="peer",
    )(idx3, idx3, xn3, gate.reshape(T, 1, nk), uv4)


SC_CORES = 2
SC_SUBCORES = 16
SC_LANES = 16
SC_WORKERS = SC_CORES * SC_SUBCORES
SC_EXPERT_BLOCK = 16
SC_CHUNK_GROUP = 4
IDX_SLOTS = 4
SC_TOKEN_SHARE_NUM, SC_TOKEN_SHARE_DEN = 17, 32


def _peer_sc(idx, gate, xn, uv, tok0, toks_per_worker):
    T, nk = idx.shape
    D = xn.shape[1]
    L = SC_LANES
    kb_rows = SC_EXPERT_BLOCK
    n_blocks = nk // kb_rows
    n_chunks = D // L
    half = D // 2
    uc = SC_CHUNK_GROUP
    word_lo = lambda w: plsc.bitcast(w << 16, F32)
    word_hi = lambda w: plsc.bitcast(w & HI_HALF_MASK, F32)
    mesh = plsc.VectorSubcoreMesh(core_axis_name="c", subcore_axis_name="s")

    def body(idx_hbm, gate_hbm, xn_hbm, uv_hbm, out_hbm, idx_v, gate_v, x_v, o_v, rows_v, acc_v, sems, tsems):
        wid = lax.axis_index("s") * SC_CORES + lax.axis_index("c")
        base = wid * toks_per_worker
        lane = lax.iota(I32, L)

        def tok_of(ti):
            return tok0 + base + jnp.minimum(ti, toks_per_worker - 1)

        def idx_copy(ti):
            return pltpu.make_async_copy(idx_hbm.at[tok_of(ti)], idx_v.at[pl.ds((ti % IDX_SLOTS) * nk, nk)],
                                         tsems.at[0])

        def gate_copy(ti):
            return pltpu.make_async_copy(gate_hbm.at[tok_of(ti)], gate_v.at[pl.ds((ti % 2) * nk, nk)], tsems.at[1])

        def x_copy(ti):
            return pltpu.make_async_copy(xn_hbm.at[tok_of(ti)], x_v.at[pl.ds((ti % 2) * D, D)], tsems.at[2])

        def gather(ti, kb, buf):
            first = (ti % IDX_SLOTS) * nk + kb * kb_rows
            return pltpu.make_async_copy(uv_hbm.at[idx_v.at[pl.ds(first, kb_rows)]], rows_v.at[buf], sems.at[buf])

        def evaluate(kb, rv, goff, xoff):
            for j in range(kb_rows):
                acc_v[j] = jnp.zeros((L,), F32)

            @plsc.parallel_loop(0, half // L // uc, unroll=2)
            def _(cg):
                x_lo = [x_v[pl.ds(xoff + (cg * uc + cc) * L, L)] for cc in range(uc)]
                x_hi = [x_v[pl.ds(xoff + half + (cg * uc + cc) * L, L)] for cc in range(uc)]
                for j in range(kb_rows):
                    pr = None
                    for cc in range(uc):
                        w = rv[j, pl.ds((cg * uc + cc) * L, L)]
                        part = word_lo(w) * x_lo[cc] + word_hi(w) * x_hi[cc]
                        pr = part if pr is None else pr + part
                    plsc.addupdate(acc_v.at[j], pr)
            logits = jnp.zeros((L,), F32)
            for j in range(kb_rows):
                logits = jnp.where(lane == j, jnp.sum(acc_v[j]), logits)
            z = math.sqrt(2.0 / math.pi) * (logits + 0.044715 * (logits * logits * logits))
            th = 1.0 - 2.0 / (1.0 + jnp.exp(2.0 * z))
            act = 0.5 * logits * (1.0 + th) * gate_v[pl.ds(goff + kb * kb_rows, kb_rows)]
            acts = [jnp.sum(jnp.where(lane == j, act, 0.0)) for j in range(kb_rows)]

            @plsc.parallel_loop(0, half // L, unroll=2)
            def _(c):
                acc_lo = None
                acc_hi = None
                for j in range(kb_rows):
                    w = rv[j, pl.ds(half + c * L, L)]
                    lo = acts[j] * word_lo(w)
                    hi = acts[j] * word_hi(w)
                    acc_lo = lo if acc_lo is None else acc_lo + lo
                    acc_hi = hi if acc_hi is None else acc_hi + hi
                plsc.addupdate(o_v.at[pl.ds(c * L, L)], acc_lo)
                plsc.addupdate(o_v.at[pl.ds(half + c * L, L)], acc_hi)

        def token(ti, carry):
            gate_copy(ti).wait()
            x_copy(ti).wait()
            idx_copy(ti + 1).wait()
            gate_copy(ti + 1).start()
            x_copy(ti + 1).start()
            idx_copy(ti + 2).start()
            goff = (ti % 2) * nk
            xoff = (ti % 2) * D

            @plsc.parallel_loop(0, n_chunks)
            def _(c):
                o_v[pl.ds(c * L, L)] = jnp.zeros((L,), F32)

            def block(kb, carry):
                buf = kb % 2
                last = kb + 1 >= n_blocks
                gather(ti, kb, buf).wait()
                gather(jnp.where(last, ti + 1, ti), jnp.where(last, 0, kb + 1), 1 - buf).start()
                evaluate(kb, rows_v.at[buf], goff, xoff)
                return carry
            lax.fori_loop(0, n_blocks, block, 0)
            pltpu.sync_copy(o_v, out_hbm.at[base + ti])
            return carry

        idx_copy(0).start()
        idx_copy(0).wait()
        idx_copy(1).start()
        gate_copy(0).start()
        x_copy(0).start()
        gather(0, 0, 0).start()
        lax.fori_loop(0, toks_per_worker, token, 0)
        gate_copy(toks_per_worker).wait()
        x_copy(toks_per_worker).wait()
        idx_copy(toks_per_worker + 1).wait()
        gather(toks_per_worker, 0, 0).wait()

    return pl.kernel(
        body, mesh=mesh,
        compiler_params=pltpu.CompilerParams(needs_layout_passes=False),
        out_type=jax.ShapeDtypeStruct((SC_WORKERS * toks_per_worker, D), F32),
        scratch_types=[
            pltpu.VMEM((IDX_SLOTS * nk,), I32),
            pltpu.VMEM((2 * nk,), F32),
            pltpu.VMEM((2 * D,), F32),
            pltpu.VMEM((D,), F32),
            pltpu.VMEM((2, kb_rows, D), I32),
            pltpu.VMEM((kb_rows, L), F32),
            pltpu.SemaphoreType.DMA((2,)),
            pltpu.SemaphoreType.DMA((3,)),
        ],
    )(idx, gate, xn, uv)


def _ple_kernel(h_ref, po_ref, p_ref, g3_ref, wg_ref, wp_ref, gf_ref, o_ref, *, final):
    h = h_ref[...] + po_ref[...]
    xn = _rms(h, g3_ref[...]).astype(BF16)
    gate = _sigmoid(jnp.dot(xn, wg_ref[...], preferred_element_type=F32))
    proj = jnp.dot(p_ref[...].astype(BF16), wp_ref[...], preferred_element_type=F32)
    h = h + gate * proj
    o_ref[...] = _rms(h, gf_ref[...]) if final else h


def _ple_out(h1, peer_out, p2, g3, wg_bf, wp_bf, gf, final, tm):
    T, D = h1.shape
    row = lambda i: (i, 0)
    full = lambda i: (0, 0)
    return pl.pallas_call(
        functools.partial(_ple_kernel, final=final),
        grid=(T // tm,),
        in_specs=[
            pl.BlockSpec((tm, D), row),
            pl.BlockSpec((tm, D), row),
            pl.BlockSpec((tm, p2.shape[1]), row),
            pl.BlockSpec((1, D), full),
            pl.BlockSpec(wg_bf.shape, full),
            pl.BlockSpec(wp_bf.shape, full),
            pl.BlockSpec((1, D), full),
        ],
        out_specs=pl.BlockSpec((tm, D), row),
        out_shape=jax.ShapeDtypeStruct((T, D), F32),
        compiler_params=pltpu.CompilerParams(
            dimension_semantics=("parallel",), vmem_limit_bytes=VMEM_LIMIT),
        name="ple_out",
    )(h1, peer_out, p2, g3.reshape(1, D), wg_bf, wp_bf, gf.reshape(1, D))


def _block_diag(w):
    nb, bw, _ = w.shape
    eye = jnp.eye(nb, dtype=w.dtype)
    return (eye[:, None, :, None] * w[:, :, None, :]).reshape(nb * bw, nb * bw)


def _key_matrix(keys, half):
    z = jnp.zeros_like(keys)
    blk = jnp.concatenate([keys, z] if half == 0 else [z, keys], axis=1)
    return jnp.kron(jnp.eye(PEER_HEADS, dtype=keys.dtype), blk)


def kernel(x, p, positions, norm_mix_g, w_in, lambda_q1, lambda_k1, lambda_q2, lambda_k2, diff_norm_g, conv_w, conv_b, lru_wa, lru_ba, lru_wx, lru_bx, lru_lambda, lru_norm_g, w_out, norm_ffn_g, peer_wq, peer_keys1, peer_keys2, peer_u, peer_v, norm_ple_g, ple_w_gate, ple_w_proj, final_norm_g):
    B, S, D = x.shape
    T = B * S
    depth = w_in.shape[0]
    h = x
    for i in range(depth):
        lambda_init = 0.8 - 0.6 * math.exp(-0.3 * i)
        q, k, v, u, gate = _in_proj(h, positions, norm_mix_g[i], w_in[i].astype(BF16), tm=512)
        attn = _diff_attn(q, k, v, lambda_q1[i], lambda_k1[i], lambda_q2[i], lambda_k2[i],
                          diff_norm_g[i], lambda_init, tq=256)
        w_all = jnp.concatenate([_block_diag(lru_wa[i, 0]), _block_diag(lru_wx[i, 0]),
                                 _block_diag(lru_wa[i, 1]), _block_diag(lru_wx[i, 1])], axis=1).astype(BF16)
        bias_all = jnp.concatenate([lru_ba[i, 0], lru_bx[i, 0], lru_ba[i, 1], lru_bx[i, 1]]).reshape(1, -1)
        rec = _bi_rglru(u, gate, conv_w[i], conv_b[i], w_all, bias_all, lru_lambda[i], lru_norm_g[i], tc=256)
        h1, xn2, idx_t, gate_t = _mix_route(
            attn.reshape(T, -1), rec.reshape(T, -1), h.reshape(T, D), w_out[i].astype(BF16), norm_ffn_g[i],
            peer_wq[i].astype(BF16), _key_matrix(peer_keys1[i], 0).astype(BF16),
            _key_matrix(peer_keys2[i], 1).astype(BF16), tm=256)
        n_exp = peer_u.shape[1]
        uv2 = _pack_rows(peer_u[i], peer_v[i])
        uv4 = uv2.reshape(n_exp, ROW_CHUNKS, 1, LANES)
        idx, gate_tk = idx_t.T, gate_t.T
        t_sc = T * SC_TOKEN_SHARE_NUM // SC_TOKEN_SHARE_DEN // (2 * SC_WORKERS) * (2 * SC_WORKERS)
        t_tc = T - t_sc
        peer_sc = _peer_sc(idx, gate_tk, xn2, uv2, t_tc, t_sc // SC_WORKERS)
        peer_tc = _peer(idx, gate_tk, xn2.reshape(T, ROW_CHUNKS, LANES), uv4, t_tc, tt=8)
        peer_out = jnp.concatenate([peer_tc.reshape(t_tc, D), peer_sc], axis=0)
        h = _ple_out(h1, peer_out, p[i].reshape(T, -1), norm_ple_g[i], ple_w_gate[i].astype(BF16),
                     ple_w_proj[i].astype(BF16), final_norm_g, final=(i == depth - 1), tm=512)
        h = h.reshape(B, S, D)
    return h
```

```python
import functools
import math

import jax
import jax.numpy as jnp
from jax import lax
from jax.experimental import pallas as pl
from jax.experimental.pallas import tpu as pltpu
from jax.experimental.pallas import tpu_sc as plsc

F32 = jnp.float32
BF16 = jnp.bfloat16
I32 = jnp.int32

EPS = 1e-6
DIFF_HEAD_DIM = 64
DIFF_V_DIM = 128
N_DIFF_HEADS = 4
ROPE_DIM = 16
ROPE_THETA = 500000.0
LRU_WIDTH = 512
LRU_C = 8.0
N_KEYS = 128
PEER_HEADS = 8
PEER_TOPK = 16
HALF_KEY = 64
LOG2_E = math.log2(math.e)
LANES = 128
SUBLANES = 8
VMEM_LIMIT = 56 * 1024 * 1024
IN_PROJ_ROWS = 512
ATTN_Q_ROWS = 256
LRU_CHUNK_ROWS = 256
ROUTE_TOKENS = 256
PEER_GROUP_TOKENS = 8
PLE_ROWS = 512
PACK_ROWS = 512


def _rms(x, g):
    return x * lax.rsqrt(jnp.mean(x * x, axis=-1, keepdims=True) + EPS) * g


def _gelu_tanh(x):
    return 0.5 * x * (1.0 + jnp.tanh(math.sqrt(2.0 / math.pi) * (x + 0.044715 * (x * x * x))))


def _sigmoid(x):
    return 1.0 / (1.0 + jnp.exp(-x))


def _inproj_kernel(x_ref, pos_ref, g_ref, w_ref, q_ref, k_ref, v_ref, u_ref, gate_ref):
    x = x_ref[0]
    xn = _rms(x, g_ref[...]).astype(BF16)
    pos = pos_ref[0].astype(F32)
    lane = lax.broadcasted_iota(I32, (1, LANES), 1)
    p = lane & (DIFF_HEAD_DIM - 1)
    freq = (p & (ROPE_DIM // 2 - 1)).astype(F32)
    inv_freq = jnp.exp(freq * (-2.0 / ROPE_DIM * math.log(ROPE_THETA)))
    ang = pos * inv_freq
    cs = jnp.cos(ang)
    sn = jnp.sin(ang)
    half = ROPE_DIM // 2
    c_mul = jnp.where(p < ROPE_DIM, cs, 1.0)
    s_up = jnp.where(p < half, -sn, 0.0)
    s_dn = jnp.where((p >= half) & (p < ROPE_DIM), sn, 0.0)

    def rope(t):
        return t * c_mul + pltpu.roll(t, LANES - half, 1) * s_up + pltpu.roll(t, half, 1) * s_dn

    nq = q_ref.shape[-1]
    pq = jnp.dot(xn, w_ref[:, 0:nq], preferred_element_type=F32)
    pk = jnp.dot(xn, w_ref[:, nq:2 * nq], preferred_element_type=F32)
    scale = DIFF_HEAD_DIM ** -0.5 * LOG2_E
    for j in range(nq // LANES):
        sl = slice(j * LANES, (j + 1) * LANES)
        q_ref[0, :, sl] = (rope(pq[:, sl]) * scale).astype(BF16)
        k_ref[0, :, sl] = rope(pk[:, sl]).astype(BF16)
    v_ref[0] = jnp.dot(xn, w_ref[:, 2 * nq:3 * nq], preferred_element_type=F32).astype(BF16)
    u_ref[0] = jnp.dot(xn, w_ref[:, 3 * nq:3 * nq + LRU_WIDTH], preferred_element_type=F32)
    gate_ref[0] = jnp.dot(xn, w_ref[:, 3 * nq + LRU_WIDTH:3 * nq + 2 * LRU_WIDTH],
                          preferred_element_type=F32).astype(BF16)


def _in_proj(x, positions, g, w_in_bf, tm):
    B, S, D = x.shape
    nq = N_DIFF_HEADS * 2 * DIFF_HEAD_DIM
    ncols = w_in_bf.shape[1]
    row = lambda b, i: (b, i, 0)
    return pl.pallas_call(
        _inproj_kernel,
        grid=(B, S // tm),
        in_specs=[
            pl.BlockSpec((1, tm, D), row),
            pl.BlockSpec((1, tm, 1), row),
            pl.BlockSpec((1, D), lambda b, i: (0, 0)),
            pl.BlockSpec((D, ncols), lambda b, i: (0, 0)),
        ],
        out_specs=[
            pl.BlockSpec((1, tm, nq), row),
            pl.BlockSpec((1, tm, nq), row),
            pl.BlockSpec((1, tm, nq), row),
            pl.BlockSpec((1, tm, LRU_WIDTH), row),
            pl.BlockSpec((1, tm, LRU_WIDTH), row),
        ],
        out_shape=[
            jax.ShapeDtypeStruct((B, S, nq), BF16),
            jax.ShapeDtypeStruct((B, S, nq), BF16),
            jax.ShapeDtypeStruct((B, S, nq), BF16),
            jax.ShapeDtypeStruct((B, S, LRU_WIDTH), F32),
            jax.ShapeDtypeStruct((B, S, LRU_WIDTH), BF16),
        ],
        compiler_params=pltpu.CompilerParams(
            dimension_semantics=("parallel", "parallel"), vmem_limit_bytes=VMEM_LIMIT),
        name="in_proj",
    )(x, positions.reshape(B, S, 1), g.reshape(1, D), w_in_bf)


def _attn_kernel(lq1_ref, lk1_ref, lq2_ref, lk2_ref, g_ref, q_ref, k_ref, v_ref, o_ref, *, lambda_init):
    lam = (jnp.exp(jnp.sum(lq1_ref[...] * lk1_ref[...], axis=-1, keepdims=True))
           - jnp.exp(jnp.sum(lq2_ref[...] * lk2_ref[...], axis=-1, keepdims=True))
           + lambda_init)
    q = q_ref[0]
    k = k_ref[0]
    v = v_ref[0]
    lane = lax.broadcasted_iota(I32, q.shape, 1)
    zero = jnp.zeros_like(q)
    q0 = jnp.where(lane < DIFF_HEAD_DIM, q, zero)
    q1 = jnp.where(lane >= DIFF_HEAD_DIM, q, zero)
    nt = (((1,), (1,)), ((), ()))
    s0 = lax.dot_general(q0, k, nt, preferred_element_type=F32)
    s1 = lax.dot_general(q1, k, nt, preferred_element_type=F32)
    p0 = jnp.exp2(s0 - jnp.max(s0, axis=-1, keepdims=True))
    p1 = jnp.exp2(s1 - jnp.max(s1, axis=-1, keepdims=True))
    l0 = jnp.sum(p0, axis=-1, keepdims=True)
    l1 = jnp.sum(p1, axis=-1, keepdims=True)
    w = (p0 - p1 * (lam * l0 / l1)).astype(BF16)
    o = jnp.dot(w, v, preferred_element_type=F32) / l0
    o_ref[0] = (_rms(o, g_ref[...]) * (1.0 - lambda_init)).astype(BF16)


def _diff_attn(q, k, v, lq1, lk1, lq2, lk2, g, lambda_init, tq):
    B, S, W = q.shape
    H = W // DIFF_V_DIM
    vec = lambda b, h, i: (0, 0)
    return pl.pallas_call(
        functools.partial(_attn_kernel, lambda_init=lambda_init),
        grid=(B, H, S // tq),
        in_specs=[
            pl.BlockSpec((1, DIFF_HEAD_DIM), vec),
            pl.BlockSpec((1, DIFF_HEAD_DIM), vec),
            pl.BlockSpec((1, DIFF_HEAD_DIM), vec),
            pl.BlockSpec((1, DIFF_HEAD_DIM), vec),
            pl.BlockSpec((1, DIFF_V_DIM), vec),
            pl.BlockSpec((1, tq, DIFF_V_DIM), lambda b, h, i: (b, i, h)),
            pl.BlockSpec((1, S, DIFF_V_DIM), lambda b, h, i: (b, 0, h)),
            pl.BlockSpec((1, S, DIFF_V_DIM), lambda b, h, i: (b, 0, h)),
        ],
        out_specs=pl.BlockSpec((1, tq, DIFF_V_DIM), lambda b, h, i: (b, i, h)),
        out_shape=jax.ShapeDtypeStruct((B, S, W), BF16),
        compiler_params=pltpu.CompilerParams(
            dimension_semantics=("parallel", "parallel", "parallel"), vmem_limit_bytes=VMEM_LIMIT),
        name="diff_attn",
    )(lq1.reshape(1, -1), lk1.reshape(1, -1), lq2.reshape(1, -1), lk2.reshape(1, -1),
      g.reshape(1, -1), q, k, v)


def _lru_kernel(u_ref, gate_ref, cw_ref, cb_ref, w_ref, bias_ref, lam_ref, g_ref, out_ref, hf_ref, *, tc):
    S = u_ref.shape[1]
    C = u_ref.shape[2]
    nc = S // tc
    halo = SUBLANES
    neg_lam = -lam_ref[...]
    sp = jnp.maximum(neg_lam, 0.0) + jnp.log(1.0 + jnp.exp(-jnp.abs(neg_lam)))
    row = lax.broadcasted_iota(I32, (tc, 1), 0)
    conv_taps = cw_ref.shape[0]
    conv_left = 2

    def gates(c, d):
        r0 = pl.multiple_of(c * tc, tc)
        x = u_ref[0, pl.ds(r0, tc), :]
        prev = u_ref[0, pl.ds(pl.multiple_of(jnp.maximum(r0 - halo, 0), halo), halo), :]
        nxt = u_ref[0, pl.ds(pl.multiple_of(jnp.minimum(r0 + tc, S - halo), halo), halo), :]
        prev = jnp.where(c > 0, prev, 0.0)
        nxt = jnp.where(c < nc - 1, nxt, 0.0)
        win = jnp.concatenate([prev, x, nxt], axis=0)
        uc = cb_ref[...]
        for j in range(conv_taps):
            o = halo - conv_left + j
            uc = uc + cw_ref[j:j + 1, :] * win[o:o + tc, :]
        pre = (jnp.dot(uc.astype(BF16), w_ref[:, d * 2 * C:(d + 1) * 2 * C], preferred_element_type=F32)
               + bias_ref[:, d * 2 * C:(d + 1) * 2 * C])
        r = _sigmoid(pre[:, :C])
        i = _sigmoid(pre[:, C:])
        log_a = -LRU_C * r * sp[d:d + 1, :]
        a = jnp.exp(log_a)
        th = jnp.tanh(log_a)
        mult = jnp.sqrt(-2.0 * th / (1.0 - th))
        return a, mult * (i * uc)

    def scan(a, b, reverse):
        d = 1
        while d < tc:
            if reverse:
                a_s = pltpu.roll(a, tc - d, 0)
                b_s = pltpu.roll(b, tc - d, 0)
                m = row < tc - d
            else:
                a_s = pltpu.roll(a, d, 0)
                b_s = pltpu.roll(b, d, 0)
                m = row >= d
            a_s = jnp.where(m, a_s, 1.0)
            b_s = jnp.where(m, b_s, 0.0)
            b = a * b_s + b
            a = a * a_s
            d *= 2
        return a, b

    def fwd_body(c, h0):
        a, b = gates(c, 0)
        a, b = scan(a, b, False)
        h = a * h0 + b
        hf_ref[pl.ds(pl.multiple_of(c * tc, tc), tc), :] = h
        return h[tc - 1:tc, :]

    lax.fori_loop(0, nc, fwd_body, jnp.zeros((1, C), F32))

    def bwd_body(j, h0):
        c = nc - 1 - j
        r0 = pl.multiple_of(c * tc, tc)
        a, b = gates(c, 1)
        a, b = scan(a, b, True)
        h = a * h0 + b
        y = (hf_ref[pl.ds(r0, tc), :] + h) * _gelu_tanh(gate_ref[0, pl.ds(r0, tc), :].astype(F32))
        out_ref[0, pl.ds(r0, tc), :] = _rms(y, g_ref[...]).astype(BF16)
        return h[0:1, :]

    lax.fori_loop(0, nc, bwd_body, jnp.zeros((1, C), F32))


def _bi_rglru(u, gate, conv_w, conv_b, w_all_bf, bias_all, lru_lambda, g, tc):
    B, S, C = u.shape
    full = lambda b: (0, 0)
    return pl.pallas_call(
        functools.partial(_lru_kernel, tc=tc),
        grid=(B,),
        in_specs=[
            pl.BlockSpec((1, S, C), lambda b: (b, 0, 0)),
            pl.BlockSpec((1, S, C), lambda b: (b, 0, 0)),
            pl.BlockSpec(conv_w.shape, full),
            pl.BlockSpec((1, C), full),
            pl.BlockSpec(w_all_bf.shape, full),
            pl.BlockSpec(bias_all.shape, full),
            pl.BlockSpec(lru_lambda.shape, full),
            pl.BlockSpec((1, C), full),
        ],
        out_specs=pl.BlockSpec((1, S, C), lambda b: (b, 0, 0)),
        out_shape=jax.ShapeDtypeStruct((B, S, C), BF16),
        scratch_shapes=[pltpu.VMEM((S, C), F32)],
        compiler_params=pltpu.CompilerParams(
            dimension_semantics=("parallel",), vmem_limit_bytes=VMEM_LIMIT),
        name="bi_rglru",
    )(u, gate, conv_w, conv_b.reshape(1, C), w_all_bf, bias_all, lru_lambda, g.reshape(1, C))


def _topk_rows(s, k):
    n = s.shape[0]
    rid = lax.broadcasted_iota(I32, s.shape, 0)
    vals, ids = [], []
    for _ in range(k):
        m = jnp.max(s, axis=0, keepdims=True)
        sel = jnp.min(jnp.where(s == m, rid, n), axis=0, keepdims=True)
        vals.append(m)
        ids.append(sel)
        s = jnp.where(rid == sel, -jnp.inf, s)
    return jnp.concatenate(vals, axis=0), jnp.concatenate(ids, axis=0)


CAND_ROW_PIECES = 4


def _candidate_pieces(k):
    up = lambda n: -(-n // SUBLANES) * SUBLANES
    pieces = [("row", i, 0, up(k // (i + 1))) for i in range(CAND_ROW_PIECES)]
    for j in range(k // (CAND_ROW_PIECES + 1)):
        for i0 in range(0, k // (j + 1), SUBLANES):
            pieces.append(("col", j, i0, SUBLANES))
    return pieces


def _route_kernel(attn_ref, rec_ref, x_ref, wo_ref, g_ref, wq_ref, k1_ref, k2_ref,
                  h_ref, xn_ref, idx_ref, gate_ref):
    aw = attn_ref.shape[1]
    h = (x_ref[...]
         + jnp.dot(attn_ref[...], wo_ref[0:aw, :], preferred_element_type=F32)
         + jnp.dot(rec_ref[...], wo_ref[aw:, :], preferred_element_type=F32))
    h_ref[...] = h
    xn = _rms(h, g_ref[...])
    xn_ref[...] = xn
    q = jnp.dot(xn.astype(BF16), wq_ref[...], preferred_element_type=F32).astype(BF16)
    nt = (((1,), (1,)), ((), ()))
    s1 = lax.dot_general(k1_ref[...], q, nt, preferred_element_type=F32)
    s2 = lax.dot_general(k2_ref[...], q, nt, preferred_element_type=F32)
    k = PEER_TOPK
    tm = q.shape[0]
    pieces = _candidate_pieces(k)
    pos_parts, ok_parts = [], []
    for kind, fixed, start, length in pieces:
        r = lax.broadcasted_iota(I32, (length, tm), 0) + start
        i, j = (fixed, r) if kind == "row" else (r, fixed)
        pos_parts.append(i * k + j)
        ok = (i + 1) * (j + 1) <= k
        ok_parts.append(ok if kind == "row" else ok & (r >= CAND_ROW_PIECES))
    pos = jnp.concatenate(pos_parts, axis=0)
    ok = jnp.concatenate(ok_parts, axis=0)
    idx_rows, gate_rows = [], []
    for hd in range(PEER_HEADS):
        v1, i1 = _topk_rows(s1[hd * N_KEYS:(hd + 1) * N_KEYS, :], k)
        v2, i2 = _topk_rows(s2[hd * N_KEYS:(hd + 1) * N_KEYS, :], k)
        cand_parts, cidx_parts = [], []
        for kind, fixed, start, length in pieces:
            if kind == "row":
                cand_parts.append(v1[fixed:fixed + 1, :] + v2[start:start + length, :])
                cidx_parts.append(i1[fixed:fixed + 1, :] * N_KEYS + i2[start:start + length, :])
            else:
                cand_parts.append(v1[start:start + length, :] + v2[fixed:fixed + 1, :])
                cidx_parts.append(i1[start:start + length, :] * N_KEYS + i2[fixed:fixed + 1, :])
        cand = jnp.where(ok, jnp.concatenate(cand_parts, axis=0), -jnp.inf)
        cidx = jnp.concatenate(cidx_parts, axis=0)
        sc, ids = [], []
        for _ in range(k):
            m = jnp.max(cand, axis=0, keepdims=True)
            sel = jnp.min(jnp.where(cand == m, pos, k * k), axis=0, keepdims=True)
            hit = pos == sel
            sc.append(m)
            ids.append(jnp.max(jnp.where(hit, cidx, -1), axis=0, keepdims=True))
            cand = jnp.where(hit, -jnp.inf, cand)
        sc = jnp.concatenate(sc, axis=0)
        e = jnp.exp(sc - sc[0:1, :])
        gate_rows.append(e / jnp.sum(e, axis=0, keepdims=True))
        idx_rows.append(jnp.concatenate(ids, axis=0))
    idx_ref[...] = jnp.concatenate(idx_rows, axis=0)
    gate_ref[...] = jnp.concatenate(gate_rows, axis=0)


def _mix_route(attn, rec, x2, wo_bf, g, wq_bf, k1t_bf, k2t_bf, tm):
    T, D = x2.shape
    NK = PEER_HEADS * PEER_TOPK
    row = lambda i: (i, 0)
    full = lambda i: (0, 0)
    return pl.pallas_call(
        _route_kernel,
        grid=(T // tm,),
        in_specs=[
            pl.BlockSpec((tm, attn.shape[1]), row),
            pl.BlockSpec((tm, rec.shape[1]), row),
            pl.BlockSpec((tm, D), row),
            pl.BlockSpec(wo_bf.shape, full),
            pl.BlockSpec((1, D), full),
            pl.BlockSpec(wq_bf.shape, full),
            pl.BlockSpec(k1t_bf.shape, full),
            pl.BlockSpec(k2t_bf.shape, full),
        ],
        out_specs=[
            pl.BlockSpec((tm, D), row),
            pl.BlockSpec((tm, D), row),
            pl.BlockSpec((NK, tm), lambda i: (0, i)),
            pl.BlockSpec((NK, tm), lambda i: (0, i)),
        ],
        out_shape=[
            jax.ShapeDtypeStruct((T, D), F32),
            jax.ShapeDtypeStruct((T, D), F32),
            jax.ShapeDtypeStruct((NK, T), I32),
            jax.ShapeDtypeStruct((NK, T), F32),
        ],
        compiler_params=pltpu.CompilerParams(
            dimension_semantics=("parallel",), vmem_limit_bytes=VMEM_LIMIT),
        name="mix_route",
    )(attn, rec, x2, wo_bf, g.reshape(1, D), wq_bf, k1t_bf, k2t_bf)


ROW_CHUNKS = 8
HI_HALF_MASK = -65536


def _pack_kernel(u_ref, v_ref, o_ref):
    half = u_ref.shape[1] // 2

    def words(a):
        bits = pltpu.bitcast(a.astype(BF16).astype(F32), I32)
        lo = lax.shift_right_logical(bits[:, :half], jnp.full((a.shape[0], half), 16, I32))
        return lo | (bits[:, half:] & HI_HALF_MASK)
    o_ref[:, :half] = words(u_ref[...])
    o_ref[:, half:] = words(v_ref[...])


def _pack_rows(u, v, rows=PACK_ROWS):
    E, D = u.shape
    rows = min(rows, E)
    blk = lambda i: (i, 0)
    return pl.pallas_call(
        _pack_kernel,
        grid=(E // rows,),
        in_specs=[pl.BlockSpec((rows, D), blk), pl.BlockSpec((rows, D), blk)],
        out_specs=pl.BlockSpec((rows, D), blk),
        out_shape=jax.ShapeDtypeStruct((E, D), I32),
        compiler_params=pltpu.CompilerParams(dimension_semantics=("parallel",), vmem_limit_bytes=VMEM_LIMIT),
        name="pack_rows",
    )(u, v)


def _word_lo(w):
    return pltpu.bitcast(w << 16, F32)


def _word_hi(w):
    return pltpu.bitcast(w & HI_HALF_MASK, F32)
TOKEN_UNROLL = 2
DMA_PRIORITIES = 2
PEER_RING = 4
PEER_PREFETCH = 2


def _peer_kernel(idx_ref, idx_next_ref, xn_ref, gate_ref, uv_hbm, o_ref, *scratch, tt, nk):
    bufs = scratch[:PEER_RING]
    lg_ref, act_ref, sems = scratch[PEER_RING:]
    g = pl.program_id(0)
    n = pl.num_programs(0)
    nt = (((1,), (1,)), ((), ()))
    ch = ROW_CHUNKS
    hc = ch // 2

    def row_copy(ids_ref, q, t, k):
        e = ids_ref[0, 0, (q * tt + t) * nk + k]
        return pltpu.make_async_copy(uv_hbm.at[e], bufs[q].at[t, :, pl.ds(k, 1), :], sems.at[q])

    def wait_group(q):
        pltpu.make_async_copy(bufs[(q + 1) % PEER_RING], bufs[q], sems.at[q]).wait()

    @pl.when(g == 0)
    def _():
        for q in range(PEER_PREFETCH):
            def body(t, carry, q=q):
                for k in range(nk):
                    row_copy(idx_ref, q, t, k).start(priority=k % DMA_PRIORITIES)
                return carry
            lax.fori_loop(0, tt, body, 0)

    def group(q):
        buf = bufs[q]
        row0 = q * tt
        ahead = q + PEER_PREFETCH
        ids_ahead = idx_ref if ahead < PEER_RING else idx_next_ref

        def issue(t, k0, k1):
            for k in range(k0, k1):
                row_copy(ids_ahead, ahead % PEER_RING, t, k).start(priority=k % DMA_PRIORITIES)

        def phase_u(tb, carry):
            for j in range(TOKEN_UNROLL):
                t = tb * TOKEN_UNROLL + j
                issue(t, 0, nk // 2)
                xt = xn_ref[row0 + t]
                acc = None
                for s in range(hc):
                    w = buf[t, s]
                    part = _word_lo(w) * xt[s:s + 1, :] + _word_hi(w) * xt[hc + s:hc + s + 1, :]
                    acc = part if acc is None else acc + part
                hi = acc.astype(BF16)
                lo = (acc - hi.astype(F32)).astype(BF16)
                ones = jnp.ones((SUBLANES, LANES), BF16)
                red = (lax.dot_general(ones, hi, nt, preferred_element_type=F32)
                       + lax.dot_general(ones, lo, nt, preferred_element_type=F32))
                lg_ref[t] = red[0:1, :]
            return carry
        lax.fori_loop(0, tt // TOKEN_UNROLL, phase_u, 0)
        act_ref[...] = _gelu_tanh(lg_ref[...]) * gate_ref[row0:row0 + tt]

        def phase_v(tb, carry):
            for j in range(TOKEN_UNROLL):
                t = tb * TOKEN_UNROLL + j
                issue(t, nk // 2, nk)
                act = act_ref[t].astype(BF16)
                words = [buf[t, hc + s] for s in range(hc)]
                cols = ([jnp.dot(act, _word_lo(w).astype(BF16), preferred_element_type=F32) for w in words]
                        + [jnp.dot(act, _word_hi(w).astype(BF16), preferred_element_type=F32) for w in words])
                o_ref[row0 + t] = jnp.concatenate(cols, axis=0)
            return carry
        lax.fori_loop(0, tt // TOKEN_UNROLL, phase_v, 0)

    for q in range(PEER_RING):
        wait_group(q)
        group(q)

    @pl.when(g == n - 1)
    def _():
        for q in range(PEER_PREFETCH):
            wait_group(q)


def _peer(idx, gate, xn3, uv4, n_tokens, tt):
    T, ch, _ = xn3.shape
    nk = idx.shape[1]
    step = PEER_RING * tt
    n = n_tokens // step
    idx3 = idx.reshape(T // step, 1, step * nk)
    tok = lambda i: (i, 0, 0)
    return pl.pallas_call(
        functools.partial(_peer_kernel, tt=tt, nk=nk),
        grid=(n,),
        in_specs=[
            pl.BlockSpec((1, 1, step * nk), tok, memory_space=pltpu.SMEM),
            pl.BlockSpec((1, 1, step * nk), lambda i: (jnp.minimum(i + 1, n - 1), 0, 0), memory_space=pltpu.SMEM),
            pl.BlockSpec((step, ch, LANES), tok),
            pl.BlockSpec((step, 1, nk), tok),
            pl.BlockSpec(memory_space=pl.ANY),
        ],
        out_specs=pl.BlockSpec((step, ch, LANES), tok),
        out_shape=jax.ShapeDtypeStruct((n_tokens, ch, LANES), F32),
        scratch_shapes=(
            [pltpu.VMEM((tt, ch, nk, LANES), I32) for _ in range(PEER_RING)]
            + [pltpu.VMEM((tt, 1, nk), F32), pltpu.VMEM((tt, 1, nk), F32), pltpu.SemaphoreType.DMA((PEER_RING,))]),
        compiler_params=pltpu.CompilerParams(
            dimension_semantics=("arbitrary",), vmem_limit_bytes=VMEM_LIMIT),
        name="peer",
    )(idx3, idx3, xn3, gate.reshape(T, 1, nk), uv4)


SC_CORES = 2
SC_SUBCORES = 16
SC_LANES = 16
SC_WORKERS = SC_CORES * SC_SUBCORES
SC_EXPERT_BLOCK = 16
SC_CHUNK_GROUP = 4
IDX_SLOTS = 4
SC_TOKEN_SHARE_NUM, SC_TOKEN_SHARE_DEN = 33, 64


def _peer_sc(idx, gate, xn, uv, tok0, toks_per_worker):
    T, nk = idx.shape
    D = xn.shape[1]
    L = SC_LANES
    kb_rows = SC_EXPERT_BLOCK
    n_blocks = nk // kb_rows
    n_chunks = D // L
    half = D // 2
    uc = SC_CHUNK_GROUP
    word_lo = lambda w: plsc.bitcast(w << 16, F32)
    word_hi = lambda w: plsc.bitcast(w & HI_HALF_MASK, F32)
    mesh = plsc.VectorSubcoreMesh(core_axis_name="c", subcore_axis_name="s")

    def body(idx_hbm, gate_hbm, xn_hbm, uv_hbm, out_hbm, idx_v, gate_v, x_v, o_v, rows_v, acc_v, sems, tsems):
        wid = lax.axis_index("s") * SC_CORES + lax.axis_index("c")
        base = wid * toks_per_worker
        lane = lax.iota(I32, L)

        def tok_of(ti):
            return tok0 + base + jnp.minimum(ti, toks_per_worker - 1)

        def idx_copy(ti):
            return pltpu.make_async_copy(idx_hbm.at[tok_of(ti)], idx_v.at[pl.ds((ti % IDX_SLOTS) * nk, nk)],
                                         tsems.at[0])

        def gate_copy(ti):
            return pltpu.make_async_copy(gate_hbm.at[tok_of(ti)], gate_v.at[pl.ds((ti % 2) * nk, nk)], tsems.at[1])

        def x_copy(ti):
            return pltpu.make_async_copy(xn_hbm.at[tok_of(ti)], x_v.at[pl.ds((ti % 2) * D, D)], tsems.at[2])

        def gather(ti, kb, buf):
            first = (ti % IDX_SLOTS) * nk + kb * kb_rows
            return pltpu.make_async_copy(uv_hbm.at[idx_v.at[pl.ds(first, kb_rows)]], rows_v.at[buf], sems.at[buf])

        def evaluate(kb, rv, goff, xoff):
            for j in range(kb_rows):
                acc_v[j] = jnp.zeros((L,), F32)

            @plsc.parallel_loop(0, half // L // uc, unroll=2)
            def _(cg):
                x_lo = [x_v[pl.ds(xoff + (cg * uc + cc) * L, L)] for cc in range(uc)]
                x_hi = [x_v[pl.ds(xoff + half + (cg * uc + cc) * L, L)] for cc in range(uc)]
                for j in range(kb_rows):
                    pr = None
                    for cc in range(uc):
                        w = rv[j, pl.ds((cg * uc + cc) * L, L)]
                        part = word_lo(w) * x_lo[cc] + word_hi(w) * x_hi[cc]
                        pr = part if pr is None else pr + part
                    plsc.addupdate(acc_v.at[j], pr)
            logits = jnp.zeros((L,), F32)
            for j in range(kb_rows):
                logits = jnp.where(lane == j, jnp.sum(acc_v[j]), logits)
            z = math.sqrt(2.0 / math.pi) * (logits + 0.044715 * (logits * logits * logits))
            th = 1.0 - 2.0 / (1.0 + jnp.exp(2.0 * z))
            act = 0.5 * logits * (1.0 + th) * gate_v[pl.ds(goff + kb * kb_rows, kb_rows)]
            acts = [jnp.sum(jnp.where(lane == j, act, 0.0)) for j in range(kb_rows)]

            @plsc.parallel_loop(0, half // L, unroll=2)
            def _(c):
                acc_lo = None
                acc_hi = None
                for j in range(kb_rows):
                    w = rv[j, pl.ds(half + c * L, L)]
                    lo = acts[j] * word_lo(w)
                    hi = acts[j] * word_hi(w)
                    acc_lo = lo if acc_lo is None else acc_lo + lo
                    acc_hi = hi if acc_hi is None else acc_hi + hi
                plsc.addupdate(o_v.at[pl.ds(c * L, L)], acc_lo)
                plsc.addupdate(o_v.at[pl.ds(half + c * L, L)], acc_hi)

        def token(ti, carry):
            gate_copy(ti).wait()
            x_copy(ti).wait()
            idx_copy(ti + 1).wait()
            gate_copy(ti + 1).start()
            x_copy(ti + 1).start()
            idx_copy(ti + 2).start()
            goff = (ti % 2) * nk
            xoff = (ti % 2) * D

            @plsc.parallel_loop(0, n_chunks)
            def _(c):
                o_v[pl.ds(c * L, L)] = jnp.zeros((L,), F32)

            def block(kb, carry):
                buf = kb % 2
                last = kb + 1 >= n_blocks
                gather(ti, kb, buf).wait()
                gather(jnp.where(last, ti + 1, ti), jnp.where(last, 0, kb + 1), 1 - buf).start()
                evaluate(kb, rows_v.at[buf], goff, xoff)
                return carry
            lax.fori_loop(0, n_blocks, block, 0)
            pltpu.sync_copy(o_v, out_hbm.at[base + ti])
            return carry

        idx_copy(0).start()
        idx_copy(0).wait()
        idx_copy(1).start()
        gate_copy(0).start()
        x_copy(0).start()
        gather(0, 0, 0).start()
        lax.fori_loop(0, toks_per_worker, token, 0)
        gate_copy(toks_per_worker).wait()
        x_copy(toks_per_worker).wait()
        idx_copy(toks_per_worker + 1).wait()
        gather(toks_per_worker, 0, 0).wait()

    return pl.kernel(
        body, mesh=mesh,
        compiler_params=pltpu.CompilerParams(needs_layout_passes=False),
        out_type=jax.ShapeDtypeStruct((SC_WORKERS * toks_per_worker, D), F32),
        scratch_types=[
            pltpu.VMEM((IDX_SLOTS * nk,), I32),
            pltpu.VMEM((2 * nk,), F32),
            pltpu.VMEM((2 * D,), F32),
            pltpu.VMEM((D,), F32),
            pltpu.VMEM((2, kb_rows, D), I32),
            pltpu.VMEM((kb_rows, L), F32),
            pltpu.SemaphoreType.DMA((2,)),
            pltpu.SemaphoreType.DMA((3,)),
        ],
    )(idx, gate, xn, uv)


def _ple_kernel(h_ref, po_ref, p_ref, g3_ref, wg_ref, wp_ref, gf_ref, o_ref, *, final):
    h = h_ref[...] + po_ref[...]
    xn = _rms(h, g3_ref[...]).astype(BF16)
    gate = _sigmoid(jnp.dot(xn, wg_ref[...], preferred_element_type=F32))
    proj = jnp.dot(p_ref[...].astype(BF16), wp_ref[...], preferred_element_type=F32)
    h = h + gate * proj
    o_ref[...] = _rms(h, gf_ref[...]) if final else h


def _ple_out(h1, peer_out, p2, g3, wg_bf, wp_bf, gf, final, tm):
    T, D = h1.shape
    row = lambda i: (i, 0)
    full = lambda i: (0, 0)
    return pl.pallas_call(
        functools.partial(_ple_kernel, final=final),
        grid=(T // tm,),
        in_specs=[
            pl.BlockSpec((tm, D), row),
            pl.BlockSpec((tm, D), row),
            pl.BlockSpec((tm, p2.shape[1]), row),
            pl.BlockSpec((1, D), full),
            pl.BlockSpec(wg_bf.shape, full),
            pl.BlockSpec(wp_bf.shape, full),
            pl.BlockSpec((1, D), full),
        ],
        out_specs=pl.BlockSpec((tm, D), row),
        out_shape=jax.ShapeDtypeStruct((T, D), F32),
        compiler_params=pltpu.CompilerParams(
            dimension_semantics=("parallel",), vmem_limit_bytes=VMEM_LIMIT),
        name="ple_out",
    )(h1, peer_out, p2, g3.reshape(1, D), wg_bf, wp_bf, gf.reshape(1, D))


def _block_diag(w):
    nb, bw, _ = w.shape
    eye = jnp.eye(nb, dtype=w.dtype)
    return (eye[:, None, :, None] * w[:, :, None, :]).reshape(nb * bw, nb * bw)


def _key_matrix(keys, half):
    z = jnp.zeros_like(keys)
    blk = jnp.concatenate([keys, z] if half == 0 else [z, keys], axis=1)
    return jnp.kron(jnp.eye(PEER_HEADS, dtype=keys.dtype), blk)


def kernel(x, p, positions, norm_mix_g, w_in, lambda_q1, lambda_k1, lambda_q2, lambda_k2, diff_norm_g, conv_w, conv_b, lru_wa, lru_ba, lru_wx, lru_bx, lru_lambda, lru_norm_g, w_out, norm_ffn_g, peer_wq, peer_keys1, peer_keys2, peer_u, peer_v, norm_ple_g, ple_w_gate, ple_w_proj, final_norm_g):
    B, S, D = x.shape
    T = B * S
    depth = w_in.shape[0]
    h = x
    for i in range(depth):
        lambda_init = 0.8 - 0.6 * math.exp(-0.3 * i)
        q, k, v, u, gate = _in_proj(h, positions, norm_mix_g[i], w_in[i].astype(BF16), tm=IN_PROJ_ROWS)
        attn = _diff_attn(q, k, v, lambda_q1[i], lambda_k1[i], lambda_q2[i], lambda_k2[i],
                          diff_norm_g[i], lambda_init, tq=ATTN_Q_ROWS)
        w_all = jnp.concatenate([_block_diag(lru_wa[i, 0]), _block_diag(lru_wx[i, 0]),
                                 _block_diag(lru_wa[i, 1]), _block_diag(lru_wx[i, 1])], axis=1).astype(BF16)
        bias_all = jnp.concatenate([lru_ba[i, 0], lru_bx[i, 0], lru_ba[i, 1], lru_bx[i, 1]]).reshape(1, -1)
        rec = _bi_rglru(u, gate, conv_w[i], conv_b[i], w_all, bias_all, lru_lambda[i], lru_norm_g[i], tc=LRU_CHUNK_ROWS)
        h1, xn2, idx_t, gate_t = _mix_route(
            attn.reshape(T, -1), rec.reshape(T, -1), h.reshape(T, D), w_out[i].astype(BF16), norm_ffn_g[i],
            peer_wq[i].astype(BF16), _key_matrix(peer_keys1[i], 0).astype(BF16),
            _key_matrix(peer_keys2[i], 1).astype(BF16), tm=ROUTE_TOKENS)
        n_exp = peer_u.shape[1]
        uv2 = _pack_rows(peer_u[i], peer_v[i])
        uv4 = uv2.reshape(n_exp, ROW_CHUNKS, 1, LANES)
        idx, gate_tk = idx_t.T, gate_t.T
        t_sc = T * SC_TOKEN_SHARE_NUM // SC_TOKEN_SHARE_DEN // (2 * SC_WORKERS) * (2 * SC_WORKERS)
        t_tc = T - t_sc
        peer_sc = _peer_sc(idx, gate_tk, xn2, uv2, t_tc, t_sc // SC_WORKERS)
        peer_tc = _peer(idx, gate_tk, xn2.reshape(T, ROW_CHUNKS, LANES), uv4, t_tc, tt=PEER_GROUP_TOKENS)
        peer_out = jnp.concatenate([peer_tc.reshape(t_tc, D), peer_sc], axis=0)
        h = _ple_out(h1, peer_out, p[i].reshape(T, -1), norm_ple_g[i], ple_w_gate[i].astype(BF16),
                     ple_w_proj[i].astype(BF16), final_norm_g, final=(i == depth - 1), tm=PLE_ROWS)
        h = h.reshape(B, S, D)
    return h
```

```python
import functools
import math

import jax
import jax.numpy as jnp
from jax import lax
from jax.experimental import pallas as pl
from jax.experimental.pallas import tpu as pltpu
from jax.experimental.pallas import tpu_sc as plsc

F32 = jnp.float32
BF16 = jnp.bfloat16
I32 = jnp.int32

EPS = 1e-6
DIFF_HEAD_DIM = 64
DIFF_V_DIM = 128
N_DIFF_HEADS = 4
ROPE_DIM = 16
ROPE_THETA = 500000.0
LRU_WIDTH = 512
LRU_C = 8.0
N_KEYS = 128
PEER_HEADS = 8
PEER_TOPK = 16
HALF_KEY = 64
LOG2_E = math.log2(math.e)
LANES = 128
SUBLANES = 8
VMEM_LIMIT = 56 * 1024 * 1024
IN_PROJ_ROWS = 512
ATTN_Q_ROWS = 256
LRU_CHUNK_ROWS = 256
ROUTE_TOKENS = 256
PEER_GROUP_TOKENS = 8
PLE_ROWS = 512
PACK_ROWS = 512


def _rms(x, g):
    return x * lax.rsqrt(jnp.mean(x * x, axis=-1, keepdims=True) + EPS) * g


def _gelu_tanh(x):
    return 0.5 * x * (1.0 + jnp.tanh(math.sqrt(2.0 / math.pi) * (x + 0.044715 * (x * x * x))))


def _sigmoid(x):
    return 1.0 / (1.0 + jnp.exp(-x))


def _inproj_kernel(x_ref, pos_ref, g_ref, w_ref, q_ref, k_ref, v_ref, u_ref, gate_ref):
    x = x_ref[0]
    xn = _rms(x, g_ref[...]).astype(BF16)
    pos = pos_ref[0].astype(F32)
    lane = lax.broadcasted_iota(I32, (1, LANES), 1)
    p = lane & (DIFF_HEAD_DIM - 1)
    freq = (p & (ROPE_DIM // 2 - 1)).astype(F32)
    inv_freq = jnp.exp(freq * (-2.0 / ROPE_DIM * math.log(ROPE_THETA)))
    ang = pos * inv_freq
    cs = jnp.cos(ang)
    sn = jnp.sin(ang)
    half = ROPE_DIM // 2
    c_mul = jnp.where(p < ROPE_DIM, cs, 1.0)
    s_up = jnp.where(p < half, -sn, 0.0)
    s_dn = jnp.where((p >= half) & (p < ROPE_DIM), sn, 0.0)

    def rope(t):
        return t * c_mul + pltpu.roll(t, LANES - half, 1) * s_up + pltpu.roll(t, half, 1) * s_dn

    nq = q_ref.shape[-1]
    pq = jnp.dot(xn, w_ref[:, 0:nq], preferred_element_type=F32)
    pk = jnp.dot(xn, w_ref[:, nq:2 * nq], preferred_element_type=F32)
    scale = DIFF_HEAD_DIM ** -0.5 * LOG2_E
    for j in range(nq // LANES):
        sl = slice(j * LANES, (j + 1) * LANES)
        q_ref[0, :, sl] = (rope(pq[:, sl]) * scale).astype(BF16)
        k_ref[0, :, sl] = rope(pk[:, sl]).astype(BF16)
    v_ref[0] = jnp.dot(xn, w_ref[:, 2 * nq:3 * nq], preferred_element_type=F32).astype(BF16)
    u_ref[0] = jnp.dot(xn, w_ref[:, 3 * nq:3 * nq + LRU_WIDTH], preferred_element_type=F32)
    gate_ref[0] = jnp.dot(xn, w_ref[:, 3 * nq + LRU_WIDTH:3 * nq + 2 * LRU_WIDTH],
                          preferred_element_type=F32).astype(BF16)


def _in_proj(x, positions, g, w_in_bf, tm):
    B, S, D = x.shape
    nq = N_DIFF_HEADS * 2 * DIFF_HEAD_DIM
    ncols = w_in_bf.shape[1]
    row = lambda b, i: (b, i, 0)
    return pl.pallas_call(
        _inproj_kernel,
        grid=(B, S // tm),
        in_specs=[
            pl.BlockSpec((1, tm, D), row),
            pl.BlockSpec((1, tm, 1), row),
            pl.BlockSpec((1, D), lambda b, i: (0, 0)),
            pl.BlockSpec((D, ncols), lambda b, i: (0, 0)),
        ],
        out_specs=[
            pl.BlockSpec((1, tm, nq), row),
            pl.BlockSpec((1, tm, nq), row),
            pl.BlockSpec((1, tm, nq), row),
            pl.BlockSpec((1, tm, LRU_WIDTH), row),
            pl.BlockSpec((1, tm, LRU_WIDTH), row),
        ],
        out_shape=[
            jax.ShapeDtypeStruct((B, S, nq), BF16),
            jax.ShapeDtypeStruct((B, S, nq), BF16),
            jax.ShapeDtypeStruct((B, S, nq), BF16),
            jax.ShapeDtypeStruct((B, S, LRU_WIDTH), F32),
            jax.ShapeDtypeStruct((B, S, LRU_WIDTH), BF16),
        ],
        compiler_params=pltpu.CompilerParams(
            dimension_semantics=("parallel", "parallel"), vmem_limit_bytes=VMEM_LIMIT),
        name="in_proj",
    )(x, positions.reshape(B, S, 1), g.reshape(1, D), w_in_bf)


def _attn_kernel(lq1_ref, lk1_ref, lq2_ref, lk2_ref, g_ref, q_ref, k_ref, v_ref, o_ref, *, lambda_init):
    lam = (jnp.exp(jnp.sum(lq1_ref[...] * lk1_ref[...], axis=-1, keepdims=True))
           - jnp.exp(jnp.sum(lq2_ref[...] * lk2_ref[...], axis=-1, keepdims=True))
           + lambda_init)
    q = q_ref[0]
    k = k_ref[0]
    v = v_ref[0]
    lane = lax.broadcasted_iota(I32, q.shape, 1)
    zero = jnp.zeros_like(q)
    q0 = jnp.where(lane < DIFF_HEAD_DIM, q, zero)
    q1 = jnp.where(lane >= DIFF_HEAD_DIM, q, zero)
    nt = (((1,), (1,)), ((), ()))
    s0 = lax.dot_general(q0, k, nt, preferred_element_type=F32)
    s1 = lax.dot_general(q1, k, nt, preferred_element_type=F32)
    v_ext = jnp.concatenate([v, jnp.ones_like(v)], axis=1)
    p0 = jnp.exp2(s0 - jnp.max(s0, axis=-1, keepdims=True)).astype(BF16)
    p1 = jnp.exp2(s1 - jnp.max(s1, axis=-1, keepdims=True)).astype(BF16)
    e0 = jnp.dot(p0, v_ext, preferred_element_type=F32)
    e1 = jnp.dot(p1, v_ext, preferred_element_type=F32)
    dv = v.shape[1]
    o = e0[:, :dv] / e0[:, dv:dv + 1] - e1[:, :dv] * (lam / e1[:, dv:dv + 1])
    o_ref[0] = (_rms(o, g_ref[...]) * (1.0 - lambda_init)).astype(BF16)


def _diff_attn(q, k, v, lq1, lk1, lq2, lk2, g, lambda_init, tq):
    B, S, W = q.shape
    H = W // DIFF_V_DIM
    vec = lambda b, h, i: (0, 0)
    return pl.pallas_call(
        functools.partial(_attn_kernel, lambda_init=lambda_init),
        grid=(B, H, S // tq),
        in_specs=[
            pl.BlockSpec((1, DIFF_HEAD_DIM), vec),
            pl.BlockSpec((1, DIFF_HEAD_DIM), vec),
            pl.BlockSpec((1, DIFF_HEAD_DIM), vec),
            pl.BlockSpec((1, DIFF_HEAD_DIM), vec),
            pl.BlockSpec((1, DIFF_V_DIM), vec),
            pl.BlockSpec((1, tq, DIFF_V_DIM), lambda b, h, i: (b, i, h)),
            pl.BlockSpec((1, S, DIFF_V_DIM), lambda b, h, i: (b, 0, h)),
            pl.BlockSpec((1, S, DIFF_V_DIM), lambda b, h, i: (b, 0, h)),
        ],
        out_specs=pl.BlockSpec((1, tq, DIFF_V_DIM), lambda b, h, i: (b, i, h)),
        out_shape=jax.ShapeDtypeStruct((B, S, W), BF16),
        compiler_params=pltpu.CompilerParams(
            dimension_semantics=("parallel", "parallel", "parallel"), vmem_limit_bytes=VMEM_LIMIT),
        name="diff_attn",
    )(lq1.reshape(1, -1), lk1.reshape(1, -1), lq2.reshape(1, -1), lk2.reshape(1, -1),
      g.reshape(1, -1), q, k, v)


def _lru_kernel(u_ref, gate_ref, cw_ref, cb_ref, w_ref, bias_ref, lam_ref, g_ref, out_ref, hf_ref, *, tc):
    S = u_ref.shape[1]
    C = u_ref.shape[2]
    nc = S // tc
    halo = SUBLANES
    neg_lam = -lam_ref[...]
    sp = jnp.maximum(neg_lam, 0.0) + jnp.log(1.0 + jnp.exp(-jnp.abs(neg_lam)))
    row = lax.broadcasted_iota(I32, (tc, 1), 0)
    conv_taps = cw_ref.shape[0]
    conv_left = 2

    def gates(c, d):
        r0 = pl.multiple_of(c * tc, tc)
        x = u_ref[0, pl.ds(r0, tc), :]
        prev = u_ref[0, pl.ds(pl.multiple_of(jnp.maximum(r0 - halo, 0), halo), halo), :]
        nxt = u_ref[0, pl.ds(pl.multiple_of(jnp.minimum(r0 + tc, S - halo), halo), halo), :]
        prev = jnp.where(c > 0, prev, 0.0)
        nxt = jnp.where(c < nc - 1, nxt, 0.0)
        win = jnp.concatenate([prev, x, nxt], axis=0)
        uc = cb_ref[...]
        for j in range(conv_taps):
            o = halo - conv_left + j
            uc = uc + cw_ref[j:j + 1, :] * win[o:o + tc, :]
        pre = (jnp.dot(uc.astype(BF16), w_ref[:, d * 2 * C:(d + 1) * 2 * C], preferred_element_type=F32)
               + bias_ref[:, d * 2 * C:(d + 1) * 2 * C])
        r = _sigmoid(pre[:, :C])
        i = _sigmoid(pre[:, C:])
        log_a = -LRU_C * r * sp[d:d + 1, :]
        a = jnp.exp(log_a)
        th = jnp.tanh(log_a)
        mult = jnp.sqrt(-2.0 * th / (1.0 - th))
        return a, mult * (i * uc)

    def scan(a, b, reverse):
        d = 1
        while d < tc:
            if reverse:
                a_s = pltpu.roll(a, tc - d, 0)
                b_s = pltpu.roll(b, tc - d, 0)
                m = row < tc - d
            else:
                a_s = pltpu.roll(a, d, 0)
                b_s = pltpu.roll(b, d, 0)
                m = row >= d
            a_s = jnp.where(m, a_s, 1.0)
            b_s = jnp.where(m, b_s, 0.0)
            b = a * b_s + b
            a = a * a_s
            d *= 2
        return a, b

    def fwd_body(c, h0):
        a, b = gates(c, 0)
        a, b = scan(a, b, False)
        h = a * h0 + b
        hf_ref[pl.ds(pl.multiple_of(c * tc, tc), tc), :] = h
        return h[tc - 1:tc, :]

    lax.fori_loop(0, nc, fwd_body, jnp.zeros((1, C), F32))

    def bwd_body(j, h0):
        c = nc - 1 - j
        r0 = pl.multiple_of(c * tc, tc)
        a, b = gates(c, 1)
        a, b = scan(a, b, True)
        h = a * h0 + b
        y = (hf_ref[pl.ds(r0, tc), :] + h) * _gelu_tanh(gate_ref[0, pl.ds(r0, tc), :].astype(F32))
        out_ref[0, pl.ds(r0, tc), :] = _rms(y, g_ref[...]).astype(BF16)
        return h[0:1, :]

    lax.fori_loop(0, nc, bwd_body, jnp.zeros((1, C), F32))


def _bi_rglru(u, gate, conv_w, conv_b, w_all_bf, bias_all, lru_lambda, g, tc):
    B, S, C = u.shape
    full = lambda b: (0, 0)
    return pl.pallas_call(
        functools.partial(_lru_kernel, tc=tc),
        grid=(B,),
        in_specs=[
            pl.BlockSpec((1, S, C), lambda b: (b, 0, 0)),
            pl.BlockSpec((1, S, C), lambda b: (b, 0, 0)),
            pl.BlockSpec(conv_w.shape, full),
            pl.BlockSpec((1, C), full),
            pl.BlockSpec(w_all_bf.shape, full),
            pl.BlockSpec(bias_all.shape, full),
            pl.BlockSpec(lru_lambda.shape, full),
            pl.BlockSpec((1, C), full),
        ],
        out_specs=pl.BlockSpec((1, S, C), lambda b: (b, 0, 0)),
        out_shape=jax.ShapeDtypeStruct((B, S, C), BF16),
        scratch_shapes=[pltpu.VMEM((S, C), F32)],
        compiler_params=pltpu.CompilerParams(
            dimension_semantics=("parallel",), vmem_limit_bytes=VMEM_LIMIT),
        name="bi_rglru",
    )(u, gate, conv_w, conv_b.reshape(1, C), w_all_bf, bias_all, lru_lambda, g.reshape(1, C))


def _topk_rows(s, k):
    n = s.shape[0]
    rid = lax.broadcasted_iota(I32, s.shape, 0)
    vals, ids = [], []
    for _ in range(k):
        m = jnp.max(s, axis=0, keepdims=True)
        sel = jnp.min(jnp.where(s == m, rid, n), axis=0, keepdims=True)
        vals.append(m)
        ids.append(sel)
        s = jnp.where(rid == sel, -jnp.inf, s)
    return jnp.concatenate(vals, axis=0), jnp.concatenate(ids, axis=0)


CAND_ROW_PIECES = 4


def _candidate_pieces(k):
    up = lambda n: -(-n // SUBLANES) * SUBLANES
    pieces = [("row", i, 0, up(k // (i + 1))) for i in range(CAND_ROW_PIECES)]
    for j in range(k // (CAND_ROW_PIECES + 1)):
        for i0 in range(0, k // (j + 1), SUBLANES):
            pieces.append(("col", j, i0, SUBLANES))
    return pieces


def _route_kernel(attn_ref, rec_ref, x_ref, wo_ref, g_ref, wq_ref, k1_ref, k2_ref,
                  h_ref, xn_ref, idx_ref, gate_ref):
    aw = attn_ref.shape[1]
    h = (x_ref[...]
         + jnp.dot(attn_ref[...], wo_ref[0:aw, :], preferred_element_type=F32)
         + jnp.dot(rec_ref[...], wo_ref[aw:, :], preferred_element_type=F32))
    h_ref[...] = h
    xn = _rms(h, g_ref[...])
    xn_ref[...] = xn
    q = jnp.dot(xn.astype(BF16), wq_ref[...], preferred_element_type=F32).astype(BF16)
    nt = (((1,), (1,)), ((), ()))
    s1 = lax.dot_general(k1_ref[...], q, nt, preferred_element_type=F32)
    s2 = lax.dot_general(k2_ref[...], q, nt, preferred_element_type=F32)
    k = PEER_TOPK
    tm = q.shape[0]
    pieces = _candidate_pieces(k)
    pos_parts, ok_parts = [], []
    for kind, fixed, start, length in pieces:
        r = lax.broadcasted_iota(I32, (length, tm), 0) + start
        i, j = (fixed, r) if kind == "row" else (r, fixed)
        pos_parts.append(i * k + j)
        ok = (i + 1) * (j + 1) <= k
        ok_parts.append(ok if kind == "row" else ok & (r >= CAND_ROW_PIECES))
    pos = jnp.concatenate(pos_parts, axis=0)
    ok = jnp.concatenate(ok_parts, axis=0)
    idx_rows, gate_rows = [], []
    for hd in range(PEER_HEADS):
        v1, i1 = _topk_rows(s1[hd * N_KEYS:(hd + 1) * N_KEYS, :], k)
        v2, i2 = _topk_rows(s2[hd * N_KEYS:(hd + 1) * N_KEYS, :], k)
        cand_parts, cidx_parts = [], []
        for kind, fixed, start, length in pieces:
            if kind == "row":
                cand_parts.append(v1[fixed:fixed + 1, :] + v2[start:start + length, :])
                cidx_parts.append(i1[fixed:fixed + 1, :] * N_KEYS + i2[start:start + length, :])
            else:
                cand_parts.append(v1[start:start + length, :] + v2[fixed:fixed + 1, :])
                cidx_parts.append(i1[start:start + length, :] * N_KEYS + i2[fixed:fixed + 1, :])
        cand = jnp.where(ok, jnp.concatenate(cand_parts, axis=0), -jnp.inf)
        cidx = jnp.concatenate(cidx_parts, axis=0)
        sc, ids = [], []
        for _ in range(k):
            m = jnp.max(cand, axis=0, keepdims=True)
            sel = jnp.min(jnp.where(cand == m, pos, k * k), axis=0, keepdims=True)
            hit = pos == sel
            sc.append(m)
            ids.append(jnp.max(jnp.where(hit, cidx, -1), axis=0, keepdims=True))
            cand = jnp.where(hit, -jnp.inf, cand)
        sc = jnp.concatenate(sc, axis=0)
        e = jnp.exp(sc - sc[0:1, :])
        gate_rows.append(e / jnp.sum(e, axis=0, keepdims=True))
        idx_rows.append(jnp.concatenate(ids, axis=0))
    idx_ref[...] = jnp.concatenate(idx_rows, axis=0)
    gate_ref[...] = jnp.concatenate(gate_rows, axis=0)


def _mix_route(attn, rec, x2, wo_bf, g, wq_bf, k1t_bf, k2t_bf, tm):
    T, D = x2.shape
    NK = PEER_HEADS * PEER_TOPK
    row = lambda i: (i, 0)
    full = lambda i: (0, 0)
    return pl.pallas_call(
        _route_kernel,
        grid=(T // tm,),
        in_specs=[
            pl.BlockSpec((tm, attn.shape[1]), row),
            pl.BlockSpec((tm, rec.shape[1]), row),
            pl.BlockSpec((tm, D), row),
            pl.BlockSpec(wo_bf.shape, full),
            pl.BlockSpec((1, D), full),
            pl.BlockSpec(wq_bf.shape, full),
            pl.BlockSpec(k1t_bf.shape, full),
            pl.BlockSpec(k2t_bf.shape, full),
        ],
        out_specs=[
            pl.BlockSpec((tm, D), row),
            pl.BlockSpec((tm, D), row),
            pl.BlockSpec((NK, tm), lambda i: (0, i)),
            pl.BlockSpec((NK, tm), lambda i: (0, i)),
        ],
        out_shape=[
            jax.ShapeDtypeStruct((T, D), F32),
            jax.ShapeDtypeStruct((T, D), F32),
            jax.ShapeDtypeStruct((NK, T), I32),
            jax.ShapeDtypeStruct((NK, T), F32),
        ],
        compiler_params=pltpu.CompilerParams(
            dimension_semantics=("parallel",), vmem_limit_bytes=VMEM_LIMIT),
        name="mix_route",
    )(attn, rec, x2, wo_bf, g.reshape(1, D), wq_bf, k1t_bf, k2t_bf)


ROW_CHUNKS = 8
HI_HALF_MASK = -65536


def _pack_kernel(u_ref, v_ref, o_ref):
    half = u_ref.shape[1] // 2

    def words(a):
        bits = pltpu.bitcast(a.astype(BF16).astype(F32), I32)
        lo = lax.shift_right_logical(bits[:, :half], jnp.full((a.shape[0], half), 16, I32))
        return lo | (bits[:, half:] & HI_HALF_MASK)
    o_ref[:, :half] = words(u_ref[...])
    o_ref[:, half:] = words(v_ref[...])


def _pack_rows(u, v, rows=PACK_ROWS):
    E, D = u.shape
    rows = min(rows, E)
    blk = lambda i: (i, 0)
    return pl.pallas_call(
        _pack_kernel,
        grid=(E // rows,),
        in_specs=[pl.BlockSpec((rows, D), blk), pl.BlockSpec((rows, D), blk)],
        out_specs=pl.BlockSpec((rows, D), blk),
        out_shape=jax.ShapeDtypeStruct((E, D), I32),
        compiler_params=pltpu.CompilerParams(dimension_semantics=("parallel",), vmem_limit_bytes=VMEM_LIMIT),
        name="pack_rows",
    )(u, v)


def _word_lo(w):
    return pltpu.bitcast(w << 16, F32)


def _word_hi(w):
    return pltpu.bitcast(w & HI_HALF_MASK, F32)
TOKEN_UNROLL = 2
DMA_PRIORITIES = 2
PEER_RING = 4
PEER_PREFETCH = 2


def _peer_kernel(idx_ref, idx_next_ref, xn_ref, gate_ref, uv_hbm, o_ref, *scratch, tt, nk):
    bufs = scratch[:PEER_RING]
    lg_ref, act_ref, sems = scratch[PEER_RING:]
    g = pl.program_id(0)
    n = pl.num_programs(0)
    nt = (((1,), (1,)), ((), ()))
    ch = ROW_CHUNKS
    hc = ch // 2

    def row_copy(ids_ref, q, t, k):
        e = ids_ref[0, 0, (q * tt + t) * nk + k]
        return pltpu.make_async_copy(uv_hbm.at[e], bufs[q].at[t, :, pl.ds(k, 1), :], sems.at[q])

    def wait_group(q):
        pltpu.make_async_copy(bufs[(q + 1) % PEER_RING], bufs[q], sems.at[q]).wait()

    @pl.when(g == 0)
    def _():
        for q in range(PEER_PREFETCH):
            def body(t, carry, q=q):
                for k in range(nk):
                    row_copy(idx_ref, q, t, k).start(priority=k % DMA_PRIORITIES)
                return carry
            lax.fori_loop(0, tt, body, 0)

    def group(q):
        buf = bufs[q]
        row0 = q * tt
        ahead = q + PEER_PREFETCH
        ids_ahead = idx_ref if ahead < PEER_RING else idx_next_ref

        def issue(t, k0, k1):
            for k in range(k0, k1):
                row_copy(ids_ahead, ahead % PEER_RING, t, k).start(priority=k % DMA_PRIORITIES)

        def phase_u(tb, carry):
            for j in range(TOKEN_UNROLL):
                t = tb * TOKEN_UNROLL + j
                issue(t, 0, nk // 2)
                xt = xn_ref[row0 + t]
                acc = None
                for s in range(hc):
                    w = buf[t, s]
                    part = _word_lo(w) * xt[s:s + 1, :] + _word_hi(w) * xt[hc + s:hc + s + 1, :]
                    acc = part if acc is None else acc + part
                hi = acc.astype(BF16)
                lo = (acc - hi.astype(F32)).astype(BF16)
                ones = jnp.ones((SUBLANES, LANES), BF16)
                red = (lax.dot_general(ones, hi, nt, preferred_element_type=F32)
                       + lax.dot_general(ones, lo, nt, preferred_element_type=F32))
                lg_ref[t] = red[0:1, :]
            return carry
        lax.fori_loop(0, tt // TOKEN_UNROLL, phase_u, 0)
        act_ref[...] = _gelu_tanh(lg_ref[...]) * gate_ref[row0:row0 + tt]

        def phase_v(tb, carry):
            for j in range(TOKEN_UNROLL):
                t = tb * TOKEN_UNROLL + j
                issue(t, nk // 2, nk)
                act = act_ref[t].astype(BF16)
                words = [buf[t, hc + s] for s in range(hc)]
                cols = ([jnp.dot(act, _word_lo(w).astype(BF16), preferred_element_type=F32) for w in words]
                        + [jnp.dot(act, _word_hi(w).astype(BF16), preferred_element_type=F32) for w in words])
                o_ref[row0 + t] = jnp.concatenate(cols, axis=0)
            return carry
        lax.fori_loop(0, tt // TOKEN_UNROLL, phase_v, 0)

    for q in range(PEER_RING):
        wait_group(q)
        group(q)

    @pl.when(g == n - 1)
    def _():
        for q in range(PEER_PREFETCH):
            wait_group(q)


def _peer(idx, gate, xn3, uv4, n_tokens, tt):
    T, ch, _ = xn3.shape
    nk = idx.shape[1]
    step = PEER_RING * tt
    n = n_tokens // step
    idx3 = idx.reshape(T // step, 1, step * nk)
    tok = lambda i: (i, 0, 0)
    return pl.pallas_call(
        functools.partial(_peer_kernel, tt=tt, nk=nk),
        grid=(n,),
        in_specs=[
            pl.BlockSpec((1, 1, step * nk), tok, memory_space=pltpu.SMEM),
            pl.BlockSpec((1, 1, step * nk), lambda i: (jnp.minimum(i + 1, n - 1), 0, 0), memory_space=pltpu.SMEM),
            pl.BlockSpec((step, ch, LANES), tok),
            pl.BlockSpec((step, 1, nk), tok),
            pl.BlockSpec(memory_space=pl.ANY),
        ],
        out_specs=pl.BlockSpec((step, ch, LANES), tok),
        out_shape=jax.ShapeDtypeStruct((n_tokens, ch, LANES), F32),
        scratch_shapes=(
            [pltpu.VMEM((tt, ch, nk, LANES), I32) for _ in range(PEER_RING)]
            + [pltpu.VMEM((tt, 1, nk), F32), pltpu.VMEM((tt, 1, nk), F32), pltpu.SemaphoreType.DMA((PEER_RING,))]),
        compiler_params=pltpu.CompilerParams(
            dimension_semantics=("arbitrary",), vmem_limit_bytes=VMEM_LIMIT),
        name="peer",
    )(idx3, idx3, xn3, gate.reshape(T, 1, nk), uv4)


SC_CORES = 2
SC_SUBCORES = 16
SC_LANES = 16
SC_WORKERS = SC_CORES * SC_SUBCORES
SC_EXPERT_BLOCK = 16
SC_CHUNK_GROUP = 4
IDX_SLOTS = 4
SC_TOKEN_SHARE_NUM, SC_TOKEN_SHARE_DEN = 33, 64


def _peer_sc(idx, gate, xn, uv, tok0, toks_per_worker):
    T, nk = idx.shape
    D = xn.shape[1]
    L = SC_LANES
    kb_rows = SC_EXPERT_BLOCK
    n_blocks = nk // kb_rows
    n_chunks = D // L
    half = D // 2
    uc = SC_CHUNK_GROUP
    word_lo = lambda w: plsc.bitcast(w << 16, F32)
    word_hi = lambda w: plsc.bitcast(w & HI_HALF_MASK, F32)
    mesh = plsc.VectorSubcoreMesh(core_axis_name="c", subcore_axis_name="s")

    def body(idx_hbm, gate_hbm, xn_hbm, uv_hbm, out_hbm, idx_v, gate_v, x_v, o_v, rows_v, acc_v, sems, tsems):
        wid = lax.axis_index("s") * SC_CORES + lax.axis_index("c")
        base = wid * toks_per_worker
        lane = lax.iota(I32, L)

        def tok_of(ti):
            return tok0 + base + jnp.minimum(ti, toks_per_worker - 1)

        def idx_copy(ti):
            return pltpu.make_async_copy(idx_hbm.at[tok_of(ti)], idx_v.at[pl.ds((ti % IDX_SLOTS) * nk, nk)],
                                         tsems.at[0])

        def gate_copy(ti):
            return pltpu.make_async_copy(gate_hbm.at[tok_of(ti)], gate_v.at[pl.ds((ti % 2) * nk, nk)], tsems.at[1])

        def x_copy(ti):
            return pltpu.make_async_copy(xn_hbm.at[tok_of(ti)], x_v.at[pl.ds((ti % 2) * D, D)], tsems.at[2])

        def gather(ti, kb, buf):
            first = (ti % IDX_SLOTS) * nk + kb * kb_rows
            return pltpu.make_async_copy(uv_hbm.at[idx_v.at[pl.ds(first, kb_rows)]], rows_v.at[buf], sems.at[buf])

        def evaluate(kb, rv, goff, xoff):
            for j in range(kb_rows):
                acc_v[j] = jnp.zeros((L,), F32)

            @plsc.parallel_loop(0, half // L // uc, unroll=2)
            def _(cg):
                x_lo = [x_v[pl.ds(xoff + (cg * uc + cc) * L, L)] for cc in range(uc)]
                x_hi = [x_v[pl.ds(xoff + half + (cg * uc + cc) * L, L)] for cc in range(uc)]
                for j in range(kb_rows):
                    pr = None
                    for cc in range(uc):
                        w = rv[j, pl.ds((cg * uc + cc) * L, L)]
                        part = word_lo(w) * x_lo[cc] + word_hi(w) * x_hi[cc]
                        pr = part if pr is None else pr + part
                    plsc.addupdate(acc_v.at[j], pr)
            logits = jnp.zeros((L,), F32)
            for j in range(kb_rows):
                logits = jnp.where(lane == j, jnp.sum(acc_v[j]), logits)
            z = math.sqrt(2.0 / math.pi) * (logits + 0.044715 * (logits * logits * logits))
            th = 1.0 - 2.0 / (1.0 + jnp.exp(2.0 * z))
            act = 0.5 * logits * (1.0 + th) * gate_v[pl.ds(goff + kb * kb_rows, kb_rows)]
            acts = [jnp.sum(jnp.where(lane == j, act, 0.0)) for j in range(kb_rows)]

            @plsc.parallel_loop(0, half // L, unroll=2)
            def _(c):
                acc_lo = None
                acc_hi = None
                for j in range(kb_rows):
                    w = rv[j, pl.ds(half + c * L, L)]
                    lo = acts[j] * word_lo(w)
                    hi = acts[j] * word_hi(w)
                    acc_lo = lo if acc_lo is None else acc_lo + lo
                    acc_hi = hi if acc_hi is None else acc_hi + hi
                plsc.addupdate(o_v.at[pl.ds(c * L, L)], acc_lo)
                plsc.addupdate(o_v.at[pl.ds(half + c * L, L)], acc_hi)

        def token(ti, carry):
            gate_copy(ti).wait()
            x_copy(ti).wait()
            idx_copy(ti + 1).wait()
            gate_copy(ti + 1).start()
            x_copy(ti + 1).start()
            idx_copy(ti + 2).start()
            goff = (ti % 2) * nk
            xoff = (ti % 2) * D

            @plsc.parallel_loop(0, n_chunks)
            def _(c):
                o_v[pl.ds(c * L, L)] = jnp.zeros((L,), F32)

            def block(kb, carry):
                buf = kb % 2
                last = kb + 1 >= n_blocks
                gather(ti, kb, buf).wait()
                gather(jnp.where(last, ti + 1, ti), jnp.where(last, 0, kb + 1), 1 - buf).start()
                evaluate(kb, rows_v.at[buf], goff, xoff)
                return carry
            lax.fori_loop(0, n_blocks, block, 0)
            pltpu.sync_copy(o_v, out_hbm.at[base + ti])
            return carry

        idx_copy(0).start()
        idx_copy(0).wait()
        idx_copy(1).start()
        gate_copy(0).start()
        x_copy(0).start()
        gather(0, 0, 0).start()
        lax.fori_loop(0, toks_per_worker, token, 0)
        gate_copy(toks_per_worker).wait()
        x_copy(toks_per_worker).wait()
        idx_copy(toks_per_worker + 1).wait()
        gather(toks_per_worker, 0, 0).wait()

    return pl.kernel(
        body, mesh=mesh,
        compiler_params=pltpu.CompilerParams(needs_layout_passes=False),
        out_type=jax.ShapeDtypeStruct((SC_WORKERS * toks_per_worker, D), F32),
        scratch_types=[
            pltpu.VMEM((IDX_SLOTS * nk,), I32),
            pltpu.VMEM((2 * nk,), F32),
            pltpu.VMEM((2 * D,), F32),
            pltpu.VMEM((D,), F32),
            pltpu.VMEM((2, kb_rows, D), I32),
            pltpu.VMEM((kb_rows, L), F32),
            pltpu.SemaphoreType.DMA((2,)),
            pltpu.SemaphoreType.DMA((3,)),
        ],
    )(idx, gate, xn, uv)


def _ple_kernel(h_ref, po_ref, p_ref, g3_ref, wg_ref, wp_ref, gf_ref, o_ref, *, final):
    h = h_ref[...] + po_ref[...]
    xn = _rms(h, g3_ref[...]).astype(BF16)
    gate = _sigmoid(jnp.dot(xn, wg_ref[...], preferred_element_type=F32))
    proj = jnp.dot(p_ref[...].astype(BF16), wp_ref[...], preferred_element_type=F32)
    h = h + gate * proj
    o_ref[...] = _rms(h, gf_ref[...]) if final else h


def _ple_out(h1, peer_out, p2, g3, wg_bf, wp_bf, gf, final, tm):
    T, D = h1.shape
    row = lambda i: (i, 0)
    full = lambda i: (0, 0)
    return pl.pallas_call(
        functools.partial(_ple_kernel, final=final),
        grid=(T // tm,),
        in_specs=[
            pl.BlockSpec((tm, D), row),
            pl.BlockSpec((tm, D), row),
            pl.BlockSpec((tm, p2.shape[1]), row),
            pl.BlockSpec((1, D), full),
            pl.BlockSpec(wg_bf.shape, full),
            pl.BlockSpec(wp_bf.shape, full),
            pl.BlockSpec((1, D), full),
        ],
        out_specs=pl.BlockSpec((tm, D), row),
        out_shape=jax.ShapeDtypeStruct((T, D), F32),
        compiler_params=pltpu.CompilerParams(
            dimension_semantics=("parallel",), vmem_limit_bytes=VMEM_LIMIT),
        name="ple_out",
    )(h1, peer_out, p2, g3.reshape(1, D), wg_bf, wp_bf, gf.reshape(1, D))


def _block_diag(w):
    nb, bw, _ = w.shape
    eye = jnp.eye(nb, dtype=w.dtype)
    return (eye[:, None, :, None] * w[:, :, None, :]).reshape(nb * bw, nb * bw)


def _key_matrix(keys, half):
    z = jnp.zeros_like(keys)
    blk = jnp.concatenate([keys, z] if half == 0 else [z, keys], axis=1)
    return jnp.kron(jnp.eye(PEER_HEADS, dtype=keys.dtype), blk)


def kernel(x, p, positions, norm_mix_g, w_in, lambda_q1, lambda_k1, lambda_q2, lambda_k2, diff_norm_g, conv_w, conv_b, lru_wa, lru_ba, lru_wx, lru_bx, lru_lambda, lru_norm_g, w_out, norm_ffn_g, peer_wq, peer_keys1, peer_keys2, peer_u, peer_v, norm_ple_g, ple_w_gate, ple_w_proj, final_norm_g):
    B, S, D = x.shape
    T = B * S
    depth = w_in.shape[0]
    h = x
    for i in range(depth):
        lambda_init = 0.8 - 0.6 * math.exp(-0.3 * i)
        q, k, v, u, gate = _in_proj(h, positions, norm_mix_g[i], w_in[i].astype(BF16), tm=IN_PROJ_ROWS)
        attn = _diff_attn(q, k, v, lambda_q1[i], lambda_k1[i], lambda_q2[i], lambda_k2[i],
                          diff_norm_g[i], lambda_init, tq=ATTN_Q_ROWS)
        w_all = jnp.concatenate([_block_diag(lru_wa[i, 0]), _block_diag(lru_wx[i, 0]),
                                 _block_diag(lru_wa[i, 1]), _block_diag(lru_wx[i, 1])], axis=1).astype(BF16)
        bias_all = jnp.concatenate([lru_ba[i, 0], lru_bx[i, 0], lru_ba[i, 1], lru_bx[i, 1]]).reshape(1, -1)
        rec = _bi_rglru(u, gate, conv_w[i], conv_b[i], w_all, bias_all, lru_lambda[i], lru_norm_g[i], tc=LRU_CHUNK_ROWS)
        h1, xn2, idx_t, gate_t = _mix_route(
            attn.reshape(T, -1), rec.reshape(T, -1), h.reshape(T, D), w_out[i].astype(BF16), norm_ffn_g[i],
            peer_wq[i].astype(BF16), _key_matrix(peer_keys1[i], 0).astype(BF16),
            _key_matrix(peer_keys2[i], 1).astype(BF16), tm=ROUTE_TOKENS)
        n_exp = peer_u.shape[1]
        uv2 = _pack_rows(peer_u[i], peer_v[i])
        uv4 = uv2.reshape(n_exp, ROW_CHUNKS, 1, LANES)
        idx, gate_tk = idx_t.T, gate_t.T
        t_sc = T * SC_TOKEN_SHARE_NUM // SC_TOKEN_SHARE_DEN // (2 * SC_WORKERS) * (2 * SC_WORKERS)
        t_tc = T - t_sc
        peer_sc = _peer_sc(idx, gate_tk, xn2, uv2, t_tc, t_sc // SC_WORKERS)
        peer_tc = _peer(idx, gate_tk, xn2.reshape(T, ROW_CHUNKS, LANES), uv4, t_tc, tt=PEER_GROUP_TOKENS)
        peer_out = jnp.concatenate([peer_tc.reshape(t_tc, D), peer_sc], axis=0)
        h = _ple_out(h1, peer_out, p[i].reshape(T, -1), norm_ple_g[i], ple_w_gate[i].astype(BF16),
                     ple_w_proj[i].astype(BF16), final_norm_g, final=(i == depth - 1), tm=PLE_ROWS)
        h = h.reshape(B, S, D)
    return h
```

```python
import functools
import math

import jax
import jax.numpy as jnp
from jax import lax
from jax.experimental import pallas as pl
from jax.experimental.pallas import tpu as pltpu
from jax.experimental.pallas import tpu_sc as plsc

F32 = jnp.float32
BF16 = jnp.bfloat16
I32 = jnp.int32

EPS = 1e-6
DIFF_HEAD_DIM = 64
DIFF_V_DIM = 128
N_DIFF_HEADS = 4
ROPE_DIM = 16
ROPE_THETA = 500000.0
LRU_WIDTH = 512
LRU_C = 8.0
N_KEYS = 128
PEER_HEADS = 8
PEER_TOPK = 16
HALF_KEY = 64
LOG2_E = math.log2(math.e)
LANES = 128
SUBLANES = 8
VMEM_LIMIT = 56 * 1024 * 1024
IN_PROJ_ROWS = 512
ATTN_Q_ROWS = 512
LRU_CHUNK_ROWS = 256
ROUTE_TOKENS = 256
PEER_GROUP_TOKENS = 8
PLE_ROWS = 512
PACK_ROWS = 512


def _rms(x, g):
    return x * lax.rsqrt(jnp.mean(x * x, axis=-1, keepdims=True) + EPS) * g


def _gelu_tanh(x):
    return 0.5 * x * (1.0 + jnp.tanh(math.sqrt(2.0 / math.pi) * (x + 0.044715 * (x * x * x))))


def _sigmoid(x):
    return 1.0 / (1.0 + jnp.exp(-x))


def _inproj_kernel(x_ref, pos_ref, g_ref, w_ref, q_ref, k_ref, v_ref, u_ref, gate_ref):
    x = x_ref[0]
    xn = _rms(x, g_ref[...]).astype(BF16)
    pos = pos_ref[0].astype(F32)
    lane = lax.broadcasted_iota(I32, (1, LANES), 1)
    p = lane & (DIFF_HEAD_DIM - 1)
    freq = (p & (ROPE_DIM // 2 - 1)).astype(F32)
    inv_freq = jnp.exp(freq * (-2.0 / ROPE_DIM * math.log(ROPE_THETA)))
    ang = pos * inv_freq
    cs = jnp.cos(ang)
    sn = jnp.sin(ang)
    half = ROPE_DIM // 2
    c_mul = jnp.where(p < ROPE_DIM, cs, 1.0)
    s_up = jnp.where(p < half, -sn, 0.0)
    s_dn = jnp.where((p >= half) & (p < ROPE_DIM), sn, 0.0)

    def rope(t):
        return t * c_mul + pltpu.roll(t, LANES - half, 1) * s_up + pltpu.roll(t, half, 1) * s_dn

    nq = q_ref.shape[-1]
    pq = jnp.dot(xn, w_ref[:, 0:nq], preferred_element_type=F32)
    pk = jnp.dot(xn, w_ref[:, nq:2 * nq], preferred_element_type=F32)
    scale = DIFF_HEAD_DIM ** -0.5 * LOG2_E
    for j in range(nq // LANES):
        sl = slice(j * LANES, (j + 1) * LANES)
        q_ref[0, :, sl] = (rope(pq[:, sl]) * scale).astype(BF16)
        k_ref[0, :, sl] = rope(pk[:, sl]).astype(BF16)
    v_ref[0] = jnp.dot(xn, w_ref[:, 2 * nq:3 * nq], preferred_element_type=F32).astype(BF16)
    u_ref[0] = jnp.dot(xn, w_ref[:, 3 * nq:3 * nq + LRU_WIDTH], preferred_element_type=F32)
    gate_ref[0] = jnp.dot(xn, w_ref[:, 3 * nq + LRU_WIDTH:3 * nq + 2 * LRU_WIDTH],
                          preferred_element_type=F32).astype(BF16)


def _in_proj(x, positions, g, w_in_bf, tm):
    B, S, D = x.shape
    nq = N_DIFF_HEADS * 2 * DIFF_HEAD_DIM
    ncols = w_in_bf.shape[1]
    row = lambda b, i: (b, i, 0)
    return pl.pallas_call(
        _inproj_kernel,
        grid=(B, S // tm),
        in_specs=[
            pl.BlockSpec((1, tm, D), row),
            pl.BlockSpec((1, tm, 1), row),
            pl.BlockSpec((1, D), lambda b, i: (0, 0)),
            pl.BlockSpec((D, ncols), lambda b, i: (0, 0)),
        ],
        out_specs=[
            pl.BlockSpec((1, tm, nq), row),
            pl.BlockSpec((1, tm, nq), row),
            pl.BlockSpec((1, tm, nq), row),
            pl.BlockSpec((1, tm, LRU_WIDTH), row),
            pl.BlockSpec((1, tm, LRU_WIDTH), row),
        ],
        out_shape=[
            jax.ShapeDtypeStruct((B, S, nq), BF16),
            jax.ShapeDtypeStruct((B, S, nq), BF16),
            jax.ShapeDtypeStruct((B, S, nq), BF16),
            jax.ShapeDtypeStruct((B, S, LRU_WIDTH), F32),
            jax.ShapeDtypeStruct((B, S, LRU_WIDTH), BF16),
        ],
        compiler_params=pltpu.CompilerParams(
            dimension_semantics=("parallel", "parallel"), vmem_limit_bytes=VMEM_LIMIT),
        name="in_proj",
    )(x, positions.reshape(B, S, 1), g.reshape(1, D), w_in_bf)


def _attn_kernel(lq1_ref, lk1_ref, lq2_ref, lk2_ref, g_ref, q_ref, k_ref, v_ref, o_ref, *, lambda_init):
    lam = (jnp.exp(jnp.sum(lq1_ref[...] * lk1_ref[...], axis=-1, keepdims=True))
           - jnp.exp(jnp.sum(lq2_ref[...] * lk2_ref[...], axis=-1, keepdims=True))
           + lambda_init)
    q = q_ref[0]
    k = k_ref[0]
    v = v_ref[0]
    lane = lax.broadcasted_iota(I32, q.shape, 1)
    zero = jnp.zeros_like(q)
    q0 = jnp.where(lane < DIFF_HEAD_DIM, q, zero)
    q1 = jnp.where(lane >= DIFF_HEAD_DIM, q, zero)
    nt = (((1,), (1,)), ((), ()))
    s0 = lax.dot_general(q0, k, nt, preferred_element_type=F32)
    s1 = lax.dot_general(q1, k, nt, preferred_element_type=F32)
    v_ext = jnp.concatenate([v, jnp.ones_like(v)], axis=1)
    p0 = jnp.exp2(s0 - jnp.max(s0, axis=-1, keepdims=True)).astype(BF16)
    p1 = jnp.exp2(s1 - jnp.max(s1, axis=-1, keepdims=True)).astype(BF16)
    e0 = jnp.dot(p0, v_ext, preferred_element_type=F32)
    e1 = jnp.dot(p1, v_ext, preferred_element_type=F32)
    dv = v.shape[1]
    o = e0[:, :dv] / e0[:, dv:dv + 1] - e1[:, :dv] * (lam / e1[:, dv:dv + 1])
    o_ref[0] = (_rms(o, g_ref[...]) * (1.0 - lambda_init)).astype(BF16)


def _diff_attn(q, k, v, lq1, lk1, lq2, lk2, g, lambda_init, tq):
    B, S, W = q.shape
    H = W // DIFF_V_DIM
    vec = lambda b, h, i: (0, 0)
    return pl.pallas_call(
        functools.partial(_attn_kernel, lambda_init=lambda_init),
        grid=(B, H, S // tq),
        in_specs=[
            pl.BlockSpec((1, DIFF_HEAD_DIM), vec),
            pl.BlockSpec((1, DIFF_HEAD_DIM), vec),
            pl.BlockSpec((1, DIFF_HEAD_DIM), vec),
            pl.BlockSpec((1, DIFF_HEAD_DIM), vec),
            pl.BlockSpec((1, DIFF_V_DIM), vec),
            pl.BlockSpec((1, tq, DIFF_V_DIM), lambda b, h, i: (b, i, h)),
            pl.BlockSpec((1, S, DIFF_V_DIM), lambda b, h, i: (b, 0, h)),
            pl.BlockSpec((1, S, DIFF_V_DIM), lambda b, h, i: (b, 0, h)),
        ],
        out_specs=pl.BlockSpec((1, tq, DIFF_V_DIM), lambda b, h, i: (b, i, h)),
        out_shape=jax.ShapeDtypeStruct((B, S, W), BF16),
        compiler_params=pltpu.CompilerParams(
            dimension_semantics=("parallel", "parallel", "parallel"), vmem_limit_bytes=VMEM_LIMIT),
        name="diff_attn",
    )(lq1.reshape(1, -1), lk1.reshape(1, -1), lq2.reshape(1, -1), lk2.reshape(1, -1),
      g.reshape(1, -1), q, k, v)


def _lru_kernel(u_ref, gate_ref, cw_ref, cb_ref, w_ref, bias_ref, lam_ref, g_ref, out_ref, hf_ref, *, tc):
    S = u_ref.shape[1]
    C = u_ref.shape[2]
    nc = S // tc
    halo = SUBLANES
    neg_lam = -lam_ref[...]
    sp = jnp.maximum(neg_lam, 0.0) + jnp.log(1.0 + jnp.exp(-jnp.abs(neg_lam)))
    row = lax.broadcasted_iota(I32, (tc, 1), 0)
    conv_taps = cw_ref.shape[0]
    conv_left = 2

    def gates(c, d):
        r0 = pl.multiple_of(c * tc, tc)
        x = u_ref[0, pl.ds(r0, tc), :]
        prev = u_ref[0, pl.ds(pl.multiple_of(jnp.maximum(r0 - halo, 0), halo), halo), :]
        nxt = u_ref[0, pl.ds(pl.multiple_of(jnp.minimum(r0 + tc, S - halo), halo), halo), :]
        prev = jnp.where(c > 0, prev, 0.0)
        nxt = jnp.where(c < nc - 1, nxt, 0.0)
        win = jnp.concatenate([prev, x, nxt], axis=0)
        uc = cb_ref[...]
        for j in range(conv_taps):
            o = halo - conv_left + j
            uc = uc + cw_ref[j:j + 1, :] * win[o:o + tc, :]
        pre = (jnp.dot(uc.astype(BF16), w_ref[:, d * 2 * C:(d + 1) * 2 * C], preferred_element_type=F32)
               + bias_ref[:, d * 2 * C:(d + 1) * 2 * C])
        r = _sigmoid(pre[:, :C])
        i = _sigmoid(pre[:, C:])
        log_a = -LRU_C * r * sp[d:d + 1, :]
        a = jnp.exp(log_a)
        th = jnp.tanh(log_a)
        mult = jnp.sqrt(-2.0 * th / (1.0 - th))
        return a, mult * (i * uc)

    def scan(a, b, reverse):
        d = 1
        while d < tc:
            if reverse:
                a_s = pltpu.roll(a, tc - d, 0)
                b_s = pltpu.roll(b, tc - d, 0)
                m = row < tc - d
            else:
                a_s = pltpu.roll(a, d, 0)
                b_s = pltpu.roll(b, d, 0)
                m = row >= d
            a_s = jnp.where(m, a_s, 1.0)
            b_s = jnp.where(m, b_s, 0.0)
            b = a * b_s + b
            a = a * a_s
            d *= 2
        return a, b

    def fwd_body(c, h0):
        a, b = gates(c, 0)
        a, b = scan(a, b, False)
        h = a * h0 + b
        hf_ref[pl.ds(pl.multiple_of(c * tc, tc), tc), :] = h
        return h[tc - 1:tc, :]

    lax.fori_loop(0, nc, fwd_body, jnp.zeros((1, C), F32))

    def bwd_body(j, h0):
        c = nc - 1 - j
        r0 = pl.multiple_of(c * tc, tc)
        a, b = gates(c, 1)
        a, b = scan(a, b, True)
        h = a * h0 + b
        y = (hf_ref[pl.ds(r0, tc), :] + h) * _gelu_tanh(gate_ref[0, pl.ds(r0, tc), :].astype(F32))
        out_ref[0, pl.ds(r0, tc), :] = _rms(y, g_ref[...]).astype(BF16)
        return h[0:1, :]

    lax.fori_loop(0, nc, bwd_body, jnp.zeros((1, C), F32))


def _bi_rglru(u, gate, conv_w, conv_b, w_all_bf, bias_all, lru_lambda, g, tc):
    B, S, C = u.shape
    full = lambda b: (0, 0)
    return pl.pallas_call(
        functools.partial(_lru_kernel, tc=tc),
        grid=(B,),
        in_specs=[
            pl.BlockSpec((1, S, C), lambda b: (b, 0, 0)),
            pl.BlockSpec((1, S, C), lambda b: (b, 0, 0)),
            pl.BlockSpec(conv_w.shape, full),
            pl.BlockSpec((1, C), full),
            pl.BlockSpec(w_all_bf.shape, full),
            pl.BlockSpec(bias_all.shape, full),
            pl.BlockSpec(lru_lambda.shape, full),
            pl.BlockSpec((1, C), full),
        ],
        out_specs=pl.BlockSpec((1, S, C), lambda b: (b, 0, 0)),
        out_shape=jax.ShapeDtypeStruct((B, S, C), BF16),
        scratch_shapes=[pltpu.VMEM((S, C), F32)],
        compiler_params=pltpu.CompilerParams(
            dimension_semantics=("parallel",), vmem_limit_bytes=VMEM_LIMIT),
        name="bi_rglru",
    )(u, gate, conv_w, conv_b.reshape(1, C), w_all_bf, bias_all, lru_lambda, g.reshape(1, C))


def _topk_rows(s, k):
    n = s.shape[0]
    rid = lax.broadcasted_iota(I32, s.shape, 0)
    vals, ids = [], []
    for _ in range(k):
        m = jnp.max(s, axis=0, keepdims=True)
        sel = jnp.min(jnp.where(s == m, rid, n), axis=0, keepdims=True)
        vals.append(m)
        ids.append(sel)
        s = jnp.where(rid == sel, -jnp.inf, s)
    return jnp.concatenate(vals, axis=0), jnp.concatenate(ids, axis=0)


CAND_ROW_PIECES = 4


def _candidate_pieces(k):
    up = lambda n: -(-n // SUBLANES) * SUBLANES
    pieces = [("row", i, 0, up(k // (i + 1))) for i in range(CAND_ROW_PIECES)]
    for j in range(k // (CAND_ROW_PIECES + 1)):
        for i0 in range(0, k // (j + 1), SUBLANES):
            pieces.append(("col", j, i0, SUBLANES))
    return pieces


def _route_kernel(attn_ref, rec_ref, x_ref, wo_ref, g_ref, wq_ref, k1_ref, k2_ref,
                  h_ref, xn_ref, idx_ref, gate_ref):
    aw = attn_ref.shape[1]
    h = (x_ref[...]
         + jnp.dot(attn_ref[...], wo_ref[0:aw, :], preferred_element_type=F32)
         + jnp.dot(rec_ref[...], wo_ref[aw:, :], preferred_element_type=F32))
    h_ref[...] = h
    xn = _rms(h, g_ref[...])
    xn_ref[...] = xn
    q = jnp.dot(xn.astype(BF16), wq_ref[...], preferred_element_type=F32).astype(BF16)
    nt = (((1,), (1,)), ((), ()))
    s1 = lax.dot_general(k1_ref[...], q, nt, preferred_element_type=F32)
    s2 = lax.dot_general(k2_ref[...], q, nt, preferred_element_type=F32)
    k = PEER_TOPK
    tm = q.shape[0]
    pieces = _candidate_pieces(k)
    pos_parts, ok_parts = [], []
    for kind, fixed, start, length in pieces:
        r = lax.broadcasted_iota(I32, (length, tm), 0) + start
        i, j = (fixed, r) if kind == "row" else (r, fixed)
        pos_parts.append(i * k + j)
        ok = (i + 1) * (j + 1) <= k
        ok_parts.append(ok if kind == "row" else ok & (r >= CAND_ROW_PIECES))
    pos = jnp.concatenate(pos_parts, axis=0)
    ok = jnp.concatenate(ok_parts, axis=0)
    idx_rows, gate_rows = [], []
    for hd in range(PEER_HEADS):
        v1, i1 = _topk_rows(s1[hd * N_KEYS:(hd + 1) * N_KEYS, :], k)
        v2, i2 = _topk_rows(s2[hd * N_KEYS:(hd + 1) * N_KEYS, :], k)
        cand_parts, cidx_parts = [], []
        for kind, fixed, start, length in pieces:
            if kind == "row":
                cand_parts.append(v1[fixed:fixed + 1, :] + v2[start:start + length, :])
                cidx_parts.append(i1[fixed:fixed + 1, :] * N_KEYS + i2[start:start + length, :])
            else:
                cand_parts.append(v1[start:start + length, :] + v2[fixed:fixed + 1, :])
                cidx_parts.append(i1[start:start + length, :] * N_KEYS + i2[fixed:fixed + 1, :])
        cand = jnp.where(ok, jnp.concatenate(cand_parts, axis=0), -jnp.inf)
        cidx = jnp.concatenate(cidx_parts, axis=0)
        sc, ids = [], []
        for _ in range(k):
            m = jnp.max(cand, axis=0, keepdims=True)
            sel = jnp.min(jnp.where(cand == m, pos, k * k), axis=0, keepdims=True)
            hit = pos == sel
            sc.append(m)
            ids.append(jnp.max(jnp.where(hit, cidx, -1), axis=0, keepdims=True))
            cand = jnp.where(hit, -jnp.inf, cand)
        sc = jnp.concatenate(sc, axis=0)
        e = jnp.exp(sc - sc[0:1, :])
        gate_rows.append(e / jnp.sum(e, axis=0, keepdims=True))
        idx_rows.append(jnp.concatenate(ids, axis=0))
    idx_ref[...] = jnp.concatenate(idx_rows, axis=0)
    gate_ref[...] = jnp.concatenate(gate_rows, axis=0)


def _mix_route(attn, rec, x2, wo_bf, g, wq_bf, k1t_bf, k2t_bf, tm):
    T, D = x2.shape
    NK = PEER_HEADS * PEER_TOPK
    row = lambda i: (i, 0)
    full = lambda i: (0, 0)
    return pl.pallas_call(
        _route_kernel,
        grid=(T // tm,),
        in_specs=[
            pl.BlockSpec((tm, attn.shape[1]), row),
            pl.BlockSpec((tm, rec.shape[1]), row),
            pl.BlockSpec((tm, D), row),
            pl.BlockSpec(wo_bf.shape, full),
            pl.BlockSpec((1, D), full),
            pl.BlockSpec(wq_bf.shape, full),
            pl.BlockSpec(k1t_bf.shape, full),
            pl.BlockSpec(k2t_bf.shape, full),
        ],
        out_specs=[
            pl.BlockSpec((tm, D), row),
            pl.BlockSpec((tm, D), row),
            pl.BlockSpec((NK, tm), lambda i: (0, i)),
            pl.BlockSpec((NK, tm), lambda i: (0, i)),
        ],
        out_shape=[
            jax.ShapeDtypeStruct((T, D), F32),
            jax.ShapeDtypeStruct((T, D), F32),
            jax.ShapeDtypeStruct((NK, T), I32),
            jax.ShapeDtypeStruct((NK, T), F32),
        ],
        compiler_params=pltpu.CompilerParams(
            dimension_semantics=("parallel",), vmem_limit_bytes=VMEM_LIMIT),
        name="mix_route",
    )(attn, rec, x2, wo_bf, g.reshape(1, D), wq_bf, k1t_bf, k2t_bf)


ROW_CHUNKS = 8
HI_HALF_MASK = -65536


def _pack_kernel(u_ref, v_ref, o_ref):
    half = u_ref.shape[1] // 2

    def words(a):
        bits = pltpu.bitcast(a.astype(BF16).astype(F32), I32)
        lo = lax.shift_right_logical(bits[:, :half], jnp.full((a.shape[0], half), 16, I32))
        return lo | (bits[:, half:] & HI_HALF_MASK)
    o_ref[:, :half] = words(u_ref[...])
    o_ref[:, half:] = words(v_ref[...])


def _pack_rows(u, v, rows=PACK_ROWS):
    E, D = u.shape
    rows = min(rows, E)
    blk = lambda i: (i, 0)
    return pl.pallas_call(
        _pack_kernel,
        grid=(E // rows,),
        in_specs=[pl.BlockSpec((rows, D), blk), pl.BlockSpec((rows, D), blk)],
        out_specs=pl.BlockSpec((rows, D), blk),
        out_shape=jax.ShapeDtypeStruct((E, D), I32),
        compiler_params=pltpu.CompilerParams(dimension_semantics=("parallel",), vmem_limit_bytes=VMEM_LIMIT),
        name="pack_rows",
    )(u, v)


def _word_lo(w):
    return pltpu.bitcast(w << 16, F32)


def _word_hi(w):
    return pltpu.bitcast(w & HI_HALF_MASK, F32)
TOKEN_UNROLL = 2
DMA_PRIORITIES = 2
PEER_RING = 4
PEER_PREFETCH = 2


def _peer_kernel(idx_ref, idx_next_ref, xn_ref, gate_ref, uv_hbm, o_ref, *scratch, tt, nk):
    bufs = scratch[:PEER_RING]
    lg_ref, act_ref, sems = scratch[PEER_RING:]
    g = pl.program_id(0)
    n = pl.num_programs(0)
    nt = (((1,), (1,)), ((), ()))
    ch = ROW_CHUNKS
    hc = ch // 2

    def row_copy(ids_ref, q, t, k):
        e = ids_ref[0, 0, (q * tt + t) * nk + k]
        return pltpu.make_async_copy(uv_hbm.at[e], bufs[q].at[t, :, pl.ds(k, 1), :], sems.at[q])

    def wait_group(q):
        pltpu.make_async_copy(bufs[(q + 1) % PEER_RING], bufs[q], sems.at[q]).wait()

    @pl.when(g == 0)
    def _():
        for q in range(PEER_PREFETCH):
            def body(t, carry, q=q):
                for k in range(nk):
                    row_copy(idx_ref, q, t, k).start(priority=k % DMA_PRIORITIES)
                return carry
            lax.fori_loop(0, tt, body, 0)

    def group(q):
        buf = bufs[q]
        row0 = q * tt
        ahead = q + PEER_PREFETCH
        ids_ahead = idx_ref if ahead < PEER_RING else idx_next_ref

        def issue(t, k0, k1):
            for k in range(k0, k1):
                row_copy(ids_ahead, ahead % PEER_RING, t, k).start(priority=k % DMA_PRIORITIES)

        def phase_u(tb, carry):
            for j in range(TOKEN_UNROLL):
                t = tb * TOKEN_UNROLL + j
                issue(t, 0, nk // 2)
                xt = xn_ref[row0 + t]
                acc = None
                for s in range(hc):
                    w = buf[t, s]
                    part = _word_lo(w) * xt[s:s + 1, :] + _word_hi(w) * xt[hc + s:hc + s + 1, :]
                    acc = part if acc is None else acc + part
                hi = acc.astype(BF16)
                lo = (acc - hi.astype(F32)).astype(BF16)
                ones = jnp.ones((SUBLANES, LANES), BF16)
                red = (lax.dot_general(ones, hi, nt, preferred_element_type=F32)
                       + lax.dot_general(ones, lo, nt, preferred_element_type=F32))
                lg_ref[t] = red[0:1, :]
            return carry
        lax.fori_loop(0, tt // TOKEN_UNROLL, phase_u, 0)
        act_ref[...] = _gelu_tanh(lg_ref[...]) * gate_ref[row0:row0 + tt]

        def phase_v(tb, carry):
            for j in range(TOKEN_UNROLL):
                t = tb * TOKEN_UNROLL + j
                issue(t, nk // 2, nk)
                act = act_ref[t].astype(BF16)
                words = [buf[t, hc + s] for s in range(hc)]
                cols = ([jnp.dot(act, _word_lo(w).astype(BF16), preferred_element_type=F32) for w in words]
                        + [jnp.dot(act, _word_hi(w).astype(BF16), preferred_element_type=F32) for w in words])
                o_ref[row0 + t] = jnp.concatenate(cols, axis=0)
            return carry
        lax.fori_loop(0, tt // TOKEN_UNROLL, phase_v, 0)

    for q in range(PEER_RING):
        wait_group(q)
        group(q)

    @pl.when(g == n - 1)
    def _():
        for q in range(PEER_PREFETCH):
            wait_group(q)


def _peer(idx, gate, xn3, uv4, n_tokens, tt):
    T, ch, _ = xn3.shape
    nk = idx.shape[1]
    step = PEER_RING * tt
    n = n_tokens // step
    idx3 = idx.reshape(T // step, 1, step * nk)
    tok = lambda i: (i, 0, 0)
    return pl.pallas_call(
        functools.partial(_peer_kernel, tt=tt, nk=nk),
        grid=(n,),
        in_specs=[
            pl.BlockSpec((1, 1, step * nk), tok, memory_space=pltpu.SMEM),
            pl.BlockSpec((1, 1, step * nk), lambda i: (jnp.minimum(i + 1, n - 1), 0, 0), memory_space=pltpu.SMEM),
            pl.BlockSpec((step, ch, LANES), tok),
            pl.BlockSpec((step, 1, nk), tok),
            pl.BlockSpec(memory_space=pl.ANY),
        ],
        out_specs=pl.BlockSpec((step, ch, LANES), tok),
        out_shape=jax.ShapeDtypeStruct((n_tokens, ch, LANES), F32),
        scratch_shapes=(
            [pltpu.VMEM((tt, ch, nk, LANES), I32) for _ in range(PEER_RING)]
            + [pltpu.VMEM((tt, 1, nk), F32), pltpu.VMEM((tt, 1, nk), F32), pltpu.SemaphoreType.DMA((PEER_RING,))]),
        compiler_params=pltpu.CompilerParams(
            dimension_semantics=("arbitrary",), vmem_limit_bytes=VMEM_LIMIT),
        name="peer",
    )(idx3, idx3, xn3, gate.reshape(T, 1, nk), uv4)


SC_CORES = 2
SC_SUBCORES = 16
SC_LANES = 16
SC_WORKERS = SC_CORES * SC_SUBCORES
SC_EXPERT_BLOCK = 16
SC_CHUNK_GROUP = 4
IDX_SLOTS = 4
SC_TOKEN_SHARE_NUM, SC_TOKEN_SHARE_DEN = 67, 128


def _peer_sc(idx, gate, xn, uv, tok0, toks_per_worker):
    T, nk = idx.shape
    D = xn.shape[1]
    L = SC_LANES
    kb_rows = SC_EXPERT_BLOCK
    n_blocks = nk // kb_rows
    n_chunks = D // L
    half = D // 2
    uc = SC_CHUNK_GROUP
    word_lo = lambda w: plsc.bitcast(w << 16, F32)
    word_hi = lambda w: plsc.bitcast(w & HI_HALF_MASK, F32)
    mesh = plsc.VectorSubcoreMesh(core_axis_name="c", subcore_axis_name="s")

    def body(idx_hbm, gate_hbm, xn_hbm, uv_hbm, out_hbm, idx_v, gate_v, x_v, o_v, rows_v, acc_v, sems, tsems):
        wid = lax.axis_index("s") * SC_CORES + lax.axis_index("c")
        base = wid * toks_per_worker
        lane = lax.iota(I32, L)

        def tok_of(ti):
            return tok0 + base + jnp.minimum(ti, toks_per_worker - 1)

        def idx_copy(ti):
            return pltpu.make_async_copy(idx_hbm.at[tok_of(ti)], idx_v.at[pl.ds((ti % IDX_SLOTS) * nk, nk)],
                                         tsems.at[0])

        def gate_copy(ti):
            return pltpu.make_async_copy(gate_hbm.at[tok_of(ti)], gate_v.at[pl.ds((ti % 2) * nk, nk)], tsems.at[1])

        def x_copy(ti):
            return pltpu.make_async_copy(xn_hbm.at[tok_of(ti)], x_v.at[pl.ds((ti % 2) * D, D)], tsems.at[2])

        def gather(ti, kb, buf):
            first = (ti % IDX_SLOTS) * nk + kb * kb_rows
            return pltpu.make_async_copy(uv_hbm.at[idx_v.at[pl.ds(first, kb_rows)]], rows_v.at[buf], sems.at[buf])

        def evaluate(kb, rv, goff, xoff):
            for j in range(kb_rows):
                acc_v[j] = jnp.zeros((L,), F32)

            @plsc.parallel_loop(0, half // L // uc, unroll=2)
            def _(cg):
                x_lo = [x_v[pl.ds(xoff + (cg * uc + cc) * L, L)] for cc in range(uc)]
                x_hi = [x_v[pl.ds(xoff + half + (cg * uc + cc) * L, L)] for cc in range(uc)]
                for j in range(kb_rows):
                    pr = None
                    for cc in range(uc):
                        w = rv[j, pl.ds((cg * uc + cc) * L, L)]
                        part = word_lo(w) * x_lo[cc] + word_hi(w) * x_hi[cc]
                        pr = part if pr is None else pr + part
                    plsc.addupdate(acc_v.at[j], pr)
            logits = jnp.zeros((L,), F32)
            for j in range(kb_rows):
                logits = jnp.where(lane == j, jnp.sum(acc_v[j]), logits)
            z = math.sqrt(2.0 / math.pi) * (logits + 0.044715 * (logits * logits * logits))
            th = 1.0 - 2.0 / (1.0 + jnp.exp(2.0 * z))
            act = 0.5 * logits * (1.0 + th) * gate_v[pl.ds(goff + kb * kb_rows, kb_rows)]
            acts = [jnp.sum(jnp.where(lane == j, act, 0.0)) for j in range(kb_rows)]

            @plsc.parallel_loop(0, half // L, unroll=2)
            def _(c):
                acc_lo = None
                acc_hi = None
                for j in range(kb_rows):
                    w = rv[j, pl.ds(half + c * L, L)]
                    lo = acts[j] * word_lo(w)
                    hi = acts[j] * word_hi(w)
                    acc_lo = lo if acc_lo is None else acc_lo + lo
                    acc_hi = hi if acc_hi is None else acc_hi + hi
                plsc.addupdate(o_v.at[pl.ds(c * L, L)], acc_lo)
                plsc.addupdate(o_v.at[pl.ds(half + c * L, L)], acc_hi)

        def token(ti, carry):
            gate_copy(ti).wait()
            x_copy(ti).wait()
            idx_copy(ti + 1).wait()
            gate_copy(ti + 1).start()
            x_copy(ti + 1).start()
            idx_copy(ti + 2).start()
            goff = (ti % 2) * nk
            xoff = (ti % 2) * D

            @plsc.parallel_loop(0, n_chunks)
            def _(c):
                o_v[pl.ds(c * L, L)] = jnp.zeros((L,), F32)

            def block(kb, carry):
                buf = kb % 2
                last = kb + 1 >= n_blocks
                gather(ti, kb, buf).wait()
                gather(jnp.where(last, ti + 1, ti), jnp.where(last, 0, kb + 1), 1 - buf).start()
                evaluate(kb, rows_v.at[buf], goff, xoff)
                return carry
            lax.fori_loop(0, n_blocks, block, 0)
            pltpu.sync_copy(o_v, out_hbm.at[base + ti])
            return carry

        idx_copy(0).start()
        idx_copy(0).wait()
        idx_copy(1).start()
        gate_copy(0).start()
        x_copy(0).start()
        gather(0, 0, 0).start()
        lax.fori_loop(0, toks_per_worker, token, 0)
        gate_copy(toks_per_worker).wait()
        x_copy(toks_per_worker).wait()
        idx_copy(toks_per_worker + 1).wait()
        gather(toks_per_worker, 0, 0).wait()

    return pl.kernel(
        body, mesh=mesh,
        compiler_params=pltpu.CompilerParams(needs_layout_passes=False),
        out_type=jax.ShapeDtypeStruct((SC_WORKERS * toks_per_worker, D), F32),
        scratch_types=[
            pltpu.VMEM((IDX_SLOTS * nk,), I32),
            pltpu.VMEM((2 * nk,), F32),
            pltpu.VMEM((2 * D,), F32),
            pltpu.VMEM((D,), F32),
            pltpu.VMEM((2, kb_rows, D), I32),
            pltpu.VMEM((kb_rows, L), F32),
            pltpu.SemaphoreType.DMA((2,)),
            pltpu.SemaphoreType.DMA((3,)),
        ],
    )(idx, gate, xn, uv)


def _ple_kernel(h_ref, po_ref, p_ref, g3_ref, wg_ref, wp_ref, gf_ref, o_ref, *, final):
    h = h_ref[...] + po_ref[...]
    xn = _rms(h, g3_ref[...]).astype(BF16)
    gate = _sigmoid(jnp.dot(xn, wg_ref[...], preferred_element_type=F32))
    proj = jnp.dot(p_ref[...].astype(BF16), wp_ref[...], preferred_element_type=F32)
    h = h + gate * proj
    o_ref[...] = _rms(h, gf_ref[...]) if final else h


def _ple_out(h1, peer_out, p2, g3, wg_bf, wp_bf, gf, final, tm):
    T, D = h1.shape
    row = lambda i: (i, 0)
    full = lambda i: (0, 0)
    return pl.pallas_call(
        functools.partial(_ple_kernel, final=final),
        grid=(T // tm,),
        in_specs=[
            pl.BlockSpec((tm, D), row),
            pl.BlockSpec((tm, D), row),
            pl.BlockSpec((tm, p2.shape[1]), row),
            pl.BlockSpec((1, D), full),
            pl.BlockSpec(wg_bf.shape, full),
            pl.BlockSpec(wp_bf.shape, full),
            pl.BlockSpec((1, D), full),
        ],
        out_specs=pl.BlockSpec((tm, D), row),
        out_shape=jax.ShapeDtypeStruct((T, D), F32),
        compiler_params=pltpu.CompilerParams(
            dimension_semantics=("parallel",), vmem_limit_bytes=VMEM_LIMIT),
        name="ple_out",
    )(h1, peer_out, p2, g3.reshape(1, D), wg_bf, wp_bf, gf.reshape(1, D))


def _block_diag(w):
    nb, bw, _ = w.shape
    eye = jnp.eye(nb, dtype=w.dtype)
    return (eye[:, None, :, None] * w[:, :, None, :]).reshape(nb * bw, nb * bw)


def _key_matrix(keys, half):
    z = jnp.zeros_like(keys)
    blk = jnp.concatenate([keys, z] if half == 0 else [z, keys], axis=1)
    return jnp.kron(jnp.eye(PEER_HEADS, dtype=keys.dtype), blk)


def kernel(x, p, positions, norm_mix_g, w_in, lambda_q1, lambda_k1, lambda_q2, lambda_k2, diff_norm_g, conv_w, conv_b, lru_wa, lru_ba, lru_wx, lru_bx, lru_lambda, lru_norm_g, w_out, norm_ffn_g, peer_wq, peer_keys1, peer_keys2, peer_u, peer_v, norm_ple_g, ple_w_gate, ple_w_proj, final_norm_g):
    B, S, D = x.shape
    T = B * S
    depth = w_in.shape[0]
    h = x
    for i in range(depth):
        lambda_init = 0.8 - 0.6 * math.exp(-0.3 * i)
        q, k, v, u, gate = _in_proj(h, positions, norm_mix_g[i], w_in[i].astype(BF16), tm=IN_PROJ_ROWS)
        attn = _diff_attn(q, k, v, lambda_q1[i], lambda_k1[i], lambda_q2[i], lambda_k2[i],
                          diff_norm_g[i], lambda_init, tq=ATTN_Q_ROWS)
        w_all = jnp.concatenate([_block_diag(lru_wa[i, 0]), _block_diag(lru_wx[i, 0]),
                                 _block_diag(lru_wa[i, 1]), _block_diag(lru_wx[i, 1])], axis=1).astype(BF16)
        bias_all = jnp.concatenate([lru_ba[i, 0], lru_bx[i, 0], lru_ba[i, 1], lru_bx[i, 1]]).reshape(1, -1)
        rec = _bi_rglru(u, gate, conv_w[i], conv_b[i], w_all, bias_all, lru_lambda[i], lru_norm_g[i], tc=LRU_CHUNK_ROWS)
        h1, xn2, idx_t, gate_t = _mix_route(
            attn.reshape(T, -1), rec.reshape(T, -1), h.reshape(T, D), w_out[i].astype(BF16), norm_ffn_g[i],
            peer_wq[i].astype(BF16), _key_matrix(peer_keys1[i], 0).astype(BF16),
            _key_matrix(peer_keys2[i], 1).astype(BF16), tm=ROUTE_TOKENS)
        n_exp = peer_u.shape[1]
        uv2 = _pack_rows(peer_u[i], peer_v[i])
        uv4 = uv2.reshape(n_exp, ROW_CHUNKS, 1, LANES)
        idx, gate_tk = idx_t.T, gate_t.T
        t_sc = T * SC_TOKEN_SHARE_NUM // SC_TOKEN_SHARE_DEN // (2 * SC_WORKERS) * (2 * SC_WORKERS)
        t_tc = T - t_sc
        peer_sc = _peer_sc(idx, gate_tk, xn2, uv2, t_tc, t_sc // SC_WORKERS)
        peer_tc = _peer(idx, gate_tk, xn2.reshape(T, ROW_CHUNKS, LANES), uv4, t_tc, tt=PEER_GROUP_TOKENS)
        peer_out = jnp.concatenate([peer_tc.reshape(t_tc, D), peer_sc], axis=0)
        h = _ple_out(h1, peer_out, p[i].reshape(T, -1), norm_ple_g[i], ple_w_gate[i].astype(BF16),
                     ple_w_proj[i].astype(BF16), final_norm_g, final=(i == depth - 1), tm=PLE_ROWS)
        h = h.reshape(B, S, D)
    return h
```

```python
import functools
import math

import jax
import jax.numpy as jnp
from jax import lax
from jax.experimental import pallas as pl
from jax.experimental.pallas import tpu as pltpu
from jax.experimental.pallas import tpu_sc as plsc

F32 = jnp.float32
BF16 = jnp.bfloat16
I32 = jnp.int32

EPS = 1e-6
DIFF_HEAD_DIM = 64
DIFF_V_DIM = 128
N_DIFF_HEADS = 4
ROPE_DIM = 16
ROPE_THETA = 500000.0
LRU_WIDTH = 512
LRU_C = 8.0
N_KEYS = 128
PEER_HEADS = 8
PEER_TOPK = 16
HALF_KEY = 64
LOG2_E = math.log2(math.e)
LANES = 128
SUBLANES = 8
VMEM_LIMIT = 56 * 1024 * 1024
IN_PROJ_ROWS = 512
ATTN_Q_ROWS = 512
LRU_CHUNK_ROWS = 256
ROUTE_TOKENS = 256
PEER_GROUP_TOKENS = 8
PLE_ROWS = 512
PACK_ROWS = 512


def _rms(x, g):
    return x * lax.rsqrt(jnp.mean(x * x, axis=-1, keepdims=True) + EPS) * g


def _gelu_tanh(x):
    return 0.5 * x * (1.0 + jnp.tanh(math.sqrt(2.0 / math.pi) * (x + 0.044715 * (x * x * x))))


def _sigmoid(x):
    return 1.0 / (1.0 + jnp.exp(-x))


def _inproj_kernel(x_ref, pos_ref, g_ref, w_ref, q_ref, k_ref, v_ref, u_ref, gate_ref):
    x = x_ref[0]
    xn = _rms(x, g_ref[...]).astype(BF16)
    pos = pos_ref[0].astype(F32)
    lane = lax.broadcasted_iota(I32, (1, LANES), 1)
    p = lane & (DIFF_HEAD_DIM - 1)
    freq = (p & (ROPE_DIM // 2 - 1)).astype(F32)
    inv_freq = jnp.exp(freq * (-2.0 / ROPE_DIM * math.log(ROPE_THETA)))
    ang = pos * inv_freq
    cs = jnp.cos(ang)
    sn = jnp.sin(ang)
    half = ROPE_DIM // 2
    c_mul = jnp.where(p < ROPE_DIM, cs, 1.0)
    s_up = jnp.where(p < half, -sn, 0.0)
    s_dn = jnp.where((p >= half) & (p < ROPE_DIM), sn, 0.0)

    def rope(t):
        return t * c_mul + pltpu.roll(t, LANES - half, 1) * s_up + pltpu.roll(t, half, 1) * s_dn

    nq = q_ref.shape[-1]
    pq = jnp.dot(xn, w_ref[:, 0:nq], preferred_element_type=F32)
    pk = jnp.dot(xn, w_ref[:, nq:2 * nq], preferred_element_type=F32)
    scale = DIFF_HEAD_DIM ** -0.5 * LOG2_E
    for j in range(nq // LANES):
        sl = slice(j * LANES, (j + 1) * LANES)
        q_ref[0, :, sl] = (rope(pq[:, sl]) * scale).astype(BF16)
        k_ref[0, :, sl] = rope(pk[:, sl]).astype(BF16)
    v_ref[0] = jnp.dot(xn, w_ref[:, 2 * nq:3 * nq], preferred_element_type=F32).astype(BF16)
    u_ref[0] = jnp.dot(xn, w_ref[:, 3 * nq:3 * nq + LRU_WIDTH], preferred_element_type=F32)
    gate_ref[0] = jnp.dot(xn, w_ref[:, 3 * nq + LRU_WIDTH:3 * nq + 2 * LRU_WIDTH],
                          preferred_element_type=F32).astype(BF16)


def _in_proj(x, positions, g, w_in_bf, tm):
    B, S, D = x.shape
    nq = N_DIFF_HEADS * 2 * DIFF_HEAD_DIM
    ncols = w_in_bf.shape[1]
    row = lambda b, i: (b, i, 0)
    return pl.pallas_call(
        _inproj_kernel,
        grid=(B, S // tm),
        in_specs=[
            pl.BlockSpec((1, tm, D), row),
            pl.BlockSpec((1, tm, 1), row),
            pl.BlockSpec((1, D), lambda b, i: (0, 0)),
            pl.BlockSpec((D, ncols), lambda b, i: (0, 0)),
        ],
        out_specs=[
            pl.BlockSpec((1, tm, nq), row),
            pl.BlockSpec((1, tm, nq), row),
            pl.BlockSpec((1, tm, nq), row),
            pl.BlockSpec((1, tm, LRU_WIDTH), row),
            pl.BlockSpec((1, tm, LRU_WIDTH), row),
        ],
        out_shape=[
            jax.ShapeDtypeStruct((B, S, nq), BF16),
            jax.ShapeDtypeStruct((B, S, nq), BF16),
            jax.ShapeDtypeStruct((B, S, nq), BF16),
            jax.ShapeDtypeStruct((B, S, LRU_WIDTH), F32),
            jax.ShapeDtypeStruct((B, S, LRU_WIDTH), BF16),
        ],
        compiler_params=pltpu.CompilerParams(
            dimension_semantics=("parallel", "parallel"), vmem_limit_bytes=VMEM_LIMIT),
        name="in_proj",
    )(x, positions.reshape(B, S, 1), g.reshape(1, D), w_in_bf)


def _attn_kernel(lq1_ref, lk1_ref, lq2_ref, lk2_ref, g_ref, q_ref, k_ref, v_ref, o_ref, *, lambda_init):
    lam = (jnp.exp(jnp.sum(lq1_ref[...] * lk1_ref[...], axis=-1, keepdims=True))
           - jnp.exp(jnp.sum(lq2_ref[...] * lk2_ref[...], axis=-1, keepdims=True))
           + lambda_init)
    q = q_ref[0]
    k = k_ref[0]
    v = v_ref[0]
    lane = lax.broadcasted_iota(I32, q.shape, 1)
    zero = jnp.zeros_like(q)
    q0 = jnp.where(lane < DIFF_HEAD_DIM, q, zero)
    q1 = jnp.where(lane >= DIFF_HEAD_DIM, q, zero)
    nt = (((1,), (1,)), ((), ()))
    s0 = lax.dot_general(q0, k, nt, preferred_element_type=F32)
    s1 = lax.dot_general(q1, k, nt, preferred_element_type=F32)
    v_ext = jnp.concatenate([v, jnp.ones_like(v)], axis=1)
    p0 = jnp.exp2(s0 - jnp.max(s0, axis=-1, keepdims=True)).astype(BF16)
    p1 = jnp.exp2(s1 - jnp.max(s1, axis=-1, keepdims=True)).astype(BF16)
    e0 = jnp.dot(p0, v_ext, preferred_element_type=F32)
    e1 = jnp.dot(p1, v_ext, preferred_element_type=F32)
    dv = v.shape[1]
    o = e0[:, :dv] / e0[:, dv:dv + 1] - e1[:, :dv] * (lam / e1[:, dv:dv + 1])
    o_ref[0] = (_rms(o, g_ref[...]) * (1.0 - lambda_init)).astype(BF16)


def _diff_attn(q, k, v, lq1, lk1, lq2, lk2, g, lambda_init, tq):
    B, S, W = q.shape
    H = W // DIFF_V_DIM
    vec = lambda b, h, i: (0, 0)
    return pl.pallas_call(
        functools.partial(_attn_kernel, lambda_init=lambda_init),
        grid=(B, H, S // tq),
        in_specs=[
            pl.BlockSpec((1, DIFF_HEAD_DIM), vec),
            pl.BlockSpec((1, DIFF_HEAD_DIM), vec),
            pl.BlockSpec((1, DIFF_HEAD_DIM), vec),
            pl.BlockSpec((1, DIFF_HEAD_DIM), vec),
            pl.BlockSpec((1, DIFF_V_DIM), vec),
            pl.BlockSpec((1, tq, DIFF_V_DIM), lambda b, h, i: (b, i, h)),
            pl.BlockSpec((1, S, DIFF_V_DIM), lambda b, h, i: (b, 0, h)),
            pl.BlockSpec((1, S, DIFF_V_DIM), lambda b, h, i: (b, 0, h)),
        ],
        out_specs=pl.BlockSpec((1, tq, DIFF_V_DIM), lambda b, h, i: (b, i, h)),
        out_shape=jax.ShapeDtypeStruct((B, S, W), BF16),
        compiler_params=pltpu.CompilerParams(
            dimension_semantics=("parallel", "parallel", "parallel"), vmem_limit_bytes=VMEM_LIMIT),
        name="diff_attn",
    )(lq1.reshape(1, -1), lk1.reshape(1, -1), lq2.reshape(1, -1), lk2.reshape(1, -1),
      g.reshape(1, -1), q, k, v)


def _lru_kernel(u_ref, gate_ref, cw_ref, cb_ref, w_ref, bias_ref, lam_ref, g_ref, out_ref, hf_ref, *, tc):
    S = u_ref.shape[1]
    C = u_ref.shape[2]
    nc = S // tc
    halo = SUBLANES
    neg_lam = -lam_ref[...]
    sp = jnp.maximum(neg_lam, 0.0) + jnp.log(1.0 + jnp.exp(-jnp.abs(neg_lam)))
    row = lax.broadcasted_iota(I32, (tc, 1), 0)
    conv_taps = cw_ref.shape[0]
    conv_left = 2

    def gates(c, d):
        r0 = pl.multiple_of(c * tc, tc)
        x = u_ref[0, pl.ds(r0, tc), :]
        prev = u_ref[0, pl.ds(pl.multiple_of(jnp.maximum(r0 - halo, 0), halo), halo), :]
        nxt = u_ref[0, pl.ds(pl.multiple_of(jnp.minimum(r0 + tc, S - halo), halo), halo), :]
        prev = jnp.where(c > 0, prev, 0.0)
        nxt = jnp.where(c < nc - 1, nxt, 0.0)
        win = jnp.concatenate([prev, x, nxt], axis=0)
        uc = cb_ref[...]
        for j in range(conv_taps):
            o = halo - conv_left + j
            uc = uc + cw_ref[j:j + 1, :] * win[o:o + tc, :]
        pre = (jnp.dot(uc.astype(BF16), w_ref[:, d * 2 * C:(d + 1) * 2 * C], preferred_element_type=F32)
               + bias_ref[:, d * 2 * C:(d + 1) * 2 * C])
        r = _sigmoid(pre[:, :C])
        i = _sigmoid(pre[:, C:])
        log_a = -LRU_C * r * sp[d:d + 1, :]
        a = jnp.exp(log_a)
        th = jnp.tanh(log_a)
        mult = jnp.sqrt(-2.0 * th / (1.0 - th))
        return a, mult * (i * uc)

    def scan(a, b, reverse):
        d = 1
        while d < tc:
            if reverse:
                a_s = pltpu.roll(a, tc - d, 0)
                b_s = pltpu.roll(b, tc - d, 0)
                m = row < tc - d
            else:
                a_s = pltpu.roll(a, d, 0)
                b_s = pltpu.roll(b, d, 0)
                m = row >= d
            a_s = jnp.where(m, a_s, 1.0)
            b_s = jnp.where(m, b_s, 0.0)
            b = a * b_s + b
            a = a * a_s
            d *= 2
        return a, b

    def fwd_body(c, h0):
        a, b = gates(c, 0)
        a, b = scan(a, b, False)
        h = a * h0 + b
        hf_ref[pl.ds(pl.multiple_of(c * tc, tc), tc), :] = h
        return h[tc - 1:tc, :]

    lax.fori_loop(0, nc, fwd_body, jnp.zeros((1, C), F32))

    def bwd_body(j, h0):
        c = nc - 1 - j
        r0 = pl.multiple_of(c * tc, tc)
        a, b = gates(c, 1)
        a, b = scan(a, b, True)
        h = a * h0 + b
        y = (hf_ref[pl.ds(r0, tc), :] + h) * _gelu_tanh(gate_ref[0, pl.ds(r0, tc), :].astype(F32))
        out_ref[0, pl.ds(r0, tc), :] = _rms(y, g_ref[...]).astype(BF16)
        return h[0:1, :]

    lax.fori_loop(0, nc, bwd_body, jnp.zeros((1, C), F32))


def _bi_rglru(u, gate, conv_w, conv_b, w_all_bf, bias_all, lru_lambda, g, tc):
    B, S, C = u.shape
    full = lambda b: (0, 0)
    return pl.pallas_call(
        functools.partial(_lru_kernel, tc=tc),
        grid=(B,),
        in_specs=[
            pl.BlockSpec((1, S, C), lambda b: (b, 0, 0)),
            pl.BlockSpec((1, S, C), lambda b: (b, 0, 0)),
            pl.BlockSpec(conv_w.shape, full),
            pl.BlockSpec((1, C), full),
            pl.BlockSpec(w_all_bf.shape, full),
            pl.BlockSpec(bias_all.shape, full),
            pl.BlockSpec(lru_lambda.shape, full),
            pl.BlockSpec((1, C), full),
        ],
        out_specs=pl.BlockSpec((1, S, C), lambda b: (b, 0, 0)),
        out_shape=jax.ShapeDtypeStruct((B, S, C), BF16),
        scratch_shapes=[pltpu.VMEM((S, C), F32)],
        compiler_params=pltpu.CompilerParams(
            dimension_semantics=("parallel",), vmem_limit_bytes=VMEM_LIMIT),
        name="bi_rglru",
    )(u, gate, conv_w, conv_b.reshape(1, C), w_all_bf, bias_all, lru_lambda, g.reshape(1, C))


def _topk_rows(s, k):
    n = s.shape[0]
    rid = lax.broadcasted_iota(I32, s.shape, 0)
    vals, ids = [], []
    for _ in range(k):
        m = jnp.max(s, axis=0, keepdims=True)
        sel = jnp.min(jnp.where(s == m, rid, n), axis=0, keepdims=True)
        vals.append(m)
        ids.append(sel)
        s = jnp.where(rid == sel, -jnp.inf, s)
    return jnp.concatenate(vals, axis=0), jnp.concatenate(ids, axis=0)


CAND_ROW_PIECES = 4


def _candidate_pieces(k):
    up = lambda n: -(-n // SUBLANES) * SUBLANES
    pieces = [("row", i, 0, up(k // (i + 1))) for i in range(CAND_ROW_PIECES)]
    for j in range(k // (CAND_ROW_PIECES + 1)):
        for i0 in range(0, k // (j + 1), SUBLANES):
            pieces.append(("col", j, i0, SUBLANES))
    return pieces


def _route_kernel(attn_ref, rec_ref, x_ref, wo_ref, g_ref, wq_ref, k1_ref, k2_ref,
                  h_ref, xn_ref, idx_ref, gate_ref):
    aw = attn_ref.shape[1]
    h = (x_ref[...]
         + jnp.dot(attn_ref[...], wo_ref[0:aw, :], preferred_element_type=F32)
         + jnp.dot(rec_ref[...], wo_ref[aw:, :], preferred_element_type=F32))
    h_ref[...] = h
    xn = _rms(h, g_ref[...])
    xn_ref[...] = xn
    q = jnp.dot(xn.astype(BF16), wq_ref[...], preferred_element_type=F32).astype(BF16)
    nt = (((1,), (1,)), ((), ()))
    s1 = lax.dot_general(k1_ref[...], q, nt, preferred_element_type=F32)
    s2 = lax.dot_general(k2_ref[...], q, nt, preferred_element_type=F32)
    k = PEER_TOPK
    tm = q.shape[0]
    pieces = _candidate_pieces(k)
    pos_parts, ok_parts = [], []
    for kind, fixed, start, length in pieces:
        r = lax.broadcasted_iota(I32, (length, tm), 0) + start
        i, j = (fixed, r) if kind == "row" else (r, fixed)
        pos_parts.append(i * k + j)
        ok = (i + 1) * (j + 1) <= k
        ok_parts.append(ok if kind == "row" else ok & (r >= CAND_ROW_PIECES))
    pos = jnp.concatenate(pos_parts, axis=0)
    ok = jnp.concatenate(ok_parts, axis=0)
    idx_rows, gate_rows = [], []
    for hd in range(PEER_HEADS):
        v1, i1 = _topk_rows(s1[hd * N_KEYS:(hd + 1) * N_KEYS, :], k)
        v2, i2 = _topk_rows(s2[hd * N_KEYS:(hd + 1) * N_KEYS, :], k)
        cand_parts, cidx_parts = [], []
        for kind, fixed, start, length in pieces:
            if kind == "row":
                cand_parts.append(v1[fixed:fixed + 1, :] + v2[start:start + length, :])
                cidx_parts.append(i1[fixed:fixed + 1, :] * N_KEYS + i2[start:start + length, :])
            else:
                cand_parts.append(v1[start:start + length, :] + v2[fixed:fixed + 1, :])
                cidx_parts.append(i1[start:start + length, :] * N_KEYS + i2[fixed:fixed + 1, :])
        cand = jnp.where(ok, jnp.concatenate(cand_parts, axis=0), -jnp.inf)
        cidx = jnp.concatenate(cidx_parts, axis=0)
        sc, ids = [], []
        for _ in range(k):
            m = jnp.max(cand, axis=0, keepdims=True)
            sel = jnp.min(jnp.where(cand == m, pos, k * k), axis=0, keepdims=True)
            hit = pos == sel
            sc.append(m)
            ids.append(jnp.max(jnp.where(hit, cidx, -1), axis=0, keepdims=True))
            cand = jnp.where(hit, -jnp.inf, cand)
        sc = jnp.concatenate(sc, axis=0)
        e = jnp.exp(sc - sc[0:1, :])
        gate_rows.append(e / jnp.sum(e, axis=0, keepdims=True))
        idx_rows.append(jnp.concatenate(ids, axis=0))
    idx_ref[...] = jnp.concatenate(idx_rows, axis=0)
    gate_ref[...] = jnp.concatenate(gate_rows, axis=0)


def _mix_route(attn, rec, x2, wo_bf, g, wq_bf, k1t_bf, k2t_bf, tm):
    T, D = x2.shape
    NK = PEER_HEADS * PEER_TOPK
    row = lambda i: (i, 0)
    full = lambda i: (0, 0)
    return pl.pallas_call(
        _route_kernel,
        grid=(T // tm,),
        in_specs=[
            pl.BlockSpec((tm, attn.shape[1]), row),
            pl.BlockSpec((tm, rec.shape[1]), row),
            pl.BlockSpec((tm, D), row),
            pl.BlockSpec(wo_bf.shape, full),
            pl.BlockSpec((1, D), full),
            pl.BlockSpec(wq_bf.shape, full),
            pl.BlockSpec(k1t_bf.shape, full),
            pl.BlockSpec(k2t_bf.shape, full),
        ],
        out_specs=[
            pl.BlockSpec((tm, D), row),
            pl.BlockSpec((tm, D), row),
            pl.BlockSpec((NK, tm), lambda i: (0, i)),
            pl.BlockSpec((NK, tm), lambda i: (0, i)),
        ],
        out_shape=[
            jax.ShapeDtypeStruct((T, D), F32),
            jax.ShapeDtypeStruct((T, D), F32),
            jax.ShapeDtypeStruct((NK, T), I32),
            jax.ShapeDtypeStruct((NK, T), F32),
        ],
        compiler_params=pltpu.CompilerParams(
            dimension_semantics=("parallel",), vmem_limit_bytes=VMEM_LIMIT),
        name="mix_route",
    )(attn, rec, x2, wo_bf, g.reshape(1, D), wq_bf, k1t_bf, k2t_bf)


ROW_CHUNKS = 8
HI_HALF_MASK = -65536


def _pack_kernel(u_ref, v_ref, o_ref):
    half = u_ref.shape[1] // 2

    def words(a):
        bits = pltpu.bitcast(a.astype(BF16).astype(F32), I32)
        lo = lax.shift_right_logical(bits[:, :half], jnp.full((a.shape[0], half), 16, I32))
        return lo | (bits[:, half:] & HI_HALF_MASK)
    o_ref[:, :half] = words(u_ref[...])
    o_ref[:, half:] = words(v_ref[...])


def _pack_rows(u, v, rows=PACK_ROWS):
    E, D = u.shape
    rows = min(rows, E)
    blk = lambda i: (i, 0)
    return pl.pallas_call(
        _pack_kernel,
        grid=(E // rows,),
        in_specs=[pl.BlockSpec((rows, D), blk), pl.BlockSpec((rows, D), blk)],
        out_specs=pl.BlockSpec((rows, D), blk),
        out_shape=jax.ShapeDtypeStruct((E, D), I32),
        compiler_params=pltpu.CompilerParams(dimension_semantics=("parallel",), vmem_limit_bytes=VMEM_LIMIT),
        name="pack_rows",
    )(u, v)


def _word_lo(w):
    return pltpu.bitcast(w << 16, F32)


def _word_hi(w):
    return pltpu.bitcast(w & HI_HALF_MASK, F32)
TOKEN_UNROLL = 2
DMA_PRIORITIES = 2
PEER_RING = 4
PEER_PREFETCH = 2


def _peer_kernel(idx_ref, idx_next_ref, xn_ref, gate_ref, uv_hbm, o_ref, *scratch, tt, nk):
    bufs = scratch[:PEER_RING]
    lg_ref, act_ref, sems = scratch[PEER_RING:]
    g = pl.program_id(0)
    n = pl.num_programs(0)
    nt = (((1,), (1,)), ((), ()))
    ch = ROW_CHUNKS
    hc = ch // 2

    def row_copy(ids_ref, q, t, k):
        e = ids_ref[0, 0, (q * tt + t) * nk + k]
        return pltpu.make_async_copy(uv_hbm.at[e], bufs[q].at[t, :, k, :], sems.at[q])

    def wait_group(q):
        pltpu.make_async_copy(bufs[(q + 1) % PEER_RING], bufs[q], sems.at[q]).wait()

    @pl.when(g == 0)
    def _():
        for q in range(PEER_PREFETCH):
            def body(t, carry, q=q):
                for k in range(nk):
                    row_copy(idx_ref, q, t, k).start(priority=k % DMA_PRIORITIES)
                return carry
            lax.fori_loop(0, tt, body, 0)

    def group(q):
        buf = bufs[q]
        row0 = q * tt
        ahead = q + PEER_PREFETCH
        ids_ahead = idx_ref if ahead < PEER_RING else idx_next_ref

        def issue(t, k0, k1):
            for k in range(k0, k1):
                row_copy(ids_ahead, ahead % PEER_RING, t, k).start(priority=k % DMA_PRIORITIES)

        def phase_u(tb, carry):
            for j in range(TOKEN_UNROLL):
                t = tb * TOKEN_UNROLL + j
                issue(t, 0, nk // 2)
                xt = xn_ref[row0 + t]
                acc = None
                for s in range(hc):
                    w = buf[t, s]
                    part = _word_lo(w) * xt[s:s + 1, :] + _word_hi(w) * xt[hc + s:hc + s + 1, :]
                    acc = part if acc is None else acc + part
                hi = acc.astype(BF16)
                lo = (acc - hi.astype(F32)).astype(BF16)
                ones = jnp.ones((SUBLANES, LANES), BF16)
                red = (lax.dot_general(ones, hi, nt, preferred_element_type=F32)
                       + lax.dot_general(ones, lo, nt, preferred_element_type=F32))
                lg_ref[t] = red[0:1, :]
            return carry
        lax.fori_loop(0, tt // TOKEN_UNROLL, phase_u, 0)
        act_ref[...] = _gelu_tanh(lg_ref[...]) * gate_ref[row0:row0 + tt]

        def phase_v(tb, carry):
            for j in range(TOKEN_UNROLL):
                t = tb * TOKEN_UNROLL + j
                issue(t, nk // 2, nk)
                act = act_ref[t].astype(BF16)
                words = [buf[t, hc + s] for s in range(hc)]
                cols = ([jnp.dot(act, _word_lo(w).astype(BF16), preferred_element_type=F32) for w in words]
                        + [jnp.dot(act, _word_hi(w).astype(BF16), preferred_element_type=F32) for w in words])
                o_ref[row0 + t] = jnp.concatenate(cols, axis=0)
            return carry
        lax.fori_loop(0, tt // TOKEN_UNROLL, phase_v, 0)

    for q in range(PEER_RING):
        wait_group(q)
        group(q)

    @pl.when(g == n - 1)
    def _():
        for q in range(PEER_PREFETCH):
            wait_group(q)


def _peer(idx, gate, xn3, uv4, n_tokens, tt):
    T, ch, _ = xn3.shape
    nk = idx.shape[1]
    step = PEER_RING * tt
    n = n_tokens // step
    idx3 = idx.reshape(T // step, 1, step * nk)
    tok = lambda i: (i, 0, 0)
    return pl.pallas_call(
        functools.partial(_peer_kernel, tt=tt, nk=nk),
        grid=(n,),
        in_specs=[
            pl.BlockSpec((1, 1, step * nk), tok, memory_space=pltpu.SMEM),
            pl.BlockSpec((1, 1, step * nk), lambda i: (jnp.minimum(i + 1, n - 1), 0, 0), memory_space=pltpu.SMEM),
            pl.BlockSpec((step, ch, LANES), tok),
            pl.BlockSpec((step, 1, nk), tok),
            pl.BlockSpec(memory_space=pl.ANY),
        ],
        out_specs=pl.BlockSpec((step, ch, LANES), tok),
        out_shape=jax.ShapeDtypeStruct((n_tokens, ch, LANES), F32),
        scratch_shapes=(
            [pltpu.VMEM((tt, ch, nk, LANES), I32) for _ in range(PEER_RING)]
            + [pltpu.VMEM((tt, 1, nk), F32), pltpu.VMEM((tt, 1, nk), F32), pltpu.SemaphoreType.DMA((PEER_RING,))]),
        compiler_params=pltpu.CompilerParams(
            dimension_semantics=("arbitrary",), vmem_limit_bytes=VMEM_LIMIT),
        name="peer",
    )(idx3, idx3, xn3, gate.reshape(T, 1, nk), uv4)


SC_CORES = 2
SC_SUBCORES = 16
SC_LANES = 16
SC_WORKERS = SC_CORES * SC_SUBCORES
SC_EXPERT_BLOCK = 16
SC_CHUNK_GROUP = 4
IDX_SLOTS = 4
SC_TOKEN_SHARE_NUM, SC_TOKEN_SHARE_DEN = 67, 128


def _peer_sc(idx, gate, xn, uv, tok0, toks_per_worker):
    T, nk = idx.shape
    D = xn.shape[1]
    L = SC_LANES
    kb_rows = SC_EXPERT_BLOCK
    n_blocks = nk // kb_rows
    n_chunks = D // L
    half = D // 2
    uc = SC_CHUNK_GROUP
    word_lo = lambda w: plsc.bitcast(w << 16, F32)
    word_hi = lambda w: plsc.bitcast(w & HI_HALF_MASK, F32)
    mesh = plsc.VectorSubcoreMesh(core_axis_name="c", subcore_axis_name="s")

    def body(idx_hbm, gate_hbm, xn_hbm, uv_hbm, out_hbm, idx_v, gate_v, x_v, o_v, rows_v, acc_v, sems, tsems):
        wid = lax.axis_index("s") * SC_CORES + lax.axis_index("c")
        base = wid * toks_per_worker
        lane = lax.iota(I32, L)

        def tok_of(ti):
            return tok0 + base + jnp.minimum(ti, toks_per_worker - 1)

        def idx_copy(ti):
            return pltpu.make_async_copy(idx_hbm.at[tok_of(ti)], idx_v.at[pl.ds((ti % IDX_SLOTS) * nk, nk)],
                                         tsems.at[0])

        def gate_copy(ti):
            return pltpu.make_async_copy(gate_hbm.at[tok_of(ti)], gate_v.at[pl.ds((ti % 2) * nk, nk)], tsems.at[1])

        def x_copy(ti):
            return pltpu.make_async_copy(xn_hbm.at[tok_of(ti)], x_v.at[pl.ds((ti % 2) * D, D)], tsems.at[2])

        def gather(ti, kb, buf):
            first = (ti % IDX_SLOTS) * nk + kb * kb_rows
            return pltpu.make_async_copy(uv_hbm.at[idx_v.at[pl.ds(first, kb_rows)]], rows_v.at[buf], sems.at[buf])

        def evaluate(kb, rv, goff, xoff):
            for j in range(kb_rows):
                acc_v[j] = jnp.zeros((L,), F32)

            @plsc.parallel_loop(0, half // L // uc, unroll=2)
            def _(cg):
                x_lo = [x_v[pl.ds(xoff + (cg * uc + cc) * L, L)] for cc in range(uc)]
                x_hi = [x_v[pl.ds(xoff + half + (cg * uc + cc) * L, L)] for cc in range(uc)]
                for j in range(kb_rows):
                    pr = None
                    for cc in range(uc):
                        w = rv[j, pl.ds((cg * uc + cc) * L, L)]
                        part = word_lo(w) * x_lo[cc] + word_hi(w) * x_hi[cc]
                        pr = part if pr is None else pr + part
                    plsc.addupdate(acc_v.at[j], pr)
            logits = jnp.zeros((L,), F32)
            for j in range(kb_rows):
                logits = jnp.where(lane == j, jnp.sum(acc_v[j]), logits)
            z = math.sqrt(2.0 / math.pi) * (logits + 0.044715 * (logits * logits * logits))
            th = 1.0 - 2.0 / (1.0 + jnp.exp(2.0 * z))
            act = 0.5 * logits * (1.0 + th) * gate_v[pl.ds(goff + kb * kb_rows, kb_rows)]
            acts = [jnp.sum(jnp.where(lane == j, act, 0.0)) for j in range(kb_rows)]

            @plsc.parallel_loop(0, half // L, unroll=2)
            def _(c):
                acc_lo = None
                acc_hi = None
                for j in range(kb_rows):
                    w = rv[j, pl.ds(half + c * L, L)]
                    lo = acts[j] * word_lo(w)
                    hi = acts[j] * word_hi(w)
                    acc_lo = lo if acc_lo is None else acc_lo + lo
                    acc_hi = hi if acc_hi is None else acc_hi + hi
                plsc.addupdate(o_v.at[pl.ds(c * L, L)], acc_lo)
                plsc.addupdate(o_v.at[pl.ds(half + c * L, L)], acc_hi)

        def token(ti, carry):
            gate_copy(ti).wait()
            x_copy(ti).wait()
            idx_copy(ti + 1).wait()
            gate_copy(ti + 1).start()
            x_copy(ti + 1).start()
            idx_copy(ti + 2).start()
            goff = (ti % 2) * nk
            xoff = (ti % 2) * D

            @plsc.parallel_loop(0, n_chunks)
            def _(c):
                o_v[pl.ds(c * L, L)] = jnp.zeros((L,), F32)

            def block(kb, carry):
                buf = kb % 2
                last = kb + 1 >= n_blocks
                gather(ti, kb, buf).wait()
                gather(jnp.where(last, ti + 1, ti), jnp.where(last, 0, kb + 1), 1 - buf).start()
                evaluate(kb, rows_v.at[buf], goff, xoff)
                return carry
            lax.fori_loop(0, n_blocks, block, 0)
            pltpu.sync_copy(o_v, out_hbm.at[base + ti])
            return carry

        idx_copy(0).start()
        idx_copy(0).wait()
        idx_copy(1).start()
        gate_copy(0).start()
        x_copy(0).start()
        gather(0, 0, 0).start()
        lax.fori_loop(0, toks_per_worker, token, 0)
        gate_copy(toks_per_worker).wait()
        x_copy(toks_per_worker).wait()
        idx_copy(toks_per_worker + 1).wait()
        gather(toks_per_worker, 0, 0).wait()

    return pl.kernel(
        body, mesh=mesh,
        compiler_params=pltpu.CompilerParams(needs_layout_passes=False),
        out_type=jax.ShapeDtypeStruct((SC_WORKERS * toks_per_worker, D), F32),
        scratch_types=[
            pltpu.VMEM((IDX_SLOTS * nk,), I32),
            pltpu.VMEM((2 * nk,), F32),
            pltpu.VMEM((2 * D,), F32),
            pltpu.VMEM((D,), F32),
            pltpu.VMEM((2, kb_rows, D), I32),
            pltpu.VMEM((kb_rows, L), F32),
            pltpu.SemaphoreType.DMA((2,)),
            pltpu.SemaphoreType.DMA((3,)),
        ],
    )(idx, gate, xn, uv)


def _ple_kernel(h_ref, po_ref, p_ref, g3_ref, wg_ref, wp_ref, gf_ref, o_ref, *, final):
    h = h_ref[...] + po_ref[...]
    xn = _rms(h, g3_ref[...]).astype(BF16)
    gate = _sigmoid(jnp.dot(xn, wg_ref[...], preferred_element_type=F32))
    proj = jnp.dot(p_ref[...].astype(BF16), wp_ref[...], preferred_element_type=F32)
    h = h + gate * proj
    o_ref[...] = _rms(h, gf_ref[...]) if final else h


def _ple_out(h1, peer_out, p2, g3, wg_bf, wp_bf, gf, final, tm):
    T, D = h1.shape
    row = lambda i: (i, 0)
    full = lambda i: (0, 0)
    return pl.pallas_call(
        functools.partial(_ple_kernel, final=final),
        grid=(T // tm,),
        in_specs=[
            pl.BlockSpec((tm, D), row),
            pl.BlockSpec((tm, D), row),
            pl.BlockSpec((tm, p2.shape[1]), row),
            pl.BlockSpec((1, D), full),
            pl.BlockSpec(wg_bf.shape, full),
            pl.BlockSpec(wp_bf.shape, full),
            pl.BlockSpec((1, D), full),
        ],
        out_specs=pl.BlockSpec((tm, D), row),
        out_shape=jax.ShapeDtypeStruct((T, D), F32),
        compiler_params=pltpu.CompilerParams(
            dimension_semantics=("parallel",), vmem_limit_bytes=VMEM_LIMIT),
        name="ple_out",
    )(h1, peer_out, p2, g3.reshape(1, D), wg_bf, wp_bf, gf.reshape(1, D))


def _block_diag(w):
    nb, bw, _ = w.shape
    eye = jnp.eye(nb, dtype=w.dtype)
    return (eye[:, None, :, None] * w[:, :, None, :]).reshape(nb * bw, nb * bw)


def _key_matrix(keys, half):
    z = jnp.zeros_like(keys)
    blk = jnp.concatenate([keys, z] if half == 0 else [z, keys], axis=1)
    return jnp.kron(jnp.eye(PEER_HEADS, dtype=keys.dtype), blk)


def kernel(x, p, positions, norm_mix_g, w_in, lambda_q1, lambda_k1, lambda_q2, lambda_k2, diff_norm_g, conv_w, conv_b, lru_wa, lru_ba, lru_wx, lru_bx, lru_lambda, lru_norm_g, w_out, norm_ffn_g, peer_wq, peer_keys1, peer_keys2, peer_u, peer_v, norm_ple_g, ple_w_gate, ple_w_proj, final_norm_g):
    B, S, D = x.shape
    T = B * S
    depth = w_in.shape[0]
    h = x
    for i in range(depth):
        lambda_init = 0.8 - 0.6 * math.exp(-0.3 * i)
        q, k, v, u, gate = _in_proj(h, positions, norm_mix_g[i], w_in[i].astype(BF16), tm=IN_PROJ_ROWS)
        attn = _diff_attn(q, k, v, lambda_q1[i], lambda_k1[i], lambda_q2[i], lambda_k2[i],
                          diff_norm_g[i], lambda_init, tq=ATTN_Q_ROWS)
        w_all = jnp.concatenate([_block_diag(lru_wa[i, 0]), _block_diag(lru_wx[i, 0]),
                                 _block_diag(lru_wa[i, 1]), _block_diag(lru_wx[i, 1])], axis=1).astype(BF16)
        bias_all = jnp.concatenate([lru_ba[i, 0], lru_bx[i, 0], lru_ba[i, 1], lru_bx[i, 1]]).reshape(1, -1)
        rec = _bi_rglru(u, gate, conv_w[i], conv_b[i], w_all, bias_all, lru_lambda[i], lru_norm_g[i], tc=LRU_CHUNK_ROWS)
        h1, xn2, idx_t, gate_t = _mix_route(
            attn.reshape(T, -1), rec.reshape(T, -1), h.reshape(T, D), w_out[i].astype(BF16), norm_ffn_g[i],
            peer_wq[i].astype(BF16), _key_matrix(peer_keys1[i], 0).astype(BF16),
            _key_matrix(peer_keys2[i], 1).astype(BF16), tm=ROUTE_TOKENS)
        n_exp = peer_u.shape[1]
        uv2 = _pack_rows(peer_u[i], peer_v[i])
        uv4 = uv2.reshape(n_exp, ROW_CHUNKS, LANES)
        idx, gate_tk = idx_t.T, gate_t.T
        t_sc = T * SC_TOKEN_SHARE_NUM // SC_TOKEN_SHARE_DEN // (2 * SC_WORKERS) * (2 * SC_WORKERS)
        t_tc = T - t_sc
        peer_sc = _peer_sc(idx, gate_tk, xn2, uv2, t_tc, t_sc // SC_WORKERS)
        peer_tc = _peer(idx, gate_tk, xn2.reshape(T, ROW_CHUNKS, LANES), uv4, t_tc, tt=PEER_GROUP_TOKENS)
        peer_out = jnp.concatenate([peer_tc.reshape(t_tc, D), peer_sc], axis=0)
        h = _ple_out(h1, peer_out, p[i].reshape(T, -1), norm_ple_g[i], ple_w_gate[i].astype(BF16),
                     ple_w_proj[i].astype(BF16), final_norm_g, final=(i == depth - 1), tm=PLE_ROWS)
        h = h.reshape(B, S, D)
    return h
```

```python
import functools
import math

import jax
import jax.numpy as jnp
from jax import lax
from jax.experimental import pallas as pl
from jax.experimental.pallas import tpu as pltpu
from jax.experimental.pallas import tpu_sc as plsc

F32 = jnp.float32
BF16 = jnp.bfloat16
I32 = jnp.int32

EPS = 1e-6
DIFF_HEAD_DIM = 64
DIFF_V_DIM = 128
N_DIFF_HEADS = 4
ROPE_DIM = 16
ROPE_THETA = 500000.0
LRU_WIDTH = 512
LRU_C = 8.0
N_KEYS = 128
PEER_HEADS = 8
PEER_TOPK = 16
HALF_KEY = 64
LOG2_E = math.log2(math.e)
LANES = 128
SUBLANES = 8
VMEM_LIMIT = 56 * 1024 * 1024
IN_PROJ_ROWS = 512
ATTN_Q_ROWS = 512
LRU_CHUNK_ROWS = 256
ROUTE_TOKENS = 256
PEER_GROUP_TOKENS = 8
PLE_ROWS = 512
PACK_ROWS = 512


def _rms(x, g):
    return x * lax.rsqrt(jnp.mean(x * x, axis=-1, keepdims=True) + EPS) * g


def _gelu_tanh(x):
    return 0.5 * x * (1.0 + jnp.tanh(math.sqrt(2.0 / math.pi) * (x + 0.044715 * (x * x * x))))


def _sigmoid(x):
    return 1.0 / (1.0 + jnp.exp(-x))


def _inproj_kernel(x_ref, pos_ref, g_ref, w_ref, q_ref, k_ref, v_ref, u_ref, gate_ref):
    x = x_ref[0]
    xn = _rms(x, g_ref[...]).astype(BF16)
    pos = pos_ref[0].astype(F32)
    lane = lax.broadcasted_iota(I32, (1, LANES), 1)
    p = lane & (DIFF_HEAD_DIM - 1)
    freq = (p & (ROPE_DIM // 2 - 1)).astype(F32)
    inv_freq = jnp.exp(freq * (-2.0 / ROPE_DIM * math.log(ROPE_THETA)))
    ang = pos * inv_freq
    cs = jnp.cos(ang)
    sn = jnp.sin(ang)
    half = ROPE_DIM // 2
    c_mul = jnp.where(p < ROPE_DIM, cs, 1.0)
    s_up = jnp.where(p < half, -sn, 0.0)
    s_dn = jnp.where((p >= half) & (p < ROPE_DIM), sn, 0.0)

    def rope(t):
        return t * c_mul + pltpu.roll(t, LANES - half, 1) * s_up + pltpu.roll(t, half, 1) * s_dn

    nq = q_ref.shape[-1]
    pq = jnp.dot(xn, w_ref[:, 0:nq], preferred_element_type=F32)
    pk = jnp.dot(xn, w_ref[:, nq:2 * nq], preferred_element_type=F32)
    scale = DIFF_HEAD_DIM ** -0.5 * LOG2_E
    for j in range(nq // LANES):
        sl = slice(j * LANES, (j + 1) * LANES)
        q_ref[0, :, sl] = (rope(pq[:, sl]) * scale).astype(BF16)
        k_ref[0, :, sl] = rope(pk[:, sl]).astype(BF16)
    v_ref[0] = jnp.dot(xn, w_ref[:, 2 * nq:3 * nq], preferred_element_type=F32).astype(BF16)
    u_ref[0] = jnp.dot(xn, w_ref[:, 3 * nq:3 * nq + LRU_WIDTH], preferred_element_type=F32)
    gate_ref[0] = jnp.dot(xn, w_ref[:, 3 * nq + LRU_WIDTH:3 * nq + 2 * LRU_WIDTH],
                          preferred_element_type=F32).astype(BF16)


def _in_proj(x, positions, g, w_in_bf, tm):
    B, S, D = x.shape
    nq = N_DIFF_HEADS * 2 * DIFF_HEAD_DIM
    ncols = w_in_bf.shape[1]
    row = lambda b, i: (b, i, 0)
    return pl.pallas_call(
        _inproj_kernel,
        grid=(B, S // tm),
        in_specs=[
            pl.BlockSpec((1, tm, D), row),
            pl.BlockSpec((1, tm, 1), row),
            pl.BlockSpec((1, D), lambda b, i: (0, 0)),
            pl.BlockSpec((D, ncols), lambda b, i: (0, 0)),
        ],
        out_specs=[
            pl.BlockSpec((1, tm, nq), row),
            pl.BlockSpec((1, tm, nq), row),
            pl.BlockSpec((1, tm, nq), row),
            pl.BlockSpec((1, tm, LRU_WIDTH), row),
            pl.BlockSpec((1, tm, LRU_WIDTH), row),
        ],
        out_shape=[
            jax.ShapeDtypeStruct((B, S, nq), BF16),
            jax.ShapeDtypeStruct((B, S, nq), BF16),
            jax.ShapeDtypeStruct((B, S, nq), BF16),
            jax.ShapeDtypeStruct((B, S, LRU_WIDTH), F32),
            jax.ShapeDtypeStruct((B, S, LRU_WIDTH), BF16),
        ],
        compiler_params=pltpu.CompilerParams(
            dimension_semantics=("parallel", "parallel"), vmem_limit_bytes=VMEM_LIMIT),
        name="in_proj",
    )(x, positions.reshape(B, S, 1), g.reshape(1, D), w_in_bf)


def _attn_kernel(lq1_ref, lk1_ref, lq2_ref, lk2_ref, g_ref, q_ref, k_ref, v_ref, o_ref, *, lambda_init):
    lam = (jnp.exp(jnp.sum(lq1_ref[...] * lk1_ref[...], axis=-1, keepdims=True))
           - jnp.exp(jnp.sum(lq2_ref[...] * lk2_ref[...], axis=-1, keepdims=True))
           + lambda_init)
    q = q_ref[0]
    k = k_ref[0]
    v = v_ref[0]
    lane = lax.broadcasted_iota(I32, q.shape, 1)
    zero = jnp.zeros_like(q)
    q0 = jnp.where(lane < DIFF_HEAD_DIM, q, zero)
    q1 = jnp.where(lane >= DIFF_HEAD_DIM, q, zero)
    nt = (((1,), (1,)), ((), ()))
    s0 = lax.dot_general(q0, k, nt, preferred_element_type=F32)
    s1 = lax.dot_general(q1, k, nt, preferred_element_type=F32)
    v_ext = jnp.concatenate([v, jnp.ones_like(v)], axis=1)
    p0 = jnp.exp2(s0 - jnp.max(s0, axis=-1, keepdims=True)).astype(BF16)
    p1 = jnp.exp2(s1 - jnp.max(s1, axis=-1, keepdims=True)).astype(BF16)
    e0 = jnp.dot(p0, v_ext, preferred_element_type=F32)
    e1 = jnp.dot(p1, v_ext, preferred_element_type=F32)
    dv = v.shape[1]
    o = e0[:, :dv] / e0[:, dv:dv + 1] - e1[:, :dv] * (lam / e1[:, dv:dv + 1])
    o_ref[0] = (_rms(o, g_ref[...]) * (1.0 - lambda_init)).astype(BF16)


def _diff_attn(q, k, v, lq1, lk1, lq2, lk2, g, lambda_init, tq):
    B, S, W = q.shape
    H = W // DIFF_V_DIM
    vec = lambda b, h, i: (0, 0)
    return pl.pallas_call(
        functools.partial(_attn_kernel, lambda_init=lambda_init),
        grid=(B, H, S // tq),
        in_specs=[
            pl.BlockSpec((1, DIFF_HEAD_DIM), vec),
            pl.BlockSpec((1, DIFF_HEAD_DIM), vec),
            pl.BlockSpec((1, DIFF_HEAD_DIM), vec),
            pl.BlockSpec((1, DIFF_HEAD_DIM), vec),
            pl.BlockSpec((1, DIFF_V_DIM), vec),
            pl.BlockSpec((1, tq, DIFF_V_DIM), lambda b, h, i: (b, i, h)),
            pl.BlockSpec((1, S, DIFF_V_DIM), lambda b, h, i: (b, 0, h)),
            pl.BlockSpec((1, S, DIFF_V_DIM), lambda b, h, i: (b, 0, h)),
        ],
        out_specs=pl.BlockSpec((1, tq, DIFF_V_DIM), lambda b, h, i: (b, i, h)),
        out_shape=jax.ShapeDtypeStruct((B, S, W), BF16),
        compiler_params=pltpu.CompilerParams(
            dimension_semantics=("parallel", "parallel", "parallel"), vmem_limit_bytes=VMEM_LIMIT),
        name="diff_attn",
    )(lq1.reshape(1, -1), lk1.reshape(1, -1), lq2.reshape(1, -1), lk2.reshape(1, -1),
      g.reshape(1, -1), q, k, v)


def _lru_kernel(u_ref, gate_ref, cw_ref, cb_ref, w_ref, bias_ref, lam_ref, g_ref, out_ref, hf_ref, *, tc):
    S = u_ref.shape[1]
    C = u_ref.shape[2]
    nc = S // tc
    halo = SUBLANES
    neg_lam = -lam_ref[...]
    sp = jnp.maximum(neg_lam, 0.0) + jnp.log(1.0 + jnp.exp(-jnp.abs(neg_lam)))
    row = lax.broadcasted_iota(I32, (tc, 1), 0)
    conv_taps = cw_ref.shape[0]
    conv_left = 2

    def gates(c, d):
        r0 = pl.multiple_of(c * tc, tc)
        x = u_ref[0, pl.ds(r0, tc), :]
        prev = u_ref[0, pl.ds(pl.multiple_of(jnp.maximum(r0 - halo, 0), halo), halo), :]
        nxt = u_ref[0, pl.ds(pl.multiple_of(jnp.minimum(r0 + tc, S - halo), halo), halo), :]
        prev = jnp.where(c > 0, prev, 0.0)
        nxt = jnp.where(c < nc - 1, nxt, 0.0)
        win = jnp.concatenate([prev, x, nxt], axis=0)
        uc = cb_ref[...]
        for j in range(conv_taps):
            o = halo - conv_left + j
            uc = uc + cw_ref[j:j + 1, :] * win[o:o + tc, :]
        pre = (jnp.dot(uc.astype(BF16), w_ref[:, d * 2 * C:(d + 1) * 2 * C], preferred_element_type=F32)
               + bias_ref[:, d * 2 * C:(d + 1) * 2 * C])
        r = _sigmoid(pre[:, :C])
        i = _sigmoid(pre[:, C:])
        log_a = -LRU_C * r * sp[d:d + 1, :]
        a = jnp.exp(log_a)
        th = jnp.tanh(log_a)
        mult = jnp.sqrt(-2.0 * th / (1.0 - th))
        return a, mult * (i * uc)

    def scan(a, b, reverse):
        d = 1
        while d < tc:
            if reverse:
                a_s = pltpu.roll(a, tc - d, 0)
                b_s = pltpu.roll(b, tc - d, 0)
                m = row < tc - d
            else:
                a_s = pltpu.roll(a, d, 0)
                b_s = pltpu.roll(b, d, 0)
                m = row >= d
            a_s = jnp.where(m, a_s, 1.0)
            b_s = jnp.where(m, b_s, 0.0)
            b = a * b_s + b
            a = a * a_s
            d *= 2
        return a, b

    def fwd_body(c, h0):
        a, b = gates(c, 0)
        a, b = scan(a, b, False)
        h = a * h0 + b
        hf_ref[pl.ds(pl.multiple_of(c * tc, tc), tc), :] = h
        return h[tc - 1:tc, :]

    lax.fori_loop(0, nc, fwd_body, jnp.zeros((1, C), F32))

    def bwd_body(j, h0):
        c = nc - 1 - j
        r0 = pl.multiple_of(c * tc, tc)
        a, b = gates(c, 1)
        a, b = scan(a, b, True)
        h = a * h0 + b
        y = (hf_ref[pl.ds(r0, tc), :] + h) * _gelu_tanh(gate_ref[0, pl.ds(r0, tc), :].astype(F32))
        out_ref[0, pl.ds(r0, tc), :] = _rms(y, g_ref[...]).astype(BF16)
        return h[0:1, :]

    lax.fori_loop(0, nc, bwd_body, jnp.zeros((1, C), F32))


def _bi_rglru(u, gate, conv_w, conv_b, w_all_bf, bias_all, lru_lambda, g, tc):
    B, S, C = u.shape
    full = lambda b: (0, 0)
    return pl.pallas_call(
        functools.partial(_lru_kernel, tc=tc),
        grid=(B,),
        in_specs=[
            pl.BlockSpec((1, S, C), lambda b: (b, 0, 0)),
            pl.BlockSpec((1, S, C), lambda b: (b, 0, 0)),
            pl.BlockSpec(conv_w.shape, full),
            pl.BlockSpec((1, C), full),
            pl.BlockSpec(w_all_bf.shape, full),
            pl.BlockSpec(bias_all.shape, full),
            pl.BlockSpec(lru_lambda.shape, full),
            pl.BlockSpec((1, C), full),
        ],
        out_specs=pl.BlockSpec((1, S, C), lambda b: (b, 0, 0)),
        out_shape=jax.ShapeDtypeStruct((B, S, C), BF16),
        scratch_shapes=[pltpu.VMEM((S, C), F32)],
        compiler_params=pltpu.CompilerParams(
            dimension_semantics=("parallel",), vmem_limit_bytes=VMEM_LIMIT),
        name="bi_rglru",
    )(u, gate, conv_w, conv_b.reshape(1, C), w_all_bf, bias_all, lru_lambda, g.reshape(1, C))


def _topk_rows(s, k):
    n = s.shape[0]
    rid = lax.broadcasted_iota(I32, s.shape, 0)
    vals, ids = [], []
    for _ in range(k):
        m = jnp.max(s, axis=0, keepdims=True)
        sel = jnp.min(jnp.where(s == m, rid, n), axis=0, keepdims=True)
        vals.append(m)
        ids.append(sel)
        s = jnp.where(rid == sel, -jnp.inf, s)
    return jnp.concatenate(vals, axis=0), jnp.concatenate(ids, axis=0)


CAND_ROW_PIECES = 4


def _candidate_pieces(k):
    up = lambda n: -(-n // SUBLANES) * SUBLANES
    pieces = [("row", i, 0, up(k // (i + 1))) for i in range(CAND_ROW_PIECES)]
    for j in range(k // (CAND_ROW_PIECES + 1)):
        for i0 in range(0, k // (j + 1), SUBLANES):
            pieces.append(("col", j, i0, SUBLANES))
    return pieces


def _route_kernel(attn_ref, rec_ref, x_ref, wo_ref, g_ref, wq_ref, k1_ref, k2_ref,
                  h_ref, xn_ref, idx_ref, gate_ref):
    aw = attn_ref.shape[1]
    h = (x_ref[...]
         + jnp.dot(attn_ref[...], wo_ref[0:aw, :], preferred_element_type=F32)
         + jnp.dot(rec_ref[...], wo_ref[aw:, :], preferred_element_type=F32))
    h_ref[...] = h
    xn = _rms(h, g_ref[...])
    xn_ref[...] = xn
    q = jnp.dot(xn.astype(BF16), wq_ref[...], preferred_element_type=F32).astype(BF16)
    nt = (((1,), (1,)), ((), ()))
    s1 = lax.dot_general(k1_ref[...], q, nt, preferred_element_type=F32)
    s2 = lax.dot_general(k2_ref[...], q, nt, preferred_element_type=F32)
    k = PEER_TOPK
    tm = q.shape[0]
    pieces = _candidate_pieces(k)
    pos_parts, ok_parts = [], []
    for kind, fixed, start, length in pieces:
        r = lax.broadcasted_iota(I32, (length, tm), 0) + start
        i, j = (fixed, r) if kind == "row" else (r, fixed)
        pos_parts.append(i * k + j)
        ok = (i + 1) * (j + 1) <= k
        ok_parts.append(ok if kind == "row" else ok & (r >= CAND_ROW_PIECES))
    pos = jnp.concatenate(pos_parts, axis=0)
    ok = jnp.concatenate(ok_parts, axis=0)
    idx_rows, gate_rows = [], []
    for hd in range(PEER_HEADS):
        v1, i1 = _topk_rows(s1[hd * N_KEYS:(hd + 1) * N_KEYS, :], k)
        v2, i2 = _topk_rows(s2[hd * N_KEYS:(hd + 1) * N_KEYS, :], k)
        cand_parts, cidx_parts = [], []
        for kind, fixed, start, length in pieces:
            if kind == "row":
                cand_parts.append(v1[fixed:fixed + 1, :] + v2[start:start + length, :])
                cidx_parts.append(i1[fixed:fixed + 1, :] * N_KEYS + i2[start:start + length, :])
            else:
                cand_parts.append(v1[start:start + length, :] + v2[fixed:fixed + 1, :])
                cidx_parts.append(i1[start:start + length, :] * N_KEYS + i2[fixed:fixed + 1, :])
        cand = jnp.where(ok, jnp.concatenate(cand_parts, axis=0), -jnp.inf)
        cidx = jnp.concatenate(cidx_parts, axis=0)
        sc, ids = [], []
        for _ in range(k):
            m = jnp.max(cand, axis=0, keepdims=True)
            sel = jnp.min(jnp.where(cand == m, pos, k * k), axis=0, keepdims=True)
            hit = pos == sel
            sc.append(m)
            ids.append(jnp.max(jnp.where(hit, cidx, -1), axis=0, keepdims=True))
            cand = jnp.where(hit, -jnp.inf, cand)
        sc = jnp.concatenate(sc, axis=0)
        e = jnp.exp(sc - sc[0:1, :])
        gate_rows.append(e / jnp.sum(e, axis=0, keepdims=True))
        idx_rows.append(jnp.concatenate(ids, axis=0))
    idx_ref[...] = jnp.concatenate(idx_rows, axis=0)
    gate_ref[...] = jnp.concatenate(gate_rows, axis=0)


def _mix_route(attn, rec, x2, wo_bf, g, wq_bf, k1t_bf, k2t_bf, tm):
    T, D = x2.shape
    NK = PEER_HEADS * PEER_TOPK
    row = lambda i: (i, 0)
    full = lambda i: (0, 0)
    return pl.pallas_call(
        _route_kernel,
        grid=(T // tm,),
        in_specs=[
            pl.BlockSpec((tm, attn.shape[1]), row),
            pl.BlockSpec((tm, rec.shape[1]), row),
            pl.BlockSpec((tm, D), row),
            pl.BlockSpec(wo_bf.shape, full),
            pl.BlockSpec((1, D), full),
            pl.BlockSpec(wq_bf.shape, full),
            pl.BlockSpec(k1t_bf.shape, full),
            pl.BlockSpec(k2t_bf.shape, full),
        ],
        out_specs=[
            pl.BlockSpec((tm, D), row),
            pl.BlockSpec((tm, D), row),
            pl.BlockSpec((NK, tm), lambda i: (0, i)),
            pl.BlockSpec((NK, tm), lambda i: (0, i)),
        ],
        out_shape=[
            jax.ShapeDtypeStruct((T, D), F32),
            jax.ShapeDtypeStruct((T, D), F32),
            jax.ShapeDtypeStruct((NK, T), I32),
            jax.ShapeDtypeStruct((NK, T), F32),
        ],
        compiler_params=pltpu.CompilerParams(
            dimension_semantics=("parallel",), vmem_limit_bytes=VMEM_LIMIT),
        name="mix_route",
    )(attn, rec, x2, wo_bf, g.reshape(1, D), wq_bf, k1t_bf, k2t_bf)


ROW_CHUNKS = 8
HI_HALF_MASK = -65536


def _pack_kernel(u_ref, v_ref, o_ref):
    half = u_ref.shape[1] // 2

    def words(a):
        bits = pltpu.bitcast(a.astype(BF16).astype(F32), I32)
        lo = lax.shift_right_logical(bits[:, :half], jnp.full((a.shape[0], half), 16, I32))
        return lo | (bits[:, half:] & HI_HALF_MASK)
    o_ref[:, :half] = words(u_ref[...])
    o_ref[:, half:] = words(v_ref[...])


def _pack_rows(u, v, rows=PACK_ROWS):
    E, D = u.shape
    rows = min(rows, E)
    blk = lambda i: (i, 0)
    return pl.pallas_call(
        _pack_kernel,
        grid=(E // rows,),
        in_specs=[pl.BlockSpec((rows, D), blk), pl.BlockSpec((rows, D), blk)],
        out_specs=pl.BlockSpec((rows, D), blk),
        out_shape=jax.ShapeDtypeStruct((E, D), I32),
        compiler_params=pltpu.CompilerParams(dimension_semantics=("parallel",), vmem_limit_bytes=VMEM_LIMIT),
        name="pack_rows",
    )(u, v)


def _word_lo(w):
    return pltpu.bitcast(w << 16, F32)


def _word_hi(w):
    return pltpu.bitcast(w & HI_HALF_MASK, F32)
TOKEN_UNROLL = 2
DMA_PRIORITIES = 2
PEER_RING = 4
PEER_PREFETCH = 2


def _peer_kernel(idx_ref, idx_next_ref, xn_ref, gate_ref, uv_hbm, o_ref, *scratch, tt, nk):
    bufs = scratch[:PEER_RING]
    lg_ref, act_ref, sems = scratch[PEER_RING:]
    g = pl.program_id(0)
    n = pl.num_programs(0)
    nt = (((1,), (1,)), ((), ()))
    ch = ROW_CHUNKS
    hc = ch // 2

    def row_copy(ids_ref, q, t, k):
        e = ids_ref[0, 0, (q * tt + t) * nk + k]
        return pltpu.make_async_copy(uv_hbm.at[e], bufs[q].at[t, :, k, :], sems.at[q])

    def wait_group(q):
        pltpu.make_async_copy(bufs[(q + 1) % PEER_RING], bufs[q], sems.at[q]).wait()

    @pl.when(g == 0)
    def _():
        for q in range(PEER_PREFETCH):
            def body(t, carry, q=q):
                for k in range(nk):
                    row_copy(idx_ref, q, t, k).start(priority=k % DMA_PRIORITIES)
                return carry
            lax.fori_loop(0, tt, body, 0)

    def group(q):
        buf = bufs[q]
        row0 = q * tt
        ahead = q + PEER_PREFETCH
        ids_ahead = idx_ref if ahead < PEER_RING else idx_next_ref

        def issue(t, k0, k1):
            for k in range(k0, k1):
                row_copy(ids_ahead, ahead % PEER_RING, t, k).start(priority=k % DMA_PRIORITIES)

        def phase_u(tb, carry):
            for j in range(TOKEN_UNROLL):
                t = tb * TOKEN_UNROLL + j
                issue(t, 0, nk // 2)
                xt = xn_ref[row0 + t]
                acc = None
                for s in range(hc):
                    w = buf[t, s]
                    part = _word_lo(w) * xt[s:s + 1, :] + _word_hi(w) * xt[hc + s:hc + s + 1, :]
                    acc = part if acc is None else acc + part
                hi = acc.astype(BF16)
                lo = (acc - hi.astype(F32)).astype(BF16)
                ones = jnp.ones((SUBLANES, LANES), BF16)
                red = (lax.dot_general(ones, hi, nt, preferred_element_type=F32)
                       + lax.dot_general(ones, lo, nt, preferred_element_type=F32))
                lg_ref[t] = red[0:1, :]
            return carry
        lax.fori_loop(0, tt // TOKEN_UNROLL, phase_u, 0)
        act_ref[...] = _gelu_tanh(lg_ref[...]) * gate_ref[row0:row0 + tt]

        def phase_v(tb, carry):
            for j in range(TOKEN_UNROLL):
                t = tb * TOKEN_UNROLL + j
                issue(t, nk // 2, nk)
                act = act_ref[t].astype(BF16)
                words = [buf[t, hc + s] for s in range(hc)]
                cols = ([jnp.dot(act, _word_lo(w).astype(BF16), preferred_element_type=F32) for w in words]
                        + [jnp.dot(act, _word_hi(w).astype(BF16), preferred_element_type=F32) for w in words])
                o_ref[row0 + t] = jnp.concatenate(cols, axis=0)
            return carry
        lax.fori_loop(0, tt // TOKEN_UNROLL, phase_v, 0)

    for q in range(PEER_RING):
        wait_group(q)
        group(q)

    @pl.when(g == n - 1)
    def _():
        for q in range(PEER_PREFETCH):
            wait_group(q)


def _peer(idx, gate, xn3, uv4, n_tokens, tt):
    T, ch, _ = xn3.shape
    nk = idx.shape[1]
    step = PEER_RING * tt
    n = n_tokens // step
    idx3 = idx.reshape(T // step, 1, step * nk)
    tok = lambda i: (i, 0, 0)
    return pl.pallas_call(
        functools.partial(_peer_kernel, tt=tt, nk=nk),
        grid=(n,),
        in_specs=[
            pl.BlockSpec((1, 1, step * nk), tok, memory_space=pltpu.SMEM),
            pl.BlockSpec((1, 1, step * nk), lambda i: (jnp.minimum(i + 1, n - 1), 0, 0), memory_space=pltpu.SMEM),
            pl.BlockSpec((step, ch, LANES), tok),
            pl.BlockSpec((step, 1, nk), tok),
            pl.BlockSpec(memory_space=pl.ANY),
        ],
        out_specs=pl.BlockSpec((step, ch, LANES), tok),
        out_shape=jax.ShapeDtypeStruct((n_tokens, ch, LANES), F32),
        scratch_shapes=(
            [pltpu.VMEM((tt, ch, nk, LANES), I32) for _ in range(PEER_RING)]
            + [pltpu.VMEM((tt, 1, nk), F32), pltpu.VMEM((tt, 1, nk), F32), pltpu.SemaphoreType.DMA((PEER_RING,))]),
        compiler_params=pltpu.CompilerParams(
            dimension_semantics=("arbitrary",), vmem_limit_bytes=VMEM_LIMIT),
        name="peer",
    )(idx3, idx3, xn3, gate.reshape(T, 1, nk), uv4)


SC_CORES = 2
SC_SUBCORES = 16
SC_LANES = 16
SC_WORKERS = SC_CORES * SC_SUBCORES
SC_EXPERT_BLOCK = 16
SC_CHUNK_GROUP = 4
IDX_SLOTS = 4
SC_TOKEN_SHARE_NUM, SC_TOKEN_SHARE_DEN = 33, 64


def _peer_sc(idx, gate, xn, uv, tok0, toks_per_worker):
    T, nk = idx.shape
    D = xn.shape[1]
    L = SC_LANES
    kb_rows = SC_EXPERT_BLOCK
    n_blocks = nk // kb_rows
    n_chunks = D // L
    half = D // 2
    uc = SC_CHUNK_GROUP
    word_lo = lambda w: plsc.bitcast(w << 16, F32)
    word_hi = lambda w: plsc.bitcast(w & HI_HALF_MASK, F32)
    mesh = plsc.VectorSubcoreMesh(core_axis_name="c", subcore_axis_name="s")

    def body(idx_hbm, gate_hbm, xn_hbm, uv_hbm, out_hbm, idx_v, gate_v, x_v, o_v, rows_v, acc_v, sems, tsems):
        wid = lax.axis_index("s") * SC_CORES + lax.axis_index("c")
        base = wid * toks_per_worker
        lane = lax.iota(I32, L)

        def tok_of(ti):
            return tok0 + base + jnp.minimum(ti, toks_per_worker - 1)

        def idx_copy(ti):
            return pltpu.make_async_copy(idx_hbm.at[tok_of(ti)], idx_v.at[pl.ds((ti % IDX_SLOTS) * nk, nk)],
                                         tsems.at[0])

        def gate_copy(ti):
            return pltpu.make_async_copy(gate_hbm.at[tok_of(ti)], gate_v.at[pl.ds((ti % 2) * nk, nk)], tsems.at[1])

        def x_copy(ti):
            return pltpu.make_async_copy(xn_hbm.at[tok_of(ti)], x_v.at[pl.ds((ti % 2) * D, D)], tsems.at[2])

        def gather(ti, kb, buf):
            first = (ti % IDX_SLOTS) * nk + kb * kb_rows
            return pltpu.make_async_copy(uv_hbm.at[idx_v.at[pl.ds(first, kb_rows)]], rows_v.at[buf], sems.at[buf])

        def evaluate(kb, rv, goff, xoff):
            for j in range(kb_rows):
                acc_v[j] = jnp.zeros((L,), F32)

            @plsc.parallel_loop(0, half // L // uc, unroll=2)
            def _(cg):
                x_lo = [x_v[pl.ds(xoff + (cg * uc + cc) * L, L)] for cc in range(uc)]
                x_hi = [x_v[pl.ds(xoff + half + (cg * uc + cc) * L, L)] for cc in range(uc)]
                for j in range(kb_rows):
                    pr = None
                    for cc in range(uc):
                        w = rv[j, pl.ds((cg * uc + cc) * L, L)]
                        part = word_lo(w) * x_lo[cc] + word_hi(w) * x_hi[cc]
                        pr = part if pr is None else pr + part
                    plsc.addupdate(acc_v.at[j], pr)
            logits = jnp.zeros((L,), F32)
            for j in range(kb_rows):
                logits = jnp.where(lane == j, jnp.sum(acc_v[j]), logits)
            z = math.sqrt(2.0 / math.pi) * (logits + 0.044715 * (logits * logits * logits))
            th = 1.0 - 2.0 / (1.0 + jnp.exp(2.0 * z))
            act = 0.5 * logits * (1.0 + th) * gate_v[pl.ds(goff + kb * kb_rows, kb_rows)]
            acts = [jnp.sum(jnp.where(lane == j, act, 0.0)) for j in range(kb_rows)]

            @plsc.parallel_loop(0, half // L, unroll=2)
            def _(c):
                acc_lo = None
                acc_hi = None
                for j in range(kb_rows):
                    w = rv[j, pl.ds(half + c * L, L)]
                    lo = acts[j] * word_lo(w)
                    hi = acts[j] * word_hi(w)
                    acc_lo = lo if acc_lo is None else acc_lo + lo
                    acc_hi = hi if acc_hi is None else acc_hi + hi
                plsc.addupdate(o_v.at[pl.ds(c * L, L)], acc_lo)
                plsc.addupdate(o_v.at[pl.ds(half + c * L, L)], acc_hi)

        def token(ti, carry):
            gate_copy(ti).wait()
            x_copy(ti).wait()
            idx_copy(ti + 1).wait()
            gate_copy(ti + 1).start()
            x_copy(ti + 1).start()
            idx_copy(ti + 2).start()
            goff = (ti % 2) * nk
            xoff = (ti % 2) * D

            @plsc.parallel_loop(0, n_chunks)
            def _(c):
                o_v[pl.ds(c * L, L)] = jnp.zeros((L,), F32)

            def block(kb, carry):
                buf = kb % 2
                last = kb + 1 >= n_blocks
                gather(ti, kb, buf).wait()
                gather(jnp.where(last, ti + 1, ti), jnp.where(last, 0, kb + 1), 1 - buf).start()
                evaluate(kb, rows_v.at[buf], goff, xoff)
                return carry
            lax.fori_loop(0, n_blocks, block, 0)
            pltpu.sync_copy(o_v, out_hbm.at[base + ti])
            return carry

        idx_copy(0).start()
        idx_copy(0).wait()
        idx_copy(1).start()
        gate_copy(0).start()
        x_copy(0).start()
        gather(0, 0, 0).start()
        lax.fori_loop(0, toks_per_worker, token, 0)
        gate_copy(toks_per_worker).wait()
        x_copy(toks_per_worker).wait()
        idx_copy(toks_per_worker + 1).wait()
        gather(toks_per_worker, 0, 0).wait()

    return pl.kernel(
        body, mesh=mesh,
        compiler_params=pltpu.CompilerParams(needs_layout_passes=False),
        out_type=jax.ShapeDtypeStruct((SC_WORKERS * toks_per_worker, D), F32),
        scratch_types=[
            pltpu.VMEM((IDX_SLOTS * nk,), I32),
            pltpu.VMEM((2 * nk,), F32),
            pltpu.VMEM((2 * D,), F32),
            pltpu.VMEM((D,), F32),
            pltpu.VMEM((2, kb_rows, D), I32),
            pltpu.VMEM((kb_rows, L), F32),
            pltpu.SemaphoreType.DMA((2,)),
            pltpu.SemaphoreType.DMA((3,)),
        ],
    )(idx, gate, xn, uv)


def _ple_kernel(h_ref, po_ref, p_ref, g3_ref, wg_ref, wp_ref, gf_ref, o_ref, *, final):
    h = h_ref[...] + po_ref[...]
    xn = _rms(h, g3_ref[...]).astype(BF16)
    gate = _sigmoid(jnp.dot(xn, wg_ref[...], preferred_element_type=F32))
    proj = jnp.dot(p_ref[...].astype(BF16), wp_ref[...], preferred_element_type=F32)
    h = h + gate * proj
    o_ref[...] = _rms(h, gf_ref[...]) if final else h


def _ple_out(h1, peer_out, p2, g3, wg_bf, wp_bf, gf, final, tm):
    T, D = h1.shape
    row = lambda i: (i, 0)
    full = lambda i: (0, 0)
    return pl.pallas_call(
        functools.partial(_ple_kernel, final=final),
        grid=(T // tm,),
        in_specs=[
            pl.BlockSpec((tm, D), row),
            pl.BlockSpec((tm, D), row),
            pl.BlockSpec((tm, p2.shape[1]), row),
            pl.BlockSpec((1, D), full),
            pl.BlockSpec(wg_bf.shape, full),
            pl.BlockSpec(wp_bf.shape, full),
            pl.BlockSpec((1, D), full),
        ],
        out_specs=pl.BlockSpec((tm, D), row),
        out_shape=jax.ShapeDtypeStruct((T, D), F32),
        compiler_params=pltpu.CompilerParams(
            dimension_semantics=("parallel",), vmem_limit_bytes=VMEM_LIMIT),
        name="ple_out",
    )(h1, peer_out, p2, g3.reshape(1, D), wg_bf, wp_bf, gf.reshape(1, D))


def _block_diag(w):
    nb, bw, _ = w.shape
    eye = jnp.eye(nb, dtype=w.dtype)
    return (eye[:, None, :, None] * w[:, :, None, :]).reshape(nb * bw, nb * bw)


def _key_matrix(keys, half):
    z = jnp.zeros_like(keys)
    blk = jnp.concatenate([keys, z] if half == 0 else [z, keys], axis=1)
    return jnp.kron(jnp.eye(PEER_HEADS, dtype=keys.dtype), blk)


def kernel(x, p, positions, norm_mix_g, w_in, lambda_q1, lambda_k1, lambda_q2, lambda_k2, diff_norm_g, conv_w, conv_b, lru_wa, lru_ba, lru_wx, lru_bx, lru_lambda, lru_norm_g, w_out, norm_ffn_g, peer_wq, peer_keys1, peer_keys2, peer_u, peer_v, norm_ple_g, ple_w_gate, ple_w_proj, final_norm_g):
    B, S, D = x.shape
    T = B * S
    depth = w_in.shape[0]
    h = x
    for i in range(depth):
        lambda_init = 0.8 - 0.6 * math.exp(-0.3 * i)
        q, k, v, u, gate = _in_proj(h, positions, norm_mix_g[i], w_in[i].astype(BF16), tm=IN_PROJ_ROWS)
        attn = _diff_attn(q, k, v, lambda_q1[i], lambda_k1[i], lambda_q2[i], lambda_k2[i],
                          diff_norm_g[i], lambda_init, tq=ATTN_Q_ROWS)
        w_all = jnp.concatenate([_block_diag(lru_wa[i, 0]), _block_diag(lru_wx[i, 0]),
                                 _block_diag(lru_wa[i, 1]), _block_diag(lru_wx[i, 1])], axis=1).astype(BF16)
        bias_all = jnp.concatenate([lru_ba[i, 0], lru_bx[i, 0], lru_ba[i, 1], lru_bx[i, 1]]).reshape(1, -1)
        rec = _bi_rglru(u, gate, conv_w[i], conv_b[i], w_all, bias_all, lru_lambda[i], lru_norm_g[i], tc=LRU_CHUNK_ROWS)
        h1, xn2, idx_t, gate_t = _mix_route(
            attn.reshape(T, -1), rec.reshape(T, -1), h.reshape(T, D), w_out[i].astype(BF16), norm_ffn_g[i],
            peer_wq[i].astype(BF16), _key_matrix(peer_keys1[i], 0).astype(BF16),
            _key_matrix(peer_keys2[i], 1).astype(BF16), tm=ROUTE_TOKENS)
        n_exp = peer_u.shape[1]
        uv2 = _pack_rows(peer_u[i], peer_v[i])
        uv4 = uv2.reshape(n_exp, ROW_CHUNKS, LANES)
        idx, gate_tk = idx_t.T, gate_t.T
        t_sc = T * SC_TOKEN_SHARE_NUM // SC_TOKEN_SHARE_DEN // (2 * SC_WORKERS) * (2 * SC_WORKERS)
        t_tc = T - t_sc
        peer_sc = _peer_sc(idx, gate_tk, xn2, uv2, t_tc, t_sc // SC_WORKERS)
        peer_tc = _peer(idx, gate_tk, xn2.reshape(T, ROW_CHUNKS, LANES), uv4, t_tc, tt=PEER_GROUP_TOKENS)
        peer_out = jnp.concatenate([peer_tc.reshape(t_tc, D), peer_sc], axis=0)
        h = _ple_out(h1, peer_out, p[i].reshape(T, -1), norm_ple_g[i], ple_w_gate[i].astype(BF16),
                     ple_w_proj[i].astype(BF16), final_norm_g, final=(i == depth - 1), tm=PLE_ROWS)
        h = h.reshape(B, S, D)
    return h
```

```python
import functools
import math

import jax
import jax.numpy as jnp
from jax import lax
from jax.experimental import pallas as pl
from jax.experimental.pallas import tpu as pltpu
from jax.experimental.pallas import tpu_sc as plsc

F32 = jnp.float32
BF16 = jnp.bfloat16
I32 = jnp.int32

EPS = 1e-6
DIFF_HEAD_DIM = 64
DIFF_V_DIM = 128
N_DIFF_HEADS = 4
ROPE_DIM = 16
ROPE_THETA = 500000.0
LRU_WIDTH = 512
LRU_C = 8.0
N_KEYS = 128
PEER_HEADS = 8
PEER_TOPK = 16
HALF_KEY = 64
LOG2_E = math.log2(math.e)
LANES = 128
SUBLANES = 8
VMEM_LIMIT = 56 * 1024 * 1024
IN_PROJ_ROWS = 512
ATTN_Q_ROWS = 512
LRU_CHUNK_ROWS = 256
ROUTE_TOKENS = 512
PEER_GROUP_TOKENS = 8
PLE_ROWS = 512
PACK_ROWS = 512


def _rms(x, g):
    return x * lax.rsqrt(jnp.mean(x * x, axis=-1, keepdims=True) + EPS) * g


def _gelu_tanh(x):
    return 0.5 * x * (1.0 + jnp.tanh(math.sqrt(2.0 / math.pi) * (x + 0.044715 * (x * x * x))))


def _sigmoid(x):
    return 1.0 / (1.0 + jnp.exp(-x))


def _inproj_kernel(x_ref, pos_ref, g_ref, w_ref, q_ref, k_ref, v_ref, u_ref, gate_ref):
    x = x_ref[0]
    xn = _rms(x, g_ref[...]).astype(BF16)
    pos = pos_ref[0].astype(F32)
    lane = lax.broadcasted_iota(I32, (1, LANES), 1)
    p = lane & (DIFF_HEAD_DIM - 1)
    freq = (p & (ROPE_DIM // 2 - 1)).astype(F32)
    inv_freq = jnp.exp(freq * (-2.0 / ROPE_DIM * math.log(ROPE_THETA)))
    ang = pos * inv_freq
    cs = jnp.cos(ang)
    sn = jnp.sin(ang)
    half = ROPE_DIM // 2
    c_mul = jnp.where(p < ROPE_DIM, cs, 1.0)
    s_up = jnp.where(p < half, -sn, 0.0)
    s_dn = jnp.where((p >= half) & (p < ROPE_DIM), sn, 0.0)

    def rope(t):
        return t * c_mul + pltpu.roll(t, LANES - half, 1) * s_up + pltpu.roll(t, half, 1) * s_dn

    nq = q_ref.shape[-1]
    pq = jnp.dot(xn, w_ref[:, 0:nq], preferred_element_type=F32)
    pk = jnp.dot(xn, w_ref[:, nq:2 * nq], preferred_element_type=F32)
    scale = DIFF_HEAD_DIM ** -0.5 * LOG2_E
    for j in range(nq // LANES):
        sl = slice(j * LANES, (j + 1) * LANES)
        q_ref[0, :, sl] = (rope(pq[:, sl]) * scale).astype(BF16)
        k_ref[0, :, sl] = rope(pk[:, sl]).astype(BF16)
    v_ref[0] = jnp.dot(xn, w_ref[:, 2 * nq:3 * nq], preferred_element_type=F32).astype(BF16)
    u_ref[0] = jnp.dot(xn, w_ref[:, 3 * nq:3 * nq + LRU_WIDTH], preferred_element_type=F32)
    gate_ref[0] = jnp.dot(xn, w_ref[:, 3 * nq + LRU_WIDTH:3 * nq + 2 * LRU_WIDTH],
                          preferred_element_type=F32).astype(BF16)


def _in_proj(x, positions, g, w_in_bf, tm):
    B, S, D = x.shape
    nq = N_DIFF_HEADS * 2 * DIFF_HEAD_DIM
    ncols = w_in_bf.shape[1]
    row = lambda b, i: (b, i, 0)
    return pl.pallas_call(
        _inproj_kernel,
        grid=(B, S // tm),
        in_specs=[
            pl.BlockSpec((1, tm, D), row),
            pl.BlockSpec((1, tm, 1), row),
            pl.BlockSpec((1, D), lambda b, i: (0, 0)),
            pl.BlockSpec((D, ncols), lambda b, i: (0, 0)),
        ],
        out_specs=[
            pl.BlockSpec((1, tm, nq), row),
            pl.BlockSpec((1, tm, nq), row),
            pl.BlockSpec((1, tm, nq), row),
            pl.BlockSpec((1, tm, LRU_WIDTH), row),
            pl.BlockSpec((1, tm, LRU_WIDTH), row),
        ],
        out_shape=[
            jax.ShapeDtypeStruct((B, S, nq), BF16),
            jax.ShapeDtypeStruct((B, S, nq), BF16),
            jax.ShapeDtypeStruct((B, S, nq), BF16),
            jax.ShapeDtypeStruct((B, S, LRU_WIDTH), F32),
            jax.ShapeDtypeStruct((B, S, LRU_WIDTH), BF16),
        ],
        compiler_params=pltpu.CompilerParams(
            dimension_semantics=("parallel", "parallel"), vmem_limit_bytes=VMEM_LIMIT),
        name="in_proj",
    )(x, positions.reshape(B, S, 1), g.reshape(1, D), w_in_bf)


def _attn_kernel(lq1_ref, lk1_ref, lq2_ref, lk2_ref, g_ref, q_ref, k_ref, v_ref, o_ref, *, lambda_init):
    lam = (jnp.exp(jnp.sum(lq1_ref[...] * lk1_ref[...], axis=-1, keepdims=True))
           - jnp.exp(jnp.sum(lq2_ref[...] * lk2_ref[...], axis=-1, keepdims=True))
           + lambda_init)
    q = q_ref[0]
    k = k_ref[0]
    v = v_ref[0]
    lane = lax.broadcasted_iota(I32, q.shape, 1)
    zero = jnp.zeros_like(q)
    q0 = jnp.where(lane < DIFF_HEAD_DIM, q, zero)
    q1 = jnp.where(lane >= DIFF_HEAD_DIM, q, zero)
    nt = (((1,), (1,)), ((), ()))
    s0 = lax.dot_general(q0, k, nt, preferred_element_type=F32)
    s1 = lax.dot_general(q1, k, nt, preferred_element_type=F32)
    v_ext = jnp.concatenate([v, jnp.ones_like(v)], axis=1)
    p0 = jnp.exp2(s0 - jnp.max(s0, axis=-1, keepdims=True)).astype(BF16)
    p1 = jnp.exp2(s1 - jnp.max(s1, axis=-1, keepdims=True)).astype(BF16)
    e0 = jnp.dot(p0, v_ext, preferred_element_type=F32)
    e1 = jnp.dot(p1, v_ext, preferred_element_type=F32)
    dv = v.shape[1]
    o = e0[:, :dv] / e0[:, dv:dv + 1] - e1[:, :dv] * (lam / e1[:, dv:dv + 1])
    o_ref[0] = (_rms(o, g_ref[...]) * (1.0 - lambda_init)).astype(BF16)


def _diff_attn(q, k, v, lq1, lk1, lq2, lk2, g, lambda_init, tq):
    B, S, W = q.shape
    H = W // DIFF_V_DIM
    vec = lambda b, h, i: (0, 0)
    return pl.pallas_call(
        functools.partial(_attn_kernel, lambda_init=lambda_init),
        grid=(B, H, S // tq),
        in_specs=[
            pl.BlockSpec((1, DIFF_HEAD_DIM), vec),
            pl.BlockSpec((1, DIFF_HEAD_DIM), vec),
            pl.BlockSpec((1, DIFF_HEAD_DIM), vec),
            pl.BlockSpec((1, DIFF_HEAD_DIM), vec),
            pl.BlockSpec((1, DIFF_V_DIM), vec),
            pl.BlockSpec((1, tq, DIFF_V_DIM), lambda b, h, i: (b, i, h)),
            pl.BlockSpec((1, S, DIFF_V_DIM), lambda b, h, i: (b, 0, h)),
            pl.BlockSpec((1, S, DIFF_V_DIM), lambda b, h, i: (b, 0, h)),
        ],
        out_specs=pl.BlockSpec((1, tq, DIFF_V_DIM), lambda b, h, i: (b, i, h)),
        out_shape=jax.ShapeDtypeStruct((B, S, W), BF16),
        compiler_params=pltpu.CompilerParams(
            dimension_semantics=("parallel", "parallel", "parallel"), vmem_limit_bytes=VMEM_LIMIT),
        name="diff_attn",
    )(lq1.reshape(1, -1), lk1.reshape(1, -1), lq2.reshape(1, -1), lk2.reshape(1, -1),
      g.reshape(1, -1), q, k, v)


def _lru_kernel(u_ref, gate_ref, cw_ref, cb_ref, w_ref, bias_ref, lam_ref, g_ref, out_ref, hf_ref, *, tc):
    S = u_ref.shape[1]
    C = u_ref.shape[2]
    nc = S // tc
    halo = SUBLANES
    neg_lam = -lam_ref[...]
    sp = jnp.maximum(neg_lam, 0.0) + jnp.log(1.0 + jnp.exp(-jnp.abs(neg_lam)))
    row = lax.broadcasted_iota(I32, (tc, 1), 0)
    conv_taps = cw_ref.shape[0]
    conv_left = 2

    def gates(c, d):
        r0 = pl.multiple_of(c * tc, tc)
        x = u_ref[0, pl.ds(r0, tc), :]
        prev = u_ref[0, pl.ds(pl.multiple_of(jnp.maximum(r0 - halo, 0), halo), halo), :]
        nxt = u_ref[0, pl.ds(pl.multiple_of(jnp.minimum(r0 + tc, S - halo), halo), halo), :]
        prev = jnp.where(c > 0, prev, 0.0)
        nxt = jnp.where(c < nc - 1, nxt, 0.0)
        win = jnp.concatenate([prev, x, nxt], axis=0)
        uc = cb_ref[...]
        for j in range(conv_taps):
            o = halo - conv_left + j
            uc = uc + cw_ref[j:j + 1, :] * win[o:o + tc, :]
        pre = (jnp.dot(uc.astype(BF16), w_ref[:, d * 2 * C:(d + 1) * 2 * C], preferred_element_type=F32)
               + bias_ref[:, d * 2 * C:(d + 1) * 2 * C])
        r = _sigmoid(pre[:, :C])
        i = _sigmoid(pre[:, C:])
        log_a = -LRU_C * r * sp[d:d + 1, :]
        a = jnp.exp(log_a)
        th = jnp.tanh(log_a)
        mult = jnp.sqrt(-2.0 * th / (1.0 - th))
        return a, mult * (i * uc)

    def scan(a, b, reverse):
        d = 1
        while d < tc:
            if reverse:
                a_s = pltpu.roll(a, tc - d, 0)
                b_s = pltpu.roll(b, tc - d, 0)
                m = row < tc - d
            else:
                a_s = pltpu.roll(a, d, 0)
                b_s = pltpu.roll(b, d, 0)
                m = row >= d
            a_s = jnp.where(m, a_s, 1.0)
            b_s = jnp.where(m, b_s, 0.0)
            b = a * b_s + b
            a = a * a_s
            d *= 2
        return a, b

    def fwd_body(c, h0):
        a, b = gates(c, 0)
        a, b = scan(a, b, False)
        h = a * h0 + b
        hf_ref[pl.ds(pl.multiple_of(c * tc, tc), tc), :] = h
        return h[tc - 1:tc, :]

    lax.fori_loop(0, nc, fwd_body, jnp.zeros((1, C), F32))

    def bwd_body(j, h0):
        c = nc - 1 - j
        r0 = pl.multiple_of(c * tc, tc)
        a, b = gates(c, 1)
        a, b = scan(a, b, True)
        h = a * h0 + b
        y = (hf_ref[pl.ds(r0, tc), :] + h) * _gelu_tanh(gate_ref[0, pl.ds(r0, tc), :].astype(F32))
        out_ref[0, pl.ds(r0, tc), :] = _rms(y, g_ref[...]).astype(BF16)
        return h[0:1, :]

    lax.fori_loop(0, nc, bwd_body, jnp.zeros((1, C), F32))


def _bi_rglru(u, gate, conv_w, conv_b, w_all_bf, bias_all, lru_lambda, g, tc):
    B, S, C = u.shape
    full = lambda b: (0, 0)
    return pl.pallas_call(
        functools.partial(_lru_kernel, tc=tc),
        grid=(B,),
        in_specs=[
            pl.BlockSpec((1, S, C), lambda b: (b, 0, 0)),
            pl.BlockSpec((1, S, C), lambda b: (b, 0, 0)),
            pl.BlockSpec(conv_w.shape, full),
            pl.BlockSpec((1, C), full),
            pl.BlockSpec(w_all_bf.shape, full),
            pl.BlockSpec(bias_all.shape, full),
            pl.BlockSpec(lru_lambda.shape, full),
            pl.BlockSpec((1, C), full),
        ],
        out_specs=pl.BlockSpec((1, S, C), lambda b: (b, 0, 0)),
        out_shape=jax.ShapeDtypeStruct((B, S, C), BF16),
        scratch_shapes=[pltpu.VMEM((S, C), F32)],
        compiler_params=pltpu.CompilerParams(
            dimension_semantics=("parallel",), vmem_limit_bytes=VMEM_LIMIT),
        name="bi_rglru",
    )(u, gate, conv_w, conv_b.reshape(1, C), w_all_bf, bias_all, lru_lambda, g.reshape(1, C))


def _topk_rows(s, k):
    n = s.shape[0]
    rid = lax.broadcasted_iota(I32, s.shape, 0)
    vals, ids = [], []
    for _ in range(k):
        m = jnp.max(s, axis=0, keepdims=True)
        sel = jnp.min(jnp.where(s == m, rid, n), axis=0, keepdims=True)
        vals.append(m)
        ids.append(sel)
        s = jnp.where(rid == sel, -jnp.inf, s)
    return jnp.concatenate(vals, axis=0), jnp.concatenate(ids, axis=0)


CAND_ROW_PIECES = 4


def _candidate_pieces(k):
    up = lambda n: -(-n // SUBLANES) * SUBLANES
    pieces = [("row", i, 0, up(k // (i + 1))) for i in range(CAND_ROW_PIECES)]
    for j in range(k // (CAND_ROW_PIECES + 1)):
        for i0 in range(0, k // (j + 1), SUBLANES):
            pieces.append(("col", j, i0, SUBLANES))
    return pieces


def _route_kernel(attn_ref, rec_ref, x_ref, wo_ref, g_ref, wq_ref, k1_ref, k2_ref,
                  h_ref, xn_ref, idx_ref, gate_ref):
    aw = attn_ref.shape[1]
    h = (x_ref[...]
         + jnp.dot(attn_ref[...], wo_ref[0:aw, :], preferred_element_type=F32)
         + jnp.dot(rec_ref[...], wo_ref[aw:, :], preferred_element_type=F32))
    h_ref[...] = h
    xn = _rms(h, g_ref[...])
    xn_ref[...] = xn
    q = jnp.dot(xn.astype(BF16), wq_ref[...], preferred_element_type=F32).astype(BF16)
    nt = (((1,), (1,)), ((), ()))
    s1 = lax.dot_general(k1_ref[...], q, nt, preferred_element_type=F32)
    s2 = lax.dot_general(k2_ref[...], q, nt, preferred_element_type=F32)
    k = PEER_TOPK
    tm = q.shape[0]
    pieces = _candidate_pieces(k)
    pos_parts, ok_parts = [], []
    for kind, fixed, start, length in pieces:
        r = lax.broadcasted_iota(I32, (length, tm), 0) + start
        i, j = (fixed, r) if kind == "row" else (r, fixed)
        pos_parts.append(i * k + j)
        ok = (i + 1) * (j + 1) <= k
        ok_parts.append(ok if kind == "row" else ok & (r >= CAND_ROW_PIECES))
    pos = jnp.concatenate(pos_parts, axis=0)
    ok = jnp.concatenate(ok_parts, axis=0)
    idx_rows, gate_rows = [], []
    for hd in range(PEER_HEADS):
        v1, i1 = _topk_rows(s1[hd * N_KEYS:(hd + 1) * N_KEYS, :], k)
        v2, i2 = _topk_rows(s2[hd * N_KEYS:(hd + 1) * N_KEYS, :], k)
        cand_parts, cidx_parts = [], []
        for kind, fixed, start, length in pieces:
            if kind == "row":
                cand_parts.append(v1[fixed:fixed + 1, :] + v2[start:start + length, :])
                cidx_parts.append(i1[fixed:fixed + 1, :] * N_KEYS + i2[start:start + length, :])
            else:
                cand_parts.append(v1[start:start + length, :] + v2[fixed:fixed + 1, :])
                cidx_parts.append(i1[start:start + length, :] * N_KEYS + i2[fixed:fixed + 1, :])
        cand = jnp.where(ok, jnp.concatenate(cand_parts, axis=0), -jnp.inf)
        cidx = jnp.concatenate(cidx_parts, axis=0)
        sc, ids = [], []
        for _ in range(k):
            m = jnp.max(cand, axis=0, keepdims=True)
            sel = jnp.min(jnp.where(cand == m, pos, k * k), axis=0, keepdims=True)
            hit = pos == sel
            sc.append(m)
            ids.append(jnp.max(jnp.where(hit, cidx, -1), axis=0, keepdims=True))
            cand = jnp.where(hit, -jnp.inf, cand)
        sc = jnp.concatenate(sc, axis=0)
        e = jnp.exp(sc - sc[0:1, :])
        gate_rows.append(e / jnp.sum(e, axis=0, keepdims=True))
        idx_rows.append(jnp.concatenate(ids, axis=0))
    idx_ref[...] = jnp.concatenate(idx_rows, axis=0)
    gate_ref[...] = jnp.concatenate(gate_rows, axis=0)


def _mix_route(attn, rec, x2, wo_bf, g, wq_bf, k1t_bf, k2t_bf, tm):
    T, D = x2.shape
    NK = PEER_HEADS * PEER_TOPK
    row = lambda i: (i, 0)
    full = lambda i: (0, 0)
    return pl.pallas_call(
        _route_kernel,
        grid=(T // tm,),
        in_specs=[
            pl.BlockSpec((tm, attn.shape[1]), row),
            pl.BlockSpec((tm, rec.shape[1]), row),
            pl.BlockSpec((tm, D), row),
            pl.BlockSpec(wo_bf.shape, full),
            pl.BlockSpec((1, D), full),
            pl.BlockSpec(wq_bf.shape, full),
            pl.BlockSpec(k1t_bf.shape, full),
            pl.BlockSpec(k2t_bf.shape, full),
        ],
        out_specs=[
            pl.BlockSpec((tm, D), row),
            pl.BlockSpec((tm, D), row),
            pl.BlockSpec((NK, tm), lambda i: (0, i)),
            pl.BlockSpec((NK, tm), lambda i: (0, i)),
        ],
        out_shape=[
            jax.ShapeDtypeStruct((T, D), F32),
            jax.ShapeDtypeStruct((T, D), F32),
            jax.ShapeDtypeStruct((NK, T), I32),
            jax.ShapeDtypeStruct((NK, T), F32),
        ],
        compiler_params=pltpu.CompilerParams(
            dimension_semantics=("parallel",), vmem_limit_bytes=VMEM_LIMIT),
        name="mix_route",
    )(attn, rec, x2, wo_bf, g.reshape(1, D), wq_bf, k1t_bf, k2t_bf)


ROW_CHUNKS = 8
HI_HALF_MASK = -65536


def _pack_kernel(u_ref, v_ref, o_ref):
    half = u_ref.shape[1] // 2

    def words(a):
        bits = pltpu.bitcast(a.astype(BF16).astype(F32), I32)
        lo = lax.shift_right_logical(bits[:, :half], jnp.full((a.shape[0], half), 16, I32))
        return lo | (bits[:, half:] & HI_HALF_MASK)
    o_ref[:, :half] = words(u_ref[...])
    o_ref[:, half:] = words(v_ref[...])


def _pack_rows(u, v, rows=PACK_ROWS):
    E, D = u.shape
    rows = min(rows, E)
    blk = lambda i: (i, 0)
    return pl.pallas_call(
        _pack_kernel,
        grid=(E // rows,),
        in_specs=[pl.BlockSpec((rows, D), blk), pl.BlockSpec((rows, D), blk)],
        out_specs=pl.BlockSpec((rows, D), blk),
        out_shape=jax.ShapeDtypeStruct((E, D), I32),
        compiler_params=pltpu.CompilerParams(dimension_semantics=("parallel",), vmem_limit_bytes=VMEM_LIMIT),
        name="pack_rows",
    )(u, v)


def _word_lo(w):
    return pltpu.bitcast(w << 16, F32)


def _word_hi(w):
    return pltpu.bitcast(w & HI_HALF_MASK, F32)
TOKEN_UNROLL = 2
DMA_PRIORITIES = 2
PEER_RING = 4
PEER_PREFETCH = 2


def _peer_kernel(idx_ref, idx_next_ref, xn_ref, gate_ref, uv_hbm, o_ref, *scratch, tt, nk):
    bufs = scratch[:PEER_RING]
    lg_ref, act_ref, sems = scratch[PEER_RING:]
    g = pl.program_id(0)
    n = pl.num_programs(0)
    nt = (((1,), (1,)), ((), ()))
    ch = ROW_CHUNKS
    hc = ch // 2

    def row_copy(ids_ref, q, t, k):
        e = ids_ref[0, 0, (q * tt + t) * nk + k]
        return pltpu.make_async_copy(uv_hbm.at[e], bufs[q].at[t, :, k, :], sems.at[q])

    def wait_group(q):
        pltpu.make_async_copy(bufs[(q + 1) % PEER_RING], bufs[q], sems.at[q]).wait()

    @pl.when(g == 0)
    def _():
        for q in range(PEER_PREFETCH):
            def body(t, carry, q=q):
                for k in range(nk):
                    row_copy(idx_ref, q, t, k).start(priority=k % DMA_PRIORITIES)
                return carry
            lax.fori_loop(0, tt, body, 0)

    def group(q):
        buf = bufs[q]
        row0 = q * tt
        ahead = q + PEER_PREFETCH
        ids_ahead = idx_ref if ahead < PEER_RING else idx_next_ref

        def issue(t, k0, k1):
            for k in range(k0, k1):
                row_copy(ids_ahead, ahead % PEER_RING, t, k).start(priority=k % DMA_PRIORITIES)

        def phase_u(tb, carry):
            for j in range(TOKEN_UNROLL):
                t = tb * TOKEN_UNROLL + j
                issue(t, 0, nk // 2)
                xt = xn_ref[row0 + t]
                acc = None
                for s in range(hc):
                    w = buf[t, s]
                    part = _word_lo(w) * xt[s:s + 1, :] + _word_hi(w) * xt[hc + s:hc + s + 1, :]
                    acc = part if acc is None else acc + part
                hi = acc.astype(BF16)
                lo = (acc - hi.astype(F32)).astype(BF16)
                ones = jnp.ones((SUBLANES, LANES), BF16)
                red = (lax.dot_general(ones, hi, nt, preferred_element_type=F32)
                       + lax.dot_general(ones, lo, nt, preferred_element_type=F32))
                lg_ref[t] = red[0:1, :]
            return carry
        lax.fori_loop(0, tt // TOKEN_UNROLL, phase_u, 0)
        act_ref[...] = _gelu_tanh(lg_ref[...]) * gate_ref[row0:row0 + tt]

        def phase_v(tb, carry):
            for j in range(TOKEN_UNROLL):
                t = tb * TOKEN_UNROLL + j
                issue(t, nk // 2, nk)
                act = act_ref[t].astype(BF16)
                words = [buf[t, hc + s] for s in range(hc)]
                cols = ([jnp.dot(act, _word_lo(w).astype(BF16), preferred_element_type=F32) for w in words]
                        + [jnp.dot(act, _word_hi(w).astype(BF16), preferred_element_type=F32) for w in words])
                o_ref[row0 + t] = jnp.concatenate(cols, axis=0)
            return carry
        lax.fori_loop(0, tt // TOKEN_UNROLL, phase_v, 0)

    for q in range(PEER_RING):
        wait_group(q)
        group(q)

    @pl.when(g == n - 1)
    def _():
        for q in range(PEER_PREFETCH):
            wait_group(q)


def _peer(idx, gate, xn3, uv4, n_tokens, tt):
    T, ch, _ = xn3.shape
    nk = idx.shape[1]
    step = PEER_RING * tt
    n = n_tokens // step
    idx3 = idx.reshape(T // step, 1, step * nk)
    tok = lambda i: (i, 0, 0)
    return pl.pallas_call(
        functools.partial(_peer_kernel, tt=tt, nk=nk),
        grid=(n,),
        in_specs=[
            pl.BlockSpec((1, 1, step * nk), tok, memory_space=pltpu.SMEM),
            pl.BlockSpec((1, 1, step * nk), lambda i: (jnp.minimum(i + 1, n - 1), 0, 0), memory_space=pltpu.SMEM),
            pl.BlockSpec((step, ch, LANES), tok),
            pl.BlockSpec((step, 1, nk), tok),
            pl.BlockSpec(memory_space=pl.ANY),
        ],
        out_specs=pl.BlockSpec((step, ch, LANES), tok),
        out_shape=jax.ShapeDtypeStruct((n_tokens, ch, LANES), F32),
        scratch_shapes=(
            [pltpu.VMEM((tt, ch, nk, LANES), I32) for _ in range(PEER_RING)]
            + [pltpu.VMEM((tt, 1, nk), F32), pltpu.VMEM((tt, 1, nk), F32), pltpu.SemaphoreType.DMA((PEER_RING,))]),
        compiler_params=pltpu.CompilerParams(
            dimension_semantics=("arbitrary",), vmem_limit_bytes=VMEM_LIMIT),
        name="peer",
    )(idx3, idx3, xn3, gate.reshape(T, 1, nk), uv4)


SC_CORES = 2
SC_SUBCORES = 16
SC_LANES = 16
SC_WORKERS = SC_CORES * SC_SUBCORES
SC_EXPERT_BLOCK = 16
SC_CHUNK_GROUP = 4
IDX_SLOTS = 4
SC_TOKEN_SHARE_NUM, SC_TOKEN_SHARE_DEN = 33, 64


def _peer_sc(idx, gate, xn, uv, tok0, toks_per_worker):
    T, nk = idx.shape
    D = xn.shape[1]
    L = SC_LANES
    kb_rows = SC_EXPERT_BLOCK
    n_blocks = nk // kb_rows
    n_chunks = D // L
    half = D // 2
    uc = SC_CHUNK_GROUP
    word_lo = lambda w: plsc.bitcast(w << 16, F32)
    word_hi = lambda w: plsc.bitcast(w & HI_HALF_MASK, F32)
    mesh = plsc.VectorSubcoreMesh(core_axis_name="c", subcore_axis_name="s")

    def body(idx_hbm, gate_hbm, xn_hbm, uv_hbm, out_hbm, idx_v, gate_v, x_v, o_v, rows_v, acc_v, sems, tsems):
        wid = lax.axis_index("s") * SC_CORES + lax.axis_index("c")
        base = wid * toks_per_worker
        lane = lax.iota(I32, L)

        def tok_of(ti):
            return tok0 + base + jnp.minimum(ti, toks_per_worker - 1)

        def idx_copy(ti):
            return pltpu.make_async_copy(idx_hbm.at[tok_of(ti)], idx_v.at[pl.ds((ti % IDX_SLOTS) * nk, nk)],
                                         tsems.at[0])

        def gate_copy(ti):
            return pltpu.make_async_copy(gate_hbm.at[tok_of(ti)], gate_v.at[pl.ds((ti % 2) * nk, nk)], tsems.at[1])

        def x_copy(ti):
            return pltpu.make_async_copy(xn_hbm.at[tok_of(ti)], x_v.at[pl.ds((ti % 2) * D, D)], tsems.at[2])

        def gather(ti, kb, buf):
            first = (ti % IDX_SLOTS) * nk + kb * kb_rows
            return pltpu.make_async_copy(uv_hbm.at[idx_v.at[pl.ds(first, kb_rows)]], rows_v.at[buf], sems.at[buf])

        def evaluate(kb, rv, goff, xoff):
            for j in range(kb_rows):
                acc_v[j] = jnp.zeros((L,), F32)

            @plsc.parallel_loop(0, half // L // uc, unroll=2)
            def _(cg):
                x_lo = [x_v[pl.ds(xoff + (cg * uc + cc) * L, L)] for cc in range(uc)]
                x_hi = [x_v[pl.ds(xoff + half + (cg * uc + cc) * L, L)] for cc in range(uc)]
                for j in range(kb_rows):
                    pr = None
                    for cc in range(uc):
                        w = rv[j, pl.ds((cg * uc + cc) * L, L)]
                        part = word_lo(w) * x_lo[cc] + word_hi(w) * x_hi[cc]
                        pr = part if pr is None else pr + part
                    plsc.addupdate(acc_v.at[j], pr)
            logits = jnp.zeros((L,), F32)
            for j in range(kb_rows):
                logits = jnp.where(lane == j, jnp.sum(acc_v[j]), logits)
            z = math.sqrt(2.0 / math.pi) * (logits + 0.044715 * (logits * logits * logits))
            th = 1.0 - 2.0 / (1.0 + jnp.exp(2.0 * z))
            act = 0.5 * logits * (1.0 + th) * gate_v[pl.ds(goff + kb * kb_rows, kb_rows)]
            acts = [jnp.sum(jnp.where(lane == j, act, 0.0)) for j in range(kb_rows)]

            @plsc.parallel_loop(0, half // L, unroll=2)
            def _(c):
                acc_lo = None
                acc_hi = None
                for j in range(kb_rows):
                    w = rv[j, pl.ds(half + c * L, L)]
                    lo = acts[j] * word_lo(w)
                    hi = acts[j] * word_hi(w)
                    acc_lo = lo if acc_lo is None else acc_lo + lo
                    acc_hi = hi if acc_hi is None else acc_hi + hi
                plsc.addupdate(o_v.at[pl.ds(c * L, L)], acc_lo)
                plsc.addupdate(o_v.at[pl.ds(half + c * L, L)], acc_hi)

        def token(ti, carry):
            gate_copy(ti).wait()
            x_copy(ti).wait()
            idx_copy(ti + 1).wait()
            gate_copy(ti + 1).start()
            x_copy(ti + 1).start()
            idx_copy(ti + 2).start()
            goff = (ti % 2) * nk
            xoff = (ti % 2) * D

            @plsc.parallel_loop(0, n_chunks)
            def _(c):
                o_v[pl.ds(c * L, L)] = jnp.zeros((L,), F32)

            def block(kb, carry):
                buf = kb % 2
                last = kb + 1 >= n_blocks
                gather(ti, kb, buf).wait()
                gather(jnp.where(last, ti + 1, ti), jnp.where(last, 0, kb + 1), 1 - buf).start()
                evaluate(kb, rows_v.at[buf], goff, xoff)
                return carry
            lax.fori_loop(0, n_blocks, block, 0)
            pltpu.sync_copy(o_v, out_hbm.at[base + ti])
            return carry

        idx_copy(0).start()
        idx_copy(0).wait()
        idx_copy(1).start()
        gate_copy(0).start()
        x_copy(0).start()
        gather(0, 0, 0).start()
        lax.fori_loop(0, toks_per_worker, token, 0)
        gate_copy(toks_per_worker).wait()
        x_copy(toks_per_worker).wait()
        idx_copy(toks_per_worker + 1).wait()
        gather(toks_per_worker, 0, 0).wait()

    return pl.kernel(
        body, mesh=mesh,
        compiler_params=pltpu.CompilerParams(needs_layout_passes=False),
        out_type=jax.ShapeDtypeStruct((SC_WORKERS * toks_per_worker, D), F32),
        scratch_types=[
            pltpu.VMEM((IDX_SLOTS * nk,), I32),
            pltpu.VMEM((2 * nk,), F32),
            pltpu.VMEM((2 * D,), F32),
            pltpu.VMEM((D,), F32),
            pltpu.VMEM((2, kb_rows, D), I32),
            pltpu.VMEM((kb_rows, L), F32),
            pltpu.SemaphoreType.DMA((2,)),
            pltpu.SemaphoreType.DMA((3,)),
        ],
    )(idx, gate, xn, uv)


def _ple_kernel(h_ref, po_ref, p_ref, g3_ref, wg_ref, wp_ref, gf_ref, o_ref, *, final):
    h = h_ref[...] + po_ref[...]
    xn = _rms(h, g3_ref[...]).astype(BF16)
    gate = _sigmoid(jnp.dot(xn, wg_ref[...], preferred_element_type=F32))
    proj = jnp.dot(p_ref[...].astype(BF16), wp_ref[...], preferred_element_type=F32)
    h = h + gate * proj
    o_ref[...] = _rms(h, gf_ref[...]) if final else h


def _ple_out(h1, peer_out, p2, g3, wg_bf, wp_bf, gf, final, tm):
    T, D = h1.shape
    row = lambda i: (i, 0)
    full = lambda i: (0, 0)
    return pl.pallas_call(
        functools.partial(_ple_kernel, final=final),
        grid=(T // tm,),
        in_specs=[
            pl.BlockSpec((tm, D), row),
            pl.BlockSpec((tm, D), row),
            pl.BlockSpec((tm, p2.shape[1]), row),
            pl.BlockSpec((1, D), full),
            pl.BlockSpec(wg_bf.shape, full),
            pl.BlockSpec(wp_bf.shape, full),
            pl.BlockSpec((1, D), full),
        ],
        out_specs=pl.BlockSpec((tm, D), row),
        out_shape=jax.ShapeDtypeStruct((T, D), F32),
        compiler_params=pltpu.CompilerParams(
            dimension_semantics=("parallel",), vmem_limit_bytes=VMEM_LIMIT),
        name="ple_out",
    )(h1, peer_out, p2, g3.reshape(1, D), wg_bf, wp_bf, gf.reshape(1, D))


def _block_diag(w):
    nb, bw, _ = w.shape
    eye = jnp.eye(nb, dtype=w.dtype)
    return (eye[:, None, :, None] * w[:, :, None, :]).reshape(nb * bw, nb * bw)


def _key_matrix(keys, half):
    z = jnp.zeros_like(keys)
    blk = jnp.concatenate([keys, z] if half == 0 else [z, keys], axis=1)
    return jnp.kron(jnp.eye(PEER_HEADS, dtype=keys.dtype), blk)


def kernel(x, p, positions, norm_mix_g, w_in, lambda_q1, lambda_k1, lambda_q2, lambda_k2, diff_norm_g, conv_w, conv_b, lru_wa, lru_ba, lru_wx, lru_bx, lru_lambda, lru_norm_g, w_out, norm_ffn_g, peer_wq, peer_keys1, peer_keys2, peer_u, peer_v, norm_ple_g, ple_w_gate, ple_w_proj, final_norm_g):
    B, S, D = x.shape
    T = B * S
    depth = w_in.shape[0]
    h = x
    for i in range(depth):
        lambda_init = 0.8 - 0.6 * math.exp(-0.3 * i)
        q, k, v, u, gate = _in_proj(h, positions, norm_mix_g[i], w_in[i].astype(BF16), tm=IN_PROJ_ROWS)
        attn = _diff_attn(q, k, v, lambda_q1[i], lambda_k1[i], lambda_q2[i], lambda_k2[i],
                          diff_norm_g[i], lambda_init, tq=ATTN_Q_ROWS)
        w_all = jnp.concatenate([_block_diag(lru_wa[i, 0]), _block_diag(lru_wx[i, 0]),
                                 _block_diag(lru_wa[i, 1]), _block_diag(lru_wx[i, 1])], axis=1).astype(BF16)
        bias_all = jnp.concatenate([lru_ba[i, 0], lru_bx[i, 0], lru_ba[i, 1], lru_bx[i, 1]]).reshape(1, -1)
        rec = _bi_rglru(u, gate, conv_w[i], conv_b[i], w_all, bias_all, lru_lambda[i], lru_norm_g[i], tc=LRU_CHUNK_ROWS)
        h1, xn2, idx_t, gate_t = _mix_route(
            attn.reshape(T, -1), rec.reshape(T, -1), h.reshape(T, D), w_out[i].astype(BF16), norm_ffn_g[i],
            peer_wq[i].astype(BF16), _key_matrix(peer_keys1[i], 0).astype(BF16),
            _key_matrix(peer_keys2[i], 1).astype(BF16), tm=ROUTE_TOKENS)
        n_exp = peer_u.shape[1]
        uv2 = _pack_rows(peer_u[i], peer_v[i])
        uv4 = uv2.reshape(n_exp, ROW_CHUNKS, LANES)
        idx, gate_tk = idx_t.T, gate_t.T
        t_sc = T * SC_TOKEN_SHARE_NUM // SC_TOKEN_SHARE_DEN // (2 * SC_WORKERS) * (2 * SC_WORKERS)
        t_tc = T - t_sc
        peer_sc = _peer_sc(idx, gate_tk, xn2, uv2, t_tc, t_sc // SC_WORKERS)
        peer_tc = _peer(idx, gate_tk, xn2.reshape(T, ROW_CHUNKS, LANES), uv4, t_tc, tt=PEER_GROUP_TOKENS)
        peer_out = jnp.concatenate([peer_tc.reshape(t_tc, D), peer_sc], axis=0)
        h = _ple_out(h1, peer_out, p[i].reshape(T, -1), norm_ple_g[i], ple_w_gate[i].astype(BF16),
                     ple_w_proj[i].astype(BF16), final_norm_g, final=(i == depth - 1), tm=PLE_ROWS)
        h = h.reshape(B, S, D)
    return h
```

```python
import functools
import math

import jax
import jax.numpy as jnp
from jax import lax
from jax.experimental import pallas as pl
from jax.experimental.pallas import tpu as pltpu
from jax.experimental.pallas import tpu_sc as plsc

F32 = jnp.float32
BF16 = jnp.bfloat16
I32 = jnp.int32

EPS = 1e-6
DIFF_HEAD_DIM = 64
DIFF_V_DIM = 128
N_DIFF_HEADS = 4
ROPE_DIM = 16
ROPE_THETA = 500000.0
LRU_WIDTH = 512
LRU_C = 8.0
N_KEYS = 128
PEER_HEADS = 8
PEER_TOPK = 16
HALF_KEY = 64
LOG2_E = math.log2(math.e)
LANES = 128
SUBLANES = 8
VMEM_LIMIT = 56 * 1024 * 1024
IN_PROJ_ROWS = 512
ATTN_Q_ROWS = 512
LRU_CHUNK_ROWS = 256
ROUTE_TOKENS = 512
PEER_GROUP_TOKENS = 8
PLE_ROWS = 512
PACK_ROWS = 512


def _rms(x, g):
    return x * lax.rsqrt(jnp.mean(x * x, axis=-1, keepdims=True) + EPS) * g


def _gelu_tanh(x):
    return 0.5 * x * (1.0 + jnp.tanh(math.sqrt(2.0 / math.pi) * (x + 0.044715 * (x * x * x))))


def _sigmoid(x):
    return 1.0 / (1.0 + jnp.exp(-x))


def _inproj_kernel(x_ref, pos_ref, g_ref, w_ref, q_ref, k_ref, v_ref, u_ref, gate_ref):
    x = x_ref[0]
    xn = _rms(x, g_ref[...]).astype(BF16)
    pos = pos_ref[0].astype(F32)
    lane = lax.broadcasted_iota(I32, (1, LANES), 1)
    p = lane & (DIFF_HEAD_DIM - 1)
    freq = (p & (ROPE_DIM // 2 - 1)).astype(F32)
    inv_freq = jnp.exp(freq * (-2.0 / ROPE_DIM * math.log(ROPE_THETA)))
    ang = pos * inv_freq
    cs = jnp.cos(ang)
    sn = jnp.sin(ang)
    half = ROPE_DIM // 2
    c_mul = jnp.where(p < ROPE_DIM, cs, 1.0)
    s_up = jnp.where(p < half, -sn, 0.0)
    s_dn = jnp.where((p >= half) & (p < ROPE_DIM), sn, 0.0)

    def rope(t):
        return t * c_mul + pltpu.roll(t, LANES - half, 1) * s_up + pltpu.roll(t, half, 1) * s_dn

    nq = q_ref.shape[-1]
    pq = jnp.dot(xn, w_ref[:, 0:nq], preferred_element_type=F32)
    pk = jnp.dot(xn, w_ref[:, nq:2 * nq], preferred_element_type=F32)
    scale = DIFF_HEAD_DIM ** -0.5 * LOG2_E
    for j in range(nq // LANES):
        sl = slice(j * LANES, (j + 1) * LANES)
        q_ref[0, :, sl] = (rope(pq[:, sl]) * scale).astype(BF16)
        k_ref[0, :, sl] = rope(pk[:, sl]).astype(BF16)
    v_ref[0] = jnp.dot(xn, w_ref[:, 2 * nq:3 * nq], preferred_element_type=F32).astype(BF16)
    u_ref[0] = jnp.dot(xn, w_ref[:, 3 * nq:3 * nq + LRU_WIDTH], preferred_element_type=F32)
    gate_ref[0] = jnp.dot(xn, w_ref[:, 3 * nq + LRU_WIDTH:3 * nq + 2 * LRU_WIDTH],
                          preferred_element_type=F32).astype(BF16)


def _in_proj(x, positions, g, w_in_bf, tm):
    B, S, D = x.shape
    nq = N_DIFF_HEADS * 2 * DIFF_HEAD_DIM
    ncols = w_in_bf.shape[1]
    row = lambda b, i: (b, i, 0)
    return pl.pallas_call(
        _inproj_kernel,
        grid=(B, S // tm),
        in_specs=[
            pl.BlockSpec((1, tm, D), row),
            pl.BlockSpec((1, tm, 1), row),
            pl.BlockSpec((1, D), lambda b, i: (0, 0)),
            pl.BlockSpec((D, ncols), lambda b, i: (0, 0)),
        ],
        out_specs=[
            pl.BlockSpec((1, tm, nq), row),
            pl.BlockSpec((1, tm, nq), row),
            pl.BlockSpec((1, tm, nq), row),
            pl.BlockSpec((1, tm, LRU_WIDTH), row),
            pl.BlockSpec((1, tm, LRU_WIDTH), row),
        ],
        out_shape=[
            jax.ShapeDtypeStruct((B, S, nq), BF16),
            jax.ShapeDtypeStruct((B, S, nq), BF16),
            jax.ShapeDtypeStruct((B, S, nq), BF16),
            jax.ShapeDtypeStruct((B, S, LRU_WIDTH), F32),
            jax.ShapeDtypeStruct((B, S, LRU_WIDTH), BF16),
        ],
        compiler_params=pltpu.CompilerParams(
            dimension_semantics=("parallel", "parallel"), vmem_limit_bytes=VMEM_LIMIT),
        name="in_proj",
    )(x, positions.reshape(B, S, 1), g.reshape(1, D), w_in_bf)


def _attn_kernel(lq1_ref, lk1_ref, lq2_ref, lk2_ref, g_ref, q_ref, k_ref, v_ref, o_ref, *, lambda_init):
    lam = (jnp.exp(jnp.sum(lq1_ref[...] * lk1_ref[...], axis=-1, keepdims=True))
           - jnp.exp(jnp.sum(lq2_ref[...] * lk2_ref[...], axis=-1, keepdims=True))
           + lambda_init)
    q = q_ref[0]
    k = k_ref[0]
    v = v_ref[0]
    lane = lax.broadcasted_iota(I32, q.shape, 1)
    zero = jnp.zeros_like(q)
    q0 = jnp.where(lane < DIFF_HEAD_DIM, q, zero)
    q1 = jnp.where(lane >= DIFF_HEAD_DIM, q, zero)
    nt = (((1,), (1,)), ((), ()))
    s0 = lax.dot_general(q0, k, nt, preferred_element_type=F32)
    s1 = lax.dot_general(q1, k, nt, preferred_element_type=F32)
    v_ext = jnp.concatenate([v, jnp.ones_like(v)], axis=1)
    p0 = jnp.exp2(s0 - jnp.max(s0, axis=-1, keepdims=True)).astype(BF16)
    p1 = jnp.exp2(s1 - jnp.max(s1, axis=-1, keepdims=True)).astype(BF16)
    e0 = jnp.dot(p0, v_ext, preferred_element_type=F32)
    e1 = jnp.dot(p1, v_ext, preferred_element_type=F32)
    dv = v.shape[1]
    o = e0[:, :dv] / e0[:, dv:dv + 1] - e1[:, :dv] * (lam / e1[:, dv:dv + 1])
    o_ref[0] = (_rms(o, g_ref[...]) * (1.0 - lambda_init)).astype(BF16)


def _diff_attn(q, k, v, lq1, lk1, lq2, lk2, g, lambda_init, tq):
    B, S, W = q.shape
    H = W // DIFF_V_DIM
    vec = lambda b, h, i: (0, 0)
    return pl.pallas_call(
        functools.partial(_attn_kernel, lambda_init=lambda_init),
        grid=(B, H, S // tq),
        in_specs=[
            pl.BlockSpec((1, DIFF_HEAD_DIM), vec),
            pl.BlockSpec((1, DIFF_HEAD_DIM), vec),
            pl.BlockSpec((1, DIFF_HEAD_DIM), vec),
            pl.BlockSpec((1, DIFF_HEAD_DIM), vec),
            pl.BlockSpec((1, DIFF_V_DIM), vec),
            pl.BlockSpec((1, tq, DIFF_V_DIM), lambda b, h, i: (b, i, h)),
            pl.BlockSpec((1, S, DIFF_V_DIM), lambda b, h, i: (b, 0, h)),
            pl.BlockSpec((1, S, DIFF_V_DIM), lambda b, h, i: (b, 0, h)),
        ],
        out_specs=pl.BlockSpec((1, tq, DIFF_V_DIM), lambda b, h, i: (b, i, h)),
        out_shape=jax.ShapeDtypeStruct((B, S, W), BF16),
        compiler_params=pltpu.CompilerParams(
            dimension_semantics=("parallel", "parallel", "parallel"), vmem_limit_bytes=VMEM_LIMIT),
        name="diff_attn",
    )(lq1.reshape(1, -1), lk1.reshape(1, -1), lq2.reshape(1, -1), lk2.reshape(1, -1),
      g.reshape(1, -1), q, k, v)


def _lru_kernel(u_ref, gate_ref, cw_ref, cb_ref, w_ref, bias_ref, lam_ref, g_ref, out_ref, hf_ref, *, tc):
    S = u_ref.shape[1]
    C = u_ref.shape[2]
    nc = S // tc
    halo = SUBLANES
    neg_lam = -lam_ref[...]
    sp = jnp.maximum(neg_lam, 0.0) + jnp.log(1.0 + jnp.exp(-jnp.abs(neg_lam)))
    row = lax.broadcasted_iota(I32, (tc, 1), 0)
    conv_taps = cw_ref.shape[0]
    conv_left = 2

    def gates(c, d):
        r0 = pl.multiple_of(c * tc, tc)
        x = u_ref[0, pl.ds(r0, tc), :]
        prev = u_ref[0, pl.ds(pl.multiple_of(jnp.maximum(r0 - halo, 0), halo), halo), :]
        nxt = u_ref[0, pl.ds(pl.multiple_of(jnp.minimum(r0 + tc, S - halo), halo), halo), :]
        prev = jnp.where(c > 0, prev, 0.0)
        nxt = jnp.where(c < nc - 1, nxt, 0.0)
        win = jnp.concatenate([prev, x, nxt], axis=0)
        uc = cb_ref[...]
        for j in range(conv_taps):
            o = halo - conv_left + j
            uc = uc + cw_ref[j:j + 1, :] * win[o:o + tc, :]
        pre = (jnp.dot(uc.astype(BF16), w_ref[:, d * 2 * C:(d + 1) * 2 * C], preferred_element_type=F32)
               + bias_ref[:, d * 2 * C:(d + 1) * 2 * C])
        r = _sigmoid(pre[:, :C])
        i = _sigmoid(pre[:, C:])
        log_a = -LRU_C * r * sp[d:d + 1, :]
        a = jnp.exp(log_a)
        th = jnp.tanh(log_a)
        mult = jnp.sqrt(-2.0 * th / (1.0 - th))
        return a, mult * (i * uc)

    def scan(a, b, reverse):
        d = 1
        while d < tc:
            if reverse:
                a_s = pltpu.roll(a, tc - d, 0)
                b_s = pltpu.roll(b, tc - d, 0)
                m = row < tc - d
            else:
                a_s = pltpu.roll(a, d, 0)
                b_s = pltpu.roll(b, d, 0)
                m = row >= d
            a_s = jnp.where(m, a_s, 1.0)
            b_s = jnp.where(m, b_s, 0.0)
            b = a * b_s + b
            a = a * a_s
            d *= 2
        return a, b

    def fwd_body(c, h0):
        a, b = gates(c, 0)
        a, b = scan(a, b, False)
        h = a * h0 + b
        hf_ref[pl.ds(pl.multiple_of(c * tc, tc), tc), :] = h
        return h[tc - 1:tc, :]

    lax.fori_loop(0, nc, fwd_body, jnp.zeros((1, C), F32))

    def bwd_body(j, h0):
        c = nc - 1 - j
        r0 = pl.multiple_of(c * tc, tc)
        a, b = gates(c, 1)
        a, b = scan(a, b, True)
        h = a * h0 + b
        y = (hf_ref[pl.ds(r0, tc), :] + h) * _gelu_tanh(gate_ref[0, pl.ds(r0, tc), :].astype(F32))
        out_ref[0, pl.ds(r0, tc), :] = _rms(y, g_ref[...]).astype(BF16)
        return h[0:1, :]

    lax.fori_loop(0, nc, bwd_body, jnp.zeros((1, C), F32))


def _bi_rglru(u, gate, conv_w, conv_b, w_all_bf, bias_all, lru_lambda, g, tc):
    B, S, C = u.shape
    full = lambda b: (0, 0)
    return pl.pallas_call(
        functools.partial(_lru_kernel, tc=tc),
        grid=(B,),
        in_specs=[
            pl.BlockSpec((1, S, C), lambda b: (b, 0, 0)),
            pl.BlockSpec((1, S, C), lambda b: (b, 0, 0)),
            pl.BlockSpec(conv_w.shape, full),
            pl.BlockSpec((1, C), full),
            pl.BlockSpec(w_all_bf.shape, full),
            pl.BlockSpec(bias_all.shape, full),
            pl.BlockSpec(lru_lambda.shape, full),
            pl.BlockSpec((1, C), full),
        ],
        out_specs=pl.BlockSpec((1, S, C), lambda b: (b, 0, 0)),
        out_shape=jax.ShapeDtypeStruct((B, S, C), BF16),
        scratch_shapes=[pltpu.VMEM((S, C), F32)],
        compiler_params=pltpu.CompilerParams(
            dimension_semantics=("parallel",), vmem_limit_bytes=VMEM_LIMIT),
        name="bi_rglru",
    )(u, gate, conv_w, conv_b.reshape(1, C), w_all_bf, bias_all, lru_lambda, g.reshape(1, C))


def _topk_rows(s, k):
    n = s.shape[0]
    rid = lax.broadcasted_iota(I32, s.shape, 0)
    vals, ids = [], []
    for _ in range(k):
        m = jnp.max(s, axis=0, keepdims=True)
        sel = jnp.min(jnp.where(s == m, rid, n), axis=0, keepdims=True)
        vals.append(m)
        ids.append(sel)
        s = jnp.where(rid == sel, -jnp.inf, s)
    return jnp.concatenate(vals, axis=0), jnp.concatenate(ids, axis=0)


CAND_ROW_PIECES = 4


def _candidate_pieces(k):
    up = lambda n: -(-n // SUBLANES) * SUBLANES
    pieces = [("row", i, 0, up(k // (i + 1))) for i in range(CAND_ROW_PIECES)]
    for j in range(k // (CAND_ROW_PIECES + 1)):
        for i0 in range(0, k // (j + 1), SUBLANES):
            pieces.append(("col", j, i0, SUBLANES))
    return pieces


def _route_kernel(attn_ref, rec_ref, x_ref, wo_ref, g_ref, wq_ref, k1_ref, k2_ref,
                  h_ref, xn_ref, idx_ref, gate_ref):
    aw = attn_ref.shape[1]
    h = (x_ref[...]
         + jnp.dot(attn_ref[...], wo_ref[0:aw, :], preferred_element_type=F32)
         + jnp.dot(rec_ref[...], wo_ref[aw:, :], preferred_element_type=F32))
    h_ref[...] = h
    xn = _rms(h, g_ref[...])
    xn_ref[...] = xn
    q = jnp.dot(xn.astype(BF16), wq_ref[...], preferred_element_type=F32).astype(BF16)
    nt = (((1,), (1,)), ((), ()))
    s1 = lax.dot_general(k1_ref[...], q, nt, preferred_element_type=F32)
    s2 = lax.dot_general(k2_ref[...], q, nt, preferred_element_type=F32)
    k = PEER_TOPK
    tm = q.shape[0]
    pieces = _candidate_pieces(k)
    pos_parts, ok_parts = [], []
    for kind, fixed, start, length in pieces:
        r = lax.broadcasted_iota(I32, (length, tm), 0) + start
        i, j = (fixed, r) if kind == "row" else (r, fixed)
        pos_parts.append(i * k + j)
        ok = (i + 1) * (j + 1) <= k
        ok_parts.append(ok if kind == "row" else ok & (r >= CAND_ROW_PIECES))
    pos = jnp.concatenate(pos_parts, axis=0)
    ok = jnp.concatenate(ok_parts, axis=0)
    idx_rows, gate_rows = [], []
    for hd in range(PEER_HEADS):
        v1, i1 = _topk_rows(s1[hd * N_KEYS:(hd + 1) * N_KEYS, :], k)
        v2, i2 = _topk_rows(s2[hd * N_KEYS:(hd + 1) * N_KEYS, :], k)
        cand_parts, cidx_parts = [], []
        for kind, fixed, start, length in pieces:
            if kind == "row":
                cand_parts.append(v1[fixed:fixed + 1, :] + v2[start:start + length, :])
                cidx_parts.append(i1[fixed:fixed + 1, :] * N_KEYS + i2[start:start + length, :])
            else:
                cand_parts.append(v1[start:start + length, :] + v2[fixed:fixed + 1, :])
                cidx_parts.append(i1[start:start + length, :] * N_KEYS + i2[fixed:fixed + 1, :])
        cand = jnp.where(ok, jnp.concatenate(cand_parts, axis=0), -jnp.inf)
        cidx = jnp.concatenate(cidx_parts, axis=0)
        sc, ids = [], []
        for _ in range(k):
            m = jnp.max(cand, axis=0, keepdims=True)
            sel = jnp.min(jnp.where(cand == m, pos, k * k), axis=0, keepdims=True)
            hit = pos == sel
            sc.append(m)
            ids.append(jnp.max(jnp.where(hit, cidx, -1), axis=0, keepdims=True))
            cand = jnp.where(hit, -jnp.inf, cand)
        sc = jnp.concatenate(sc, axis=0)
        e = jnp.exp(sc - sc[0:1, :])
        gate_rows.append(e / jnp.sum(e, axis=0, keepdims=True))
        idx_rows.append(jnp.concatenate(ids, axis=0))
    idx_ref[...] = jnp.concatenate(idx_rows, axis=0)
    gate_ref[...] = jnp.concatenate(gate_rows, axis=0)


def _mix_route(attn, rec, x2, wo_bf, g, wq_bf, k1t_bf, k2t_bf, tm):
    T, D = x2.shape
    NK = PEER_HEADS * PEER_TOPK
    row = lambda i: (i, 0)
    full = lambda i: (0, 0)
    return pl.pallas_call(
        _route_kernel,
        grid=(T // tm,),
        in_specs=[
            pl.BlockSpec((tm, attn.shape[1]), row),
            pl.BlockSpec((tm, rec.shape[1]), row),
            pl.BlockSpec((tm, D), row),
            pl.BlockSpec(wo_bf.shape, full),
            pl.BlockSpec((1, D), full),
            pl.BlockSpec(wq_bf.shape, full),
            pl.BlockSpec(k1t_bf.shape, full),
            pl.BlockSpec(k2t_bf.shape, full),
        ],
        out_specs=[
            pl.BlockSpec((tm, D), row),
            pl.BlockSpec((tm, D), row),
            pl.BlockSpec((NK, tm), lambda i: (0, i)),
            pl.BlockSpec((NK, tm), lambda i: (0, i)),
        ],
        out_shape=[
            jax.ShapeDtypeStruct((T, D), F32),
            jax.ShapeDtypeStruct((T, D), F32),
            jax.ShapeDtypeStruct((NK, T), I32),
            jax.ShapeDtypeStruct((NK, T), F32),
        ],
        compiler_params=pltpu.CompilerParams(
            dimension_semantics=("parallel",), vmem_limit_bytes=VMEM_LIMIT),
        name="mix_route",
    )(attn, rec, x2, wo_bf, g.reshape(1, D), wq_bf, k1t_bf, k2t_bf)


ROW_CHUNKS = 8
HI_HALF_MASK = -65536


def _pack_kernel(u_ref, v_ref, o_ref):
    half = u_ref.shape[1] // 2

    def words(a):
        bits = pltpu.bitcast(a.astype(BF16).astype(F32), I32)
        lo = lax.shift_right_logical(bits[:, :half], jnp.full((a.shape[0], half), 16, I32))
        return lo | (bits[:, half:] & HI_HALF_MASK)
    o_ref[:, :half] = words(u_ref[...])
    o_ref[:, half:] = words(v_ref[...])


def _pack_rows(u, v, rows=PACK_ROWS):
    E, D = u.shape
    rows = min(rows, E)
    blk = lambda i: (i, 0)
    return pl.pallas_call(
        _pack_kernel,
        grid=(E // rows,),
        in_specs=[pl.BlockSpec((rows, D), blk), pl.BlockSpec((rows, D), blk)],
        out_specs=pl.BlockSpec((rows, D), blk),
        out_shape=jax.ShapeDtypeStruct((E, D), I32),
        compiler_params=pltpu.CompilerParams(dimension_semantics=("parallel",), vmem_limit_bytes=VMEM_LIMIT),
        name="pack_rows",
    )(u, v)


def _word_lo(w):
    return pltpu.bitcast(w << 16, F32)


def _word_hi(w):
    return pltpu.bitcast(w & HI_HALF_MASK, F32)
TOKEN_UNROLL = 2
DMA_PRIORITIES = 2
PEER_RING = 4
PEER_PREFETCH = 2


def _peer_kernel(idx_ref, idx_next_ref, xn_ref, gate_ref, uv_hbm, o_ref, *scratch, tt, nk):
    bufs = scratch[:PEER_RING]
    lg_ref, act_ref, sems = scratch[PEER_RING:]
    g = pl.program_id(0)
    n = pl.num_programs(0)
    nt = (((1,), (1,)), ((), ()))
    ch = ROW_CHUNKS
    hc = ch // 2

    def row_copy(ids_ref, q, t, k):
        e = ids_ref[0, 0, (q * tt + t) * nk + k]
        return pltpu.make_async_copy(uv_hbm.at[e], bufs[q].at[t, :, k, :], sems.at[q])

    def wait_group(q):
        pltpu.make_async_copy(bufs[(q + 1) % PEER_RING], bufs[q], sems.at[q]).wait()

    @pl.when(g == 0)
    def _():
        for q in range(PEER_PREFETCH):
            def body(t, carry, q=q):
                for k in range(nk):
                    row_copy(idx_ref, q, t, k).start(priority=k % DMA_PRIORITIES)
                return carry
            lax.fori_loop(0, tt, body, 0)

    def group(q):
        buf = bufs[q]
        row0 = q * tt
        ahead = q + PEER_PREFETCH
        ids_ahead = idx_ref if ahead < PEER_RING else idx_next_ref

        def issue(t, k0, k1):
            for k in range(k0, k1):
                row_copy(ids_ahead, ahead % PEER_RING, t, k).start(priority=k % DMA_PRIORITIES)

        def phase_u(tb, carry):
            for j in range(TOKEN_UNROLL):
                t = tb * TOKEN_UNROLL + j
                issue(t, 0, nk // 2)
                xt = xn_ref[row0 + t]
                acc = None
                for s in range(hc):
                    w = buf[t, s]
                    part = _word_lo(w) * xt[s:s + 1, :] + _word_hi(w) * xt[hc + s:hc + s + 1, :]
                    acc = part if acc is None else acc + part
                hi = acc.astype(BF16)
                lo = (acc - hi.astype(F32)).astype(BF16)
                ones = jnp.ones((SUBLANES, LANES), BF16)
                red = (lax.dot_general(ones, hi, nt, preferred_element_type=F32)
                       + lax.dot_general(ones, lo, nt, preferred_element_type=F32))
                lg_ref[t] = red[0:1, :]
            return carry
        lax.fori_loop(0, tt // TOKEN_UNROLL, phase_u, 0)
        act_ref[...] = _gelu_tanh(lg_ref[...]) * gate_ref[row0:row0 + tt]

        def phase_v(tb, carry):
            for j in range(TOKEN_UNROLL):
                t = tb * TOKEN_UNROLL + j
                issue(t, nk // 2, nk)
                act = act_ref[t].astype(BF16)
                words = [buf[t, hc + s] for s in range(hc)]
                cols = ([jnp.dot(act, _word_lo(w).astype(BF16), preferred_element_type=F32) for w in words]
                        + [jnp.dot(act, _word_hi(w).astype(BF16), preferred_element_type=F32) for w in words])
                o_ref[row0 + t] = jnp.concatenate(cols, axis=0)
            return carry
        lax.fori_loop(0, tt // TOKEN_UNROLL, phase_v, 0)

    for q in range(PEER_RING):
        wait_group(q)
        group(q)

    @pl.when(g == n - 1)
    def _():
        for q in range(PEER_PREFETCH):
            wait_group(q)


def _peer(idx, gate, xn3, uv4, n_tokens, tt):
    T, ch, _ = xn3.shape
    nk = idx.shape[1]
    step = PEER_RING * tt
    n = n_tokens // step
    idx3 = idx.reshape(T // step, 1, step * nk)
    tok = lambda i: (i, 0, 0)
    return pl.pallas_call(
        functools.partial(_peer_kernel, tt=tt, nk=nk),
        grid=(n,),
        in_specs=[
            pl.BlockSpec((1, 1, step * nk), tok, memory_space=pltpu.SMEM),
            pl.BlockSpec((1, 1, step * nk), lambda i: (jnp.minimum(i + 1, n - 1), 0, 0), memory_space=pltpu.SMEM),
            pl.BlockSpec((step, ch, LANES), tok),
            pl.BlockSpec((step, 1, nk), tok),
            pl.BlockSpec(memory_space=pl.ANY),
        ],
        out_specs=pl.BlockSpec((step, ch, LANES), tok),
        out_shape=jax.ShapeDtypeStruct((n_tokens, ch, LANES), F32),
        scratch_shapes=(
            [pltpu.VMEM((tt, ch, nk, LANES), I32) for _ in range(PEER_RING)]
            + [pltpu.VMEM((tt, 1, nk), F32), pltpu.VMEM((tt, 1, nk), F32), pltpu.SemaphoreType.DMA((PEER_RING,))]),
        compiler_params=pltpu.CompilerParams(
            dimension_semantics=("arbitrary",), vmem_limit_bytes=VMEM_LIMIT),
        name="peer",
    )(idx3, idx3, xn3, gate.reshape(T, 1, nk), uv4)


SC_CORES = 2
SC_SUBCORES = 16
SC_LANES = 16
SC_WORKERS = SC_CORES * SC_SUBCORES
SC_EXPERT_BLOCK = 16
SC_CHUNK_GROUP = 4
IDX_SLOTS = 4
SC_TOKEN_SHARE_NUM, SC_TOKEN_SHARE_DEN = 131, 256


def _peer_sc(idx, gate, xn, uv, tok0, toks_per_worker):
    T, nk = idx.shape
    D = xn.shape[1]
    L = SC_LANES
    kb_rows = SC_EXPERT_BLOCK
    n_blocks = nk // kb_rows
    n_chunks = D // L
    half = D // 2
    uc = SC_CHUNK_GROUP
    word_lo = lambda w: plsc.bitcast(w << 16, F32)
    word_hi = lambda w: plsc.bitcast(w & HI_HALF_MASK, F32)
    mesh = plsc.VectorSubcoreMesh(core_axis_name="c", subcore_axis_name="s")

    def body(idx_hbm, gate_hbm, xn_hbm, uv_hbm, out_hbm, idx_v, gate_v, x_v, o_v, rows_v, acc_v, sems, tsems):
        wid = lax.axis_index("s") * SC_CORES + lax.axis_index("c")
        base = wid * toks_per_worker
        lane = lax.iota(I32, L)

        def tok_of(ti):
            return tok0 + base + jnp.minimum(ti, toks_per_worker - 1)

        def idx_copy(ti):
            return pltpu.make_async_copy(idx_hbm.at[tok_of(ti)], idx_v.at[pl.ds((ti % IDX_SLOTS) * nk, nk)],
                                         tsems.at[0])

        def gate_copy(ti):
            return pltpu.make_async_copy(gate_hbm.at[tok_of(ti)], gate_v.at[pl.ds((ti % 2) * nk, nk)], tsems.at[1])

        def x_copy(ti):
            return pltpu.make_async_copy(xn_hbm.at[tok_of(ti)], x_v.at[pl.ds((ti % 2) * D, D)], tsems.at[2])

        def gather(ti, kb, buf):
            first = (ti % IDX_SLOTS) * nk + kb * kb_rows
            return pltpu.make_async_copy(uv_hbm.at[idx_v.at[pl.ds(first, kb_rows)]], rows_v.at[buf], sems.at[buf])

        def evaluate(kb, rv, goff, xoff):
            for j in range(kb_rows):
                acc_v[j] = jnp.zeros((L,), F32)

            @plsc.parallel_loop(0, half // L // uc, unroll=2)
            def _(cg):
                x_lo = [x_v[pl.ds(xoff + (cg * uc + cc) * L, L)] for cc in range(uc)]
                x_hi = [x_v[pl.ds(xoff + half + (cg * uc + cc) * L, L)] for cc in range(uc)]
                for j in range(kb_rows):
                    pr = None
                    for cc in range(uc):
                        w = rv[j, pl.ds((cg * uc + cc) * L, L)]
                        part = word_lo(w) * x_lo[cc] + word_hi(w) * x_hi[cc]
                        pr = part if pr is None else pr + part
                    plsc.addupdate(acc_v.at[j], pr)
            logits = jnp.zeros((L,), F32)
            for j in range(kb_rows):
                logits = jnp.where(lane == j, jnp.sum(acc_v[j]), logits)
            z = math.sqrt(2.0 / math.pi) * (logits + 0.044715 * (logits * logits * logits))
            th = 1.0 - 2.0 / (1.0 + jnp.exp(2.0 * z))
            act = 0.5 * logits * (1.0 + th) * gate_v[pl.ds(goff + kb * kb_rows, kb_rows)]
            acts = [jnp.sum(jnp.where(lane == j, act, 0.0)) for j in range(kb_rows)]

            @plsc.parallel_loop(0, half // L, unroll=2)
            def _(c):
                acc_lo = None
                acc_hi = None
                for j in range(kb_rows):
                    w = rv[j, pl.ds(half + c * L, L)]
                    lo = acts[j] * word_lo(w)
                    hi = acts[j] * word_hi(w)
                    acc_lo = lo if acc_lo is None else acc_lo + lo
                    acc_hi = hi if acc_hi is None else acc_hi + hi
                plsc.addupdate(o_v.at[pl.ds(c * L, L)], acc_lo)
                plsc.addupdate(o_v.at[pl.ds(half + c * L, L)], acc_hi)

        def token(ti, carry):
            gate_copy(ti).wait()
            x_copy(ti).wait()
            idx_copy(ti + 1).wait()
            gate_copy(ti + 1).start()
            x_copy(ti + 1).start()
            idx_copy(ti + 2).start()
            goff = (ti % 2) * nk
            xoff = (ti % 2) * D

            @plsc.parallel_loop(0, n_chunks)
            def _(c):
                o_v[pl.ds(c * L, L)] = jnp.zeros((L,), F32)

            def block(kb, carry):
                buf = kb % 2
                last = kb + 1 >= n_blocks
                gather(ti, kb, buf).wait()
                gather(jnp.where(last, ti + 1, ti), jnp.where(last, 0, kb + 1), 1 - buf).start()
                evaluate(kb, rows_v.at[buf], goff, xoff)
                return carry
            lax.fori_loop(0, n_blocks, block, 0)
            pltpu.sync_copy(o_v, out_hbm.at[base + ti])
            return carry

        idx_copy(0).start()
        idx_copy(0).wait()
        idx_copy(1).start()
        gate_copy(0).start()
        x_copy(0).start()
        gather(0, 0, 0).start()
        lax.fori_loop(0, toks_per_worker, token, 0)
        gate_copy(toks_per_worker).wait()
        x_copy(toks_per_worker).wait()
        idx_copy(toks_per_worker + 1).wait()
        gather(toks_per_worker, 0, 0).wait()

    return pl.kernel(
        body, mesh=mesh,
        compiler_params=pltpu.CompilerParams(needs_layout_passes=False),
        out_type=jax.ShapeDtypeStruct((SC_WORKERS * toks_per_worker, D), F32),
        scratch_types=[
            pltpu.VMEM((IDX_SLOTS * nk,), I32),
            pltpu.VMEM((2 * nk,), F32),
            pltpu.VMEM((2 * D,), F32),
            pltpu.VMEM((D,), F32),
            pltpu.VMEM((2, kb_rows, D), I32),
            pltpu.VMEM((kb_rows, L), F32),
            pltpu.SemaphoreType.DMA((2,)),
            pltpu.SemaphoreType.DMA((3,)),
        ],
    )(idx, gate, xn, uv)


def _ple_kernel(h_ref, po_ref, p_ref, g3_ref, wg_ref, wp_ref, gf_ref, o_ref, *, final):
    h = h_ref[...] + po_ref[...]
    xn = _rms(h, g3_ref[...]).astype(BF16)
    gate = _sigmoid(jnp.dot(xn, wg_ref[...], preferred_element_type=F32))
    proj = jnp.dot(p_ref[...].astype(BF16), wp_ref[...], preferred_element_type=F32)
    h = h + gate * proj
    o_ref[...] = _rms(h, gf_ref[...]) if final else h


def _ple_out(h1, peer_out, p2, g3, wg_bf, wp_bf, gf, final, tm):
    T, D = h1.shape
    row = lambda i: (i, 0)
    full = lambda i: (0, 0)
    return pl.pallas_call(
        functools.partial(_ple_kernel, final=final),
        grid=(T // tm,),
        in_specs=[
            pl.BlockSpec((tm, D), row),
            pl.BlockSpec((tm, D), row),
            pl.BlockSpec((tm, p2.shape[1]), row),
            pl.BlockSpec((1, D), full),
            pl.BlockSpec(wg_bf.shape, full),
            pl.BlockSpec(wp_bf.shape, full),
            pl.BlockSpec((1, D), full),
        ],
        out_specs=pl.BlockSpec((tm, D), row),
        out_shape=jax.ShapeDtypeStruct((T, D), F32),
        compiler_params=pltpu.CompilerParams(
            dimension_semantics=("parallel",), vmem_limit_bytes=VMEM_LIMIT),
        name="ple_out",
    )(h1, peer_out, p2, g3.reshape(1, D), wg_bf, wp_bf, gf.reshape(1, D))


def _block_diag(w):
    nb, bw, _ = w.shape
    eye = jnp.eye(nb, dtype=w.dtype)
    return (eye[:, None, :, None] * w[:, :, None, :]).reshape(nb * bw, nb * bw)


def _key_matrix(keys, half):
    z = jnp.zeros_like(keys)
    blk = jnp.concatenate([keys, z] if half == 0 else [z, keys], axis=1)
    return jnp.kron(jnp.eye(PEER_HEADS, dtype=keys.dtype), blk)


def kernel(x, p, positions, norm_mix_g, w_in, lambda_q1, lambda_k1, lambda_q2, lambda_k2, diff_norm_g, conv_w, conv_b, lru_wa, lru_ba, lru_wx, lru_bx, lru_lambda, lru_norm_g, w_out, norm_ffn_g, peer_wq, peer_keys1, peer_keys2, peer_u, peer_v, norm_ple_g, ple_w_gate, ple_w_proj, final_norm_g):
    B, S, D = x.shape
    T = B * S
    depth = w_in.shape[0]
    h = x
    for i in range(depth):
        lambda_init = 0.8 - 0.6 * math.exp(-0.3 * i)
        q, k, v, u, gate = _in_proj(h, positions, norm_mix_g[i], w_in[i].astype(BF16), tm=IN_PROJ_ROWS)
        attn = _diff_attn(q, k, v, lambda_q1[i], lambda_k1[i], lambda_q2[i], lambda_k2[i],
                          diff_norm_g[i], lambda_init, tq=ATTN_Q_ROWS)
        w_all = jnp.concatenate([_block_diag(lru_wa[i, 0]), _block_diag(lru_wx[i, 0]),
                                 _block_diag(lru_wa[i, 1]), _block_diag(lru_wx[i, 1])], axis=1).astype(BF16)
        bias_all = jnp.concatenate([lru_ba[i, 0], lru_bx[i, 0], lru_ba[i, 1], lru_bx[i, 1]]).reshape(1, -1)
        rec = _bi_rglru(u, gate, conv_w[i], conv_b[i], w_all, bias_all, lru_lambda[i], lru_norm_g[i], tc=LRU_CHUNK_ROWS)
        h1, xn2, idx_t, gate_t = _mix_route(
            attn.reshape(T, -1), rec.reshape(T, -1), h.reshape(T, D), w_out[i].astype(BF16), norm_ffn_g[i],
            peer_wq[i].astype(BF16), _key_matrix(peer_keys1[i], 0).astype(BF16),
            _key_matrix(peer_keys2[i], 1).astype(BF16), tm=ROUTE_TOKENS)
        n_exp = peer_u.shape[1]
        uv2 = _pack_rows(peer_u[i], peer_v[i])
        uv4 = uv2.reshape(n_exp, ROW_CHUNKS, LANES)
        idx, gate_tk = idx_t.T, gate_t.T
        t_sc = T * SC_TOKEN_SHARE_NUM // SC_TOKEN_SHARE_DEN // (2 * SC_WORKERS) * (2 * SC_WORKERS)
        t_tc = T - t_sc
        peer_sc = _peer_sc(idx, gate_tk, xn2, uv2, t_tc, t_sc // SC_WORKERS)
        peer_tc = _peer(idx, gate_tk, xn2.reshape(T, ROW_CHUNKS, LANES), uv4, t_tc, tt=PEER_GROUP_TOKENS)
        peer_out = jnp.concatenate([peer_tc.reshape(t_tc, D), peer_sc], axis=0)
        h = _ple_out(h1, peer_out, p[i].reshape(T, -1), norm_ple_g[i], ple_w_gate[i].astype(BF16),
                     ple_w_proj[i].astype(BF16), final_norm_g, final=(i == depth - 1), tm=PLE_ROWS)
        h = h.reshape(B, S, D)
    return h
```

```python
import functools
import math

import jax
import jax.numpy as jnp
from jax import lax
from jax.experimental import pallas as pl
from jax.experimental.pallas import tpu as pltpu
from jax.experimental.pallas import tpu_sc as plsc

F32 = jnp.float32
BF16 = jnp.bfloat16
I32 = jnp.int32

EPS = 1e-6
DIFF_HEAD_DIM = 64
DIFF_V_DIM = 128
N_DIFF_HEADS = 4
ROPE_DIM = 16
ROPE_THETA = 500000.0
LRU_WIDTH = 512
LRU_C = 8.0
N_KEYS = 128
PEER_HEADS = 8
PEER_TOPK = 16
HALF_KEY = 64
LOG2_E = math.log2(math.e)
LANES = 128
SUBLANES = 8
VMEM_LIMIT = 56 * 1024 * 1024
IN_PROJ_ROWS = 512
ATTN_Q_ROWS = 256
LRU_CHUNK_ROWS = 256
ROUTE_TOKENS = 512
PEER_GROUP_TOKENS = 8
PLE_ROWS = 512
PACK_ROWS = 512


def _rms(x, g):
    return x * lax.rsqrt(jnp.mean(x * x, axis=-1, keepdims=True) + EPS) * g


def _gelu_tanh(x):
    return 0.5 * x * (1.0 + jnp.tanh(math.sqrt(2.0 / math.pi) * (x + 0.044715 * (x * x * x))))


def _sigmoid(x):
    return 1.0 / (1.0 + jnp.exp(-x))


def _inproj_kernel(x_ref, pos_ref, g_ref, w_ref, q_ref, k_ref, v_ref, u_ref, gate_ref):
    x = x_ref[0]
    xn = _rms(x, g_ref[...]).astype(BF16)
    pos = pos_ref[0].astype(F32)
    lane = lax.broadcasted_iota(I32, (1, LANES), 1)
    p = lane & (DIFF_HEAD_DIM - 1)
    freq = (p & (ROPE_DIM // 2 - 1)).astype(F32)
    inv_freq = jnp.exp(freq * (-2.0 / ROPE_DIM * math.log(ROPE_THETA)))
    ang = pos * inv_freq
    cs = jnp.cos(ang)
    sn = jnp.sin(ang)
    half = ROPE_DIM // 2
    c_mul = jnp.where(p < ROPE_DIM, cs, 1.0)
    s_up = jnp.where(p < half, -sn, 0.0)
    s_dn = jnp.where((p >= half) & (p < ROPE_DIM), sn, 0.0)

    def rope(t):
        return t * c_mul + pltpu.roll(t, LANES - half, 1) * s_up + pltpu.roll(t, half, 1) * s_dn

    nq = q_ref.shape[-1]
    pq = jnp.dot(xn, w_ref[:, 0:nq], preferred_element_type=F32)
    pk = jnp.dot(xn, w_ref[:, nq:2 * nq], preferred_element_type=F32)
    scale = DIFF_HEAD_DIM ** -0.5 * LOG2_E
    for j in range(nq // LANES):
        sl = slice(j * LANES, (j + 1) * LANES)
        q_ref[0, :, sl] = (rope(pq[:, sl]) * scale).astype(BF16)
        k_ref[0, :, sl] = rope(pk[:, sl]).astype(BF16)
    v_ref[0] = jnp.dot(xn, w_ref[:, 2 * nq:3 * nq], preferred_element_type=F32).astype(BF16)
    u_ref[0] = jnp.dot(xn, w_ref[:, 3 * nq:3 * nq + LRU_WIDTH], preferred_element_type=F32)
    gate_ref[0] = jnp.dot(xn, w_ref[:, 3 * nq + LRU_WIDTH:3 * nq + 2 * LRU_WIDTH],
                          preferred_element_type=F32).astype(BF16)


def _in_proj(x, positions, g, w_in_bf, tm):
    B, S, D = x.shape
    nq = N_DIFF_HEADS * 2 * DIFF_HEAD_DIM
    ncols = w_in_bf.shape[1]
    row = lambda b, i: (b, i, 0)
    return pl.pallas_call(
        _inproj_kernel,
        grid=(B, S // tm),
        in_specs=[
            pl.BlockSpec((1, tm, D), row),
            pl.BlockSpec((1, tm, 1), row),
            pl.BlockSpec((1, D), lambda b, i: (0, 0)),
            pl.BlockSpec((D, ncols), lambda b, i: (0, 0)),
        ],
        out_specs=[
            pl.BlockSpec((1, tm, nq), row),
            pl.BlockSpec((1, tm, nq), row),
            pl.BlockSpec((1, tm, nq), row),
            pl.BlockSpec((1, tm, LRU_WIDTH), row),
            pl.BlockSpec((1, tm, LRU_WIDTH), row),
        ],
        out_shape=[
            jax.ShapeDtypeStruct((B, S, nq), BF16),
            jax.ShapeDtypeStruct((B, S, nq), BF16),
            jax.ShapeDtypeStruct((B, S, nq), BF16),
            jax.ShapeDtypeStruct((B, S, LRU_WIDTH), F32),
            jax.ShapeDtypeStruct((B, S, LRU_WIDTH), BF16),
        ],
        compiler_params=pltpu.CompilerParams(
            dimension_semantics=("parallel", "parallel"), vmem_limit_bytes=VMEM_LIMIT),
        name="in_proj",
    )(x, positions.reshape(B, S, 1), g.reshape(1, D), w_in_bf)


HEADS_PER_STEP = 4


def _attn_kernel(lq1_ref, lk1_ref, lq2_ref, lk2_ref, g_ref, q_ref, k_ref, v_ref, o_ref, *, lambda_init):
    for hh in range(HEADS_PER_STEP):
        _attn_head(lq1_ref, lk1_ref, lq2_ref, lk2_ref, g_ref, q_ref, k_ref, v_ref, o_ref, hh, lambda_init)


def _attn_head(lq1_ref, lk1_ref, lq2_ref, lk2_ref, g_ref, q_ref, k_ref, v_ref, o_ref, hh, lambda_init):
    hs = slice(hh * DIFF_V_DIM, (hh + 1) * DIFF_V_DIM)
    lam = (jnp.exp(jnp.sum(lq1_ref[...] * lk1_ref[...], axis=-1, keepdims=True))
           - jnp.exp(jnp.sum(lq2_ref[...] * lk2_ref[...], axis=-1, keepdims=True))
           + lambda_init)
    q = q_ref[0, :, hs]
    k = k_ref[0, :, hs]
    v = v_ref[0, :, hs]
    lane = lax.broadcasted_iota(I32, q.shape, 1)
    zero = jnp.zeros_like(q)
    q0 = jnp.where(lane < DIFF_HEAD_DIM, q, zero)
    q1 = jnp.where(lane >= DIFF_HEAD_DIM, q, zero)
    nt = (((1,), (1,)), ((), ()))
    s0 = lax.dot_general(q0, k, nt, preferred_element_type=F32)
    s1 = lax.dot_general(q1, k, nt, preferred_element_type=F32)
    v_ext = jnp.concatenate([v, jnp.ones_like(v)], axis=1)
    p0 = jnp.exp2(s0 - jnp.max(s0, axis=-1, keepdims=True)).astype(BF16)
    p1 = jnp.exp2(s1 - jnp.max(s1, axis=-1, keepdims=True)).astype(BF16)
    e0 = jnp.dot(p0, v_ext, preferred_element_type=F32)
    e1 = jnp.dot(p1, v_ext, preferred_element_type=F32)
    dv = v.shape[1]
    o = e0[:, :dv] / e0[:, dv:dv + 1] - e1[:, :dv] * (lam / e1[:, dv:dv + 1])
    o_ref[0, :, hs] = (_rms(o, g_ref[...]) * (1.0 - lambda_init)).astype(BF16)


def _diff_attn(q, k, v, lq1, lk1, lq2, lk2, g, lambda_init, tq):
    B, S, W = q.shape
    hw = DIFF_V_DIM * HEADS_PER_STEP
    H = W // hw
    vec = lambda b, h, i: (0, 0)
    return pl.pallas_call(
        functools.partial(_attn_kernel, lambda_init=lambda_init),
        grid=(B, H, S // tq),
        in_specs=[
            pl.BlockSpec((1, DIFF_HEAD_DIM), vec),
            pl.BlockSpec((1, DIFF_HEAD_DIM), vec),
            pl.BlockSpec((1, DIFF_HEAD_DIM), vec),
            pl.BlockSpec((1, DIFF_HEAD_DIM), vec),
            pl.BlockSpec((1, DIFF_V_DIM), vec),
            pl.BlockSpec((1, tq, hw), lambda b, h, i: (b, i, h)),
            pl.BlockSpec((1, S, hw), lambda b, h, i: (b, 0, h)),
            pl.BlockSpec((1, S, hw), lambda b, h, i: (b, 0, h)),
        ],
        out_specs=pl.BlockSpec((1, tq, hw), lambda b, h, i: (b, i, h)),
        out_shape=jax.ShapeDtypeStruct((B, S, W), BF16),
        compiler_params=pltpu.CompilerParams(
            dimension_semantics=("parallel", "parallel", "parallel"), vmem_limit_bytes=VMEM_LIMIT),
        name="diff_attn",
    )(lq1.reshape(1, -1), lk1.reshape(1, -1), lq2.reshape(1, -1), lk2.reshape(1, -1),
      g.reshape(1, -1), q, k, v)


def _lru_kernel(u_ref, gate_ref, cw_ref, cb_ref, w_ref, bias_ref, lam_ref, g_ref, out_ref, hf_ref, *, tc):
    S = u_ref.shape[1]
    C = u_ref.shape[2]
    nc = S // tc
    halo = SUBLANES
    neg_lam = -lam_ref[...]
    sp = jnp.maximum(neg_lam, 0.0) + jnp.log(1.0 + jnp.exp(-jnp.abs(neg_lam)))
    row = lax.broadcasted_iota(I32, (tc, 1), 0)
    conv_taps = cw_ref.shape[0]
    conv_left = 2

    def gates(c, d):
        r0 = pl.multiple_of(c * tc, tc)
        x = u_ref[0, pl.ds(r0, tc), :]
        prev = u_ref[0, pl.ds(pl.multiple_of(jnp.maximum(r0 - halo, 0), halo), halo), :]
        nxt = u_ref[0, pl.ds(pl.multiple_of(jnp.minimum(r0 + tc, S - halo), halo), halo), :]
        prev = jnp.where(c > 0, prev, 0.0)
        nxt = jnp.where(c < nc - 1, nxt, 0.0)
        win = jnp.concatenate([prev, x, nxt], axis=0)
        uc = cb_ref[...]
        for j in range(conv_taps):
            o = halo - conv_left + j
            uc = uc + cw_ref[j:j + 1, :] * win[o:o + tc, :]
        pre = (jnp.dot(uc.astype(BF16), w_ref[:, d * 2 * C:(d + 1) * 2 * C], preferred_element_type=F32)
               + bias_ref[:, d * 2 * C:(d + 1) * 2 * C])
        r = _sigmoid(pre[:, :C])
        i = _sigmoid(pre[:, C:])
        log_a = -LRU_C * r * sp[d:d + 1, :]
        a = jnp.exp(log_a)
        th = jnp.tanh(log_a)
        mult = jnp.sqrt(-2.0 * th / (1.0 - th))
        return a, mult * (i * uc)

    def scan(a, b, reverse):
        d = 1
        while d < tc:
            if reverse:
                a_s = pltpu.roll(a, tc - d, 0)
                b_s = pltpu.roll(b, tc - d, 0)
                m = row < tc - d
            else:
                a_s = pltpu.roll(a, d, 0)
                b_s = pltpu.roll(b, d, 0)
                m = row >= d
            a_s = jnp.where(m, a_s, 1.0)
            b_s = jnp.where(m, b_s, 0.0)
            b = a * b_s + b
            a = a * a_s
            d *= 2
        return a, b

    def fwd_body(c, h0):
        a, b = gates(c, 0)
        a, b = scan(a, b, False)
        h = a * h0 + b
        hf_ref[pl.ds(pl.multiple_of(c * tc, tc), tc), :] = h
        return h[tc - 1:tc, :]

    lax.fori_loop(0, nc, fwd_body, jnp.zeros((1, C), F32))

    def bwd_body(j, h0):
        c = nc - 1 - j
        r0 = pl.multiple_of(c * tc, tc)
        a, b = gates(c, 1)
        a, b = scan(a, b, True)
        h = a * h0 + b
        y = (hf_ref[pl.ds(r0, tc), :] + h) * _gelu_tanh(gate_ref[0, pl.ds(r0, tc), :].astype(F32))
        out_ref[0, pl.ds(r0, tc), :] = _rms(y, g_ref[...]).astype(BF16)
        return h[0:1, :]

    lax.fori_loop(0, nc, bwd_body, jnp.zeros((1, C), F32))


def _bi_rglru(u, gate, conv_w, conv_b, w_all_bf, bias_all, lru_lambda, g, tc):
    B, S, C = u.shape
    full = lambda b: (0, 0)
    return pl.pallas_call(
        functools.partial(_lru_kernel, tc=tc),
        grid=(B,),
        in_specs=[
            pl.BlockSpec((1, S, C), lambda b: (b, 0, 0)),
            pl.BlockSpec((1, S, C), lambda b: (b, 0, 0)),
            pl.BlockSpec(conv_w.shape, full),
            pl.BlockSpec((1, C), full),
            pl.BlockSpec(w_all_bf.shape, full),
            pl.BlockSpec(bias_all.shape, full),
            pl.BlockSpec(lru_lambda.shape, full),
            pl.BlockSpec((1, C), full),
        ],
        out_specs=pl.BlockSpec((1, S, C), lambda b: (b, 0, 0)),
        out_shape=jax.ShapeDtypeStruct((B, S, C), BF16),
        scratch_shapes=[pltpu.VMEM((S, C), F32)],
        compiler_params=pltpu.CompilerParams(
            dimension_semantics=("parallel",), vmem_limit_bytes=VMEM_LIMIT),
        name="bi_rglru",
    )(u, gate, conv_w, conv_b.reshape(1, C), w_all_bf, bias_all, lru_lambda, g.reshape(1, C))


def _topk_rows(s, k):
    n = s.shape[0]
    rid = lax.broadcasted_iota(I32, s.shape, 0)
    vals, ids = [], []
    for _ in range(k):
        m = jnp.max(s, axis=0, keepdims=True)
        sel = jnp.min(jnp.where(s == m, rid, n), axis=0, keepdims=True)
        vals.append(m)
        ids.append(sel)
        s = jnp.where(rid == sel, -jnp.inf, s)
    return jnp.concatenate(vals, axis=0), jnp.concatenate(ids, axis=0)


CAND_ROW_PIECES = 4


def _candidate_pieces(k):
    up = lambda n: -(-n // SUBLANES) * SUBLANES
    pieces = [("row", i, 0, up(k // (i + 1))) for i in range(CAND_ROW_PIECES)]
    for j in range(k // (CAND_ROW_PIECES + 1)):
        for i0 in range(0, k // (j + 1), SUBLANES):
            pieces.append(("col", j, i0, SUBLANES))
    return pieces


def _route_kernel(attn_ref, rec_ref, x_ref, wo_ref, g_ref, wq_ref, k1_ref, k2_ref,
                  h_ref, xn_ref, idx_ref, gate_ref):
    aw = attn_ref.shape[1]
    h = (x_ref[...]
         + jnp.dot(attn_ref[...], wo_ref[0:aw, :], preferred_element_type=F32)
         + jnp.dot(rec_ref[...], wo_ref[aw:, :], preferred_element_type=F32))
    h_ref[...] = h
    xn = _rms(h, g_ref[...])
    xn_ref[...] = xn
    q = jnp.dot(xn.astype(BF16), wq_ref[...], preferred_element_type=F32).astype(BF16)
    nt = (((1,), (1,)), ((), ()))
    s1 = lax.dot_general(k1_ref[...], q, nt, preferred_element_type=F32)
    s2 = lax.dot_general(k2_ref[...], q, nt, preferred_element_type=F32)
    k = PEER_TOPK
    tm = q.shape[0]
    pieces = _candidate_pieces(k)
    pos_parts, ok_parts = [], []
    for kind, fixed, start, length in pieces:
        r = lax.broadcasted_iota(I32, (length, tm), 0) + start
        i, j = (fixed, r) if kind == "row" else (r, fixed)
        pos_parts.append(i * k + j)
        ok = (i + 1) * (j + 1) <= k
        ok_parts.append(ok if kind == "row" else ok & (r >= CAND_ROW_PIECES))
    pos = jnp.concatenate(pos_parts, axis=0)
    ok = jnp.concatenate(ok_parts, axis=0)
    idx_rows, gate_rows = [], []
    for hd in range(PEER_HEADS):
        v1, i1 = _topk_rows(s1[hd * N_KEYS:(hd + 1) * N_KEYS, :], k)
        v2, i2 = _topk_rows(s2[hd * N_KEYS:(hd + 1) * N_KEYS, :], k)
        cand_parts, cidx_parts = [], []
        for kind, fixed, start, length in pieces:
            if kind == "row":
                cand_parts.append(v1[fixed:fixed + 1, :] + v2[start:start + length, :])
                cidx_parts.append(i1[fixed:fixed + 1, :] * N_KEYS + i2[start:start + length, :])
            else:
                cand_parts.append(v1[start:start + length, :] + v2[fixed:fixed + 1, :])
                cidx_parts.append(i1[start:start + length, :] * N_KEYS + i2[fixed:fixed + 1, :])
        cand = jnp.where(ok, jnp.concatenate(cand_parts, axis=0), -jnp.inf)
        cidx = jnp.concatenate(cidx_parts, axis=0)
        sc, ids = [], []
        for _ in range(k):
            m = jnp.max(cand, axis=0, keepdims=True)
            sel = jnp.min(jnp.where(cand == m, pos, k * k), axis=0, keepdims=True)
            hit = pos == sel
            sc.append(m)
            ids.append(jnp.max(jnp.where(hit, cidx, -1), axis=0, keepdims=True))
            cand = jnp.where(hit, -jnp.inf, cand)
        sc = jnp.concatenate(sc, axis=0)
        e = jnp.exp(sc - sc[0:1, :])
        gate_rows.append(e / jnp.sum(e, axis=0, keepdims=True))
        idx_rows.append(jnp.concatenate(ids, axis=0))
    idx_ref[...] = jnp.concatenate(idx_rows, axis=0)
    gate_ref[...] = jnp.concatenate(gate_rows, axis=0)


def _mix_route(attn, rec, x2, wo_bf, g, wq_bf, k1t_bf, k2t_bf, tm):
    T, D = x2.shape
    NK = PEER_HEADS * PEER_TOPK
    row = lambda i: (i, 0)
    full = lambda i: (0, 0)
    return pl.pallas_call(
        _route_kernel,
        grid=(T // tm,),
        in_specs=[
            pl.BlockSpec((tm, attn.shape[1]), row),
            pl.BlockSpec((tm, rec.shape[1]), row),
            pl.BlockSpec((tm, D), row),
            pl.BlockSpec(wo_bf.shape, full),
            pl.BlockSpec((1, D), full),
            pl.BlockSpec(wq_bf.shape, full),
            pl.BlockSpec(k1t_bf.shape, full),
            pl.BlockSpec(k2t_bf.shape, full),
        ],
        out_specs=[
            pl.BlockSpec((tm, D), row),
            pl.BlockSpec((tm, D), row),
            pl.BlockSpec((NK, tm), lambda i: (0, i)),
            pl.BlockSpec((NK, tm), lambda i: (0, i)),
        ],
        out_shape=[
            jax.ShapeDtypeStruct((T, D), F32),
            jax.ShapeDtypeStruct((T, D), F32),
            jax.ShapeDtypeStruct((NK, T), I32),
            jax.ShapeDtypeStruct((NK, T), F32),
        ],
        compiler_params=pltpu.CompilerParams(
            dimension_semantics=("parallel",), vmem_limit_bytes=VMEM_LIMIT),
        name="mix_route",
    )(attn, rec, x2, wo_bf, g.reshape(1, D), wq_bf, k1t_bf, k2t_bf)


ROW_CHUNKS = 8
HI_HALF_MASK = -65536


def _pack_kernel(u_ref, v_ref, o_ref):
    half = u_ref.shape[1] // 2

    def words(a):
        bits = pltpu.bitcast(a.astype(BF16).astype(F32), I32)
        lo = lax.shift_right_logical(bits[:, :half], jnp.full((a.shape[0], half), 16, I32))
        return lo | (bits[:, half:] & HI_HALF_MASK)
    o_ref[:, :half] = words(u_ref[...])
    o_ref[:, half:] = words(v_ref[...])


def _pack_rows(u, v, rows=PACK_ROWS):
    E, D = u.shape
    rows = min(rows, E)
    blk = lambda i: (i, 0)
    return pl.pallas_call(
        _pack_kernel,
        grid=(E // rows,),
        in_specs=[pl.BlockSpec((rows, D), blk), pl.BlockSpec((rows, D), blk)],
        out_specs=pl.BlockSpec((rows, D), blk),
        out_shape=jax.ShapeDtypeStruct((E, D), I32),
        compiler_params=pltpu.CompilerParams(dimension_semantics=("parallel",), vmem_limit_bytes=VMEM_LIMIT),
        name="pack_rows",
    )(u, v)


def _word_lo(w):
    return pltpu.bitcast(w << 16, F32)


def _word_hi(w):
    return pltpu.bitcast(w & HI_HALF_MASK, F32)
TOKEN_UNROLL = 2
DMA_PRIORITIES = 2
PEER_RING = 4
PEER_PREFETCH = 2


def _peer_kernel(idx_ref, idx_next_ref, xn_ref, gate_ref, uv_hbm, o_ref, *scratch, tt, nk):
    bufs = scratch[:PEER_RING]
    lg_ref, act_ref, sems = scratch[PEER_RING:]
    g = pl.program_id(0)
    n = pl.num_programs(0)
    nt = (((1,), (1,)), ((), ()))
    ch = ROW_CHUNKS
    hc = ch // 2

    def row_copy(ids_ref, q, t, k):
        e = ids_ref[0, 0, (q * tt + t) * nk + k]
        return pltpu.make_async_copy(uv_hbm.at[e], bufs[q].at[t, :, k, :], sems.at[q])

    def wait_group(q):
        pltpu.make_async_copy(bufs[(q + 1) % PEER_RING], bufs[q], sems.at[q]).wait()

    @pl.when(g == 0)
    def _():
        for q in range(PEER_PREFETCH):
            def body(t, carry, q=q):
                for k in range(nk):
                    row_copy(idx_ref, q, t, k).start(priority=k % DMA_PRIORITIES)
                return carry
            lax.fori_loop(0, tt, body, 0)

    def group(q):
        buf = bufs[q]
        row0 = q * tt
        ahead = q + PEER_PREFETCH
        ids_ahead = idx_ref if ahead < PEER_RING else idx_next_ref

        def issue(t, k0, k1):
            for k in range(k0, k1):
                row_copy(ids_ahead, ahead % PEER_RING, t, k).start(priority=k % DMA_PRIORITIES)

        def phase_u(tb, carry):
            for j in range(TOKEN_UNROLL):
                t = tb * TOKEN_UNROLL + j
                issue(t, 0, nk // 2)
                xt = xn_ref[row0 + t]
                acc = None
                for s in range(hc):
                    w = buf[t, s]
                    part = _word_lo(w) * xt[s:s + 1, :] + _word_hi(w) * xt[hc + s:hc + s + 1, :]
                    acc = part if acc is None else acc + part
                hi = acc.astype(BF16)
                lo = (acc - hi.astype(F32)).astype(BF16)
                ones = jnp.ones((SUBLANES, LANES), BF16)
                red = (lax.dot_general(ones, hi, nt, preferred_element_type=F32)
                       + lax.dot_general(ones, lo, nt, preferred_element_type=F32))
                lg_ref[t] = red[0:1, :]
            return carry
        lax.fori_loop(0, tt // TOKEN_UNROLL, phase_u, 0)
        act_ref[...] = _gelu_tanh(lg_ref[...]) * gate_ref[row0:row0 + tt]

        def phase_v(tb, carry):
            for j in range(TOKEN_UNROLL):
                t = tb * TOKEN_UNROLL + j
                issue(t, nk // 2, nk)
                act = act_ref[t].astype(BF16)
                words = [buf[t, hc + s] for s in range(hc)]
                cols = ([jnp.dot(act, _word_lo(w).astype(BF16), preferred_element_type=F32) for w in words]
                        + [jnp.dot(act, _word_hi(w).astype(BF16), preferred_element_type=F32) for w in words])
                o_ref[row0 + t] = jnp.concatenate(cols, axis=0)
            return carry
        lax.fori_loop(0, tt // TOKEN_UNROLL, phase_v, 0)

    for q in range(PEER_RING):
        wait_group(q)
        group(q)

    @pl.when(g == n - 1)
    def _():
        for q in range(PEER_PREFETCH):
            wait_group(q)


def _peer(idx, gate, xn3, uv4, n_tokens, tt):
    T, ch, _ = xn3.shape
    nk = idx.shape[1]
    step = PEER_RING * tt
    n = n_tokens // step
    idx3 = idx.reshape(T // step, 1, step * nk)
    tok = lambda i: (i, 0, 0)
    return pl.pallas_call(
        functools.partial(_peer_kernel, tt=tt, nk=nk),
        grid=(n,),
        in_specs=[
            pl.BlockSpec((1, 1, step * nk), tok, memory_space=pltpu.SMEM),
            pl.BlockSpec((1, 1, step * nk), lambda i: (jnp.minimum(i + 1, n - 1), 0, 0), memory_space=pltpu.SMEM),
            pl.BlockSpec((step, ch, LANES), tok),
            pl.BlockSpec((step, 1, nk), tok),
            pl.BlockSpec(memory_space=pl.ANY),
        ],
        out_specs=pl.BlockSpec((step, ch, LANES), tok),
        out_shape=jax.ShapeDtypeStruct((n_tokens, ch, LANES), F32),
        scratch_shapes=(
            [pltpu.VMEM((tt, ch, nk, LANES), I32) for _ in range(PEER_RING)]
            + [pltpu.VMEM((tt, 1, nk), F32), pltpu.VMEM((tt, 1, nk), F32), pltpu.SemaphoreType.DMA((PEER_RING,))]),
        compiler_params=pltpu.CompilerParams(
            dimension_semantics=("arbitrary",), vmem_limit_bytes=VMEM_LIMIT),
        name="peer",
    )(idx3, idx3, xn3, gate.reshape(T, 1, nk), uv4)


SC_CORES = 2
SC_SUBCORES = 16
SC_LANES = 16
SC_WORKERS = SC_CORES * SC_SUBCORES
SC_EXPERT_BLOCK = 16
SC_CHUNK_GROUP = 4
IDX_SLOTS = 4
SC_TOKEN_SHARE_NUM, SC_TOKEN_SHARE_DEN = 131, 256


def _peer_sc(idx, gate, xn, uv, tok0, toks_per_worker):
    T, nk = idx.shape
    D = xn.shape[1]
    L = SC_LANES
    kb_rows = SC_EXPERT_BLOCK
    n_blocks = nk // kb_rows
    n_chunks = D // L
    half = D // 2
    uc = SC_CHUNK_GROUP
    word_lo = lambda w: plsc.bitcast(w << 16, F32)
    word_hi = lambda w: plsc.bitcast(w & HI_HALF_MASK, F32)
    mesh = plsc.VectorSubcoreMesh(core_axis_name="c", subcore_axis_name="s")

    def body(idx_hbm, gate_hbm, xn_hbm, uv_hbm, out_hbm, idx_v, gate_v, x_v, o_v, rows_v, acc_v, sems, tsems):
        wid = lax.axis_index("s") * SC_CORES + lax.axis_index("c")
        base = wid * toks_per_worker
        lane = lax.iota(I32, L)

        def tok_of(ti):
            return tok0 + base + jnp.minimum(ti, toks_per_worker - 1)

        def idx_copy(ti):
            return pltpu.make_async_copy(idx_hbm.at[tok_of(ti)], idx_v.at[pl.ds((ti % IDX_SLOTS) * nk, nk)],
                                         tsems.at[0])

        def gate_copy(ti):
            return pltpu.make_async_copy(gate_hbm.at[tok_of(ti)], gate_v.at[pl.ds((ti % 2) * nk, nk)], tsems.at[1])

        def x_copy(ti):
            return pltpu.make_async_copy(xn_hbm.at[tok_of(ti)], x_v.at[pl.ds((ti % 2) * D, D)], tsems.at[2])

        def gather(ti, kb, buf):
            first = (ti % IDX_SLOTS) * nk + kb * kb_rows
            return pltpu.make_async_copy(uv_hbm.at[idx_v.at[pl.ds(first, kb_rows)]], rows_v.at[buf], sems.at[buf])

        def evaluate(kb, rv, goff, xoff):
            for j in range(kb_rows):
                acc_v[j] = jnp.zeros((L,), F32)

            @plsc.parallel_loop(0, half // L // uc, unroll=2)
            def _(cg):
                x_lo = [x_v[pl.ds(xoff + (cg * uc + cc) * L, L)] for cc in range(uc)]
                x_hi = [x_v[pl.ds(xoff + half + (cg * uc + cc) * L, L)] for cc in range(uc)]
                for j in range(kb_rows):
                    pr = None
                    for cc in range(uc):
                        w = rv[j, pl.ds((cg * uc + cc) * L, L)]
                        part = word_lo(w) * x_lo[cc] + word_hi(w) * x_hi[cc]
                        pr = part if pr is None else pr + part
                    plsc.addupdate(acc_v.at[j], pr)
            logits = jnp.zeros((L,), F32)
            for j in range(kb_rows):
                logits = jnp.where(lane == j, jnp.sum(acc_v[j]), logits)
            z = math.sqrt(2.0 / math.pi) * (logits + 0.044715 * (logits * logits * logits))
            th = 1.0 - 2.0 / (1.0 + jnp.exp(2.0 * z))
            act = 0.5 * logits * (1.0 + th) * gate_v[pl.ds(goff + kb * kb_rows, kb_rows)]
            acts = [jnp.sum(jnp.where(lane == j, act, 0.0)) for j in range(kb_rows)]

            @plsc.parallel_loop(0, half // L, unroll=2)
            def _(c):
                acc_lo = None
                acc_hi = None
                for j in range(kb_rows):
                    w = rv[j, pl.ds(half + c * L, L)]
                    lo = acts[j] * word_lo(w)
                    hi = acts[j] * word_hi(w)
                    acc_lo = lo if acc_lo is None else acc_lo + lo
                    acc_hi = hi if acc_hi is None else acc_hi + hi
                plsc.addupdate(o_v.at[pl.ds(c * L, L)], acc_lo)
                plsc.addupdate(o_v.at[pl.ds(half + c * L, L)], acc_hi)

        def token(ti, carry):
            gate_copy(ti).wait()
            x_copy(ti).wait()
            idx_copy(ti + 1).wait()
            gate_copy(ti + 1).start()
            x_copy(ti + 1).start()
            idx_copy(ti + 2).start()
            goff = (ti % 2) * nk
            xoff = (ti % 2) * D

            @plsc.parallel_loop(0, n_chunks)
            def _(c):
                o_v[pl.ds(c * L, L)] = jnp.zeros((L,), F32)

            def block(kb, carry):
                buf = kb % 2
                last = kb + 1 >= n_blocks
                gather(ti, kb, buf).wait()
                gather(jnp.where(last, ti + 1, ti), jnp.where(last, 0, kb + 1), 1 - buf).start()
                evaluate(kb, rows_v.at[buf], goff, xoff)
                return carry
            lax.fori_loop(0, n_blocks, block, 0)
            pltpu.sync_copy(o_v, out_hbm.at[base + ti])
            return carry

        idx_copy(0).start()
        idx_copy(0).wait()
        idx_copy(1).start()
        gate_copy(0).start()
        x_copy(0).start()
        gather(0, 0, 0).start()
        lax.fori_loop(0, toks_per_worker, token, 0)
        gate_copy(toks_per_worker).wait()
        x_copy(toks_per_worker).wait()
        idx_copy(toks_per_worker + 1).wait()
        gather(toks_per_worker, 0, 0).wait()

    return pl.kernel(
        body, mesh=mesh,
        compiler_params=pltpu.CompilerParams(needs_layout_passes=False),
        out_type=jax.ShapeDtypeStruct((SC_WORKERS * toks_per_worker, D), F32),
        scratch_types=[
            pltpu.VMEM((IDX_SLOTS * nk,), I32),
            pltpu.VMEM((2 * nk,), F32),
            pltpu.VMEM((2 * D,), F32),
            pltpu.VMEM((D,), F32),
            pltpu.VMEM((2, kb_rows, D), I32),
            pltpu.VMEM((kb_rows, L), F32),
            pltpu.SemaphoreType.DMA((2,)),
            pltpu.SemaphoreType.DMA((3,)),
        ],
    )(idx, gate, xn, uv)


def _ple_kernel(h_ref, po_ref, p_ref, g3_ref, wg_ref, wp_ref, gf_ref, o_ref, *, final):
    h = h_ref[...] + po_ref[...]
    xn = _rms(h, g3_ref[...]).astype(BF16)
    gate = _sigmoid(jnp.dot(xn, wg_ref[...], preferred_element_type=F32))
    proj = jnp.dot(p_ref[...].astype(BF16), wp_ref[...], preferred_element_type=F32)
    h = h + gate * proj
    o_ref[...] = _rms(h, gf_ref[...]) if final else h


def _ple_out(h1, peer_out, p2, g3, wg_bf, wp_bf, gf, final, tm):
    T, D = h1.shape
    row = lambda i: (i, 0)
    full = lambda i: (0, 0)
    return pl.pallas_call(
        functools.partial(_ple_kernel, final=final),
        grid=(T // tm,),
        in_specs=[
            pl.BlockSpec((tm, D), row),
            pl.BlockSpec((tm, D), row),
            pl.BlockSpec((tm, p2.shape[1]), row),
            pl.BlockSpec((1, D), full),
            pl.BlockSpec(wg_bf.shape, full),
            pl.BlockSpec(wp_bf.shape, full),
            pl.BlockSpec((1, D), full),
        ],
        out_specs=pl.BlockSpec((tm, D), row),
        out_shape=jax.ShapeDtypeStruct((T, D), F32),
        compiler_params=pltpu.CompilerParams(
            dimension_semantics=("parallel",), vmem_limit_bytes=VMEM_LIMIT),
        name="ple_out",
    )(h1, peer_out, p2, g3.reshape(1, D), wg_bf, wp_bf, gf.reshape(1, D))


def _block_diag(w):
    nb, bw, _ = w.shape
    eye = jnp.eye(nb, dtype=w.dtype)
    return (eye[:, None, :, None] * w[:, :, None, :]).reshape(nb * bw, nb * bw)


def _key_matrix(keys, half):
    z = jnp.zeros_like(keys)
    blk = jnp.concatenate([keys, z] if half == 0 else [z, keys], axis=1)
    return jnp.kron(jnp.eye(PEER_HEADS, dtype=keys.dtype), blk)


def kernel(x, p, positions, norm_mix_g, w_in, lambda_q1, lambda_k1, lambda_q2, lambda_k2, diff_norm_g, conv_w, conv_b, lru_wa, lru_ba, lru_wx, lru_bx, lru_lambda, lru_norm_g, w_out, norm_ffn_g, peer_wq, peer_keys1, peer_keys2, peer_u, peer_v, norm_ple_g, ple_w_gate, ple_w_proj, final_norm_g):
    B, S, D = x.shape
    T = B * S
    depth = w_in.shape[0]
    h = x
    for i in range(depth):
        lambda_init = 0.8 - 0.6 * math.exp(-0.3 * i)
        q, k, v, u, gate = _in_proj(h, positions, norm_mix_g[i], w_in[i].astype(BF16), tm=IN_PROJ_ROWS)
        attn = _diff_attn(q, k, v, lambda_q1[i], lambda_k1[i], lambda_q2[i], lambda_k2[i],
                          diff_norm_g[i], lambda_init, tq=ATTN_Q_ROWS)
        w_all = jnp.concatenate([_block_diag(lru_wa[i, 0]), _block_diag(lru_wx[i, 0]),
                                 _block_diag(lru_wa[i, 1]), _block_diag(lru_wx[i, 1])], axis=1).astype(BF16)
        bias_all = jnp.concatenate([lru_ba[i, 0], lru_bx[i, 0], lru_ba[i, 1], lru_bx[i, 1]]).reshape(1, -1)
        rec = _bi_rglru(u, gate, conv_w[i], conv_b[i], w_all, bias_all, lru_lambda[i], lru_norm_g[i], tc=LRU_CHUNK_ROWS)
        h1, xn2, idx_t, gate_t = _mix_route(
            attn.reshape(T, -1), rec.reshape(T, -1), h.reshape(T, D), w_out[i].astype(BF16), norm_ffn_g[i],
            peer_wq[i].astype(BF16), _key_matrix(peer_keys1[i], 0).astype(BF16),
            _key_matrix(peer_keys2[i], 1).astype(BF16), tm=ROUTE_TOKENS)
        n_exp = peer_u.shape[1]
        uv2 = _pack_rows(peer_u[i], peer_v[i])
        uv4 = uv2.reshape(n_exp, ROW_CHUNKS, LANES)
        idx, gate_tk = idx_t.T, gate_t.T
        t_sc = T * SC_TOKEN_SHARE_NUM // SC_TOKEN_SHARE_DEN // (2 * SC_WORKERS) * (2 * SC_WORKERS)
        t_tc = T - t_sc
        peer_sc = _peer_sc(idx, gate_tk, xn2, uv2, t_tc, t_sc // SC_WORKERS)
        peer_tc = _peer(idx, gate_tk, xn2.reshape(T, ROW_CHUNKS, LANES), uv4, t_tc, tt=PEER_GROUP_TOKENS)
        peer_out = jnp.concatenate([peer_tc.reshape(t_tc, D), peer_sc], axis=0)
        h = _ple_out(h1, peer_out, p[i].reshape(T, -1), norm_ple_g[i], ple_w_gate[i].astype(BF16),
                     ple_w_proj[i].astype(BF16), final_norm_g, final=(i == depth - 1), tm=PLE_ROWS)
        h = h.reshape(B, S, D)
    return h
```
